```python
import math
import jax, jax.numpy as jnp
from jax import lax
import numpy as np

D_MODEL = 1024
BATCH = 4
SEQ = 4096
DEPTH = 1
DEC_BATCH = 16
DEC_SEQ = 4096
PAST_LEN = 128

MIX_WIDTH = D_MODEL
MLA_HEADS = 8
QK_NOPE = 64
QK_ROPE = 32
V_HEAD = 64
Q_LORA = 256
KV_LORA = 128
MLA_WIDTH = MLA_HEADS * V_HEAD
Q_BLOCK = 128
ROPE_THETA = 10000.0
GMLP_GROUPS = 8
GMLP_WIDTH = MIX_WIDTH - MLA_WIDTH
GMLP_GROUP_DIM = GMLP_WIDTH // GMLP_GROUPS
CHUNK = 128
IN_COLS = Q_LORA + KV_LORA + QK_ROPE + 2 * GMLP_WIDTH
N_EXPERT_GROUPS = 4
EXPERTS_PER_GROUP = 8
N_EXPERTS = N_EXPERT_GROUPS * EXPERTS_PER_GROUP
TOP_K_IN_GROUP = 2
EXPERT_FF = 256
EPS = 1e-6

kernel_name = "hymba_style_mla_gmlp_hmoe_encoder"


def _rmsnorm(x, g):
    xf = x.astype(jnp.float32)
    y = xf * lax.rsqrt(jnp.mean(xf * xf, axis=-1, keepdims=True) + EPS)
    return (y * g.astype(jnp.float32)).astype(x.dtype)


def _rope(x, seq_len):
    half = x.shape[-1] // 2
    inv = ROPE_THETA ** (-jnp.arange(half, dtype=jnp.float32) / half)
    ang = jnp.arange(seq_len, dtype=jnp.float32)[:, None] * inv[None, :]
    cos = jnp.cos(ang)[None, :, None, :].astype(x.dtype)
    sin = jnp.sin(ang)[None, :, None, :].astype(x.dtype)
    x1, x2 = x[..., :half], x[..., half:]
    return jnp.concatenate([x1 * cos - x2 * sin, x1 * sin + x2 * cos], axis=-1)


def _block_attention(q, k, v):
    B, S, H, DQK = q.shape
    nb = S // Q_BLOCK
    qb = q.reshape(B, nb, Q_BLOCK, H, DQK).transpose(1, 0, 2, 3, 4)
    scale = DQK ** -0.5

    def one(qblk):
        s = jnp.einsum('bqhd,bkhd->bhqk', qblk, k).astype(jnp.float32) * scale
        p = jax.nn.softmax(s, axis=-1).astype(v.dtype)
        return jnp.einsum('bhqk,bkhd->bqhd', p, v)

    o = lax.map(one, qb)
    return o.transpose(1, 0, 2, 3, 4).reshape(B, S, H * v.shape[-1])


def _mla(cq, ckv, kr, g_q, w_uq, g_kv, w_ukv):
    B, S, _ = cq.shape
    q = (_rmsnorm(cq, g_q) @ w_uq).reshape(B, S, MLA_HEADS, QK_NOPE + QK_ROPE)
    q = jnp.concatenate([q[..., :QK_NOPE], _rope(q[..., QK_NOPE:], S)], axis=-1)
    kv = (_rmsnorm(ckv, g_kv) @ w_ukv).reshape(B, S, MLA_HEADS, QK_NOPE + V_HEAD)
    k_nope, v = kv[..., :QK_NOPE], kv[..., QK_NOPE:]
    k_rope = _rope(kr[:, :, None, :], S)
    k = jnp.concatenate([k_nope, jnp.broadcast_to(k_rope, (B, S, MLA_HEADS, QK_ROPE))], axis=-1)
    return _block_attention(q, k, v)


def _chunked_sgu(u, v, g_v, w_spatial, b_spatial):
    B, S, _ = u.shape
    u = jax.nn.gelu(u)
    v = _rmsnorm(jax.nn.gelu(v), g_v)
    vc = v.reshape(B, S // CHUNK, CHUNK, GMLP_GROUPS, GMLP_GROUP_DIM)
    vm = jnp.einsum('gts,bnsgc->bntgc', w_spatial, vc) + b_spatial.T[None, None, :, :, None]
    return u * vm.reshape(B, S, GMLP_WIDTH)


def _hier_moe(h, w_router_group, b_router_group, w_router_expert, b_router_expert,
              w_gate, w_up, w_down):
    B, S, D = h.shape
    t = h.reshape(B * S, D)
    tf = t.astype(jnp.float32)
    pg = jax.nn.softmax(tf @ w_router_group.astype(jnp.float32) + b_router_group.astype(jnp.float32), axis=-1)
    gi = jnp.argmax(pg, axis=-1)
    pg_sel = jnp.max(pg, axis=-1)
    le = (tf @ w_router_expert.astype(jnp.float32) + b_router_expert.astype(jnp.float32))
    le = le.reshape(-1, N_EXPERT_GROUPS, EXPERTS_PER_GROUP)
    le_sel = jnp.take_along_axis(le, gi[:, None, None], axis=1)[:, 0, :]
    pe = jax.nn.softmax(le_sel, axis=-1)
    vals, idx = lax.top_k(pe, TOP_K_IN_GROUP)
    w = pg_sel[:, None] * vals / jnp.sum(vals, axis=-1, keepdims=True)
    eid = gi[:, None] * EXPERTS_PER_GROUP + idx
    gates = jnp.einsum('tk,tke->te', w, jax.nn.one_hot(eid, N_EXPERTS, dtype=jnp.float32)).astype(t.dtype)

    def body(acc, xs):
        wg, wu, wd, g = xs
        hid = jax.nn.silu(t @ wg) * (t @ wu)
        return acc + g[:, None] * (hid @ wd), None

    out, _ = lax.scan(body, jnp.zeros_like(t), (w_gate, w_up, w_down, gates.T))
    return out.reshape(B, S, D)


def _layer(x, c, p):
    mod = jax.nn.silu(c) @ p['w_ada'] + p['b_ada']
    sh1, sc1, gt1, sh2, sc2, gt2 = [m[:, None, :] for m in jnp.split(mod, 6, axis=-1)]

    h = _rmsnorm(x, p['g_pre1']) * (1 + sc1) + sh1
    z = h @ p['w_in']
    o0 = Q_LORA
    o1 = o0 + KV_LORA
    o2 = o1 + QK_ROPE
    o3 = o2 + GMLP_WIDTH
    cq, ckv, kr, u, v = z[..., :o0], z[..., o0:o1], z[..., o1:o2], z[..., o2:o3], z[..., o3:]
    a = _mla(cq, ckv, kr, p['g_q'], p['w_uq'], p['g_kv'], p['w_ukv'])
    s = _chunked_sgu(u, v, p['g_v_gmlp'], p['w_spatial'], p['b_spatial'])
    merged = jnp.concatenate([_rmsnorm(a, p['g_attn_out']), _rmsnorm(s, p['g_gmlp_out'])], axis=-1)
    x = x + gt1 * _rmsnorm(merged @ p['w_out'], p['g_post1'])

    h2 = _rmsnorm(x, p['g_pre2']) * (1 + sc2) + sh2
    m = _hier_moe(h2, p['w_router_group'], p['b_router_group'], p['w_router_expert'],
                  p['b_router_expert'], p['w_gate'], p['w_up'], p['w_down'])
    return x + gt2 * _rmsnorm(m, p['g_post2'])


def setup_inputs(seed: int = 0) -> dict:
    key = jax.random.key(seed)
    ks = iter(jax.random.split(key, 40))

    def nrm(shape, scale):
        return jax.random.normal(next(ks), shape, jnp.float32) * scale

    def gain(n):
        return 1.0 + nrm((n,), 0.05)

    D = D_MODEL
    return {
        'x_prompt': nrm((BATCH, SEQ, D), 1.0),
        'x_sample': nrm((DEC_BATCH, DEC_SEQ, D), 1.0),
        'c_prompt': nrm((BATCH, D), 1.0),
        'c_sample': nrm((DEC_BATCH, D), 1.0),
        'w_ada': nrm((D, 6 * D), 0.1 * D ** -0.5),
        'b_ada': nrm((6 * D,), 0.02),
        'g_pre1': gain(D),
        'g_post1': gain(D),
        'g_pre2': gain(D),
        'g_post2': gain(D),
        'w_in': nrm((D, IN_COLS), D ** -0.5),
        'g_q': gain(Q_LORA),
        'w_uq': nrm((Q_LORA, MLA_HEADS * (QK_NOPE + QK_ROPE)), Q_LORA ** -0.5),
        'g_kv': gain(KV_LORA),
        'w_ukv': nrm((KV_LORA, MLA_HEADS * (QK_NOPE + V_HEAD)), KV_LORA ** -0.5),
        'g_v_gmlp': gain(GMLP_WIDTH),
        'w_spatial': nrm((GMLP_GROUPS, CHUNK, CHUNK), CHUNK ** -0.5),
        'b_spatial': 1.0 + nrm((GMLP_GROUPS, CHUNK), 0.05),
        'g_attn_out': gain(MLA_WIDTH),
        'g_gmlp_out': gain(GMLP_WIDTH),
        'w_out': nrm((MIX_WIDTH, D), MIX_WIDTH ** -0.5),
        'w_router_group': nrm((D, N_EXPERT_GROUPS), D ** -0.5),
        'b_router_group': nrm((N_EXPERT_GROUPS,), 0.01),
        'w_router_expert': nrm((D, N_EXPERTS), D ** -0.5),
        'b_router_expert': nrm((N_EXPERTS,), 0.01),
        'w_gate': nrm((N_EXPERTS, D, EXPERT_FF), D ** -0.5),
        'w_up': nrm((N_EXPERTS, D, EXPERT_FF), D ** -0.5),
        'w_down': nrm((N_EXPERTS, EXPERT_FF, D), EXPERT_FF ** -0.5),
    }


def reference(x_prompt, x_sample, c_prompt, c_sample, w_ada, b_ada, g_pre1, g_post1, g_pre2,
              g_post2, w_in, g_q, w_uq, g_kv, w_ukv, g_v_gmlp, w_spatial, b_spatial,
              g_attn_out, g_gmlp_out, w_out, w_router_group, b_router_group,
              w_router_expert, b_router_expert, w_gate, w_up, w_down):
    p = {
        'w_ada': w_ada, 'b_ada': b_ada, 'g_pre1': g_pre1, 'g_post1': g_post1,
        'g_pre2': g_pre2, 'g_post2': g_post2, 'w_in': w_in, 'g_q': g_q, 'w_uq': w_uq,
        'g_kv': g_kv, 'w_ukv': w_ukv, 'g_v_gmlp': g_v_gmlp, 'w_spatial': w_spatial,
        'b_spatial': b_spatial, 'g_attn_out': g_attn_out, 'g_gmlp_out': g_gmlp_out,
        'w_out': w_out, 'w_router_group': w_router_group, 'b_router_group': b_router_group,
        'w_router_expert': w_router_expert, 'b_router_expert': b_router_expert,
        'w_gate': w_gate, 'w_up': w_up, 'w_down': w_down,
    }
    y_prompt = x_prompt
    y_sample = x_sample
    for _ in range(DEPTH):
        y_prompt = _layer(y_prompt, c_prompt, p)
        y_sample = _layer(y_sample, c_sample, p)
    return (y_prompt, y_sample)
```

```python
import functools
import math

import jax
import jax.numpy as jnp
from jax import lax
from jax.experimental import pallas as pl
from jax.experimental.pallas import tpu as pltpu

F32 = jnp.float32
BF16 = jnp.bfloat16

D_MODEL = 1024
N_HEADS = 8
QK_NOPE = 64
QK_ROPE = 32
ROPE_HALF = QK_ROPE // 2
V_HEAD = 64
Q_LORA = 256
KV_LORA = 128
GMLP_WIDTH = 512
GMLP_GROUPS = 8
GMLP_GROUP_DIM = 64
CHUNK = 128
N_EXPERTS = 32
N_EXPERT_GROUPS = 4
EXPERTS_PER_GROUP = 8
EXPERT_FF = 256
ROPE_THETA = 10000.0
EPS = 1e-6

LANES = 128
SUBLANES = 8
HEAD_PAD = LANES

TOKEN_TILE = 512
Q_TILE = 512
K_TILE = 512
EXPERT_TILE = 256
MOVE_TILE = 2048
MOVE_WINDOW = 32
RANK_CHUNK = 256
VMEM_LIMIT = 56 * 1024 * 1024

ROW_TILES = D_MODEL // LANES

_SQRT_2_OVER_PI = math.sqrt(2.0 / math.pi)


def _rms(x):
    return x * lax.rsqrt(jnp.mean(x * x, axis=-1, keepdims=True) + EPS)


def _gelu_tanh(x):
    return 0.5 * x * (1.0 + jnp.tanh(_SQRT_2_OVER_PI * (x + 0.044715 * (x * x * x))))


def _split_bf16(x):
    hi = x.astype(BF16)
    lo = (x - hi.astype(F32)).astype(BF16)
    return hi, lo


def _dot(a, b):
    return jnp.dot(a, b, preferred_element_type=F32)


def _params(sem, vmem=VMEM_LIMIT):
    return pltpu.CompilerParams(dimension_semantics=sem, vmem_limit_bytes=vmem)


def _ada_kernel(c_ref, w_ref, b_ref, o_ref):
    c = c_ref[...]
    a = c * jax.nn.sigmoid(c)
    a_hi, a_lo = _split_bf16(a)
    w_hi, w_lo = _split_bf16(w_ref[...])
    o_ref[...] = _dot(a_hi, w_hi) + _dot(a_hi, w_lo) + _dot(a_lo, w_hi) + b_ref[...]


def _ada(c, w_ada, b_ada):
    nb = c.shape[0]
    n_out = w_ada.shape[1]
    blk = D_MODEL
    return pl.pallas_call(
        _ada_kernel,
        grid=(n_out // blk,),
        in_specs=[
            pl.BlockSpec((nb, D_MODEL), lambda j: (0, 0)),
            pl.BlockSpec((D_MODEL, blk), lambda j: (0, j)),
            pl.BlockSpec((1, blk), lambda j: (0, j)),
        ],
        out_specs=pl.BlockSpec((nb, blk), lambda j: (0, j)),
        out_shape=jax.ShapeDtypeStruct((nb, n_out), F32),
        compiler_params=_params(("arbitrary",)),
        name="ada",
    )(c, w_ada, b_ada.reshape(1, n_out))


_C_CQ = 0
_C_CKV = _C_CQ + Q_LORA
_C_KR = _C_CKV + KV_LORA
_C_U = _C_KR + LANES
_C_V = _C_U + GMLP_WIDTH
_C_END = _C_V + GMLP_WIDTH
_ROPE_LO = QK_NOPE
_ROLL_PARTNER = LANES - QK_ROPE


def _premix_kernel(x_ref, mod_ref, cos_ref, sin_ref, gpre_ref, win_ref, gq_ref, wuq_ref, gkv_ref, wukv_ref,
                   gv_ref, wsp_ref, bsp_ref, ggo_ref, q_ref, kt_ref, v_ref, sn_ref):
    x = x_ref[0]
    mod = mod_ref[0]
    shift1, scale1 = mod[0:1], mod[1:2]
    h = _rms(x) * gpre_ref[...] * (1.0 + scale1) + shift1
    z = _dot(h.astype(BF16), win_ref[...])

    cosb = cos_ref[...]
    sinb = sin_ref[...]
    lane = lax.broadcasted_iota(jnp.int32, (1, LANES), 1)
    nope_mask = jnp.where(lane < QK_NOPE, 1.0, 0.0).astype(F32)

    qscale = (QK_NOPE + QK_ROPE) ** -0.5 * math.log2(math.e)
    cq_tab = (nope_mask + cosb) * qscale
    sq_tab = sinb * qscale
    cqn = (_rms(z[:, _C_CQ:_C_CKV]) * gq_ref[...]).astype(BF16)
    qb = _dot(cqn, wuq_ref[...])
    for hd in range(N_HEADS):
        blk = qb[:, hd * HEAD_PAD:(hd + 1) * HEAD_PAD]
        qh = blk * cq_tab + pltpu.roll(blk, _ROLL_PARTNER, 1) * sq_tab
        q_ref[0, hd] = qh.astype(BF16)

    ckvn = (_rms(z[:, _C_CKV:_C_KR]) * gkv_ref[...]).astype(BF16)
    kvb = _dot(ckvn, wukv_ref[...])
    krb = z[:, _C_KR:_C_U]
    krope = krb * cosb + pltpu.roll(krb, _ROLL_PARTNER, 1) * sinb
    v_off = N_HEADS * HEAD_PAD
    for hd in range(N_HEADS):
        kh = kvb[:, hd * HEAD_PAD:(hd + 1) * HEAD_PAD] + krope
        kt_ref[0, hd, 0] = kh.T.astype(BF16)
        ones_lane = V_HEAD if hd % 2 == 0 else 0
        vh = kvb[:, v_off + hd * HEAD_PAD:v_off + (hd + 1) * HEAD_PAD] + jnp.where(lane == ones_lane, 1.0, 0.0)
        v_ref[0, hd] = vh.astype(BF16)

    ua = _gelu_tanh(z[:, _C_U:_C_V])
    vn = (_rms(_gelu_tanh(z[:, _C_V:_C_END])) * gv_ref[...]).astype(BF16)
    n_tok = x.shape[0]
    bsp = bsp_ref[...]
    rows = []
    for n in range(n_tok // CHUNK):
        cols = []
        for j in range(GMLP_GROUPS // 2):
            rhs = vn[n * CHUNK:(n + 1) * CHUNK, j * LANES:(j + 1) * LANES]
            ab = _dot(wsp_ref[j], rhs)
            cols.append(jnp.where(lane < GMLP_GROUP_DIM, ab[:CHUNK], ab[CHUNK:]))
        rows.append(jnp.concatenate(cols, axis=1) + bsp)
    s = ua * jnp.concatenate(rows, axis=0)
    sn_ref[0] = (_rms(s) * ggo_ref[...]).astype(BF16)


def _premix(x, mod, P):
    nb, seq, _ = x.shape
    tm = TOKEN_TILE
    nck = seq // K_TILE
    const = lambda i, b: (0, 0)
    return pl.pallas_call(
        _premix_kernel,
        grid=(seq // tm, nb),
        in_specs=[
            pl.BlockSpec((1, tm, D_MODEL), lambda i, b: (b, i, 0)),
            pl.BlockSpec((1, SUBLANES, D_MODEL), lambda i, b: (b, 0, 0)),
            pl.BlockSpec((tm, LANES), lambda i, b: (i, 0)),
            pl.BlockSpec((tm, LANES), lambda i, b: (i, 0)),
            pl.BlockSpec((1, D_MODEL), const),
            pl.BlockSpec((D_MODEL, _C_END), const),
            pl.BlockSpec((1, Q_LORA), const),
            pl.BlockSpec((Q_LORA, N_HEADS * HEAD_PAD), const),
            pl.BlockSpec((1, KV_LORA), const),
            pl.BlockSpec((KV_LORA, 2 * N_HEADS * HEAD_PAD), const),
            pl.BlockSpec((1, GMLP_WIDTH), const),
            pl.BlockSpec((GMLP_GROUPS // 2, 2 * CHUNK, CHUNK), lambda i, b: (0, 0, 0)),
            pl.BlockSpec((CHUNK, GMLP_WIDTH), const),
            pl.BlockSpec((1, GMLP_WIDTH), const),
        ],
        out_specs=[
            pl.BlockSpec((1, N_HEADS, tm, HEAD_PAD), lambda i, b: (b, 0, i, 0)),
            pl.BlockSpec((1, N_HEADS, 1, HEAD_PAD, K_TILE), lambda i, b: (b, 0, i, 0, 0)),
            pl.BlockSpec((1, N_HEADS, tm, HEAD_PAD), lambda i, b: (b, 0, i, 0)),
            pl.BlockSpec((1, tm, GMLP_WIDTH), lambda i, b: (b, i, 0)),
        ],
        out_shape=[
            jax.ShapeDtypeStruct((nb, N_HEADS, seq, HEAD_PAD), BF16),
            jax.ShapeDtypeStruct((nb, N_HEADS, nck, HEAD_PAD, K_TILE), BF16),
            jax.ShapeDtypeStruct((nb, N_HEADS, seq, HEAD_PAD), BF16),
            jax.ShapeDtypeStruct((nb, seq, GMLP_WIDTH), BF16),
        ],
        compiler_params=_params(("arbitrary", "arbitrary")),
        name="premix",
    )(x, mod, P["cos"][:seq], P["sin"][:seq], P["g_pre1"], P["w_in"], P["g_q"], P["w_uq"], P["g_kv"], P["w_ukv"],
      P["g_v"], P["w_sp"], P["b_sp"], P["g_gmlp_out"])


def _attn_kernel(q_ref, kt_ref, v_ref, o_ref):
    n_chunks = kt_ref.shape[2]
    tk = kt_ref.shape[4]
    lane = lax.broadcasted_iota(jnp.int32, (1, LANES), 1)

    def one_head(hd, ones_lane):
        q = q_ref[0, hd]
        m = None
        acc = None
        for c in range(n_chunks):
            s = _dot(q, kt_ref[0, hd, c])
            v = v_ref[0, hd, c * tk:(c + 1) * tk, :]
            smax = jnp.max(s, axis=1, keepdims=True)
            if c == 0:
                m = smax
                acc = _dot(jnp.exp2(s - m).astype(BF16), v)
            else:
                m_new = jnp.maximum(m, smax)
                acc = acc * jnp.exp2(m - m_new) + _dot(jnp.exp2(s - m_new).astype(BF16), v)
                m = m_new
        row_sum = acc[:, ones_lane:ones_lane + 1]
        return acc * (1.0 / row_sum)

    def pair(j, carry):
        even = one_head(2 * j, V_HEAD)
        odd = one_head(2 * j + 1, 0)
        o_ref[0, j] = jnp.where(lane < V_HEAD, even, odd).astype(BF16)
        return carry

    lax.fori_loop(0, N_HEADS // 2, pair, 0)


def _attention(q, kt, v):
    nb, _, seq, _ = q.shape
    nck = kt.shape[2]
    return pl.pallas_call(
        _attn_kernel,
        grid=(nb, seq // Q_TILE),
        in_specs=[
            pl.BlockSpec((1, N_HEADS, Q_TILE, HEAD_PAD), lambda b, i: (b, 0, i, 0)),
            pl.BlockSpec((1, N_HEADS, nck, HEAD_PAD, K_TILE), lambda b, i: (b, 0, 0, 0, 0)),
            pl.BlockSpec((1, N_HEADS, seq, HEAD_PAD), lambda b, i: (b, 0, 0, 0)),
        ],
        out_specs=pl.BlockSpec((1, N_HEADS // 2, Q_TILE, LANES), lambda b, i: (b, 0, i, 0)),
        out_shape=jax.ShapeDtypeStruct((nb, N_HEADS // 2, seq, LANES), BF16),
        compiler_params=_params(("arbitrary", "arbitrary")),
        name="attn",
    )(q, kt, v)


_R_GROUP_ROW = N_EXPERTS


def _postmix_kernel(a_ref, sn_ref, x_ref, mod_ref, gao_ref, wout_ref, gpost_ref, gpre_ref, wr_ref, br_ref, tri_ref,
                    x1_ref, h2_ref, eid_ref, rank_ref, wcol_ref, cnt_ref, run_ref):
    first = jnp.logical_and(pl.program_id(0) == 0, pl.program_id(1) == 0)

    @pl.when(first)
    def _():
        run_ref[...] = jnp.zeros_like(run_ref)

    mod = mod_ref[0]
    gate1, shift2, scale2 = mod[2:3], mod[3:4], mod[4:5]
    a = jnp.concatenate([a_ref[0, j] for j in range(N_HEADS // 2)], axis=1).astype(F32)
    an = (_rms(a) * gao_ref[...]).astype(BF16)
    merged = jnp.concatenate([an, sn_ref[0]], axis=1)
    o = _dot(merged, wout_ref[...])
    x1 = x_ref[0] + gate1 * (_rms(o) * gpost_ref[...])
    x1_ref[0] = x1
    h2 = _rms(x1) * gpre_ref[...] * (1.0 + scale2) + shift2
    n_tok = h2.shape[0]
    for c in range(ROW_TILES):
        h2_ref[:, c, :] = h2[:, c * LANES:(c + 1) * LANES]

    h_hi, h_lo = _split_bf16(h2)
    wr = wr_ref[...]
    hh = _dot(h_hi, wr)
    lh = _dot(h_lo, wr[:, :LANES])
    logits = hh[:, :LANES] + hh[:, LANES:] + lh + br_ref[...]
    lt = logits.T

    neg = jnp.float32(-jnp.inf)
    row8 = lax.broadcasted_iota(jnp.int32, (SUBLANES, n_tok), 0).astype(F32)
    lg = jnp.where(row8 < N_EXPERT_GROUPS, lt[_R_GROUP_ROW:_R_GROUP_ROW + SUBLANES], neg)
    gmax = jnp.max(lg, axis=0, keepdims=True)
    gi = jnp.min(jnp.where(lg == gmax, row8, float(SUBLANES)), axis=0, keepdims=True)
    pg_sel = 1.0 / jnp.sum(jnp.exp(lg - gmax), axis=0, keepdims=True)

    le = jnp.zeros((EXPERTS_PER_GROUP, n_tok), F32)
    for g in range(N_EXPERT_GROUPS):
        le = jnp.where(gi == float(g), lt[g * EXPERTS_PER_GROUP:(g + 1) * EXPERTS_PER_GROUP], le)
    v1 = jnp.max(le, axis=0, keepdims=True)
    i1 = jnp.min(jnp.where(le == v1, row8, float(SUBLANES)), axis=0, keepdims=True)
    le2 = jnp.where(row8 == i1, neg, le)
    v2 = jnp.max(le2, axis=0, keepdims=True)
    i2 = jnp.min(jnp.where(le2 == v2, row8, float(SUBLANES)), axis=0, keepdims=True)
    r = jnp.exp(v2 - v1)
    w1 = pg_sel / (1.0 + r)
    w2 = w1 * r
    e1 = gi * float(EXPERTS_PER_GROUP) + i1
    e2 = gi * float(EXPERTS_PER_GROUP) + i2
    eid_ref[0] = jnp.concatenate([e1, e2], axis=0).astype(jnp.int32)

    row32 = lax.broadcasted_iota(jnp.int32, (N_EXPERTS, n_tok), 0).astype(F32)
    hit1 = row32 == e1
    hit2 = row32 == e2
    onehot = jnp.where(jnp.logical_or(hit1, hit2), 1.0, 0.0)
    run = run_ref[...][:, 0:1]
    ranks1, ranks2 = [], []
    for c in range(n_tok // RANK_CHUNK):
        sl = slice(c * RANK_CHUNK, (c + 1) * RANK_CHUNK)
        oh = onehot[:, sl]
        before = _dot(oh.astype(BF16), tri_ref[...]) + run
        ranks1.append(jnp.sum(jnp.where(hit1[:, sl], before, 0.0), axis=0, keepdims=True))
        ranks2.append(jnp.sum(jnp.where(hit2[:, sl], before, 0.0), axis=0, keepdims=True))
        run = run + jnp.sum(oh, axis=1, keepdims=True)
    rank_ref[0] = jnp.concatenate(
        [jnp.concatenate(ranks1, axis=1), jnp.concatenate(ranks2, axis=1)], axis=0).astype(jnp.int32)
    run_b = jnp.broadcast_to(run, run_ref.shape)
    run_ref[...] = run_b
    cnt_ref[...] = run_b

    row128 = lax.broadcasted_iota(jnp.int32, (LANES, n_tok), 0)
    wt = jnp.where(row128 == 0, w1, jnp.where(row128 == 1, w2, 0.0))
    wcol_ref[0] = wt.T


def _postmix(a, sn, x, mod, P):
    nb, seq, _ = x.shape
    tm = TOKEN_TILE
    const = lambda b, i: (0, 0)
    return pl.pallas_call(
        _postmix_kernel,
        grid=(nb, seq // tm),
        in_specs=[
            pl.BlockSpec((1, N_HEADS // 2, tm, LANES), lambda b, i: (b, 0, i, 0)),
            pl.BlockSpec((1, tm, GMLP_WIDTH), lambda b, i: (b, i, 0)),
            pl.BlockSpec((1, tm, D_MODEL), lambda b, i: (b, i, 0)),
            pl.BlockSpec((1, SUBLANES, D_MODEL), lambda b, i: (b, 0, 0)),
            pl.BlockSpec((1, N_HEADS * V_HEAD), const),
            pl.BlockSpec((D_MODEL, D_MODEL), const),
            pl.BlockSpec((1, D_MODEL), const),
            pl.BlockSpec((1, D_MODEL), const),
            pl.BlockSpec((D_MODEL, 2 * LANES), const),
            pl.BlockSpec((1, LANES), const),
            pl.BlockSpec((RANK_CHUNK, RANK_CHUNK), const),
        ],
        out_specs=[
            pl.BlockSpec((1, tm, D_MODEL), lambda b, i: (b, i, 0)),
            pl.BlockSpec((tm, ROW_TILES, LANES), lambda b, i: (b * (seq // tm) + i, 0, 0)),
            pl.BlockSpec((1, 2, tm), lambda b, i: (b, 0, i)),
            pl.BlockSpec((1, 2, tm), lambda b, i: (b, 0, i)),
            pl.BlockSpec((1, tm, LANES), lambda b, i: (b, i, 0)),
            pl.BlockSpec((N_EXPERTS, LANES), const),
        ],
        out_shape=[
            jax.ShapeDtypeStruct((nb, seq, D_MODEL), F32),
            jax.ShapeDtypeStruct((nb * seq, ROW_TILES, LANES), F32),
            jax.ShapeDtypeStruct((nb, 2, seq), jnp.int32),
            jax.ShapeDtypeStruct((nb, 2, seq), jnp.int32),
            jax.ShapeDtypeStruct((nb, seq, LANES), F32),
            jax.ShapeDtypeStruct((N_EXPERTS, LANES), F32),
        ],
        scratch_shapes=[pltpu.VMEM((N_EXPERTS, LANES), F32)],
        compiler_params=_params(("arbitrary", "arbitrary")),
        name="postmix",
    )(a, sn, x, mod, P["g_attn_out"], P["w_out"], P["g_post1"], P["g_pre2"], P["w_router"], P["b_router"], P["tri"])


def _move_kernel(sidx_ref, didx_ref, src_ref, dst_in_ref, dst_ref, sem):
    del dst_in_ref
    n = sidx_ref.shape[2]

    def copy(t):
        return pltpu.make_async_copy(src_ref.at[sidx_ref[0, 0, t]], dst_ref.at[didx_ref[0, 0, t]], sem)

    def body(t, carry):
        copy(t).start()

        @pl.when(t >= MOVE_WINDOW)
        def _():
            copy(t - MOVE_WINDOW).wait()

        return carry

    lax.fori_loop(0, n, body, 0)

    def drain(t, carry):
        copy(t).wait()
        return carry

    lax.fori_loop(n - MOVE_WINDOW, n, drain, 0)


def _move_rows(src, sidx, didx, dst_init):
    n = sidx.shape[0]
    nblk = n // MOVE_TILE
    idx_spec = pl.BlockSpec((1, 1, MOVE_TILE), lambda i: (i, 0, 0), memory_space=pltpu.SMEM)
    return pl.pallas_call(
        _move_kernel,
        grid=(nblk,),
        in_specs=[idx_spec, idx_spec, pl.BlockSpec(memory_space=pl.ANY), pl.BlockSpec(memory_space=pl.ANY)],
        out_specs=pl.BlockSpec(memory_space=pl.ANY),
        out_shape=jax.ShapeDtypeStruct(dst_init.shape, dst_init.dtype),
        scratch_shapes=[pltpu.SemaphoreType.DMA],
        input_output_aliases={3: 0},
        compiler_params=pltpu.CompilerParams(dimension_semantics=("arbitrary",), has_side_effects=True),
        name="move_rows",
    )(sidx.reshape(nblk, 1, MOVE_TILE), didx.reshape(nblk, 1, MOVE_TILE), src, dst_init)


def _expert_kernel(te_ref, nv_ref, xs_ref, wgu_ref, wd_ref, ys_ref):
    del te_ref

    @pl.when(pl.program_id(0) < nv_ref[0])
    def _():
        x = jnp.concatenate([xs_ref[:, c, :] for c in range(ROW_TILES)], axis=1).astype(BF16)
        gu = _dot(x, wgu_ref[0])
        g, u = gu[:, :EXPERT_FF], gu[:, EXPERT_FF:]
        act = (g * jax.nn.sigmoid(g) * u).astype(BF16)
        y = _dot(act, wd_ref[0])
        for c in range(ROW_TILES):
            ys_ref[:, c, :] = y[:, c * LANES:(c + 1) * LANES]

    @pl.when(pl.program_id(0) >= nv_ref[0])
    def _():
        ys_ref[...] = jnp.zeros_like(ys_ref)


def _experts(xs, tile_expert, n_valid, P):
    n_tiles = xs.shape[0] // EXPERT_TILE

    def row_map(i, te, nv):
        return (jnp.minimum(i, nv[0] - 1), 0, 0)

    def out_map(i, te, nv):
        return (i, 0, 0)

    def w_map(i, te, nv):
        return (te[jnp.minimum(i, nv[0] - 1)], 0, 0)

    return pl.pallas_call(
        _expert_kernel,
        grid_spec=pltpu.PrefetchScalarGridSpec(
            num_scalar_prefetch=2,
            grid=(n_tiles,),
            in_specs=[
                pl.BlockSpec((EXPERT_TILE, ROW_TILES, LANES), row_map),
                pl.BlockSpec((1, D_MODEL, 2 * EXPERT_FF), w_map),
                pl.BlockSpec((1, EXPERT_FF, D_MODEL), w_map),
            ],
            out_specs=pl.BlockSpec((EXPERT_TILE, ROW_TILES, LANES), out_map),
        ),
        out_shape=jax.ShapeDtypeStruct(xs.shape, F32),
        compiler_params=_params(("arbitrary",)),
        name="experts",
    )(tile_expert, n_valid, xs, P["w_gu"], P["w_down"])


def _final_kernel(y0_ref, y1_ref, wcol_ref, x1_ref, mod_ref, gpost_ref, o_ref):
    w = wcol_ref[0]
    w0, w1 = w[:, 0:1], w[:, 1:2]
    m = jnp.concatenate([w0 * y0_ref[:, c, :] + w1 * y1_ref[:, c, :] for c in range(ROW_TILES)], axis=1)
    gate2 = mod_ref[0][5:6]
    o_ref[0] = x1_ref[0] + gate2 * (_rms(m) * gpost_ref[...])


def _final(yg, wcol, x1, mod, P):
    nb, seq, _ = x1.shape
    tm = TOKEN_TILE
    nt = seq // tm
    n_tok_tiles = nb * nt
    return pl.pallas_call(
        _final_kernel,
        grid=(nb, nt),
        in_specs=[
            pl.BlockSpec((tm, ROW_TILES, LANES), lambda b, i: (b * nt + i, 0, 0)),
            pl.BlockSpec((tm, ROW_TILES, LANES), lambda b, i: (n_tok_tiles + b * nt + i, 0, 0)),
            pl.BlockSpec((1, tm, LANES), lambda b, i: (b, i, 0)),
            pl.BlockSpec((1, tm, D_MODEL), lambda b, i: (b, i, 0)),
            pl.BlockSpec((1, SUBLANES, D_MODEL), lambda b, i: (b, 0, 0)),
            pl.BlockSpec((1, D_MODEL), lambda b, i: (0, 0)),
        ],
        out_specs=pl.BlockSpec((1, tm, D_MODEL), lambda b, i: (b, i, 0)),
        out_shape=jax.ShapeDtypeStruct((nb, seq, D_MODEL), F32),
        compiler_params=_params(("arbitrary", "arbitrary")),
        name="final",
    )(yg, yg, wcol, x1, mod, P["g_post2"])


def _prepare(w):
    f = lambda a: a.astype(F32)
    P = {}
    for k in ("g_pre1", "g_post1", "g_pre2", "g_post2", "g_q", "g_kv", "g_attn_out", "g_gmlp_out"):
        P[k] = f(w[k]).reshape(1, -1)
    P["g_v"] = f(w["g_v_gmlp"]).reshape(1, -1)

    w_in = f(w["w_in"])
    o0, o1, o2, o3 = Q_LORA, Q_LORA + KV_LORA, Q_LORA + KV_LORA + QK_ROPE, Q_LORA + KV_LORA + QK_ROPE + GMLP_WIDTH
    w_kr = w_in[:, o1:o2]
    kr_partner = jnp.concatenate([-w_kr[:, ROPE_HALF:], w_kr[:, :ROPE_HALF]], axis=1)
    rope_blk = jnp.concatenate([jnp.zeros((D_MODEL, QK_NOPE), F32), w_kr, kr_partner], axis=1)
    P["w_in"] = jnp.concatenate([w_in[:, :o1], rope_blk, w_in[:, o2:o3], w_in[:, o3:]], axis=1).astype(BF16)

    w_uq = f(w["w_uq"]).reshape(Q_LORA, N_HEADS, QK_NOPE + QK_ROPE)
    q_rope = w_uq[:, :, QK_NOPE:]
    q_partner = jnp.concatenate([-q_rope[:, :, ROPE_HALF:], q_rope[:, :, :ROPE_HALF]], axis=2)
    P["w_uq"] = jnp.concatenate([w_uq, q_partner], axis=2).reshape(Q_LORA, N_HEADS * HEAD_PAD).astype(BF16)

    w_ukv = f(w["w_ukv"]).reshape(KV_LORA, N_HEADS, QK_NOPE + V_HEAD)
    zeros = jnp.zeros((KV_LORA, N_HEADS, HEAD_PAD - QK_NOPE), F32)
    w_k = jnp.concatenate([w_ukv[:, :, :QK_NOPE], zeros], axis=2)
    w_v = w_ukv[:, :, QK_NOPE:]
    even = (jnp.arange(N_HEADS) % 2 == 0)[None, :, None]
    zv = jnp.zeros_like(w_v)
    w_v = jnp.concatenate([jnp.where(even, w_v, zv), jnp.where(even, zv, w_v)], axis=2)
    P["w_ukv"] = jnp.concatenate([w_k.reshape(KV_LORA, -1), w_v.reshape(KV_LORA, -1)], axis=1).astype(BF16)

    P["w_sp"] = f(w["w_spatial"]).reshape(GMLP_GROUPS // 2, 2 * CHUNK, CHUNK).astype(BF16)
    P["b_sp"] = jnp.repeat(f(w["b_spatial"]).T, GMLP_GROUP_DIM, axis=1)

    P["w_out"] = f(w["w_out"]).astype(BF16)

    pad = jnp.zeros((D_MODEL, LANES - N_EXPERTS - N_EXPERT_GROUPS), F32)
    wr = jnp.concatenate([f(w["w_router_expert"]), f(w["w_router_group"]), pad], axis=1)
    wr_hi = wr.astype(BF16)
    wr_lo = (wr - wr_hi.astype(F32)).astype(BF16)
    P["w_router"] = jnp.concatenate([wr_hi, wr_lo], axis=1)
    P["b_router"] = jnp.concatenate(
        [f(w["b_router_expert"]), f(w["b_router_group"]), jnp.zeros((LANES - N_EXPERTS - N_EXPERT_GROUPS,), F32)]
    ).reshape(1, LANES)
    P["tri"] = jnp.triu(jnp.ones((RANK_CHUNK, RANK_CHUNK), F32), k=1).astype(BF16)

    P["w_gu"] = jnp.concatenate([f(w["w_gate"]), f(w["w_up"])], axis=2).astype(BF16)
    P["w_down"] = f(w["w_down"]).astype(BF16)
    return P


def _rope_tables(seq):
    inv = ROPE_THETA ** (-jnp.arange(ROPE_HALF, dtype=F32) / ROPE_HALF)
    ang = jnp.arange(seq, dtype=F32)[:, None] * inv[None, :]
    z_lo = jnp.zeros((seq, _ROPE_LO), F32)
    z_hi = jnp.zeros((seq, LANES - _ROPE_LO - QK_ROPE), F32)
    cos = jnp.concatenate([z_lo, jnp.cos(ang), jnp.cos(ang), z_hi], axis=1)
    sin = jnp.concatenate([z_lo, jnp.sin(ang), jnp.sin(ang), z_hi], axis=1)
    return cos, sin


def _layer(x, mod, P):
    nb, seq, _ = x.shape
    n_tok = nb * seq
    q, kt, v, sn = _premix(x, mod, P)
    a = _attention(q, kt, v)
    x1, h2rows, eid, rank, wcol, counts = _postmix(a, sn, x, mod, P)

    cnt = counts[:, 0].astype(jnp.int32)
    padded = ((cnt + EXPERT_TILE - 1) // EXPERT_TILE) * EXPERT_TILE
    ends = jnp.cumsum(padded)
    starts = ends - padded
    eflat = jnp.transpose(eid, (1, 0, 2)).reshape(2, n_tok)
    rflat = jnp.transpose(rank, (1, 0, 2)).reshape(2, n_tok)
    onehot = eflat[:, :, None] == jnp.arange(N_EXPERTS, dtype=jnp.int32)[None, None, :]
    pos = rflat + jnp.sum(jnp.where(onehot, starts[None, None, :], 0), axis=2)
    n_rows = 2 * n_tok + N_EXPERTS * EXPERT_TILE
    n_tiles = n_rows // EXPERT_TILE
    tile_start = jnp.arange(n_tiles, dtype=jnp.int32) * EXPERT_TILE
    tile_expert = jnp.minimum(
        jnp.sum((tile_start[:, None] >= ends[None, :]).astype(jnp.int32), axis=1), N_EXPERTS - 1).astype(jnp.int32)
    n_valid = (ends[-1:] // EXPERT_TILE).astype(jnp.int32)

    tok = jnp.arange(n_tok, dtype=jnp.int32)
    lin = jnp.arange(2 * n_tok, dtype=jnp.int32)
    posf = pos.reshape(2 * n_tok)
    xs = _move_rows(h2rows, jnp.concatenate([tok, tok]), posf, jnp.zeros((n_rows, ROW_TILES, LANES), F32))
    ys = _experts(xs, tile_expert, n_valid, P)
    yg = _move_rows(ys, posf, lin, jnp.zeros((2 * n_tok, ROW_TILES, LANES), F32))
    return _final(yg, wcol, x1, mod, P)


def kernel(x_prompt, x_sample, c_prompt, c_sample, w_ada, b_ada, g_pre1, g_post1, g_pre2, g_post2, w_in, g_q, w_uq,
           g_kv, w_ukv, g_v_gmlp, w_spatial, b_spatial, g_attn_out, g_gmlp_out, w_out, w_router_group,
           b_router_group, w_router_expert, b_router_expert, w_gate, w_up, w_down):
    P = _prepare(dict(
        g_pre1=g_pre1, g_post1=g_post1, g_pre2=g_pre2, g_post2=g_post2, w_in=w_in, g_q=g_q, w_uq=w_uq, g_kv=g_kv,
        w_ukv=w_ukv, g_v_gmlp=g_v_gmlp, w_spatial=w_spatial, b_spatial=b_spatial, g_attn_out=g_attn_out,
        g_gmlp_out=g_gmlp_out, w_out=w_out, w_router_group=w_router_group, b_router_group=b_router_group,
        w_router_expert=w_router_expert, b_router_expert=b_router_expert, w_gate=w_gate, w_up=w_up, w_down=w_down))
    P["cos"], P["sin"] = _rope_tables(max(x_prompt.shape[1], x_sample.shape[1]))

    nbp = c_prompt.shape[0]
    c_all = jnp.concatenate([c_prompt, c_sample], axis=0).astype(F32)
    mod = _ada(c_all, w_ada.astype(F32), b_ada.astype(F32))
    mod = mod.reshape(c_all.shape[0], 6, D_MODEL)
    mod = jnp.concatenate([mod, jnp.zeros((c_all.shape[0], SUBLANES - 6, D_MODEL), F32)], axis=1)

    y_prompt = _layer(x_prompt, mod[:nbp], P)
    y_sample = _layer(x_sample, mod[nbp:], P)
    return (y_prompt, y_sample)
```

```python
import functools
import math

import jax
import jax.numpy as jnp
from jax import lax
from jax.experimental import pallas as pl
from jax.experimental.pallas import tpu as pltpu

F32 = jnp.float32
BF16 = jnp.bfloat16

D_MODEL = 1024
N_HEADS = 8
QK_NOPE = 64
QK_ROPE = 32
ROPE_HALF = QK_ROPE // 2
V_HEAD = 64
Q_LORA = 256
KV_LORA = 128
GMLP_WIDTH = 512
GMLP_GROUPS = 8
GMLP_GROUP_DIM = 64
CHUNK = 128
N_EXPERTS = 32
N_EXPERT_GROUPS = 4
EXPERTS_PER_GROUP = 8
EXPERT_FF = 256
ROPE_THETA = 10000.0
EPS = 1e-6

LANES = 128
SUBLANES = 8
HEAD_PAD = LANES

TOKEN_TILE = 512
Q_TILE = 512
K_TILE = 512
EXPERT_TILE = 256
MOVE_TILE = 2048
MOVE_UNROLL = 8
RANK_CHUNK = 256
VMEM_LIMIT = 56 * 1024 * 1024

ROW_TILES = D_MODEL // LANES

_SQRT_2_OVER_PI = math.sqrt(2.0 / math.pi)


def _rms(x):
    return x * lax.rsqrt(jnp.mean(x * x, axis=-1, keepdims=True) + EPS)


def _gelu_tanh(x):
    return 0.5 * x * (1.0 + jnp.tanh(_SQRT_2_OVER_PI * (x + 0.044715 * (x * x * x))))


def _split_bf16(x):
    hi = x.astype(BF16)
    lo = (x - hi.astype(F32)).astype(BF16)
    return hi, lo


def _dot(a, b):
    return jnp.dot(a, b, preferred_element_type=F32)


def _params(sem, vmem=VMEM_LIMIT):
    return pltpu.CompilerParams(dimension_semantics=sem, vmem_limit_bytes=vmem)


def _ada_kernel(c_ref, w_ref, b_ref, o_ref):
    c = c_ref[...]
    a = c * jax.nn.sigmoid(c)
    a_hi, a_lo = _split_bf16(a)
    w_hi, w_lo = _split_bf16(w_ref[...])
    o_ref[...] = _dot(a_hi, w_hi) + _dot(a_hi, w_lo) + _dot(a_lo, w_hi) + b_ref[...]


def _ada(c, w_ada, b_ada):
    nb = c.shape[0]
    n_out = w_ada.shape[1]
    blk = D_MODEL
    return pl.pallas_call(
        _ada_kernel,
        grid=(n_out // blk,),
        in_specs=[
            pl.BlockSpec((nb, D_MODEL), lambda j: (0, 0)),
            pl.BlockSpec((D_MODEL, blk), lambda j: (0, j)),
            pl.BlockSpec((1, blk), lambda j: (0, j)),
        ],
        out_specs=pl.BlockSpec((nb, blk), lambda j: (0, j)),
        out_shape=jax.ShapeDtypeStruct((nb, n_out), F32),
        compiler_params=_params(("arbitrary",)),
        name="ada",
    )(c, w_ada, b_ada.reshape(1, n_out))


_C_CQ = 0
_C_CKV = _C_CQ + Q_LORA
_C_KR = _C_CKV + KV_LORA
_C_U = _C_KR + LANES
_C_V = _C_U + GMLP_WIDTH
_C_END = _C_V + GMLP_WIDTH
_ROPE_LO = QK_NOPE
_ROLL_PARTNER = LANES - QK_ROPE


def _premix_kernel(x_ref, mod_ref, cos_ref, sin_ref, gpre_ref, win_ref, gq_ref, wuq_ref, gkv_ref, wukv_ref,
                   gv_ref, wsp_ref, bsp_ref, ggo_ref, q_ref, kt_ref, v_ref, sn_ref):
    x = x_ref[0]
    mod = mod_ref[0]
    shift1, scale1 = mod[0:1], mod[1:2]
    h = _rms(x) * gpre_ref[...] * (1.0 + scale1) + shift1
    z = _dot(h.astype(BF16), win_ref[...])

    cosb = cos_ref[...]
    sinb = sin_ref[...]
    lane = lax.broadcasted_iota(jnp.int32, (1, LANES), 1)
    nope_mask = jnp.where(lane < QK_NOPE, 1.0, 0.0).astype(F32)

    qscale = (QK_NOPE + QK_ROPE) ** -0.5 * math.log2(math.e)
    cq_tab = (nope_mask + cosb) * qscale
    sq_tab = sinb * qscale
    cqn = (_rms(z[:, _C_CQ:_C_CKV]) * gq_ref[...]).astype(BF16)
    qb = _dot(cqn, wuq_ref[...])
    for hd in range(N_HEADS):
        blk = qb[:, hd * HEAD_PAD:(hd + 1) * HEAD_PAD]
        qh = blk * cq_tab + pltpu.roll(blk, _ROLL_PARTNER, 1) * sq_tab
        q_ref[0, hd] = qh.astype(BF16)

    ckvn = (_rms(z[:, _C_CKV:_C_KR]) * gkv_ref[...]).astype(BF16)
    kvb = _dot(ckvn, wukv_ref[...])
    krb = z[:, _C_KR:_C_U]
    krope = krb * cosb + pltpu.roll(krb, _ROLL_PARTNER, 1) * sinb
    v_off = N_HEADS * HEAD_PAD
    for hd in range(N_HEADS):
        kh = kvb[:, hd * HEAD_PAD:(hd + 1) * HEAD_PAD] + krope
        kt_ref[0, hd, 0] = kh.T.astype(BF16)
        ones_lane = V_HEAD if hd % 2 == 0 else 0
        vh = kvb[:, v_off + hd * HEAD_PAD:v_off + (hd + 1) * HEAD_PAD] + jnp.where(lane == ones_lane, 1.0, 0.0)
        v_ref[0, hd] = vh.astype(BF16)

    ua = _gelu_tanh(z[:, _C_U:_C_V])
    vn = (_rms(_gelu_tanh(z[:, _C_V:_C_END])) * gv_ref[...]).astype(BF16)
    n_tok = x.shape[0]
    bsp = bsp_ref[...]
    rows = []
    for n in range(n_tok // CHUNK):
        cols = []
        for j in range(GMLP_GROUPS // 2):
            rhs = vn[n * CHUNK:(n + 1) * CHUNK, j * LANES:(j + 1) * LANES]
            ab = _dot(wsp_ref[j], rhs)
            cols.append(jnp.where(lane < GMLP_GROUP_DIM, ab[:CHUNK], ab[CHUNK:]))
        rows.append(jnp.concatenate(cols, axis=1) + bsp)
    s = ua * jnp.concatenate(rows, axis=0)
    sn_ref[0] = (_rms(s) * ggo_ref[...]).astype(BF16)


def _premix(x, mod, P):
    nb, seq, _ = x.shape
    tm = TOKEN_TILE
    nck = seq // K_TILE
    const = lambda i, b: (0, 0)
    return pl.pallas_call(
        _premix_kernel,
        grid=(seq // tm, nb),
        in_specs=[
            pl.BlockSpec((1, tm, D_MODEL), lambda i, b: (b, i, 0)),
            pl.BlockSpec((1, SUBLANES, D_MODEL), lambda i, b: (b, 0, 0)),
            pl.BlockSpec((tm, LANES), lambda i, b: (i, 0)),
            pl.BlockSpec((tm, LANES), lambda i, b: (i, 0)),
            pl.BlockSpec((1, D_MODEL), const),
            pl.BlockSpec((D_MODEL, _C_END), const),
            pl.BlockSpec((1, Q_LORA), const),
            pl.BlockSpec((Q_LORA, N_HEADS * HEAD_PAD), const),
            pl.BlockSpec((1, KV_LORA), const),
            pl.BlockSpec((KV_LORA, 2 * N_HEADS * HEAD_PAD), const),
            pl.BlockSpec((1, GMLP_WIDTH), const),
            pl.BlockSpec((GMLP_GROUPS // 2, 2 * CHUNK, CHUNK), lambda i, b: (0, 0, 0)),
            pl.BlockSpec((CHUNK, GMLP_WIDTH), const),
            pl.BlockSpec((1, GMLP_WIDTH), const),
        ],
        out_specs=[
            pl.BlockSpec((1, N_HEADS, tm, HEAD_PAD), lambda i, b: (b, 0, i, 0)),
            pl.BlockSpec((1, N_HEADS, 1, HEAD_PAD, K_TILE), lambda i, b: (b, 0, i, 0, 0)),
            pl.BlockSpec((1, N_HEADS, tm, HEAD_PAD), lambda i, b: (b, 0, i, 0)),
            pl.BlockSpec((1, tm, GMLP_WIDTH), lambda i, b: (b, i, 0)),
        ],
        out_shape=[
            jax.ShapeDtypeStruct((nb, N_HEADS, seq, HEAD_PAD), BF16),
            jax.ShapeDtypeStruct((nb, N_HEADS, nck, HEAD_PAD, K_TILE), BF16),
            jax.ShapeDtypeStruct((nb, N_HEADS, seq, HEAD_PAD), BF16),
            jax.ShapeDtypeStruct((nb, seq, GMLP_WIDTH), BF16),
        ],
        compiler_params=_params(("arbitrary", "arbitrary")),
        name="premix",
    )(x, mod, P["cos"][:seq], P["sin"][:seq], P["g_pre1"], P["w_in"], P["g_q"], P["w_uq"], P["g_kv"], P["w_ukv"],
      P["g_v"], P["w_sp"], P["b_sp"], P["g_gmlp_out"])


def _attn_kernel(q_ref, kt_ref, v_ref, o_ref):
    n_chunks = kt_ref.shape[2]
    tk = kt_ref.shape[4]
    lane = lax.broadcasted_iota(jnp.int32, (1, LANES), 1)

    def one_head(hd, ones_lane):
        q = q_ref[0, hd]
        m = None
        acc = None
        for c in range(n_chunks):
            s = _dot(q, kt_ref[0, hd, c])
            v = v_ref[0, hd, c * tk:(c + 1) * tk, :]
            smax = jnp.max(s, axis=1, keepdims=True)
            if c == 0:
                m = smax
                acc = _dot(jnp.exp2(s - m).astype(BF16), v)
            else:
                m_new = jnp.maximum(m, smax)
                acc = acc * jnp.exp2(m - m_new) + _dot(jnp.exp2(s - m_new).astype(BF16), v)
                m = m_new
        row_sum = acc[:, ones_lane:ones_lane + 1]
        return acc * (1.0 / row_sum)

    def pair(j, carry):
        even = one_head(2 * j, V_HEAD)
        odd = one_head(2 * j + 1, 0)
        o_ref[0, j] = jnp.where(lane < V_HEAD, even, odd).astype(BF16)
        return carry

    lax.fori_loop(0, N_HEADS // 2, pair, 0)


def _attention(q, kt, v):
    nb, _, seq, _ = q.shape
    nck = kt.shape[2]
    return pl.pallas_call(
        _attn_kernel,
        grid=(nb, seq // Q_TILE),
        in_specs=[
            pl.BlockSpec((1, N_HEADS, Q_TILE, HEAD_PAD), lambda b, i: (b, 0, i, 0)),
            pl.BlockSpec((1, N_HEADS, nck, HEAD_PAD, K_TILE), lambda b, i: (b, 0, 0, 0, 0)),
            pl.BlockSpec((1, N_HEADS, seq, HEAD_PAD), lambda b, i: (b, 0, 0, 0)),
        ],
        out_specs=pl.BlockSpec((1, N_HEADS // 2, Q_TILE, LANES), lambda b, i: (b, 0, i, 0)),
        out_shape=jax.ShapeDtypeStruct((nb, N_HEADS // 2, seq, LANES), BF16),
        compiler_params=_params(("arbitrary", "arbitrary")),
        name="attn",
    )(q, kt, v)


_R_GROUP_ROW = N_EXPERTS


def _postmix_kernel(a_ref, sn_ref, x_ref, mod_ref, gao_ref, wout_ref, gpost_ref, gpre_ref, wr_ref, br_ref, tri_ref,
                    x1_ref, h2_ref, eid_ref, rank_ref, wcol_ref, cnt_ref, run_ref):
    first = jnp.logical_and(pl.program_id(0) == 0, pl.program_id(1) == 0)

    @pl.when(first)
    def _():
        run_ref[...] = jnp.zeros_like(run_ref)

    mod = mod_ref[0]
    gate1, shift2, scale2 = mod[2:3], mod[3:4], mod[4:5]
    a = jnp.concatenate([a_ref[0, j] for j in range(N_HEADS // 2)], axis=1).astype(F32)
    an = (_rms(a) * gao_ref[...]).astype(BF16)
    merged = jnp.concatenate([an, sn_ref[0]], axis=1)
    o = _dot(merged, wout_ref[...])
    x1 = x_ref[0] + gate1 * (_rms(o) * gpost_ref[...])
    x1_ref[0] = x1
    h2 = _rms(x1) * gpre_ref[...] * (1.0 + scale2) + shift2
    n_tok = h2.shape[0]
    for c in range(ROW_TILES):
        h2_ref[:, c, :] = h2[:, c * LANES:(c + 1) * LANES]

    h_hi, h_lo = _split_bf16(h2)
    wr = wr_ref[...]
    hh = _dot(h_hi, wr)
    lh = _dot(h_lo, wr[:, :LANES])
    logits = hh[:, :LANES] + hh[:, LANES:] + lh + br_ref[...]
    lt = logits.T

    neg = jnp.float32(-jnp.inf)
    row8 = lax.broadcasted_iota(jnp.int32, (SUBLANES, n_tok), 0).astype(F32)
    lg = jnp.where(row8 < N_EXPERT_GROUPS, lt[_R_GROUP_ROW:_R_GROUP_ROW + SUBLANES], neg)
    gmax = jnp.max(lg, axis=0, keepdims=True)
    gi = jnp.min(jnp.where(lg == gmax, row8, float(SUBLANES)), axis=0, keepdims=True)
    pg_sel = 1.0 / jnp.sum(jnp.exp(lg - gmax), axis=0, keepdims=True)

    le = jnp.zeros((EXPERTS_PER_GROUP, n_tok), F32)
    for g in range(N_EXPERT_GROUPS):
        le = jnp.where(gi == float(g), lt[g * EXPERTS_PER_GROUP:(g + 1) * EXPERTS_PER_GROUP], le)
    v1 = jnp.max(le, axis=0, keepdims=True)
    i1 = jnp.min(jnp.where(le == v1, row8, float(SUBLANES)), axis=0, keepdims=True)
    le2 = jnp.where(row8 == i1, neg, le)
    v2 = jnp.max(le2, axis=0, keepdims=True)
    i2 = jnp.min(jnp.where(le2 == v2, row8, float(SUBLANES)), axis=0, keepdims=True)
    r = jnp.exp(v2 - v1)
    w1 = pg_sel / (1.0 + r)
    w2 = w1 * r
    e1 = gi * float(EXPERTS_PER_GROUP) + i1
    e2 = gi * float(EXPERTS_PER_GROUP) + i2
    eid_ref[0] = jnp.concatenate([e1, e2], axis=0).astype(jnp.int32)

    row32 = lax.broadcasted_iota(jnp.int32, (N_EXPERTS, n_tok), 0).astype(F32)
    hit1 = row32 == e1
    hit2 = row32 == e2
    onehot = jnp.where(jnp.logical_or(hit1, hit2), 1.0, 0.0)
    run = run_ref[...][:, 0:1]
    ranks1, ranks2 = [], []
    for c in range(n_tok // RANK_CHUNK):
        sl = slice(c * RANK_CHUNK, (c + 1) * RANK_CHUNK)
        oh = onehot[:, sl]
        before = _dot(oh.astype(BF16), tri_ref[...]) + run
        ranks1.append(jnp.sum(jnp.where(hit1[:, sl], before, 0.0), axis=0, keepdims=True))
        ranks2.append(jnp.sum(jnp.where(hit2[:, sl], before, 0.0), axis=0, keepdims=True))
        run = run + jnp.sum(oh, axis=1, keepdims=True)
    rank_ref[0] = jnp.concatenate(
        [jnp.concatenate(ranks1, axis=1), jnp.concatenate(ranks2, axis=1)], axis=0).astype(jnp.int32)
    run_b = jnp.broadcast_to(run, run_ref.shape)
    run_ref[...] = run_b
    cnt_ref[...] = run_b

    row128 = lax.broadcasted_iota(jnp.int32, (LANES, n_tok), 0)
    wt = jnp.where(row128 == 0, w1, jnp.where(row128 == 1, w2, 0.0))
    wcol_ref[0] = wt.T


def _postmix(a, sn, x, mod, P):
    nb, seq, _ = x.shape
    tm = TOKEN_TILE
    const = lambda b, i: (0, 0)
    return pl.pallas_call(
        _postmix_kernel,
        grid=(nb, seq // tm),
        in_specs=[
            pl.BlockSpec((1, N_HEADS // 2, tm, LANES), lambda b, i: (b, 0, i, 0)),
            pl.BlockSpec((1, tm, GMLP_WIDTH), lambda b, i: (b, i, 0)),
            pl.BlockSpec((1, tm, D_MODEL), lambda b, i: (b, i, 0)),
            pl.BlockSpec((1, SUBLANES, D_MODEL), lambda b, i: (b, 0, 0)),
            pl.BlockSpec((1, N_HEADS * V_HEAD), const),
            pl.BlockSpec((D_MODEL, D_MODEL), const),
            pl.BlockSpec((1, D_MODEL), const),
            pl.BlockSpec((1, D_MODEL), const),
            pl.BlockSpec((D_MODEL, 2 * LANES), const),
            pl.BlockSpec((1, LANES), const),
            pl.BlockSpec((RANK_CHUNK, RANK_CHUNK), const),
        ],
        out_specs=[
            pl.BlockSpec((1, tm, D_MODEL), lambda b, i: (b, i, 0)),
            pl.BlockSpec((tm, ROW_TILES, LANES), lambda b, i: (b * (seq // tm) + i, 0, 0)),
            pl.BlockSpec((1, 2, tm), lambda b, i: (b, 0, i)),
            pl.BlockSpec((1, 2, tm), lambda b, i: (b, 0, i)),
            pl.BlockSpec((1, tm, LANES), lambda b, i: (b, i, 0)),
            pl.BlockSpec((N_EXPERTS, LANES), const),
        ],
        out_shape=[
            jax.ShapeDtypeStruct((nb, seq, D_MODEL), F32),
            jax.ShapeDtypeStruct((nb * seq, ROW_TILES, LANES), F32),
            jax.ShapeDtypeStruct((nb, 2, seq), jnp.int32),
            jax.ShapeDtypeStruct((nb, 2, seq), jnp.int32),
            jax.ShapeDtypeStruct((nb, seq, LANES), F32),
            jax.ShapeDtypeStruct((N_EXPERTS, LANES), F32),
        ],
        scratch_shapes=[pltpu.VMEM((N_EXPERTS, LANES), F32)],
        compiler_params=_params(("arbitrary", "arbitrary")),
        name="postmix",
    )(a, sn, x, mod, P["g_attn_out"], P["w_out"], P["g_post1"], P["g_pre2"], P["w_router"], P["b_router"], P["tri"])


def _move_kernel(sidx_ref, didx_ref, sprev_ref, dprev_ref, src_ref, *rest):
    dst_ref, sem = rest[-2], rest[-1]
    n = sidx_ref.shape[2]
    step = pl.program_id(0)

    def copy(s_ref, d_ref, t):
        return pltpu.make_async_copy(src_ref.at[s_ref[0, 0, t]], dst_ref.at[d_ref[0, 0, t]], sem)

    def issue(t, carry):
        copy(sidx_ref, didx_ref, t).start()
        return carry

    lax.fori_loop(0, n, issue, 0, unroll=MOVE_UNROLL)

    @pl.when(step > 0)
    def _():
        def wait_prev(t, carry):
            copy(sprev_ref, dprev_ref, t).wait()
            return carry

        lax.fori_loop(0, n, wait_prev, 0, unroll=MOVE_UNROLL)

    @pl.when(step == pl.num_programs(0) - 1)
    def _():
        def wait_last(t, carry):
            copy(sidx_ref, didx_ref, t).wait()
            return carry

        lax.fori_loop(0, n, wait_last, 0, unroll=MOVE_UNROLL)


def _move_rows(src, sidx, didx, n_dst, dst_init=None):
    n = sidx.shape[0]
    nblk = n // MOVE_TILE
    idx_spec = pl.BlockSpec((1, 1, MOVE_TILE), lambda i: (i, 0, 0), memory_space=pltpu.SMEM)
    prev_spec = pl.BlockSpec((1, 1, MOVE_TILE), lambda i: (jnp.maximum(i - 1, 0), 0, 0), memory_space=pltpu.SMEM)
    any_spec = pl.BlockSpec(memory_space=pl.ANY)
    sidx3 = sidx.reshape(nblk, 1, MOVE_TILE)
    didx3 = didx.reshape(nblk, 1, MOVE_TILE)
    args = [sidx3, didx3, sidx3, didx3, src]
    in_specs = [idx_spec, idx_spec, prev_spec, prev_spec, any_spec]
    aliases = {}
    if dst_init is not None:
        args.append(dst_init)
        in_specs.append(any_spec)
        aliases = {5: 0}
    return pl.pallas_call(
        _move_kernel,
        grid=(nblk,),
        in_specs=in_specs,
        out_specs=any_spec,
        out_shape=jax.ShapeDtypeStruct((n_dst, ROW_TILES, LANES), F32),
        scratch_shapes=[pltpu.SemaphoreType.DMA],
        input_output_aliases=aliases,
        compiler_params=pltpu.CompilerParams(dimension_semantics=("arbitrary",), has_side_effects=True),
        name="move_rows",
    )(*args)


def _expert_kernel(te_ref, nv_ref, xs_ref, wgu_ref, wd_ref, ys_ref):
    del te_ref

    @pl.when(pl.program_id(0) < nv_ref[0])
    def _():
        x = jnp.concatenate([xs_ref[:, c, :] for c in range(ROW_TILES)], axis=1).astype(BF16)
        gu = _dot(x, wgu_ref[0])
        g, u = gu[:, :EXPERT_FF], gu[:, EXPERT_FF:]
        act = (g * jax.nn.sigmoid(g) * u).astype(BF16)
        y = _dot(act, wd_ref[0])
        for c in range(ROW_TILES):
            ys_ref[:, c, :] = y[:, c * LANES:(c + 1) * LANES]

    @pl.when(pl.program_id(0) >= nv_ref[0])
    def _():
        ys_ref[...] = jnp.zeros_like(ys_ref)


def _experts(xs, tile_expert, n_valid, P):
    n_tiles = xs.shape[0] // EXPERT_TILE

    def row_map(i, te, nv):
        return (jnp.minimum(i, nv[0] - 1), 0, 0)

    def out_map(i, te, nv):
        return (i, 0, 0)

    def w_map(i, te, nv):
        return (te[jnp.minimum(i, nv[0] - 1)], 0, 0)

    return pl.pallas_call(
        _expert_kernel,
        grid_spec=pltpu.PrefetchScalarGridSpec(
            num_scalar_prefetch=2,
            grid=(n_tiles,),
            in_specs=[
                pl.BlockSpec((EXPERT_TILE, ROW_TILES, LANES), row_map),
                pl.BlockSpec((1, D_MODEL, 2 * EXPERT_FF), w_map),
                pl.BlockSpec((1, EXPERT_FF, D_MODEL), w_map),
            ],
            out_specs=pl.BlockSpec((EXPERT_TILE, ROW_TILES, LANES), out_map),
        ),
        out_shape=jax.ShapeDtypeStruct(xs.shape, F32),
        compiler_params=_params(("arbitrary",)),
        name="experts",
    )(tile_expert, n_valid, xs, P["w_gu"], P["w_down"])


def _final_kernel(y0_ref, y1_ref, wcol_ref, x1_ref, mod_ref, gpost_ref, o_ref):
    w = wcol_ref[0]
    w0, w1 = w[:, 0:1], w[:, 1:2]
    m = jnp.concatenate([w0 * y0_ref[:, c, :] + w1 * y1_ref[:, c, :] for c in range(ROW_TILES)], axis=1)
    gate2 = mod_ref[0][5:6]
    o_ref[0] = x1_ref[0] + gate2 * (_rms(m) * gpost_ref[...])


def _final(yg, wcol, x1, mod, P):
    nb, seq, _ = x1.shape
    tm = TOKEN_TILE
    nt = seq // tm
    n_tok_tiles = nb * nt
    return pl.pallas_call(
        _final_kernel,
        grid=(nb, nt),
        in_specs=[
            pl.BlockSpec((tm, ROW_TILES, LANES), lambda b, i: (b * nt + i, 0, 0)),
            pl.BlockSpec((tm, ROW_TILES, LANES), lambda b, i: (n_tok_tiles + b * nt + i, 0, 0)),
            pl.BlockSpec((1, tm, LANES), lambda b, i: (b, i, 0)),
            pl.BlockSpec((1, tm, D_MODEL), lambda b, i: (b, i, 0)),
            pl.BlockSpec((1, SUBLANES, D_MODEL), lambda b, i: (b, 0, 0)),
            pl.BlockSpec((1, D_MODEL), lambda b, i: (0, 0)),
        ],
        out_specs=pl.BlockSpec((1, tm, D_MODEL), lambda b, i: (b, i, 0)),
        out_shape=jax.ShapeDtypeStruct((nb, seq, D_MODEL), F32),
        compiler_params=_params(("arbitrary", "arbitrary")),
        name="final",
    )(yg, yg, wcol, x1, mod, P["g_post2"])


def _prepare(w):
    f = lambda a: a.astype(F32)
    P = {}
    for k in ("g_pre1", "g_post1", "g_pre2", "g_post2", "g_q", "g_kv", "g_attn_out", "g_gmlp_out"):
        P[k] = f(w[k]).reshape(1, -1)
    P["g_v"] = f(w["g_v_gmlp"]).reshape(1, -1)

    w_in = f(w["w_in"])
    o0, o1, o2, o3 = Q_LORA, Q_LORA + KV_LORA, Q_LORA + KV_LORA + QK_ROPE, Q_LORA + KV_LORA + QK_ROPE + GMLP_WIDTH
    w_kr = w_in[:, o1:o2]
    kr_partner = jnp.concatenate([-w_kr[:, ROPE_HALF:], w_kr[:, :ROPE_HALF]], axis=1)
    rope_blk = jnp.concatenate([jnp.zeros((D_MODEL, QK_NOPE), F32), w_kr, kr_partner], axis=1)
    P["w_in"] = jnp.concatenate([w_in[:, :o1], rope_blk, w_in[:, o2:o3], w_in[:, o3:]], axis=1).astype(BF16)

    w_uq = f(w["w_uq"]).reshape(Q_LORA, N_HEADS, QK_NOPE + QK_ROPE)
    q_rope = w_uq[:, :, QK_NOPE:]
    q_partner = jnp.concatenate([-q_rope[:, :, ROPE_HALF:], q_rope[:, :, :ROPE_HALF]], axis=2)
    P["w_uq"] = jnp.concatenate([w_uq, q_partner], axis=2).reshape(Q_LORA, N_HEADS * HEAD_PAD).astype(BF16)

    w_ukv = f(w["w_ukv"]).reshape(KV_LORA, N_HEADS, QK_NOPE + V_HEAD)
    zeros = jnp.zeros((KV_LORA, N_HEADS, HEAD_PAD - QK_NOPE), F32)
    w_k = jnp.concatenate([w_ukv[:, :, :QK_NOPE], zeros], axis=2)
    w_v = w_ukv[:, :, QK_NOPE:]
    even = (jnp.arange(N_HEADS) % 2 == 0)[None, :, None]
    zv = jnp.zeros_like(w_v)
    w_v = jnp.concatenate([jnp.where(even, w_v, zv), jnp.where(even, zv, w_v)], axis=2)
    P["w_ukv"] = jnp.concatenate([w_k.reshape(KV_LORA, -1), w_v.reshape(KV_LORA, -1)], axis=1).astype(BF16)

    P["w_sp"] = f(w["w_spatial"]).reshape(GMLP_GROUPS // 2, 2 * CHUNK, CHUNK).astype(BF16)
    P["b_sp"] = jnp.repeat(f(w["b_spatial"]).T, GMLP_GROUP_DIM, axis=1)

    P["w_out"] = f(w["w_out"]).astype(BF16)

    pad = jnp.zeros((D_MODEL, LANES - N_EXPERTS - N_EXPERT_GROUPS), F32)
    wr = jnp.concatenate([f(w["w_router_expert"]), f(w["w_router_group"]), pad], axis=1)
    wr_hi = wr.astype(BF16)
    wr_lo = (wr - wr_hi.astype(F32)).astype(BF16)
    P["w_router"] = jnp.concatenate([wr_hi, wr_lo], axis=1)
    P["b_router"] = jnp.concatenate(
        [f(w["b_router_expert"]), f(w["b_router_group"]), jnp.zeros((LANES - N_EXPERTS - N_EXPERT_GROUPS,), F32)]
    ).reshape(1, LANES)
    P["tri"] = jnp.triu(jnp.ones((RANK_CHUNK, RANK_CHUNK), F32), k=1).astype(BF16)

    P["w_gu"] = jnp.concatenate([f(w["w_gate"]), f(w["w_up"])], axis=2).astype(BF16)
    P["w_down"] = f(w["w_down"]).astype(BF16)
    return P


def _rope_tables(seq):
    inv = ROPE_THETA ** (-jnp.arange(ROPE_HALF, dtype=F32) / ROPE_HALF)
    ang = jnp.arange(seq, dtype=F32)[:, None] * inv[None, :]
    z_lo = jnp.zeros((seq, _ROPE_LO), F32)
    z_hi = jnp.zeros((seq, LANES - _ROPE_LO - QK_ROPE), F32)
    cos = jnp.concatenate([z_lo, jnp.cos(ang), jnp.cos(ang), z_hi], axis=1)
    sin = jnp.concatenate([z_lo, jnp.sin(ang), jnp.sin(ang), z_hi], axis=1)
    return cos, sin


def _layer(x, mod, P):
    nb, seq, _ = x.shape
    n_tok = nb * seq
    q, kt, v, sn = _premix(x, mod, P)
    a = _attention(q, kt, v)
    x1, h2rows, eid, rank, wcol, counts = _postmix(a, sn, x, mod, P)

    cnt = counts[:, 0].astype(jnp.int32)
    padded = ((cnt + EXPERT_TILE - 1) // EXPERT_TILE) * EXPERT_TILE
    ends = jnp.cumsum(padded)
    starts = ends - padded
    eflat = jnp.transpose(eid, (1, 0, 2)).reshape(2, n_tok)
    rflat = jnp.transpose(rank, (1, 0, 2)).reshape(2, n_tok)
    onehot = eflat[:, :, None] == jnp.arange(N_EXPERTS, dtype=jnp.int32)[None, None, :]
    pos = rflat + jnp.sum(jnp.where(onehot, starts[None, None, :], 0), axis=2)
    n_rows = 2 * n_tok + N_EXPERTS * EXPERT_TILE
    n_tiles = n_rows // EXPERT_TILE
    tile_start = jnp.arange(n_tiles, dtype=jnp.int32) * EXPERT_TILE
    tile_expert = jnp.minimum(
        jnp.sum((tile_start[:, None] >= ends[None, :]).astype(jnp.int32), axis=1), N_EXPERTS - 1).astype(jnp.int32)
    n_valid = (ends[-1:] // EXPERT_TILE).astype(jnp.int32)

    tok = jnp.arange(n_tok, dtype=jnp.int32)
    lin = jnp.arange(2 * n_tok, dtype=jnp.int32)
    posf = pos.reshape(2 * n_tok)
    xs = _move_rows(h2rows, jnp.concatenate([tok, tok]), posf, n_rows, jnp.zeros((n_rows, ROW_TILES, LANES), F32))
    ys = _experts(xs, tile_expert, n_valid, P)
    yg = _move_rows(ys, posf, lin, 2 * n_tok)
    return _final(yg, wcol, x1, mod, P)


def kernel(x_prompt, x_sample, c_prompt, c_sample, w_ada, b_ada, g_pre1, g_post1, g_pre2, g_post2, w_in, g_q, w_uq,
           g_kv, w_ukv, g_v_gmlp, w_spatial, b_spatial, g_attn_out, g_gmlp_out, w_out, w_router_group,
           b_router_group, w_router_expert, b_router_expert, w_gate, w_up, w_down):
    P = _prepare(dict(
        g_pre1=g_pre1, g_post1=g_post1, g_pre2=g_pre2, g_post2=g_post2, w_in=w_in, g_q=g_q, w_uq=w_uq, g_kv=g_kv,
        w_ukv=w_ukv, g_v_gmlp=g_v_gmlp, w_spatial=w_spatial, b_spatial=b_spatial, g_attn_out=g_attn_out,
        g_gmlp_out=g_gmlp_out, w_out=w_out, w_router_group=w_router_group, b_router_group=b_router_group,
        w_router_expert=w_router_expert, b_router_expert=b_router_expert, w_gate=w_gate, w_up=w_up, w_down=w_down))
    P["cos"], P["sin"] = _rope_tables(max(x_prompt.shape[1], x_sample.shape[1]))

    nbp = c_prompt.shape[0]
    c_all = jnp.concatenate([c_prompt, c_sample], axis=0).astype(F32)
    mod = _ada(c_all, w_ada.astype(F32), b_ada.astype(F32))
    mod = mod.reshape(c_all.shape[0], 6, D_MODEL)
    mod = jnp.concatenate([mod, jnp.zeros((c_all.shape[0], SUBLANES - 6, D_MODEL), F32)], axis=1)

    y_prompt = _layer(x_prompt, mod[:nbp], P)
    y_sample = _layer(x_sample, mod[nbp:], P)
    return (y_prompt, y_sample)
```

```python
import functools
import math

import jax
import jax.numpy as jnp
from jax import lax
from jax.experimental import pallas as pl
from jax.experimental.pallas import tpu as pltpu
from jax.experimental.pallas import tpu_sc as plsc

F32 = jnp.float32
BF16 = jnp.bfloat16

D_MODEL = 1024
N_HEADS = 8
QK_NOPE = 64
QK_ROPE = 32
ROPE_HALF = QK_ROPE // 2
V_HEAD = 64
Q_LORA = 256
KV_LORA = 128
GMLP_WIDTH = 512
GMLP_GROUPS = 8
GMLP_GROUP_DIM = 64
CHUNK = 128
N_EXPERTS = 32
N_EXPERT_GROUPS = 4
EXPERTS_PER_GROUP = 8
EXPERT_FF = 256
ROPE_THETA = 10000.0
EPS = 1e-6

LANES = 128
SUBLANES = 8
HEAD_PAD = LANES

TOKEN_TILE = 512
Q_TILE = 512
K_TILE = 512
EXPERT_TILE = 256
SC_CORES = 2
SC_WORKERS = 32
SC_ROWS = 32
RANK_CHUNK = 256
VMEM_LIMIT = 56 * 1024 * 1024

ROW_TILES = D_MODEL // LANES

_SQRT_2_OVER_PI = math.sqrt(2.0 / math.pi)


def _rms(x):
    return x * lax.rsqrt(jnp.mean(x * x, axis=-1, keepdims=True) + EPS)


def _gelu_tanh(x):
    return 0.5 * x * (1.0 + jnp.tanh(_SQRT_2_OVER_PI * (x + 0.044715 * (x * x * x))))


def _split_bf16(x):
    hi = x.astype(BF16)
    lo = (x - hi.astype(F32)).astype(BF16)
    return hi, lo


def _dot(a, b):
    return jnp.dot(a, b, preferred_element_type=F32)


def _params(sem, vmem=VMEM_LIMIT):
    return pltpu.CompilerParams(dimension_semantics=sem, vmem_limit_bytes=vmem)


def _ada_kernel(c_ref, w_ref, b_ref, o_ref):
    c = c_ref[...]
    a = c * jax.nn.sigmoid(c)
    a_hi, a_lo = _split_bf16(a)
    w_hi, w_lo = _split_bf16(w_ref[...])
    o_ref[...] = _dot(a_hi, w_hi) + _dot(a_hi, w_lo) + _dot(a_lo, w_hi) + b_ref[...]


def _ada(c, w_ada, b_ada):
    nb = c.shape[0]
    n_out = w_ada.shape[1]
    blk = D_MODEL
    return pl.pallas_call(
        _ada_kernel,
        grid=(n_out // blk,),
        in_specs=[
            pl.BlockSpec((nb, D_MODEL), lambda j: (0, 0)),
            pl.BlockSpec((D_MODEL, blk), lambda j: (0, j)),
            pl.BlockSpec((1, blk), lambda j: (0, j)),
        ],
        out_specs=pl.BlockSpec((nb, blk), lambda j: (0, j)),
        out_shape=jax.ShapeDtypeStruct((nb, n_out), F32),
        compiler_params=_params(("arbitrary",)),
        name="ada",
    )(c, w_ada, b_ada.reshape(1, n_out))


_C_CQ = 0
_C_CKV = _C_CQ + Q_LORA
_C_KR = _C_CKV + KV_LORA
_C_U = _C_KR + LANES
_C_V = _C_U + GMLP_WIDTH
_C_END = _C_V + GMLP_WIDTH
_ROPE_LO = QK_NOPE
_ROLL_PARTNER = LANES - QK_ROPE


def _premix_kernel(x_ref, mod_ref, cos_ref, sin_ref, gpre_ref, win_ref, gq_ref, wuq_ref, gkv_ref, wukv_ref,
                   gv_ref, wsp_ref, bsp_ref, ggo_ref, q_ref, kt_ref, v_ref, sn_ref):
    x = x_ref[0]
    mod = mod_ref[0]
    shift1, scale1 = mod[0:1], mod[1:2]
    h = _rms(x) * gpre_ref[...] * (1.0 + scale1) + shift1
    z = _dot(h.astype(BF16), win_ref[...])

    cosb = cos_ref[...]
    sinb = sin_ref[...]
    lane = lax.broadcasted_iota(jnp.int32, (1, LANES), 1)
    nope_mask = jnp.where(lane < QK_NOPE, 1.0, 0.0).astype(F32)

    qscale = (QK_NOPE + QK_ROPE) ** -0.5 * math.log2(math.e)
    cq_tab = (nope_mask + cosb) * qscale
    sq_tab = sinb * qscale
    cqn = (_rms(z[:, _C_CQ:_C_CKV]) * gq_ref[...]).astype(BF16)
    qb = _dot(cqn, wuq_ref[...])
    for hd in range(N_HEADS):
        blk = qb[:, hd * HEAD_PAD:(hd + 1) * HEAD_PAD]
        qh = blk * cq_tab + pltpu.roll(blk, _ROLL_PARTNER, 1) * sq_tab
        q_ref[0, hd] = qh.astype(BF16)

    ckvn = (_rms(z[:, _C_CKV:_C_KR]) * gkv_ref[...]).astype(BF16)
    kvb = _dot(ckvn, wukv_ref[...])
    krb = z[:, _C_KR:_C_U]
    krope = krb * cosb + pltpu.roll(krb, _ROLL_PARTNER, 1) * sinb
    v_off = N_HEADS * HEAD_PAD
    for hd in range(N_HEADS):
        kh = kvb[:, hd * HEAD_PAD:(hd + 1) * HEAD_PAD] + krope
        kt_ref[0, hd, 0] = kh.T.astype(BF16)
        ones_lane = V_HEAD if hd % 2 == 0 else 0
        vh = kvb[:, v_off + hd * HEAD_PAD:v_off + (hd + 1) * HEAD_PAD] + jnp.where(lane == ones_lane, 1.0, 0.0)
        v_ref[0, hd] = vh.astype(BF16)

    ua = _gelu_tanh(z[:, _C_U:_C_V])
    vn = (_rms(_gelu_tanh(z[:, _C_V:_C_END])) * gv_ref[...]).astype(BF16)
    n_tok = x.shape[0]
    bsp = bsp_ref[...]
    rows = []
    for n in range(n_tok // CHUNK):
        cols = []
        for j in range(GMLP_GROUPS // 2):
            rhs = vn[n * CHUNK:(n + 1) * CHUNK, j * LANES:(j + 1) * LANES]
            ab = _dot(wsp_ref[j], rhs)
            cols.append(jnp.where(lane < GMLP_GROUP_DIM, ab[:CHUNK], ab[CHUNK:]))
        rows.append(jnp.concatenate(cols, axis=1) + bsp)
    s = ua * jnp.concatenate(rows, axis=0)
    sn_ref[0] = (_rms(s) * ggo_ref[...]).astype(BF16)


def _premix(x, mod, P):
    nb, seq, _ = x.shape
    tm = TOKEN_TILE
    nck = seq // K_TILE
    const = lambda i, b: (0, 0)
    return pl.pallas_call(
        _premix_kernel,
        grid=(seq // tm, nb),
        in_specs=[
            pl.BlockSpec((1, tm, D_MODEL), lambda i, b: (b, i, 0)),
            pl.BlockSpec((1, SUBLANES, D_MODEL), lambda i, b: (b, 0, 0)),
            pl.BlockSpec((tm, LANES), lambda i, b: (i, 0)),
            pl.BlockSpec((tm, LANES), lambda i, b: (i, 0)),
            pl.BlockSpec((1, D_MODEL), const),
            pl.BlockSpec((D_MODEL, _C_END), const),
            pl.BlockSpec((1, Q_LORA), const),
            pl.BlockSpec((Q_LORA, N_HEADS * HEAD_PAD), const),
            pl.BlockSpec((1, KV_LORA), const),
            pl.BlockSpec((KV_LORA, 2 * N_HEADS * HEAD_PAD), const),
            pl.BlockSpec((1, GMLP_WIDTH), const),
            pl.BlockSpec((GMLP_GROUPS // 2, 2 * CHUNK, CHUNK), lambda i, b: (0, 0, 0)),
            pl.BlockSpec((CHUNK, GMLP_WIDTH), const),
            pl.BlockSpec((1, GMLP_WIDTH), const),
        ],
        out_specs=[
            pl.BlockSpec((1, N_HEADS, tm, HEAD_PAD), lambda i, b: (b, 0, i, 0)),
            pl.BlockSpec((1, N_HEADS, 1, HEAD_PAD, K_TILE), lambda i, b: (b, 0, i, 0, 0)),
            pl.BlockSpec((1, N_HEADS, tm, HEAD_PAD), lambda i, b: (b, 0, i, 0)),
            pl.BlockSpec((1, tm, GMLP_WIDTH), lambda i, b: (b, i, 0)),
        ],
        out_shape=[
            jax.ShapeDtypeStruct((nb, N_HEADS, seq, HEAD_PAD), BF16),
            jax.ShapeDtypeStruct((nb, N_HEADS, nck, HEAD_PAD, K_TILE), BF16),
            jax.ShapeDtypeStruct((nb, N_HEADS, seq, HEAD_PAD), BF16),
            jax.ShapeDtypeStruct((nb, seq, GMLP_WIDTH), BF16),
        ],
        compiler_params=_params(("arbitrary", "arbitrary")),
        name="premix",
    )(x, mod, P["cos"][:seq], P["sin"][:seq], P["g_pre1"], P["w_in"], P["g_q"], P["w_uq"], P["g_kv"], P["w_ukv"],
      P["g_v"], P["w_sp"], P["b_sp"], P["g_gmlp_out"])


def _attn_kernel(q_ref, kt_ref, v_ref, o_ref):
    n_chunks = kt_ref.shape[2]
    tk = kt_ref.shape[4]
    lane = lax.broadcasted_iota(jnp.int32, (1, LANES), 1)

    def one_head(hd, ones_lane):
        q = q_ref[0, hd]
        m = None
        acc = None
        for c in range(n_chunks):
            s = _dot(q, kt_ref[0, hd, c])
            v = v_ref[0, hd, c * tk:(c + 1) * tk, :]
            smax = jnp.max(s, axis=1, keepdims=True)
            if c == 0:
                m = smax
                acc = _dot(jnp.exp2(s - m).astype(BF16), v)
            else:
                m_new = jnp.maximum(m, smax)
                acc = acc * jnp.exp2(m - m_new) + _dot(jnp.exp2(s - m_new).astype(BF16), v)
                m = m_new
        row_sum = acc[:, ones_lane:ones_lane + 1]
        return acc * (1.0 / row_sum)

    def pair(j, carry):
        even = one_head(2 * j, V_HEAD)
        odd = one_head(2 * j + 1, 0)
        o_ref[0, j] = jnp.where(lane < V_HEAD, even, odd).astype(BF16)
        return carry

    lax.fori_loop(0, N_HEADS // 2, pair, 0)


def _attention(q, kt, v):
    nb, _, seq, _ = q.shape
    nck = kt.shape[2]
    return pl.pallas_call(
        _attn_kernel,
        grid=(nb, seq // Q_TILE),
        in_specs=[
            pl.BlockSpec((1, N_HEADS, Q_TILE, HEAD_PAD), lambda b, i: (b, 0, i, 0)),
            pl.BlockSpec((1, N_HEADS, nck, HEAD_PAD, K_TILE), lambda b, i: (b, 0, 0, 0, 0)),
            pl.BlockSpec((1, N_HEADS, seq, HEAD_PAD), lambda b, i: (b, 0, 0, 0)),
        ],
        out_specs=pl.BlockSpec((1, N_HEADS // 2, Q_TILE, LANES), lambda b, i: (b, 0, i, 0)),
        out_shape=jax.ShapeDtypeStruct((nb, N_HEADS // 2, seq, LANES), BF16),
        compiler_params=_params(("arbitrary", "arbitrary")),
        name="attn",
    )(q, kt, v)


_R_GROUP_ROW = N_EXPERTS


def _postmix_kernel(a_ref, sn_ref, x_ref, mod_ref, gao_ref, wout_ref, gpost_ref, gpre_ref, wr_ref, br_ref, tri_ref,
                    x1_ref, h2_ref, eid_ref, rank_ref, wcol_ref, cnt_ref, run_ref):
    first = jnp.logical_and(pl.program_id(0) == 0, pl.program_id(1) == 0)

    @pl.when(first)
    def _():
        run_ref[...] = jnp.zeros_like(run_ref)

    mod = mod_ref[0]
    gate1, shift2, scale2 = mod[2:3], mod[3:4], mod[4:5]
    a = jnp.concatenate([a_ref[0, j] for j in range(N_HEADS // 2)], axis=1).astype(F32)
    an = (_rms(a) * gao_ref[...]).astype(BF16)
    merged = jnp.concatenate([an, sn_ref[0]], axis=1)
    o = _dot(merged, wout_ref[...])
    x1 = x_ref[0] + gate1 * (_rms(o) * gpost_ref[...])
    x1_ref[0] = x1
    h2 = _rms(x1) * gpre_ref[...] * (1.0 + scale2) + shift2
    n_tok = h2.shape[0]
    for c in range(ROW_TILES):
        h2_ref[:, c, :] = h2[:, c * LANES:(c + 1) * LANES]

    h_hi, h_lo = _split_bf16(h2)
    wr = wr_ref[...]
    hh = _dot(h_hi, wr)
    lh = _dot(h_lo, wr[:, :LANES])
    logits = hh[:, :LANES] + hh[:, LANES:] + lh + br_ref[...]
    lt = logits.T

    neg = jnp.float32(-jnp.inf)
    row8 = lax.broadcasted_iota(jnp.int32, (SUBLANES, n_tok), 0).astype(F32)
    lg = jnp.where(row8 < N_EXPERT_GROUPS, lt[_R_GROUP_ROW:_R_GROUP_ROW + SUBLANES], neg)
    gmax = jnp.max(lg, axis=0, keepdims=True)
    gi = jnp.min(jnp.where(lg == gmax, row8, float(SUBLANES)), axis=0, keepdims=True)
    pg_sel = 1.0 / jnp.sum(jnp.exp(lg - gmax), axis=0, keepdims=True)

    le = jnp.zeros((EXPERTS_PER_GROUP, n_tok), F32)
    for g in range(N_EXPERT_GROUPS):
        le = jnp.where(gi == float(g), lt[g * EXPERTS_PER_GROUP:(g + 1) * EXPERTS_PER_GROUP], le)
    v1 = jnp.max(le, axis=0, keepdims=True)
    i1 = jnp.min(jnp.where(le == v1, row8, float(SUBLANES)), axis=0, keepdims=True)
    le2 = jnp.where(row8 == i1, neg, le)
    v2 = jnp.max(le2, axis=0, keepdims=True)
    i2 = jnp.min(jnp.where(le2 == v2, row8, float(SUBLANES)), axis=0, keepdims=True)
    r = jnp.exp(v2 - v1)
    w1 = pg_sel / (1.0 + r)
    w2 = w1 * r
    e1 = gi * float(EXPERTS_PER_GROUP) + i1
    e2 = gi * float(EXPERTS_PER_GROUP) + i2
    eid_ref[0] = jnp.concatenate([e1, e2], axis=0).astype(jnp.int32)

    row32 = lax.broadcasted_iota(jnp.int32, (N_EXPERTS, n_tok), 0).astype(F32)
    hit1 = row32 == e1
    hit2 = row32 == e2
    onehot = jnp.where(jnp.logical_or(hit1, hit2), 1.0, 0.0)
    run = run_ref[...][:, 0:1]
    ranks1, ranks2 = [], []
    for c in range(n_tok // RANK_CHUNK):
        sl = slice(c * RANK_CHUNK, (c + 1) * RANK_CHUNK)
        oh = onehot[:, sl]
        before = _dot(oh.astype(BF16), tri_ref[...]) + run
        ranks1.append(jnp.sum(jnp.where(hit1[:, sl], before, 0.0), axis=0, keepdims=True))
        ranks2.append(jnp.sum(jnp.where(hit2[:, sl], before, 0.0), axis=0, keepdims=True))
        run = run + jnp.sum(oh, axis=1, keepdims=True)
    rank_ref[0] = jnp.concatenate(
        [jnp.concatenate(ranks1, axis=1), jnp.concatenate(ranks2, axis=1)], axis=0).astype(jnp.int32)
    run_b = jnp.broadcast_to(run, run_ref.shape)
    run_ref[...] = run_b
    cnt_ref[...] = run_b

    row128 = lax.broadcasted_iota(jnp.int32, (LANES, n_tok), 0)
    wt = jnp.where(row128 == 0, w1, jnp.where(row128 == 1, w2, 0.0))
    wcol_ref[0] = wt.T


def _postmix(a, sn, x, mod, P):
    nb, seq, _ = x.shape
    tm = TOKEN_TILE
    const = lambda b, i: (0, 0)
    return pl.pallas_call(
        _postmix_kernel,
        grid=(nb, seq // tm),
        in_specs=[
            pl.BlockSpec((1, N_HEADS // 2, tm, LANES), lambda b, i: (b, 0, i, 0)),
            pl.BlockSpec((1, tm, GMLP_WIDTH), lambda b, i: (b, i, 0)),
            pl.BlockSpec((1, tm, D_MODEL), lambda b, i: (b, i, 0)),
            pl.BlockSpec((1, SUBLANES, D_MODEL), lambda b, i: (b, 0, 0)),
            pl.BlockSpec((1, N_HEADS * V_HEAD), const),
            pl.BlockSpec((D_MODEL, D_MODEL), const),
            pl.BlockSpec((1, D_MODEL), const),
            pl.BlockSpec((1, D_MODEL), const),
            pl.BlockSpec((D_MODEL, 2 * LANES), const),
            pl.BlockSpec((1, LANES), const),
            pl.BlockSpec((RANK_CHUNK, RANK_CHUNK), const),
        ],
        out_specs=[
            pl.BlockSpec((1, tm, D_MODEL), lambda b, i: (b, i, 0)),
            pl.BlockSpec((tm, ROW_TILES, LANES), lambda b, i: (b * (seq // tm) + i, 0, 0)),
            pl.BlockSpec((1, 2, tm), lambda b, i: (b, 0, i)),
            pl.BlockSpec((1, 2, tm), lambda b, i: (b, 0, i)),
            pl.BlockSpec((1, tm, LANES), lambda b, i: (b, i, 0)),
            pl.BlockSpec((N_EXPERTS, LANES), const),
        ],
        out_shape=[
            jax.ShapeDtypeStruct((nb, seq, D_MODEL), F32),
            jax.ShapeDtypeStruct((nb * seq, ROW_TILES, LANES), F32),
            jax.ShapeDtypeStruct((nb, 2, seq), jnp.int32),
            jax.ShapeDtypeStruct((nb, 2, seq), jnp.int32),
            jax.ShapeDtypeStruct((nb, seq, LANES), F32),
            jax.ShapeDtypeStruct((N_EXPERTS, LANES), F32),
        ],
        scratch_shapes=[pltpu.VMEM((N_EXPERTS, LANES), F32)],
        compiler_params=_params(("arbitrary", "arbitrary")),
        name="postmix",
    )(a, sn, x, mod, P["g_attn_out"], P["w_out"], P["g_post1"], P["g_pre2"], P["w_router"], P["b_router"], P["tri"])


def _sc_mesh():
    return plsc.VectorSubcoreMesh(core_axis_name="c", subcore_axis_name="s")


def _sc_worker():
    return lax.axis_index("s") * SC_CORES + lax.axis_index("c")


def _sc_gather_rows(table, idx):
    n = idx.shape[0]
    per_w = n // SC_WORKERS
    steps = per_w // SC_ROWS
    idx3 = idx.reshape(SC_WORKERS, steps, SC_ROWS)

    @functools.partial(
        pl.kernel, mesh=_sc_mesh(),
        out_type=jax.ShapeDtypeStruct((n, ROW_TILES, LANES), F32),
        scratch_types=[pltpu.VMEM((steps, SC_ROWS), jnp.int32), pltpu.VMEM((SC_ROWS, ROW_TILES, LANES), F32),
                       pltpu.SemaphoreType.DMA],
        name="sc_gather_rows",
    )
    def gather(table_hbm, idx_hbm, out_hbm, idx_v, rows_v, sem):
        wid = _sc_worker()
        pltpu.sync_copy(idx_hbm.at[wid], idx_v)

        @pl.loop(0, steps)
        def _(j):
            pltpu.async_copy(table_hbm.at[idx_v.at[j]], rows_v, sem).wait()
            pltpu.sync_copy(rows_v, out_hbm.at[pl.ds(wid * per_w + j * SC_ROWS, SC_ROWS)])

    return gather(table, idx3)


def _sc_scatter_rows(src, idx, n_dst):
    n_dup, n = idx.shape
    per_w = n // SC_WORKERS
    steps = per_w // SC_ROWS
    idx4 = idx.reshape(n_dup, SC_WORKERS, steps, SC_ROWS).transpose(1, 0, 2, 3)

    @functools.partial(
        pl.kernel, mesh=_sc_mesh(),
        out_type=jax.ShapeDtypeStruct((n_dst, ROW_TILES, LANES), F32),
        scratch_types=[pltpu.VMEM((n_dup, steps, SC_ROWS), jnp.int32), pltpu.VMEM((SC_ROWS, ROW_TILES, LANES), F32),
                       pltpu.SemaphoreType.DMA],
        name="sc_scatter_rows",
    )
    def scatter(src_hbm, idx_hbm, dst_hbm, idx_v, rows_v, sem):
        wid = _sc_worker()
        pltpu.sync_copy(idx_hbm.at[wid], idx_v)

        @pl.loop(0, steps)
        def _(j):
            pltpu.sync_copy(src_hbm.at[pl.ds(wid * per_w + j * SC_ROWS, SC_ROWS)], rows_v)
            for k in range(n_dup):
                pltpu.async_copy(rows_v, dst_hbm.at[idx_v.at[k, j]], sem).wait()

    return scatter(src, idx4)


def _expert_kernel(te_ref, nv_ref, xs_ref, wgu_ref, wd_ref, ys_ref):
    del te_ref

    @pl.when(pl.program_id(0) < nv_ref[0])
    def _():
        x = jnp.concatenate([xs_ref[:, c, :] for c in range(ROW_TILES)], axis=1).astype(BF16)
        gu = _dot(x, wgu_ref[0])
        g, u = gu[:, :EXPERT_FF], gu[:, EXPERT_FF:]
        act = (g * jax.nn.sigmoid(g) * u).astype(BF16)
        y = _dot(act, wd_ref[0])
        for c in range(ROW_TILES):
            ys_ref[:, c, :] = y[:, c * LANES:(c + 1) * LANES]

    @pl.when(pl.program_id(0) >= nv_ref[0])
    def _():
        ys_ref[...] = jnp.zeros_like(ys_ref)


def _experts(xs, tile_expert, n_valid, P):
    n_tiles = xs.shape[0] // EXPERT_TILE

    def row_map(i, te, nv):
        return (jnp.minimum(i, nv[0] - 1), 0, 0)

    def out_map(i, te, nv):
        return (i, 0, 0)

    def w_map(i, te, nv):
        return (te[jnp.minimum(i, nv[0] - 1)], 0, 0)

    return pl.pallas_call(
        _expert_kernel,
        grid_spec=pltpu.PrefetchScalarGridSpec(
            num_scalar_prefetch=2,
            grid=(n_tiles,),
            in_specs=[
                pl.BlockSpec((EXPERT_TILE, ROW_TILES, LANES), row_map),
                pl.BlockSpec((1, D_MODEL, 2 * EXPERT_FF), w_map),
                pl.BlockSpec((1, EXPERT_FF, D_MODEL), w_map),
            ],
            out_specs=pl.BlockSpec((EXPERT_TILE, ROW_TILES, LANES), out_map),
        ),
        out_shape=jax.ShapeDtypeStruct(xs.shape, F32),
        compiler_params=_params(("arbitrary",)),
        name="experts",
    )(tile_expert, n_valid, xs, P["w_gu"], P["w_down"])


def _final_kernel(y0_ref, y1_ref, wcol_ref, x1_ref, mod_ref, gpost_ref, o_ref):
    w = wcol_ref[0]
    w0, w1 = w[:, 0:1], w[:, 1:2]
    m = jnp.concatenate([w0 * y0_ref[:, c, :] + w1 * y1_ref[:, c, :] for c in range(ROW_TILES)], axis=1)
    gate2 = mod_ref[0][5:6]
    o_ref[0] = x1_ref[0] + gate2 * (_rms(m) * gpost_ref[...])


def _final(yg, wcol, x1, mod, P):
    nb, seq, _ = x1.shape
    tm = TOKEN_TILE
    nt = seq // tm
    n_tok_tiles = nb * nt
    return pl.pallas_call(
        _final_kernel,
        grid=(nb, nt),
        in_specs=[
            pl.BlockSpec((tm, ROW_TILES, LANES), lambda b, i: (b * nt + i, 0, 0)),
            pl.BlockSpec((tm, ROW_TILES, LANES), lambda b, i: (n_tok_tiles + b * nt + i, 0, 0)),
            pl.BlockSpec((1, tm, LANES), lambda b, i: (b, i, 0)),
            pl.BlockSpec((1, tm, D_MODEL), lambda b, i: (b, i, 0)),
            pl.BlockSpec((1, SUBLANES, D_MODEL), lambda b, i: (b, 0, 0)),
            pl.BlockSpec((1, D_MODEL), lambda b, i: (0, 0)),
        ],
        out_specs=pl.BlockSpec((1, tm, D_MODEL), lambda b, i: (b, i, 0)),
        out_shape=jax.ShapeDtypeStruct((nb, seq, D_MODEL), F32),
        compiler_params=_params(("arbitrary", "arbitrary")),
        name="final",
    )(yg, yg, wcol, x1, mod, P["g_post2"])


def _prepare(w):
    f = lambda a: a.astype(F32)
    P = {}
    for k in ("g_pre1", "g_post1", "g_pre2", "g_post2", "g_q", "g_kv", "g_attn_out", "g_gmlp_out"):
        P[k] = f(w[k]).reshape(1, -1)
    P["g_v"] = f(w["g_v_gmlp"]).reshape(1, -1)

    w_in = f(w["w_in"])
    o0, o1, o2, o3 = Q_LORA, Q_LORA + KV_LORA, Q_LORA + KV_LORA + QK_ROPE, Q_LORA + KV_LORA + QK_ROPE + GMLP_WIDTH
    w_kr = w_in[:, o1:o2]
    kr_partner = jnp.concatenate([-w_kr[:, ROPE_HALF:], w_kr[:, :ROPE_HALF]], axis=1)
    rope_blk = jnp.concatenate([jnp.zeros((D_MODEL, QK_NOPE), F32), w_kr, kr_partner], axis=1)
    P["w_in"] = jnp.concatenate([w_in[:, :o1], rope_blk, w_in[:, o2:o3], w_in[:, o3:]], axis=1).astype(BF16)

    w_uq = f(w["w_uq"]).reshape(Q_LORA, N_HEADS, QK_NOPE + QK_ROPE)
    q_rope = w_uq[:, :, QK_NOPE:]
    q_partner = jnp.concatenate([-q_rope[:, :, ROPE_HALF:], q_rope[:, :, :ROPE_HALF]], axis=2)
    P["w_uq"] = jnp.concatenate([w_uq, q_partner], axis=2).reshape(Q_LORA, N_HEADS * HEAD_PAD).astype(BF16)

    w_ukv = f(w["w_ukv"]).reshape(KV_LORA, N_HEADS, QK_NOPE + V_HEAD)
    zeros = jnp.zeros((KV_LORA, N_HEADS, HEAD_PAD - QK_NOPE), F32)
    w_k = jnp.concatenate([w_ukv[:, :, :QK_NOPE], zeros], axis=2)
    w_v = w_ukv[:, :, QK_NOPE:]
    even = (jnp.arange(N_HEADS) % 2 == 0)[None, :, None]
    zv = jnp.zeros_like(w_v)
    w_v = jnp.concatenate([jnp.where(even, w_v, zv), jnp.where(even, zv, w_v)], axis=2)
    P["w_ukv"] = jnp.concatenate([w_k.reshape(KV_LORA, -1), w_v.reshape(KV_LORA, -1)], axis=1).astype(BF16)

    P["w_sp"] = f(w["w_spatial"]).reshape(GMLP_GROUPS // 2, 2 * CHUNK, CHUNK).astype(BF16)
    P["b_sp"] = jnp.repeat(f(w["b_spatial"]).T, GMLP_GROUP_DIM, axis=1)

    P["w_out"] = f(w["w_out"]).astype(BF16)

    pad = jnp.zeros((D_MODEL, LANES - N_EXPERTS - N_EXPERT_GROUPS), F32)
    wr = jnp.concatenate([f(w["w_router_expert"]), f(w["w_router_group"]), pad], axis=1)
    wr_hi = wr.astype(BF16)
    wr_lo = (wr - wr_hi.astype(F32)).astype(BF16)
    P["w_router"] = jnp.concatenate([wr_hi, wr_lo], axis=1)
    P["b_router"] = jnp.concatenate(
        [f(w["b_router_expert"]), f(w["b_router_group"]), jnp.zeros((LANES - N_EXPERTS - N_EXPERT_GROUPS,), F32)]
    ).reshape(1, LANES)
    P["tri"] = jnp.triu(jnp.ones((RANK_CHUNK, RANK_CHUNK), F32), k=1).astype(BF16)

    P["w_gu"] = jnp.concatenate([f(w["w_gate"]), f(w["w_up"])], axis=2).astype(BF16)
    P["w_down"] = f(w["w_down"]).astype(BF16)
    return P


def _rope_tables(seq):
    inv = ROPE_THETA ** (-jnp.arange(ROPE_HALF, dtype=F32) / ROPE_HALF)
    ang = jnp.arange(seq, dtype=F32)[:, None] * inv[None, :]
    z_lo = jnp.zeros((seq, _ROPE_LO), F32)
    z_hi = jnp.zeros((seq, LANES - _ROPE_LO - QK_ROPE), F32)
    cos = jnp.concatenate([z_lo, jnp.cos(ang), jnp.cos(ang), z_hi], axis=1)
    sin = jnp.concatenate([z_lo, jnp.sin(ang), jnp.sin(ang), z_hi], axis=1)
    return cos, sin


def _layer(x, mod, P):
    nb, seq, _ = x.shape
    n_tok = nb * seq
    q, kt, v, sn = _premix(x, mod, P)
    a = _attention(q, kt, v)
    x1, h2rows, eid, rank, wcol, counts = _postmix(a, sn, x, mod, P)

    cnt = counts[:, 0].astype(jnp.int32)
    padded = ((cnt + EXPERT_TILE - 1) // EXPERT_TILE) * EXPERT_TILE
    ends = jnp.cumsum(padded)
    starts = ends - padded
    eflat = jnp.transpose(eid, (1, 0, 2)).reshape(2, n_tok)
    rflat = jnp.transpose(rank, (1, 0, 2)).reshape(2, n_tok)
    onehot = eflat[:, :, None] == jnp.arange(N_EXPERTS, dtype=jnp.int32)[None, None, :]
    pos = rflat + jnp.sum(jnp.where(onehot, starts[None, None, :], 0), axis=2)
    n_rows = 2 * n_tok + N_EXPERTS * EXPERT_TILE
    n_tiles = n_rows // EXPERT_TILE
    tile_start = jnp.arange(n_tiles, dtype=jnp.int32) * EXPERT_TILE
    tile_expert = jnp.minimum(
        jnp.sum((tile_start[:, None] >= ends[None, :]).astype(jnp.int32), axis=1), N_EXPERTS - 1).astype(jnp.int32)
    n_valid = (ends[-1:] // EXPERT_TILE).astype(jnp.int32)

    xs = _sc_scatter_rows(h2rows, pos, n_rows)
    ys = _experts(xs, tile_expert, n_valid, P)
    yg = _sc_gather_rows(ys, pos.reshape(2 * n_tok))
    return _final(yg, wcol, x1, mod, P)


def kernel(x_prompt, x_sample, c_prompt, c_sample, w_ada, b_ada, g_pre1, g_post1, g_pre2, g_post2, w_in, g_q, w_uq,
           g_kv, w_ukv, g_v_gmlp, w_spatial, b_spatial, g_attn_out, g_gmlp_out, w_out, w_router_group,
           b_router_group, w_router_expert, b_router_expert, w_gate, w_up, w_down):
    P = _prepare(dict(
        g_pre1=g_pre1, g_post1=g_post1, g_pre2=g_pre2, g_post2=g_post2, w_in=w_in, g_q=g_q, w_uq=w_uq, g_kv=g_kv,
        w_ukv=w_ukv, g_v_gmlp=g_v_gmlp, w_spatial=w_spatial, b_spatial=b_spatial, g_attn_out=g_attn_out,
        g_gmlp_out=g_gmlp_out, w_out=w_out, w_router_group=w_router_group, b_router_group=b_router_group,
        w_router_expert=w_router_expert, b_router_expert=b_router_expert, w_gate=w_gate, w_up=w_up, w_down=w_down))
    P["cos"], P["sin"] = _rope_tables(max(x_prompt.shape[1], x_sample.shape[1]))

    nbp = c_prompt.shape[0]
    c_all = jnp.concatenate([c_prompt, c_sample], axis=0).astype(F32)
    mod = _ada(c_all, w_ada.astype(F32), b_ada.astype(F32))
    mod = mod.reshape(c_all.shape[0], 6, D_MODEL)
    mod = jnp.concatenate([mod, jnp.zeros((c_all.shape[0], SUBLANES - 6, D_MODEL), F32)], axis=1)

    y_prompt = _layer(x_prompt, mod[:nbp], P)
    y_sample = _layer(x_sample, mod[nbp:], P)
    return (y_prompt, y_sample)
```

```python
import functools
import math

import jax
import jax.numpy as jnp
from jax import lax
from jax.experimental import pallas as pl
from jax.experimental.pallas import tpu as pltpu
from jax.experimental.pallas import tpu_sc as plsc

F32 = jnp.float32
BF16 = jnp.bfloat16

D_MODEL = 1024
N_HEADS = 8
QK_NOPE = 64
QK_ROPE = 32
ROPE_HALF = QK_ROPE // 2
V_HEAD = 64
Q_LORA = 256
KV_LORA = 128
GMLP_WIDTH = 512
GMLP_GROUPS = 8
GMLP_GROUP_DIM = 64
CHUNK = 128
N_EXPERTS = 32
N_EXPERT_GROUPS = 4
EXPERTS_PER_GROUP = 8
EXPERT_FF = 256
ROPE_THETA = 10000.0
EPS = 1e-6

LANES = 128
SUBLANES = 8
HEAD_PAD = LANES

TOKEN_TILE = 512
Q_TILE = 1024
K_TILE = 512
EXPERT_TILE = 256
SC_CORES = 2
SC_WORKERS = 32
SC_ROWS = 32
RANK_CHUNK = 256
VMEM_LIMIT = 56 * 1024 * 1024

ROW_TILES = D_MODEL // LANES

_SQRT_2_OVER_PI = math.sqrt(2.0 / math.pi)


def _rms(x):
    return x * lax.rsqrt(jnp.mean(x * x, axis=-1, keepdims=True) + EPS)


def _gelu_tanh(x):
    return 0.5 * x * (1.0 + jnp.tanh(_SQRT_2_OVER_PI * (x + 0.044715 * (x * x * x))))


def _split_bf16(x):
    hi = x.astype(BF16)
    lo = (x - hi.astype(F32)).astype(BF16)
    return hi, lo


def _dot(a, b):
    return jnp.dot(a, b, preferred_element_type=F32)


def _load_token_rows(ref, n):
    return jnp.concatenate([ref[pl.ds(c, n, stride=ROW_TILES), :] for c in range(ROW_TILES)], axis=1)


def _store_token_rows(ref, val):
    n = val.shape[0]
    for c in range(ROW_TILES):
        ref[pl.ds(c, n, stride=ROW_TILES), :] = val[:, c * LANES:(c + 1) * LANES]


def _params(sem, vmem=VMEM_LIMIT):
    return pltpu.CompilerParams(dimension_semantics=sem, vmem_limit_bytes=vmem)


def _ada_kernel(c_ref, w_ref, b_ref, o_ref):
    c = c_ref[...]
    a = c * jax.nn.sigmoid(c)
    a_hi, a_lo = _split_bf16(a)
    w_hi, w_lo = _split_bf16(w_ref[...])
    o_ref[...] = _dot(a_hi, w_hi) + _dot(a_hi, w_lo) + _dot(a_lo, w_hi) + b_ref[...]


def _ada(c, w_ada, b_ada):
    nb = c.shape[0]
    n_out = w_ada.shape[1]
    blk = D_MODEL
    return pl.pallas_call(
        _ada_kernel,
        grid=(n_out // blk,),
        in_specs=[
            pl.BlockSpec((nb, D_MODEL), lambda j: (0, 0)),
            pl.BlockSpec((D_MODEL, blk), lambda j: (0, j)),
            pl.BlockSpec((1, blk), lambda j: (0, j)),
        ],
        out_specs=pl.BlockSpec((nb, blk), lambda j: (0, j)),
        out_shape=jax.ShapeDtypeStruct((nb, n_out), F32),
        compiler_params=_params(("arbitrary",)),
        name="ada",
    )(c, w_ada, b_ada.reshape(1, n_out))


_C_CQ = 0
_C_CKV = _C_CQ + Q_LORA
_C_KR = _C_CKV + KV_LORA
_C_U = _C_KR + LANES
_C_V = _C_U + GMLP_WIDTH
_C_END = _C_V + GMLP_WIDTH
_ROPE_LO = QK_NOPE
_ROLL_PARTNER = LANES - QK_ROPE


def _premix_kernel(x_ref, mod_ref, cos_ref, sin_ref, gpre_ref, win_ref, gq_ref, wuq_ref, gkv_ref, wukv_ref,
                   gv_ref, wsp_ref, bsp_ref, ggo_ref, q_ref, kt_ref, v_ref, sn_ref):
    x = x_ref[0]
    mod = mod_ref[0]
    shift1, scale1 = mod[0:1], mod[1:2]
    h = _rms(x) * gpre_ref[...] * (1.0 + scale1) + shift1
    z = _dot(h.astype(BF16), win_ref[...])

    cosb = cos_ref[...]
    sinb = sin_ref[...]
    lane = lax.broadcasted_iota(jnp.int32, (1, LANES), 1)
    nope_mask = jnp.where(lane < QK_NOPE, 1.0, 0.0).astype(F32)

    qscale = (QK_NOPE + QK_ROPE) ** -0.5 * math.log2(math.e)
    cq_tab = (nope_mask + cosb) * qscale
    sq_tab = sinb * qscale
    cqn = (_rms(z[:, _C_CQ:_C_CKV]) * gq_ref[...]).astype(BF16)
    qb = _dot(cqn, wuq_ref[...])
    for hd in range(N_HEADS):
        blk = qb[:, hd * HEAD_PAD:(hd + 1) * HEAD_PAD]
        qh = blk * cq_tab + pltpu.roll(blk, _ROLL_PARTNER, 1) * sq_tab
        q_ref[0, hd] = qh.astype(BF16)

    ckvn = (_rms(z[:, _C_CKV:_C_KR]) * gkv_ref[...]).astype(BF16)
    kvb = _dot(ckvn, wukv_ref[...])
    krb = z[:, _C_KR:_C_U]
    krope = krb * cosb + pltpu.roll(krb, _ROLL_PARTNER, 1) * sinb
    v_off = N_HEADS * HEAD_PAD
    for hd in range(N_HEADS):
        kh = kvb[:, hd * HEAD_PAD:(hd + 1) * HEAD_PAD] + krope
        kt_ref[0, hd, 0] = kh.T.astype(BF16)
        ones_lane = V_HEAD if hd % 2 == 0 else 0
        vh = kvb[:, v_off + hd * HEAD_PAD:v_off + (hd + 1) * HEAD_PAD] + jnp.where(lane == ones_lane, 1.0, 0.0)
        v_ref[0, hd] = vh.astype(BF16)

    ua = _gelu_tanh(z[:, _C_U:_C_V])
    vn = (_rms(_gelu_tanh(z[:, _C_V:_C_END])) * gv_ref[...]).astype(BF16)
    n_tok = x.shape[0]
    bsp = bsp_ref[...]
    rows = []
    for n in range(n_tok // CHUNK):
        cols = []
        for j in range(GMLP_GROUPS // 2):
            rhs = vn[n * CHUNK:(n + 1) * CHUNK, j * LANES:(j + 1) * LANES]
            ab = _dot(wsp_ref[j], rhs)
            cols.append(jnp.where(lane < GMLP_GROUP_DIM, ab[:CHUNK], ab[CHUNK:]))
        rows.append(jnp.concatenate(cols, axis=1) + bsp)
    s = ua * jnp.concatenate(rows, axis=0)
    sn_ref[0] = (_rms(s) * ggo_ref[...]).astype(BF16)


def _premix(x, mod, P):
    nb, seq, _ = x.shape
    tm = TOKEN_TILE
    nck = seq // K_TILE
    const = lambda i, b: (0, 0)
    return pl.pallas_call(
        _premix_kernel,
        grid=(seq // tm, nb),
        in_specs=[
            pl.BlockSpec((1, tm, D_MODEL), lambda i, b: (b, i, 0)),
            pl.BlockSpec((1, SUBLANES, D_MODEL), lambda i, b: (b, 0, 0)),
            pl.BlockSpec((tm, LANES), lambda i, b: (i, 0)),
            pl.BlockSpec((tm, LANES), lambda i, b: (i, 0)),
            pl.BlockSpec((1, D_MODEL), const),
            pl.BlockSpec((D_MODEL, _C_END), const),
            pl.BlockSpec((1, Q_LORA), const),
            pl.BlockSpec((Q_LORA, N_HEADS * HEAD_PAD), const),
            pl.BlockSpec((1, KV_LORA), const),
            pl.BlockSpec((KV_LORA, 2 * N_HEADS * HEAD_PAD), const),
            pl.BlockSpec((1, GMLP_WIDTH), const),
            pl.BlockSpec((GMLP_GROUPS // 2, 2 * CHUNK, CHUNK), lambda i, b: (0, 0, 0)),
            pl.BlockSpec((CHUNK, GMLP_WIDTH), const),
            pl.BlockSpec((1, GMLP_WIDTH), const),
        ],
        out_specs=[
            pl.BlockSpec((1, N_HEADS, tm, HEAD_PAD), lambda i, b: (b, 0, i, 0)),
            pl.BlockSpec((1, N_HEADS, 1, HEAD_PAD, K_TILE), lambda i, b: (b, 0, i, 0, 0)),
            pl.BlockSpec((1, N_HEADS, tm, HEAD_PAD), lambda i, b: (b, 0, i, 0)),
            pl.BlockSpec((1, tm, GMLP_WIDTH), lambda i, b: (b, i, 0)),
        ],
        out_shape=[
            jax.ShapeDtypeStruct((nb, N_HEADS, seq, HEAD_PAD), BF16),
            jax.ShapeDtypeStruct((nb, N_HEADS, nck, HEAD_PAD, K_TILE), BF16),
            jax.ShapeDtypeStruct((nb, N_HEADS, seq, HEAD_PAD), BF16),
            jax.ShapeDtypeStruct((nb, seq, GMLP_WIDTH), BF16),
        ],
        compiler_params=_params(("arbitrary", "arbitrary")),
        name="premix",
    )(x, mod, P["cos"][:seq], P["sin"][:seq], P["g_pre1"], P["w_in"], P["g_q"], P["w_uq"], P["g_kv"], P["w_ukv"],
      P["g_v"], P["w_sp"], P["b_sp"], P["g_gmlp_out"])


def _attn_kernel(q_ref, kt_ref, v_ref, o_ref):
    n_chunks = kt_ref.shape[2]
    tk = kt_ref.shape[4]
    lane = lax.broadcasted_iota(jnp.int32, (1, LANES), 1)

    def one_head(hd, ones_lane):
        q = q_ref[0, hd]
        m = None
        acc = None
        for c in range(n_chunks):
            s = _dot(q, kt_ref[0, hd, c])
            v = v_ref[0, hd, c * tk:(c + 1) * tk, :]
            smax = jnp.max(s, axis=1, keepdims=True)
            if c == 0:
                m = smax
                acc = _dot(jnp.exp2((s - m).astype(BF16)), v)
            else:
                m_new = jnp.maximum(m, smax)
                acc = acc * jnp.exp2(m - m_new) + _dot(jnp.exp2((s - m_new).astype(BF16)), v)
                m = m_new
        row_sum = acc[:, ones_lane:ones_lane + 1]
        return acc * (1.0 / row_sum)

    def pair(j, carry):
        even = one_head(2 * j, V_HEAD)
        odd = one_head(2 * j + 1, 0)
        o_ref[0, j] = jnp.where(lane < V_HEAD, even, odd).astype(BF16)
        return carry

    lax.fori_loop(0, N_HEADS // 2, pair, 0)


def _attention(q, kt, v):
    nb, _, seq, _ = q.shape
    nck = kt.shape[2]
    return pl.pallas_call(
        _attn_kernel,
        grid=(nb, seq // Q_TILE),
        in_specs=[
            pl.BlockSpec((1, N_HEADS, Q_TILE, HEAD_PAD), lambda b, i: (b, 0, i, 0)),
            pl.BlockSpec((1, N_HEADS, nck, HEAD_PAD, K_TILE), lambda b, i: (b, 0, 0, 0, 0)),
            pl.BlockSpec((1, N_HEADS, seq, HEAD_PAD), lambda b, i: (b, 0, 0, 0)),
        ],
        out_specs=pl.BlockSpec((1, N_HEADS // 2, Q_TILE, LANES), lambda b, i: (b, 0, i, 0)),
        out_shape=jax.ShapeDtypeStruct((nb, N_HEADS // 2, seq, LANES), BF16),
        compiler_params=_params(("arbitrary", "arbitrary")),
        name="attn",
    )(q, kt, v)


_R_GROUP_ROW = N_EXPERTS


def _postmix_kernel(a_ref, sn_ref, x_ref, mod_ref, gao_ref, wout_ref, gpost_ref, gpre_ref, wr_ref, br_ref, tri_ref,
                    x1_ref, h2_ref, eid_ref, rank_ref, wcol_ref, cnt_ref, run_ref):
    first = jnp.logical_and(pl.program_id(0) == 0, pl.program_id(1) == 0)

    @pl.when(first)
    def _():
        run_ref[...] = jnp.zeros_like(run_ref)

    mod = mod_ref[0]
    gate1, shift2, scale2 = mod[2:3], mod[3:4], mod[4:5]
    a = jnp.concatenate([a_ref[0, j] for j in range(N_HEADS // 2)], axis=1).astype(F32)
    an = (_rms(a) * gao_ref[...]).astype(BF16)
    merged = jnp.concatenate([an, sn_ref[0]], axis=1)
    o = _dot(merged, wout_ref[...])
    x1 = x_ref[0] + gate1 * (_rms(o) * gpost_ref[...])
    x1_ref[0] = x1
    h2 = _rms(x1) * gpre_ref[...] * (1.0 + scale2) + shift2
    n_tok = h2.shape[0]
    _store_token_rows(h2_ref, h2)

    h_hi, h_lo = _split_bf16(h2)
    wr = wr_ref[...]
    hh = _dot(h_hi, wr)
    lh = _dot(h_lo, wr[:, :LANES])
    logits = hh[:, :LANES] + hh[:, LANES:] + lh + br_ref[...]
    lt = logits.T

    neg = jnp.float32(-jnp.inf)
    row8 = lax.broadcasted_iota(jnp.int32, (SUBLANES, n_tok), 0).astype(F32)
    lg = jnp.where(row8 < N_EXPERT_GROUPS, lt[_R_GROUP_ROW:_R_GROUP_ROW + SUBLANES], neg)
    gmax = jnp.max(lg, axis=0, keepdims=True)
    gi = jnp.min(jnp.where(lg == gmax, row8, float(SUBLANES)), axis=0, keepdims=True)
    pg_sel = 1.0 / jnp.sum(jnp.exp(lg - gmax), axis=0, keepdims=True)

    le = jnp.zeros((EXPERTS_PER_GROUP, n_tok), F32)
    for g in range(N_EXPERT_GROUPS):
        le = jnp.where(gi == float(g), lt[g * EXPERTS_PER_GROUP:(g + 1) * EXPERTS_PER_GROUP], le)
    v1 = jnp.max(le, axis=0, keepdims=True)
    i1 = jnp.min(jnp.where(le == v1, row8, float(SUBLANES)), axis=0, keepdims=True)
    le2 = jnp.where(row8 == i1, neg, le)
    v2 = jnp.max(le2, axis=0, keepdims=True)
    i2 = jnp.min(jnp.where(le2 == v2, row8, float(SUBLANES)), axis=0, keepdims=True)
    r = jnp.exp(v2 - v1)
    w1 = pg_sel / (1.0 + r)
    w2 = w1 * r
    e1 = gi * float(EXPERTS_PER_GROUP) + i1
    e2 = gi * float(EXPERTS_PER_GROUP) + i2
    eid_ref[0] = jnp.concatenate([e1, e2], axis=0).astype(jnp.int32)

    row32 = lax.broadcasted_iota(jnp.int32, (N_EXPERTS, n_tok), 0).astype(F32)
    hit1 = row32 == e1
    hit2 = row32 == e2
    onehot = jnp.where(jnp.logical_or(hit1, hit2), 1.0, 0.0)
    run = run_ref[...][:, 0:1]
    ranks1, ranks2 = [], []
    for c in range(n_tok // RANK_CHUNK):
        sl = slice(c * RANK_CHUNK, (c + 1) * RANK_CHUNK)
        oh = onehot[:, sl]
        before = _dot(oh.astype(BF16), tri_ref[...]) + run
        ranks1.append(jnp.sum(jnp.where(hit1[:, sl], before, 0.0), axis=0, keepdims=True))
        ranks2.append(jnp.sum(jnp.where(hit2[:, sl], before, 0.0), axis=0, keepdims=True))
        run = run + jnp.sum(oh, axis=1, keepdims=True)
    rank_ref[0] = jnp.concatenate(
        [jnp.concatenate(ranks1, axis=1), jnp.concatenate(ranks2, axis=1)], axis=0).astype(jnp.int32)
    run_b = jnp.broadcast_to(run, run_ref.shape)
    run_ref[...] = run_b
    cnt_ref[...] = run_b

    row128 = lax.broadcasted_iota(jnp.int32, (LANES, n_tok), 0)
    wt = jnp.where(row128 == 0, w1, jnp.where(row128 == 1, w2, 0.0))
    wcol_ref[0] = wt.T


def _postmix(a, sn, x, mod, P):
    nb, seq, _ = x.shape
    tm = TOKEN_TILE
    const = lambda b, i: (0, 0)
    return pl.pallas_call(
        _postmix_kernel,
        grid=(nb, seq // tm),
        in_specs=[
            pl.BlockSpec((1, N_HEADS // 2, tm, LANES), lambda b, i: (b, 0, i, 0)),
            pl.BlockSpec((1, tm, GMLP_WIDTH), lambda b, i: (b, i, 0)),
            pl.BlockSpec((1, tm, D_MODEL), lambda b, i: (b, i, 0)),
            pl.BlockSpec((1, SUBLANES, D_MODEL), lambda b, i: (b, 0, 0)),
            pl.BlockSpec((1, N_HEADS * V_HEAD), const),
            pl.BlockSpec((D_MODEL, D_MODEL), const),
            pl.BlockSpec((1, D_MODEL), const),
            pl.BlockSpec((1, D_MODEL), const),
            pl.BlockSpec((D_MODEL, 2 * LANES), const),
            pl.BlockSpec((1, LANES), const),
            pl.BlockSpec((RANK_CHUNK, RANK_CHUNK), const),
        ],
        out_specs=[
            pl.BlockSpec((1, tm, D_MODEL), lambda b, i: (b, i, 0)),
            pl.BlockSpec((tm * ROW_TILES, LANES), lambda b, i: (b * (seq // tm) + i, 0)),
            pl.BlockSpec((1, 2, tm), lambda b, i: (b, 0, i)),
            pl.BlockSpec((1, 2, tm), lambda b, i: (b, 0, i)),
            pl.BlockSpec((1, tm, LANES), lambda b, i: (b, i, 0)),
            pl.BlockSpec((N_EXPERTS, LANES), const),
        ],
        out_shape=[
            jax.ShapeDtypeStruct((nb, seq, D_MODEL), F32),
            jax.ShapeDtypeStruct((nb * seq * ROW_TILES, LANES), F32),
            jax.ShapeDtypeStruct((nb, 2, seq), jnp.int32),
            jax.ShapeDtypeStruct((nb, 2, seq), jnp.int32),
            jax.ShapeDtypeStruct((nb, seq, LANES), F32),
            jax.ShapeDtypeStruct((N_EXPERTS, LANES), F32),
        ],
        scratch_shapes=[pltpu.VMEM((N_EXPERTS, LANES), F32)],
        compiler_params=_params(("arbitrary", "arbitrary")),
        name="postmix",
    )(a, sn, x, mod, P["g_attn_out"], P["w_out"], P["g_post1"], P["g_pre2"], P["w_router"], P["b_router"], P["tri"])


def _sc_mesh():
    return plsc.VectorSubcoreMesh(core_axis_name="c", subcore_axis_name="s")


def _sc_worker():
    return lax.axis_index("s") * SC_CORES + lax.axis_index("c")


def _sc_gather_rows(table, idx):
    n = idx.shape[0]
    per_w = n // SC_WORKERS
    steps = per_w // SC_ROWS
    idx3 = idx.reshape(SC_WORKERS, steps, SC_ROWS)

    @functools.partial(
        pl.kernel, mesh=_sc_mesh(),
        out_type=jax.ShapeDtypeStruct((n, ROW_TILES, LANES), F32),
        scratch_types=[pltpu.VMEM((steps, SC_ROWS), jnp.int32), pltpu.VMEM((SC_ROWS, ROW_TILES, LANES), F32),
                       pltpu.SemaphoreType.DMA],
        name="sc_gather_rows",
    )
    def gather(table_hbm, idx_hbm, out_hbm, idx_v, rows_v, sem):
        wid = _sc_worker()
        pltpu.sync_copy(idx_hbm.at[wid], idx_v)

        @pl.loop(0, steps)
        def _(j):
            pltpu.async_copy(table_hbm.at[idx_v.at[j]], rows_v, sem).wait()
            pltpu.sync_copy(rows_v, out_hbm.at[pl.ds(wid * per_w + j * SC_ROWS, SC_ROWS)])

    return gather(table, idx3)


def _sc_scatter_rows(src, idx, n_dst):
    n_dup, n = idx.shape
    per_w = n // SC_WORKERS
    steps = per_w // SC_ROWS
    idx4 = idx.reshape(n_dup, SC_WORKERS, steps, SC_ROWS).transpose(1, 0, 2, 3)

    @functools.partial(
        pl.kernel, mesh=_sc_mesh(),
        out_type=jax.ShapeDtypeStruct((n_dst, ROW_TILES, LANES), F32),
        scratch_types=[pltpu.VMEM((n_dup, steps, SC_ROWS), jnp.int32), pltpu.VMEM((SC_ROWS, ROW_TILES, LANES), F32),
                       pltpu.SemaphoreType.DMA],
        name="sc_scatter_rows",
    )
    def scatter(src_hbm, idx_hbm, dst_hbm, idx_v, rows_v, sem):
        wid = _sc_worker()
        pltpu.sync_copy(idx_hbm.at[wid], idx_v)

        @pl.loop(0, steps)
        def _(j):
            pltpu.sync_copy(src_hbm.at[pl.ds(wid * per_w + j * SC_ROWS, SC_ROWS)], rows_v)
            for k in range(n_dup):
                pltpu.async_copy(rows_v, dst_hbm.at[idx_v.at[k, j]], sem).wait()

    return scatter(src, idx4)


def _expert_kernel(te_ref, nv_ref, xs_ref, wgu_ref, wd_ref, ys_ref):
    del te_ref

    @pl.when(pl.program_id(0) < nv_ref[0])
    def _():
        x = _load_token_rows(xs_ref, EXPERT_TILE).astype(BF16)
        gu = _dot(x, wgu_ref[0])
        g, u = gu[:, :EXPERT_FF], gu[:, EXPERT_FF:]
        act = (g * jax.nn.sigmoid(g) * u).astype(BF16)
        _store_token_rows(ys_ref, _dot(act, wd_ref[0]))

    @pl.when(pl.program_id(0) >= nv_ref[0])
    def _():
        ys_ref[...] = jnp.zeros_like(ys_ref)


def _experts(xs, tile_expert, n_valid, P):
    rows_blk = EXPERT_TILE * ROW_TILES
    n_tiles = xs.shape[0] // rows_blk

    def row_map(i, te, nv):
        return (jnp.minimum(i, nv[0] - 1), 0)

    def out_map(i, te, nv):
        return (i, 0)

    def w_map(i, te, nv):
        return (te[jnp.minimum(i, nv[0] - 1)], 0, 0)

    return pl.pallas_call(
        _expert_kernel,
        grid_spec=pltpu.PrefetchScalarGridSpec(
            num_scalar_prefetch=2,
            grid=(n_tiles,),
            in_specs=[
                pl.BlockSpec((rows_blk, LANES), row_map),
                pl.BlockSpec((1, D_MODEL, 2 * EXPERT_FF), w_map),
                pl.BlockSpec((1, EXPERT_FF, D_MODEL), w_map),
            ],
            out_specs=pl.BlockSpec((rows_blk, LANES), out_map),
        ),
        out_shape=jax.ShapeDtypeStruct(xs.shape, F32),
        compiler_params=_params(("arbitrary",)),
        name="experts",
    )(tile_expert, n_valid, xs, P["w_gu"], P["w_down"])


def _final_kernel(y0_ref, y1_ref, wcol_ref, x1_ref, mod_ref, gpost_ref, o_ref):
    w = wcol_ref[0]
    w0, w1 = w[:, 0:1], w[:, 1:2]
    n_tok = w.shape[0]
    m = w0 * _load_token_rows(y0_ref, n_tok) + w1 * _load_token_rows(y1_ref, n_tok)
    gate2 = mod_ref[0][5:6]
    o_ref[0] = x1_ref[0] + gate2 * (_rms(m) * gpost_ref[...])


def _final(yg, wcol, x1, mod, P):
    nb, seq, _ = x1.shape
    tm = TOKEN_TILE
    nt = seq // tm
    n_tok_tiles = nb * nt
    return pl.pallas_call(
        _final_kernel,
        grid=(nb, nt),
        in_specs=[
            pl.BlockSpec((tm * ROW_TILES, LANES), lambda b, i: (b * nt + i, 0)),
            pl.BlockSpec((tm * ROW_TILES, LANES), lambda b, i: (n_tok_tiles + b * nt + i, 0)),
            pl.BlockSpec((1, tm, LANES), lambda b, i: (b, i, 0)),
            pl.BlockSpec((1, tm, D_MODEL), lambda b, i: (b, i, 0)),
            pl.BlockSpec((1, SUBLANES, D_MODEL), lambda b, i: (b, 0, 0)),
            pl.BlockSpec((1, D_MODEL), lambda b, i: (0, 0)),
        ],
        out_specs=pl.BlockSpec((1, tm, D_MODEL), lambda b, i: (b, i, 0)),
        out_shape=jax.ShapeDtypeStruct((nb, seq, D_MODEL), F32),
        compiler_params=_params(("arbitrary", "arbitrary")),
        name="final",
    )(yg, yg, wcol, x1, mod, P["g_post2"])


def _prepare(w):
    f = lambda a: a.astype(F32)
    P = {}
    for k in ("g_pre1", "g_post1", "g_pre2", "g_post2", "g_q", "g_kv", "g_attn_out", "g_gmlp_out"):
        P[k] = f(w[k]).reshape(1, -1)
    P["g_v"] = f(w["g_v_gmlp"]).reshape(1, -1)

    w_in = f(w["w_in"])
    o0, o1, o2, o3 = Q_LORA, Q_LORA + KV_LORA, Q_LORA + KV_LORA + QK_ROPE, Q_LORA + KV_LORA + QK_ROPE + GMLP_WIDTH
    w_kr = w_in[:, o1:o2]
    kr_partner = jnp.concatenate([-w_kr[:, ROPE_HALF:], w_kr[:, :ROPE_HALF]], axis=1)
    rope_blk = jnp.concatenate([jnp.zeros((D_MODEL, QK_NOPE), F32), w_kr, kr_partner], axis=1)
    P["w_in"] = jnp.concatenate([w_in[:, :o1], rope_blk, w_in[:, o2:o3], w_in[:, o3:]], axis=1).astype(BF16)

    w_uq = f(w["w_uq"]).reshape(Q_LORA, N_HEADS, QK_NOPE + QK_ROPE)
    q_rope = w_uq[:, :, QK_NOPE:]
    q_partner = jnp.concatenate([-q_rope[:, :, ROPE_HALF:], q_rope[:, :, :ROPE_HALF]], axis=2)
    P["w_uq"] = jnp.concatenate([w_uq, q_partner], axis=2).reshape(Q_LORA, N_HEADS * HEAD_PAD).astype(BF16)

    w_ukv = f(w["w_ukv"]).reshape(KV_LORA, N_HEADS, QK_NOPE + V_HEAD)
    zeros = jnp.zeros((KV_LORA, N_HEADS, HEAD_PAD - QK_NOPE), F32)
    w_k = jnp.concatenate([w_ukv[:, :, :QK_NOPE], zeros], axis=2)
    w_v = w_ukv[:, :, QK_NOPE:]
    even = (jnp.arange(N_HEADS) % 2 == 0)[None, :, None]
    zv = jnp.zeros_like(w_v)
    w_v = jnp.concatenate([jnp.where(even, w_v, zv), jnp.where(even, zv, w_v)], axis=2)
    P["w_ukv"] = jnp.concatenate([w_k.reshape(KV_LORA, -1), w_v.reshape(KV_LORA, -1)], axis=1).astype(BF16)

    P["w_sp"] = f(w["w_spatial"]).reshape(GMLP_GROUPS // 2, 2 * CHUNK, CHUNK).astype(BF16)
    P["b_sp"] = jnp.repeat(f(w["b_spatial"]).T, GMLP_GROUP_DIM, axis=1)

    P["w_out"] = f(w["w_out"]).astype(BF16)

    pad = jnp.zeros((D_MODEL, LANES - N_EXPERTS - N_EXPERT_GROUPS), F32)
    wr = jnp.concatenate([f(w["w_router_expert"]), f(w["w_router_group"]), pad], axis=1)
    wr_hi = wr.astype(BF16)
    wr_lo = (wr - wr_hi.astype(F32)).astype(BF16)
    P["w_router"] = jnp.concatenate([wr_hi, wr_lo], axis=1)
    P["b_router"] = jnp.concatenate(
        [f(w["b_router_expert"]), f(w["b_router_group"]), jnp.zeros((LANES - N_EXPERTS - N_EXPERT_GROUPS,), F32)]
    ).reshape(1, LANES)
    P["tri"] = jnp.triu(jnp.ones((RANK_CHUNK, RANK_CHUNK), F32), k=1).astype(BF16)

    P["w_gu"] = jnp.concatenate([f(w["w_gate"]), f(w["w_up"])], axis=2).astype(BF16)
    P["w_down"] = f(w["w_down"]).astype(BF16)
    return P


def _rope_tables(seq):
    inv = ROPE_THETA ** (-jnp.arange(ROPE_HALF, dtype=F32) / ROPE_HALF)
    ang = jnp.arange(seq, dtype=F32)[:, None] * inv[None, :]
    z_lo = jnp.zeros((seq, _ROPE_LO), F32)
    z_hi = jnp.zeros((seq, LANES - _ROPE_LO - QK_ROPE), F32)
    cos = jnp.concatenate([z_lo, jnp.cos(ang), jnp.cos(ang), z_hi], axis=1)
    sin = jnp.concatenate([z_lo, jnp.sin(ang), jnp.sin(ang), z_hi], axis=1)
    return cos, sin


def _layer(x, mod, P):
    nb, seq, _ = x.shape
    n_tok = nb * seq
    q, kt, v, sn = _premix(x, mod, P)
    a = _attention(q, kt, v)
    x1, h2rows, eid, rank, wcol, counts = _postmix(a, sn, x, mod, P)

    cnt = counts[:, 0].astype(jnp.int32)
    padded = ((cnt + EXPERT_TILE - 1) // EXPERT_TILE) * EXPERT_TILE
    ends = jnp.cumsum(padded)
    starts = ends - padded
    eflat = jnp.transpose(eid, (1, 0, 2)).reshape(2, n_tok)
    rflat = jnp.transpose(rank, (1, 0, 2)).reshape(2, n_tok)
    onehot = eflat[:, :, None] == jnp.arange(N_EXPERTS, dtype=jnp.int32)[None, None, :]
    pos = rflat + jnp.sum(jnp.where(onehot, starts[None, None, :], 0), axis=2)
    n_rows = 2 * n_tok + N_EXPERTS * EXPERT_TILE
    n_tiles = n_rows // EXPERT_TILE
    tile_start = jnp.arange(n_tiles, dtype=jnp.int32) * EXPERT_TILE
    tile_expert = jnp.minimum(
        jnp.sum((tile_start[:, None] >= ends[None, :]).astype(jnp.int32), axis=1), N_EXPERTS - 1).astype(jnp.int32)
    n_valid = (ends[-1:] // EXPERT_TILE).astype(jnp.int32)

    as_tiles = lambda a: a.reshape(-1, ROW_TILES, LANES)
    as_rows = lambda a: a.reshape(-1, LANES)
    xs = _sc_scatter_rows(as_tiles(h2rows), pos, n_rows)
    ys = _experts(as_rows(xs), tile_expert, n_valid, P)
    yg = as_rows(_sc_gather_rows(as_tiles(ys), pos.reshape(2 * n_tok)))
    return _final(yg, wcol, x1, mod, P)


def kernel(x_prompt, x_sample, c_prompt, c_sample, w_ada, b_ada, g_pre1, g_post1, g_pre2, g_post2, w_in, g_q, w_uq,
           g_kv, w_ukv, g_v_gmlp, w_spatial, b_spatial, g_attn_out, g_gmlp_out, w_out, w_router_group,
           b_router_group, w_router_expert, b_router_expert, w_gate, w_up, w_down):
    P = _prepare(dict(
        g_pre1=g_pre1, g_post1=g_post1, g_pre2=g_pre2, g_post2=g_post2, w_in=w_in, g_q=g_q, w_uq=w_uq, g_kv=g_kv,
        w_ukv=w_ukv, g_v_gmlp=g_v_gmlp, w_spatial=w_spatial, b_spatial=b_spatial, g_attn_out=g_attn_out,
        g_gmlp_out=g_gmlp_out, w_out=w_out, w_router_group=w_router_group, b_router_group=b_router_group,
        w_router_expert=w_router_expert, b_router_expert=b_router_expert, w_gate=w_gate, w_up=w_up, w_down=w_down))
    P["cos"], P["sin"] = _rope_tables(max(x_prompt.shape[1], x_sample.shape[1]))

    nbp = c_prompt.shape[0]
    c_all = jnp.concatenate([c_prompt, c_sample], axis=0).astype(F32)
    mod = _ada(c_all, w_ada.astype(F32), b_ada.astype(F32))
    mod = mod.reshape(c_all.shape[0], 6, D_MODEL)
    mod = jnp.concatenate([mod, jnp.zeros((c_all.shape[0], SUBLANES - 6, D_MODEL), F32)], axis=1)

    y_prompt = _layer(x_prompt, mod[:nbp], P)
    y_sample = _layer(x_sample, mod[nbp:], P)
    return (y_prompt, y_sample)
```

```python
import functools
import math

import jax
import jax.numpy as jnp
from jax import lax
from jax.experimental import pallas as pl
from jax.experimental.pallas import tpu as pltpu
from jax.experimental.pallas import tpu_sc as plsc

F32 = jnp.float32
BF16 = jnp.bfloat16

D_MODEL = 1024
N_HEADS = 8
QK_NOPE = 64
QK_ROPE = 32
ROPE_HALF = QK_ROPE // 2
V_HEAD = 64
Q_LORA = 256
KV_LORA = 128
GMLP_WIDTH = 512
GMLP_GROUPS = 8
GMLP_GROUP_DIM = 64
CHUNK = 128
N_EXPERTS = 32
N_EXPERT_GROUPS = 4
EXPERTS_PER_GROUP = 8
EXPERT_FF = 256
ROPE_THETA = 10000.0
EPS = 1e-6

LANES = 128
SUBLANES = 8
HEAD_PAD = LANES

TOKEN_TILE = 512
Q_TILE = 1024
K_TILE = 512
EXPERT_TILE = 512
SC_CORES = 2
SC_WORKERS = 32
SC_ROWS = 32
RANK_CHUNK = 256
VMEM_LIMIT = 56 * 1024 * 1024

ROW_TILES = D_MODEL // LANES

_SQRT_2_OVER_PI = math.sqrt(2.0 / math.pi)


def _rms(x):
    return x * lax.rsqrt(jnp.mean(x * x, axis=-1, keepdims=True) + EPS)


def _gelu_tanh(x):
    return 0.5 * x * (1.0 + jnp.tanh(_SQRT_2_OVER_PI * (x + 0.044715 * (x * x * x))))


def _split_bf16(x):
    hi = x.astype(BF16)
    lo = (x - hi.astype(F32)).astype(BF16)
    return hi, lo


def _dot(a, b):
    return jnp.dot(a, b, preferred_element_type=F32)


def _load_token_rows(ref, n):
    return jnp.concatenate([ref[pl.ds(c, n, stride=ROW_TILES), :] for c in range(ROW_TILES)], axis=1)


def _store_token_rows(ref, val):
    n = val.shape[0]
    for c in range(ROW_TILES):
        ref[pl.ds(c, n, stride=ROW_TILES), :] = val[:, c * LANES:(c + 1) * LANES]


def _params(sem, vmem=VMEM_LIMIT):
    return pltpu.CompilerParams(dimension_semantics=sem, vmem_limit_bytes=vmem)


def _ada_kernel(c_ref, w_ref, b_ref, o_ref):
    c = c_ref[...]
    a = c * jax.nn.sigmoid(c)
    a_hi, a_lo = _split_bf16(a)
    w_hi, w_lo = _split_bf16(w_ref[...])
    o_ref[...] = _dot(a_hi, w_hi) + _dot(a_hi, w_lo) + _dot(a_lo, w_hi) + b_ref[...]


def _ada(c, w_ada, b_ada):
    nb = c.shape[0]
    n_out = w_ada.shape[1]
    blk = D_MODEL
    return pl.pallas_call(
        _ada_kernel,
        grid=(n_out // blk,),
        in_specs=[
            pl.BlockSpec((nb, D_MODEL), lambda j: (0, 0)),
            pl.BlockSpec((D_MODEL, blk), lambda j: (0, j)),
            pl.BlockSpec((1, blk), lambda j: (0, j)),
        ],
        out_specs=pl.BlockSpec((nb, blk), lambda j: (0, j)),
        out_shape=jax.ShapeDtypeStruct((nb, n_out), F32),
        compiler_params=_params(("arbitrary",)),
        name="ada",
    )(c, w_ada, b_ada.reshape(1, n_out))


_C_CQ = 0
_C_CKV = _C_CQ + Q_LORA
_C_KR = _C_CKV + KV_LORA
_C_U = _C_KR + LANES
_C_V = _C_U + GMLP_WIDTH
_C_END = _C_V + GMLP_WIDTH
_ROPE_LO = QK_NOPE
_ROLL_PARTNER = LANES - QK_ROPE


def _premix_kernel(x_ref, mod_ref, cos_ref, sin_ref, gpre_ref, win_ref, gq_ref, wuq_ref, gkv_ref, wukv_ref,
                   gv_ref, wsp_ref, bsp_ref, ggo_ref, q_ref, kt_ref, v_ref, sn_ref):
    x = x_ref[0]
    mod = mod_ref[0]
    shift1, scale1 = mod[0:1], mod[1:2]
    h = _rms(x) * gpre_ref[...] * (1.0 + scale1) + shift1
    z = _dot(h.astype(BF16), win_ref[...])

    cosb = cos_ref[...]
    sinb = sin_ref[...]
    lane = lax.broadcasted_iota(jnp.int32, (1, LANES), 1)
    nope_mask = jnp.where(lane < QK_NOPE, 1.0, 0.0).astype(F32)

    qscale = (QK_NOPE + QK_ROPE) ** -0.5 * math.log2(math.e)
    cq_tab = (nope_mask + cosb) * qscale
    sq_tab = sinb * qscale
    cqn = (_rms(z[:, _C_CQ:_C_CKV]) * gq_ref[...]).astype(BF16)
    qb = _dot(cqn, wuq_ref[...])
    for hd in range(N_HEADS):
        blk = qb[:, hd * HEAD_PAD:(hd + 1) * HEAD_PAD]
        qh = blk * cq_tab + pltpu.roll(blk, _ROLL_PARTNER, 1) * sq_tab
        q_ref[0, hd] = qh.astype(BF16)

    ckvn = (_rms(z[:, _C_CKV:_C_KR]) * gkv_ref[...]).astype(BF16)
    kvb = _dot(ckvn, wukv_ref[...])
    krb = z[:, _C_KR:_C_U]
    krope = krb * cosb + pltpu.roll(krb, _ROLL_PARTNER, 1) * sinb
    v_off = N_HEADS * HEAD_PAD
    for hd in range(N_HEADS):
        kh = kvb[:, hd * HEAD_PAD:(hd + 1) * HEAD_PAD] + krope
        kt_ref[0, hd, 0] = kh.T.astype(BF16)
        ones_lane = V_HEAD if hd % 2 == 0 else 0
        vh = kvb[:, v_off + hd * HEAD_PAD:v_off + (hd + 1) * HEAD_PAD] + jnp.where(lane == ones_lane, 1.0, 0.0)
        v_ref[0, hd] = vh.astype(BF16)

    ua = _gelu_tanh(z[:, _C_U:_C_V])
    vn = (_rms(_gelu_tanh(z[:, _C_V:_C_END])) * gv_ref[...]).astype(BF16)
    n_tok = x.shape[0]
    bsp = bsp_ref[...]
    rows = []
    for n in range(n_tok // CHUNK):
        cols = []
        for j in range(GMLP_GROUPS // 2):
            rhs = vn[n * CHUNK:(n + 1) * CHUNK, j * LANES:(j + 1) * LANES]
            ab = _dot(wsp_ref[j], rhs)
            cols.append(jnp.where(lane < GMLP_GROUP_DIM, ab[:CHUNK], ab[CHUNK:]))
        rows.append(jnp.concatenate(cols, axis=1) + bsp)
    s = ua * jnp.concatenate(rows, axis=0)
    sn_ref[0] = (_rms(s) * ggo_ref[...]).astype(BF16)


def _premix(x, mod, P):
    nb, seq, _ = x.shape
    tm = TOKEN_TILE
    nck = seq // K_TILE
    const = lambda i, b: (0, 0)
    return pl.pallas_call(
        _premix_kernel,
        grid=(seq // tm, nb),
        in_specs=[
            pl.BlockSpec((1, tm, D_MODEL), lambda i, b: (b, i, 0)),
            pl.BlockSpec((1, SUBLANES, D_MODEL), lambda i, b: (b, 0, 0)),
            pl.BlockSpec((tm, LANES), lambda i, b: (i, 0)),
            pl.BlockSpec((tm, LANES), lambda i, b: (i, 0)),
            pl.BlockSpec((1, D_MODEL), const),
            pl.BlockSpec((D_MODEL, _C_END), const),
            pl.BlockSpec((1, Q_LORA), const),
            pl.BlockSpec((Q_LORA, N_HEADS * HEAD_PAD), const),
            pl.BlockSpec((1, KV_LORA), const),
            pl.BlockSpec((KV_LORA, 2 * N_HEADS * HEAD_PAD), const),
            pl.BlockSpec((1, GMLP_WIDTH), const),
            pl.BlockSpec((GMLP_GROUPS // 2, 2 * CHUNK, CHUNK), lambda i, b: (0, 0, 0)),
            pl.BlockSpec((CHUNK, GMLP_WIDTH), const),
            pl.BlockSpec((1, GMLP_WIDTH), const),
        ],
        out_specs=[
            pl.BlockSpec((1, N_HEADS, tm, HEAD_PAD), lambda i, b: (b, 0, i, 0)),
            pl.BlockSpec((1, N_HEADS, 1, HEAD_PAD, K_TILE), lambda i, b: (b, 0, i, 0, 0)),
            pl.BlockSpec((1, N_HEADS, tm, HEAD_PAD), lambda i, b: (b, 0, i, 0)),
            pl.BlockSpec((1, tm, GMLP_WIDTH), lambda i, b: (b, i, 0)),
        ],
        out_shape=[
            jax.ShapeDtypeStruct((nb, N_HEADS, seq, HEAD_PAD), BF16),
            jax.ShapeDtypeStruct((nb, N_HEADS, nck, HEAD_PAD, K_TILE), BF16),
            jax.ShapeDtypeStruct((nb, N_HEADS, seq, HEAD_PAD), BF16),
            jax.ShapeDtypeStruct((nb, seq, GMLP_WIDTH), BF16),
        ],
        compiler_params=_params(("arbitrary", "arbitrary")),
        name="premix",
    )(x, mod, P["cos"][:seq], P["sin"][:seq], P["g_pre1"], P["w_in"], P["g_q"], P["w_uq"], P["g_kv"], P["w_ukv"],
      P["g_v"], P["w_sp"], P["b_sp"], P["g_gmlp_out"])


def _attn_kernel(q_ref, kt_ref, v_ref, o_ref):
    n_chunks = kt_ref.shape[2]
    tk = kt_ref.shape[4]
    lane = lax.broadcasted_iota(jnp.int32, (1, LANES), 1)

    def one_head(hd, ones_lane):
        q = q_ref[0, hd]
        m = None
        acc = None
        for c in range(n_chunks):
            s = _dot(q, kt_ref[0, hd, c])
            v = v_ref[0, hd, c * tk:(c + 1) * tk, :]
            smax = jnp.max(s, axis=1, keepdims=True)
            if c == 0:
                m = smax
                acc = _dot(jnp.exp2((s - m).astype(BF16)), v)
            else:
                m_new = jnp.maximum(m, smax)
                acc = acc * jnp.exp2(m - m_new) + _dot(jnp.exp2((s - m_new).astype(BF16)), v)
                m = m_new
        row_sum = acc[:, ones_lane:ones_lane + 1]
        return acc * (1.0 / row_sum)

    def pair(j, carry):
        even = one_head(2 * j, V_HEAD)
        odd = one_head(2 * j + 1, 0)
        o_ref[0, j] = jnp.where(lane < V_HEAD, even, odd).astype(BF16)
        return carry

    lax.fori_loop(0, N_HEADS // 2, pair, 0)


def _attention(q, kt, v):
    nb, _, seq, _ = q.shape
    nck = kt.shape[2]
    return pl.pallas_call(
        _attn_kernel,
        grid=(nb, seq // Q_TILE),
        in_specs=[
            pl.BlockSpec((1, N_HEADS, Q_TILE, HEAD_PAD), lambda b, i: (b, 0, i, 0)),
            pl.BlockSpec((1, N_HEADS, nck, HEAD_PAD, K_TILE), lambda b, i: (b, 0, 0, 0, 0)),
            pl.BlockSpec((1, N_HEADS, seq, HEAD_PAD), lambda b, i: (b, 0, 0, 0)),
        ],
        out_specs=pl.BlockSpec((1, N_HEADS // 2, Q_TILE, LANES), lambda b, i: (b, 0, i, 0)),
        out_shape=jax.ShapeDtypeStruct((nb, N_HEADS // 2, seq, LANES), BF16),
        compiler_params=_params(("arbitrary", "arbitrary")),
        name="attn",
    )(q, kt, v)


_R_GROUP_ROW = N_EXPERTS


def _postmix_kernel(a_ref, sn_ref, x_ref, mod_ref, gao_ref, wout_ref, gpost_ref, gpre_ref, wr_ref, br_ref, tri_ref,
                    x1_ref, h2_ref, eid_ref, rank_ref, wcol_ref, cnt_ref, run_ref):
    first = jnp.logical_and(pl.program_id(0) == 0, pl.program_id(1) == 0)

    @pl.when(first)
    def _():
        run_ref[...] = jnp.zeros_like(run_ref)

    mod = mod_ref[0]
    gate1, shift2, scale2 = mod[2:3], mod[3:4], mod[4:5]
    a = jnp.concatenate([a_ref[0, j] for j in range(N_HEADS // 2)], axis=1).astype(F32)
    an = (_rms(a) * gao_ref[...]).astype(BF16)
    merged = jnp.concatenate([an, sn_ref[0]], axis=1)
    o = _dot(merged, wout_ref[...])
    x1 = x_ref[0] + gate1 * (_rms(o) * gpost_ref[...])
    x1_ref[0] = x1
    h2 = _rms(x1) * gpre_ref[...] * (1.0 + scale2) + shift2
    n_tok = h2.shape[0]
    _store_token_rows(h2_ref, h2)

    h_hi, h_lo = _split_bf16(h2)
    wr = wr_ref[...]
    hh = _dot(h_hi, wr)
    lh = _dot(h_lo, wr[:, :LANES])
    logits = hh[:, :LANES] + hh[:, LANES:] + lh + br_ref[...]
    lt = logits.T

    neg = jnp.float32(-jnp.inf)
    row8 = lax.broadcasted_iota(jnp.int32, (SUBLANES, n_tok), 0).astype(F32)
    lg = jnp.where(row8 < N_EXPERT_GROUPS, lt[_R_GROUP_ROW:_R_GROUP_ROW + SUBLANES], neg)
    gmax = jnp.max(lg, axis=0, keepdims=True)
    gi = jnp.min(jnp.where(lg == gmax, row8, float(SUBLANES)), axis=0, keepdims=True)
    pg_sel = 1.0 / jnp.sum(jnp.exp(lg - gmax), axis=0, keepdims=True)

    le = jnp.zeros((EXPERTS_PER_GROUP, n_tok), F32)
    for g in range(N_EXPERT_GROUPS):
        le = jnp.where(gi == float(g), lt[g * EXPERTS_PER_GROUP:(g + 1) * EXPERTS_PER_GROUP], le)
    v1 = jnp.max(le, axis=0, keepdims=True)
    i1 = jnp.min(jnp.where(le == v1, row8, float(SUBLANES)), axis=0, keepdims=True)
    le2 = jnp.where(row8 == i1, neg, le)
    v2 = jnp.max(le2, axis=0, keepdims=True)
    i2 = jnp.min(jnp.where(le2 == v2, row8, float(SUBLANES)), axis=0, keepdims=True)
    r = jnp.exp(v2 - v1)
    w1 = pg_sel / (1.0 + r)
    w2 = w1 * r
    e1 = gi * float(EXPERTS_PER_GROUP) + i1
    e2 = gi * float(EXPERTS_PER_GROUP) + i2
    eid_ref[0] = jnp.concatenate([e1, e2], axis=0).astype(jnp.int32)

    row32 = lax.broadcasted_iota(jnp.int32, (N_EXPERTS, n_tok), 0).astype(F32)
    hit1 = row32 == e1
    hit2 = row32 == e2
    onehot = jnp.where(jnp.logical_or(hit1, hit2), 1.0, 0.0)
    run = run_ref[...][:, 0:1]
    ranks1, ranks2 = [], []
    for c in range(n_tok // RANK_CHUNK):
        sl = slice(c * RANK_CHUNK, (c + 1) * RANK_CHUNK)
        oh = onehot[:, sl]
        before = _dot(oh.astype(BF16), tri_ref[...]) + run
        ranks1.append(jnp.sum(jnp.where(hit1[:, sl], before, 0.0), axis=0, keepdims=True))
        ranks2.append(jnp.sum(jnp.where(hit2[:, sl], before, 0.0), axis=0, keepdims=True))
        run = run + jnp.sum(oh, axis=1, keepdims=True)
    rank_ref[0] = jnp.concatenate(
        [jnp.concatenate(ranks1, axis=1), jnp.concatenate(ranks2, axis=1)], axis=0).astype(jnp.int32)
    run_b = jnp.broadcast_to(run, run_ref.shape)
    run_ref[...] = run_b
    cnt_ref[...] = run_b

    row128 = lax.broadcasted_iota(jnp.int32, (LANES, n_tok), 0)
    wt = jnp.where(row128 == 0, w1, jnp.where(row128 == 1, w2, 0.0))
    wcol_ref[0] = wt.T


def _postmix(a, sn, x, mod, P):
    nb, seq, _ = x.shape
    tm = TOKEN_TILE
    const = lambda b, i: (0, 0)
    return pl.pallas_call(
        _postmix_kernel,
        grid=(nb, seq // tm),
        in_specs=[
            pl.BlockSpec((1, N_HEADS // 2, tm, LANES), lambda b, i: (b, 0, i, 0)),
            pl.BlockSpec((1, tm, GMLP_WIDTH), lambda b, i: (b, i, 0)),
            pl.BlockSpec((1, tm, D_MODEL), lambda b, i: (b, i, 0)),
            pl.BlockSpec((1, SUBLANES, D_MODEL), lambda b, i: (b, 0, 0)),
            pl.BlockSpec((1, N_HEADS * V_HEAD), const),
            pl.BlockSpec((D_MODEL, D_MODEL), const),
            pl.BlockSpec((1, D_MODEL), const),
            pl.BlockSpec((1, D_MODEL), const),
            pl.BlockSpec((D_MODEL, 2 * LANES), const),
            pl.BlockSpec((1, LANES), const),
            pl.BlockSpec((RANK_CHUNK, RANK_CHUNK), const),
        ],
        out_specs=[
            pl.BlockSpec((1, tm, D_MODEL), lambda b, i: (b, i, 0)),
            pl.BlockSpec((tm * ROW_TILES, LANES), lambda b, i: (b * (seq // tm) + i, 0)),
            pl.BlockSpec((1, 2, tm), lambda b, i: (b, 0, i)),
            pl.BlockSpec((1, 2, tm), lambda b, i: (b, 0, i)),
            pl.BlockSpec((1, tm, LANES), lambda b, i: (b, i, 0)),
            pl.BlockSpec((N_EXPERTS, LANES), const),
        ],
        out_shape=[
            jax.ShapeDtypeStruct((nb, seq, D_MODEL), F32),
            jax.ShapeDtypeStruct((nb * seq * ROW_TILES, LANES), F32),
            jax.ShapeDtypeStruct((nb, 2, seq), jnp.int32),
            jax.ShapeDtypeStruct((nb, 2, seq), jnp.int32),
            jax.ShapeDtypeStruct((nb, seq, LANES), F32),
            jax.ShapeDtypeStruct((N_EXPERTS, LANES), F32),
        ],
        scratch_shapes=[pltpu.VMEM((N_EXPERTS, LANES), F32)],
        compiler_params=_params(("arbitrary", "arbitrary")),
        name="postmix",
    )(a, sn, x, mod, P["g_attn_out"], P["w_out"], P["g_post1"], P["g_pre2"], P["w_router"], P["b_router"], P["tri"])


def _sc_mesh():
    return plsc.VectorSubcoreMesh(core_axis_name="c", subcore_axis_name="s")


def _sc_worker():
    return lax.axis_index("s") * SC_CORES + lax.axis_index("c")


def _sc_gather_rows(table, idx):
    n = idx.shape[0]
    per_w = n // SC_WORKERS
    steps = per_w // SC_ROWS
    idx3 = idx.reshape(SC_WORKERS, steps, SC_ROWS)

    @functools.partial(
        pl.kernel, mesh=_sc_mesh(),
        out_type=jax.ShapeDtypeStruct((n, ROW_TILES, LANES), F32),
        scratch_types=[pltpu.VMEM((steps, SC_ROWS), jnp.int32), pltpu.VMEM((SC_ROWS, ROW_TILES, LANES), F32),
                       pltpu.SemaphoreType.DMA],
        name="sc_gather_rows",
    )
    def gather(table_hbm, idx_hbm, out_hbm, idx_v, rows_v, sem):
        wid = _sc_worker()
        pltpu.sync_copy(idx_hbm.at[wid], idx_v)

        @pl.loop(0, steps)
        def _(j):
            pltpu.async_copy(table_hbm.at[idx_v.at[j]], rows_v, sem).wait()
            pltpu.sync_copy(rows_v, out_hbm.at[pl.ds(wid * per_w + j * SC_ROWS, SC_ROWS)])

    return gather(table, idx3)


def _sc_scatter_rows(src, idx, n_dst):
    n_dup, n = idx.shape
    per_w = n // SC_WORKERS
    steps = per_w // SC_ROWS
    idx4 = idx.reshape(n_dup, SC_WORKERS, steps, SC_ROWS).transpose(1, 0, 2, 3)

    @functools.partial(
        pl.kernel, mesh=_sc_mesh(),
        out_type=jax.ShapeDtypeStruct((n_dst, ROW_TILES, LANES), F32),
        scratch_types=[pltpu.VMEM((n_dup, steps, SC_ROWS), jnp.int32), pltpu.VMEM((SC_ROWS, ROW_TILES, LANES), F32),
                       pltpu.SemaphoreType.DMA],
        name="sc_scatter_rows",
    )
    def scatter(src_hbm, idx_hbm, dst_hbm, idx_v, rows_v, sem):
        wid = _sc_worker()
        pltpu.sync_copy(idx_hbm.at[wid], idx_v)

        @pl.loop(0, steps)
        def _(j):
            pltpu.sync_copy(src_hbm.at[pl.ds(wid * per_w + j * SC_ROWS, SC_ROWS)], rows_v)
            for k in range(n_dup):
                pltpu.async_copy(rows_v, dst_hbm.at[idx_v.at[k, j]], sem).wait()

    return scatter(src, idx4)


def _expert_kernel(te_ref, nv_ref, xs_ref, wg_ref, wu_ref, wd_ref, ys_ref, wgu_bf, wd_bf):
    i = pl.program_id(0)
    valid = i < nv_ref[0]
    new_expert = jnp.logical_or(i == 0, te_ref[i] != te_ref[jnp.maximum(i - 1, 0)])

    @pl.when(jnp.logical_and(valid, new_expert))
    def _():
        wgu_bf[:, :EXPERT_FF] = wg_ref[0].astype(BF16)
        wgu_bf[:, EXPERT_FF:] = wu_ref[0].astype(BF16)
        wd_bf[...] = wd_ref[0].astype(BF16)

    @pl.when(valid)
    def _():
        x = _load_token_rows(xs_ref, EXPERT_TILE).astype(BF16)
        gu = _dot(x, wgu_bf[...])
        g, u = gu[:, :EXPERT_FF], gu[:, EXPERT_FF:]
        act = (g * jax.nn.sigmoid(g) * u).astype(BF16)
        _store_token_rows(ys_ref, _dot(act, wd_bf[...]))

    @pl.when(jnp.logical_not(valid))
    def _():
        ys_ref[...] = jnp.zeros_like(ys_ref)


def _experts(xs, tile_expert, n_valid, P):
    rows_blk = EXPERT_TILE * ROW_TILES
    n_tiles = xs.shape[0] // rows_blk

    def row_map(i, te, nv):
        return (jnp.minimum(i, nv[0] - 1), 0)

    def out_map(i, te, nv):
        return (i, 0)

    def w_map(i, te, nv):
        return (te[jnp.minimum(i, nv[0] - 1)], 0, 0)

    return pl.pallas_call(
        _expert_kernel,
        grid_spec=pltpu.PrefetchScalarGridSpec(
            num_scalar_prefetch=2,
            grid=(n_tiles,),
            in_specs=[
                pl.BlockSpec((rows_blk, LANES), row_map),
                pl.BlockSpec((1, D_MODEL, EXPERT_FF), w_map),
                pl.BlockSpec((1, D_MODEL, EXPERT_FF), w_map),
                pl.BlockSpec((1, EXPERT_FF, D_MODEL), w_map),
            ],
            out_specs=pl.BlockSpec((rows_blk, LANES), out_map),
            scratch_shapes=[pltpu.VMEM((D_MODEL, 2 * EXPERT_FF), BF16), pltpu.VMEM((EXPERT_FF, D_MODEL), BF16)],
        ),
        out_shape=jax.ShapeDtypeStruct(xs.shape, F32),
        compiler_params=_params(("arbitrary",)),
        name="experts",
    )(tile_expert, n_valid, xs, P["w_gate"], P["w_up"], P["w_down"])


def _final_kernel(y0_ref, y1_ref, wcol_ref, x1_ref, mod_ref, gpost_ref, o_ref):
    w = wcol_ref[0]
    w0, w1 = w[:, 0:1], w[:, 1:2]
    n_tok = w.shape[0]
    m = w0 * _load_token_rows(y0_ref, n_tok) + w1 * _load_token_rows(y1_ref, n_tok)
    gate2 = mod_ref[0][5:6]
    o_ref[0] = x1_ref[0] + gate2 * (_rms(m) * gpost_ref[...])


def _final(yg, wcol, x1, mod, P):
    nb, seq, _ = x1.shape
    tm = TOKEN_TILE
    nt = seq // tm
    n_tok_tiles = nb * nt
    return pl.pallas_call(
        _final_kernel,
        grid=(nb, nt),
        in_specs=[
            pl.BlockSpec((tm * ROW_TILES, LANES), lambda b, i: (b * nt + i, 0)),
            pl.BlockSpec((tm * ROW_TILES, LANES), lambda b, i: (n_tok_tiles + b * nt + i, 0)),
            pl.BlockSpec((1, tm, LANES), lambda b, i: (b, i, 0)),
            pl.BlockSpec((1, tm, D_MODEL), lambda b, i: (b, i, 0)),
            pl.BlockSpec((1, SUBLANES, D_MODEL), lambda b, i: (b, 0, 0)),
            pl.BlockSpec((1, D_MODEL), lambda b, i: (0, 0)),
        ],
        out_specs=pl.BlockSpec((1, tm, D_MODEL), lambda b, i: (b, i, 0)),
        out_shape=jax.ShapeDtypeStruct((nb, seq, D_MODEL), F32),
        compiler_params=_params(("arbitrary", "arbitrary")),
        name="final",
    )(yg, yg, wcol, x1, mod, P["g_post2"])


def _prepare(w):
    f = lambda a: a.astype(F32)
    P = {}
    for k in ("g_pre1", "g_post1", "g_pre2", "g_post2", "g_q", "g_kv", "g_attn_out", "g_gmlp_out"):
        P[k] = f(w[k]).reshape(1, -1)
    P["g_v"] = f(w["g_v_gmlp"]).reshape(1, -1)

    w_in = f(w["w_in"])
    o0, o1, o2, o3 = Q_LORA, Q_LORA + KV_LORA, Q_LORA + KV_LORA + QK_ROPE, Q_LORA + KV_LORA + QK_ROPE + GMLP_WIDTH
    w_kr = w_in[:, o1:o2]
    kr_partner = jnp.concatenate([-w_kr[:, ROPE_HALF:], w_kr[:, :ROPE_HALF]], axis=1)
    rope_blk = jnp.concatenate([jnp.zeros((D_MODEL, QK_NOPE), F32), w_kr, kr_partner], axis=1)
    P["w_in"] = jnp.concatenate([w_in[:, :o1], rope_blk, w_in[:, o2:o3], w_in[:, o3:]], axis=1).astype(BF16)

    w_uq = f(w["w_uq"]).reshape(Q_LORA, N_HEADS, QK_NOPE + QK_ROPE)
    q_rope = w_uq[:, :, QK_NOPE:]
    q_partner = jnp.concatenate([-q_rope[:, :, ROPE_HALF:], q_rope[:, :, :ROPE_HALF]], axis=2)
    P["w_uq"] = jnp.concatenate([w_uq, q_partner], axis=2).reshape(Q_LORA, N_HEADS * HEAD_PAD).astype(BF16)

    w_ukv = f(w["w_ukv"]).reshape(KV_LORA, N_HEADS, QK_NOPE + V_HEAD)
    zeros = jnp.zeros((KV_LORA, N_HEADS, HEAD_PAD - QK_NOPE), F32)
    w_k = jnp.concatenate([w_ukv[:, :, :QK_NOPE], zeros], axis=2)
    w_v = w_ukv[:, :, QK_NOPE:]
    even = (jnp.arange(N_HEADS) % 2 == 0)[None, :, None]
    zv = jnp.zeros_like(w_v)
    w_v = jnp.concatenate([jnp.where(even, w_v, zv), jnp.where(even, zv, w_v)], axis=2)
    P["w_ukv"] = jnp.concatenate([w_k.reshape(KV_LORA, -1), w_v.reshape(KV_LORA, -1)], axis=1).astype(BF16)

    P["w_sp"] = f(w["w_spatial"]).reshape(GMLP_GROUPS // 2, 2 * CHUNK, CHUNK).astype(BF16)
    P["b_sp"] = jnp.repeat(f(w["b_spatial"]).T, GMLP_GROUP_DIM, axis=1)

    P["w_out"] = f(w["w_out"]).astype(BF16)

    pad = jnp.zeros((D_MODEL, LANES - N_EXPERTS - N_EXPERT_GROUPS), F32)
    wr = jnp.concatenate([f(w["w_router_expert"]), f(w["w_router_group"]), pad], axis=1)
    wr_hi = wr.astype(BF16)
    wr_lo = (wr - wr_hi.astype(F32)).astype(BF16)
    P["w_router"] = jnp.concatenate([wr_hi, wr_lo], axis=1)
    P["b_router"] = jnp.concatenate(
        [f(w["b_router_expert"]), f(w["b_router_group"]), jnp.zeros((LANES - N_EXPERTS - N_EXPERT_GROUPS,), F32)]
    ).reshape(1, LANES)
    P["tri"] = jnp.triu(jnp.ones((RANK_CHUNK, RANK_CHUNK), F32), k=1).astype(BF16)

    P["w_gate"], P["w_up"], P["w_down"] = f(w["w_gate"]), f(w["w_up"]), f(w["w_down"])
    return P


def _rope_tables(seq):
    inv = ROPE_THETA ** (-jnp.arange(ROPE_HALF, dtype=F32) / ROPE_HALF)
    ang = jnp.arange(seq, dtype=F32)[:, None] * inv[None, :]
    z_lo = jnp.zeros((seq, _ROPE_LO), F32)
    z_hi = jnp.zeros((seq, LANES - _ROPE_LO - QK_ROPE), F32)
    cos = jnp.concatenate([z_lo, jnp.cos(ang), jnp.cos(ang), z_hi], axis=1)
    sin = jnp.concatenate([z_lo, jnp.sin(ang), jnp.sin(ang), z_hi], axis=1)
    return cos, sin


def _layer(x, mod, P):
    nb, seq, _ = x.shape
    n_tok = nb * seq
    q, kt, v, sn = _premix(x, mod, P)
    a = _attention(q, kt, v)
    x1, h2rows, eid, rank, wcol, counts = _postmix(a, sn, x, mod, P)

    cnt = counts[:, 0].astype(jnp.int32)
    padded = ((cnt + EXPERT_TILE - 1) // EXPERT_TILE) * EXPERT_TILE
    ends = jnp.cumsum(padded)
    starts = ends - padded
    eflat = jnp.transpose(eid, (1, 0, 2)).reshape(2, n_tok)
    rflat = jnp.transpose(rank, (1, 0, 2)).reshape(2, n_tok)
    onehot = eflat[:, :, None] == jnp.arange(N_EXPERTS, dtype=jnp.int32)[None, None, :]
    pos = rflat + jnp.sum(jnp.where(onehot, starts[None, None, :], 0), axis=2)
    n_rows = 2 * n_tok + N_EXPERTS * EXPERT_TILE
    n_tiles = n_rows // EXPERT_TILE
    tile_start = jnp.arange(n_tiles, dtype=jnp.int32) * EXPERT_TILE
    tile_expert = jnp.minimum(
        jnp.sum((tile_start[:, None] >= ends[None, :]).astype(jnp.int32), axis=1), N_EXPERTS - 1).astype(jnp.int32)
    n_valid = (ends[-1:] // EXPERT_TILE).astype(jnp.int32)

    as_tiles = lambda a: a.reshape(-1, ROW_TILES, LANES)
    as_rows = lambda a: a.reshape(-1, LANES)
    xs = _sc_scatter_rows(as_tiles(h2rows), pos, n_rows)
    ys = _experts(as_rows(xs), tile_expert, n_valid, P)
    yg = as_rows(_sc_gather_rows(as_tiles(ys), pos.reshape(2 * n_tok)))
    return _final(yg, wcol, x1, mod, P)


def kernel(x_prompt, x_sample, c_prompt, c_sample, w_ada, b_ada, g_pre1, g_post1, g_pre2, g_post2, w_in, g_q, w_uq,
           g_kv, w_ukv, g_v_gmlp, w_spatial, b_spatial, g_attn_out, g_gmlp_out, w_out, w_router_group,
           b_router_group, w_router_expert, b_router_expert, w_gate, w_up, w_down):
    P = _prepare(dict(
        g_pre1=g_pre1, g_post1=g_post1, g_pre2=g_pre2, g_post2=g_post2, w_in=w_in, g_q=g_q, w_uq=w_uq, g_kv=g_kv,
        w_ukv=w_ukv, g_v_gmlp=g_v_gmlp, w_spatial=w_spatial, b_spatial=b_spatial, g_attn_out=g_attn_out,
        g_gmlp_out=g_gmlp_out, w_out=w_out, w_router_group=w_router_group, b_router_group=b_router_group,
        w_router_expert=w_router_expert, b_router_expert=b_router_expert, w_gate=w_gate, w_up=w_up, w_down=w_down))
    P["cos"], P["sin"] = _rope_tables(max(x_prompt.shape[1], x_sample.shape[1]))

    nbp = c_prompt.shape[0]
    c_all = jnp.concatenate([c_prompt, c_sample], axis=0).astype(F32)
    mod = _ada(c_all, w_ada.astype(F32), b_ada.astype(F32))
    mod = mod.reshape(c_all.shape[0], 6, D_MODEL)
    mod = jnp.concatenate([mod, jnp.zeros((c_all.shape[0], SUBLANES - 6, D_MODEL), F32)], axis=1)

    y_prompt = _layer(x_prompt, mod[:nbp], P)
    y_sample = _layer(x_sample, mod[nbp:], P)
    return (y_prompt, y_sample)
```

```python
import functools
import math

import jax
import jax.numpy as jnp
from jax import lax
from jax.experimental import pallas as pl
from jax.experimental.pallas import tpu as pltpu
from jax.experimental.pallas import tpu_sc as plsc

F32 = jnp.float32
BF16 = jnp.bfloat16

D_MODEL = 1024
N_HEADS = 8
QK_NOPE = 64
QK_ROPE = 32
ROPE_HALF = QK_ROPE // 2
V_HEAD = 64
Q_LORA = 256
KV_LORA = 128
GMLP_WIDTH = 512
GMLP_GROUPS = 8
GMLP_GROUP_DIM = 64
CHUNK = 128
N_EXPERTS = 32
N_EXPERT_GROUPS = 4
EXPERTS_PER_GROUP = 8
EXPERT_FF = 256
ROPE_THETA = 10000.0
EPS = 1e-6

LANES = 128
SUBLANES = 8
HEAD_PAD = LANES

TOKEN_TILE = 512
Q_TILE = 1024
K_TILE = 512
EXPERT_TILE = 512
SC_CORES = 2
SC_WORKERS = 32
SC_ROWS = 64
RANK_CHUNK = 256
VMEM_LIMIT = 56 * 1024 * 1024

U32 = jnp.uint32
PACKED_WIDTH = D_MODEL // 2
ROW_TILES = PACKED_WIDTH // LANES
_HI_MASK = 0xFFFF0000

_SQRT_2_OVER_PI = math.sqrt(2.0 / math.pi)


def _rms(x):
    return x * lax.rsqrt(jnp.mean(x * x, axis=-1, keepdims=True) + EPS)


def _gelu_tanh(x):
    return 0.5 * x * (1.0 + jnp.tanh(_SQRT_2_OVER_PI * (x + 0.044715 * (x * x * x))))


def _split_bf16(x):
    hi = x.astype(BF16)
    lo = (x - hi.astype(F32)).astype(BF16)
    return hi, lo


def _dot(a, b):
    return jnp.dot(a, b, preferred_element_type=F32)


def _bf16_bits(x):
    return lax.bitcast_convert_type(x.astype(BF16).astype(F32), U32)


def _load_token_rows(ref, n):
    w = jnp.concatenate([ref[pl.ds(c, n, stride=ROW_TILES), :] for c in range(ROW_TILES)], axis=1)
    lo = lax.bitcast_convert_type(w << 16, F32)
    hi = lax.bitcast_convert_type(w & jnp.uint32(_HI_MASK), F32)
    return jnp.concatenate([lo, hi], axis=1)


def _store_token_rows(ref, val):
    n = val.shape[0]
    w = (_bf16_bits(val[:, :PACKED_WIDTH]) >> 16) | (_bf16_bits(val[:, PACKED_WIDTH:]) & jnp.uint32(_HI_MASK))
    for c in range(ROW_TILES):
        ref[pl.ds(c, n, stride=ROW_TILES), :] = w[:, c * LANES:(c + 1) * LANES]


def _params(sem, vmem=VMEM_LIMIT):
    return pltpu.CompilerParams(dimension_semantics=sem, vmem_limit_bytes=vmem)


def _ada_kernel(c_ref, w_ref, b_ref, o_ref):
    c = c_ref[...]
    a = c * jax.nn.sigmoid(c)
    a_hi, a_lo = _split_bf16(a)
    w_hi, w_lo = _split_bf16(w_ref[...])
    o_ref[...] = _dot(a_hi, w_hi) + _dot(a_hi, w_lo) + _dot(a_lo, w_hi) + b_ref[...]


def _ada(c, w_ada, b_ada):
    nb = c.shape[0]
    n_out = w_ada.shape[1]
    blk = D_MODEL
    return pl.pallas_call(
        _ada_kernel,
        grid=(n_out // blk,),
        in_specs=[
            pl.BlockSpec((nb, D_MODEL), lambda j: (0, 0)),
            pl.BlockSpec((D_MODEL, blk), lambda j: (0, j)),
            pl.BlockSpec((1, blk), lambda j: (0, j)),
        ],
        out_specs=pl.BlockSpec((nb, blk), lambda j: (0, j)),
        out_shape=jax.ShapeDtypeStruct((nb, n_out), F32),
        compiler_params=_params(("arbitrary",)),
        name="ada",
    )(c, w_ada, b_ada.reshape(1, n_out))


_C_CQ = 0
_C_CKV = _C_CQ + Q_LORA
_C_KR = _C_CKV + KV_LORA
_C_U = _C_KR + LANES
_C_V = _C_U + GMLP_WIDTH
_C_END = _C_V + GMLP_WIDTH
_ROPE_LO = QK_NOPE
_ROLL_PARTNER = LANES - QK_ROPE


def _premix_kernel(x_ref, mod_ref, cos_ref, sin_ref, gpre_ref, win_ref, gq_ref, wuq_ref, gkv_ref, wukv_ref,
                   gv_ref, wsp_ref, bsp_ref, ggo_ref, q_ref, kt_ref, v_ref, sn_ref):
    x = x_ref[0]
    mod = mod_ref[0]
    shift1, scale1 = mod[0:1], mod[1:2]
    h = _rms(x) * gpre_ref[...] * (1.0 + scale1) + shift1
    z = _dot(h.astype(BF16), win_ref[...])

    cosb = cos_ref[...]
    sinb = sin_ref[...]
    lane = lax.broadcasted_iota(jnp.int32, (1, LANES), 1)
    nope_mask = jnp.where(lane < QK_NOPE, 1.0, 0.0).astype(F32)

    qscale = (QK_NOPE + QK_ROPE) ** -0.5 * math.log2(math.e)
    cq_tab = (nope_mask + cosb) * qscale
    sq_tab = sinb * qscale
    cqn = (_rms(z[:, _C_CQ:_C_CKV]) * gq_ref[...]).astype(BF16)
    qb = _dot(cqn, wuq_ref[...])
    for hd in range(N_HEADS):
        blk = qb[:, hd * HEAD_PAD:(hd + 1) * HEAD_PAD]
        qh = blk * cq_tab + pltpu.roll(blk, _ROLL_PARTNER, 1) * sq_tab
        q_ref[0, hd] = qh.astype(BF16)

    ckvn = (_rms(z[:, _C_CKV:_C_KR]) * gkv_ref[...]).astype(BF16)
    kvb = _dot(ckvn, wukv_ref[...])
    krb = z[:, _C_KR:_C_U]
    krope = krb * cosb + pltpu.roll(krb, _ROLL_PARTNER, 1) * sinb
    v_off = N_HEADS * HEAD_PAD
    for hd in range(N_HEADS):
        kh = kvb[:, hd * HEAD_PAD:(hd + 1) * HEAD_PAD] + krope
        kt_ref[0, hd, 0] = kh.T.astype(BF16)
        ones_lane = V_HEAD if hd % 2 == 0 else 0
        vh = kvb[:, v_off + hd * HEAD_PAD:v_off + (hd + 1) * HEAD_PAD] + jnp.where(lane == ones_lane, 1.0, 0.0)
        v_ref[0, hd] = vh.astype(BF16)

    ua = _gelu_tanh(z[:, _C_U:_C_V])
    vn = (_rms(_gelu_tanh(z[:, _C_V:_C_END])) * gv_ref[...]).astype(BF16)
    n_tok = x.shape[0]
    bsp = bsp_ref[...]
    rows = []
    for n in range(n_tok // CHUNK):
        cols = []
        for j in range(GMLP_GROUPS // 2):
            rhs = vn[n * CHUNK:(n + 1) * CHUNK, j * LANES:(j + 1) * LANES]
            ab = _dot(wsp_ref[j], rhs)
            cols.append(jnp.where(lane < GMLP_GROUP_DIM, ab[:CHUNK], ab[CHUNK:]))
        rows.append(jnp.concatenate(cols, axis=1) + bsp)
    s = ua * jnp.concatenate(rows, axis=0)
    sn_ref[0] = (_rms(s) * ggo_ref[...]).astype(BF16)


def _premix(x, mod, P):
    nb, seq, _ = x.shape
    tm = TOKEN_TILE
    nck = seq // K_TILE
    const = lambda i, b: (0, 0)
    return pl.pallas_call(
        _premix_kernel,
        grid=(seq // tm, nb),
        in_specs=[
            pl.BlockSpec((1, tm, D_MODEL), lambda i, b: (b, i, 0)),
            pl.BlockSpec((1, SUBLANES, D_MODEL), lambda i, b: (b, 0, 0)),
            pl.BlockSpec((tm, LANES), lambda i, b: (i, 0)),
            pl.BlockSpec((tm, LANES), lambda i, b: (i, 0)),
            pl.BlockSpec((1, D_MODEL), const),
            pl.BlockSpec((D_MODEL, _C_END), const),
            pl.BlockSpec((1, Q_LORA), const),
            pl.BlockSpec((Q_LORA, N_HEADS * HEAD_PAD), const),
            pl.BlockSpec((1, KV_LORA), const),
            pl.BlockSpec((KV_LORA, 2 * N_HEADS * HEAD_PAD), const),
            pl.BlockSpec((1, GMLP_WIDTH), const),
            pl.BlockSpec((GMLP_GROUPS // 2, 2 * CHUNK, CHUNK), lambda i, b: (0, 0, 0)),
            pl.BlockSpec((CHUNK, GMLP_WIDTH), const),
            pl.BlockSpec((1, GMLP_WIDTH), const),
        ],
        out_specs=[
            pl.BlockSpec((1, N_HEADS, tm, HEAD_PAD), lambda i, b: (b, 0, i, 0)),
            pl.BlockSpec((1, N_HEADS, 1, HEAD_PAD, K_TILE), lambda i, b: (b, 0, i, 0, 0)),
            pl.BlockSpec((1, N_HEADS, tm, HEAD_PAD), lambda i, b: (b, 0, i, 0)),
            pl.BlockSpec((1, tm, GMLP_WIDTH), lambda i, b: (b, i, 0)),
        ],
        out_shape=[
            jax.ShapeDtypeStruct((nb, N_HEADS, seq, HEAD_PAD), BF16),
            jax.ShapeDtypeStruct((nb, N_HEADS, nck, HEAD_PAD, K_TILE), BF16),
            jax.ShapeDtypeStruct((nb, N_HEADS, seq, HEAD_PAD), BF16),
            jax.ShapeDtypeStruct((nb, seq, GMLP_WIDTH), BF16),
        ],
        compiler_params=_params(("arbitrary", "arbitrary")),
        name="premix",
    )(x, mod, P["cos"][:seq], P["sin"][:seq], P["g_pre1"], P["w_in"], P["g_q"], P["w_uq"], P["g_kv"], P["w_ukv"],
      P["g_v"], P["w_sp"], P["b_sp"], P["g_gmlp_out"])


def _attn_kernel(q_ref, kt_ref, v_ref, o_ref):
    n_chunks = kt_ref.shape[2]
    tk = kt_ref.shape[4]
    lane = lax.broadcasted_iota(jnp.int32, (1, LANES), 1)

    def one_head(hd, ones_lane):
        q = q_ref[0, hd]
        m = None
        acc = None
        for c in range(n_chunks):
            s = _dot(q, kt_ref[0, hd, c])
            v = v_ref[0, hd, c * tk:(c + 1) * tk, :]
            smax = jnp.max(s, axis=1, keepdims=True)
            if c == 0:
                m = smax
                acc = _dot(jnp.exp2((s - m).astype(BF16)), v)
            else:
                m_new = jnp.maximum(m, smax)
                acc = acc * jnp.exp2(m - m_new) + _dot(jnp.exp2((s - m_new).astype(BF16)), v)
                m = m_new
        row_sum = acc[:, ones_lane:ones_lane + 1]
        return acc * (1.0 / row_sum)

    def pair(j, carry):
        even = one_head(2 * j, V_HEAD)
        odd = one_head(2 * j + 1, 0)
        o_ref[0, j] = jnp.where(lane < V_HEAD, even, odd).astype(BF16)
        return carry

    lax.fori_loop(0, N_HEADS // 2, pair, 0)


def _attention(q, kt, v):
    nb, _, seq, _ = q.shape
    nck = kt.shape[2]
    return pl.pallas_call(
        _attn_kernel,
        grid=(nb, seq // Q_TILE),
        in_specs=[
            pl.BlockSpec((1, N_HEADS, Q_TILE, HEAD_PAD), lambda b, i: (b, 0, i, 0)),
            pl.BlockSpec((1, N_HEADS, nck, HEAD_PAD, K_TILE), lambda b, i: (b, 0, 0, 0, 0)),
            pl.BlockSpec((1, N_HEADS, seq, HEAD_PAD), lambda b, i: (b, 0, 0, 0)),
        ],
        out_specs=pl.BlockSpec((1, N_HEADS // 2, Q_TILE, LANES), lambda b, i: (b, 0, i, 0)),
        out_shape=jax.ShapeDtypeStruct((nb, N_HEADS // 2, seq, LANES), BF16),
        compiler_params=_params(("arbitrary", "arbitrary")),
        name="attn",
    )(q, kt, v)


_R_GROUP_ROW = N_EXPERTS


def _postmix_kernel(a_ref, sn_ref, x_ref, mod_ref, gao_ref, wout_ref, gpost_ref, gpre_ref, wr_ref, br_ref, tri_ref,
                    x1_ref, h2_ref, eid_ref, rank_ref, wcol_ref, cnt_ref, run_ref):
    first = jnp.logical_and(pl.program_id(0) == 0, pl.program_id(1) == 0)

    @pl.when(first)
    def _():
        run_ref[...] = jnp.zeros_like(run_ref)

    mod = mod_ref[0]
    gate1, shift2, scale2 = mod[2:3], mod[3:4], mod[4:5]
    a = jnp.concatenate([a_ref[0, j] for j in range(N_HEADS // 2)], axis=1).astype(F32)
    an = (_rms(a) * gao_ref[...]).astype(BF16)
    merged = jnp.concatenate([an, sn_ref[0]], axis=1)
    o = _dot(merged, wout_ref[...])
    x1 = x_ref[0] + gate1 * (_rms(o) * gpost_ref[...])
    x1_ref[0] = x1
    h2 = _rms(x1) * gpre_ref[...] * (1.0 + scale2) + shift2
    n_tok = h2.shape[0]
    _store_token_rows(h2_ref, h2)

    h_hi, h_lo = _split_bf16(h2)
    wr = wr_ref[...]
    hh = _dot(h_hi, wr)
    lh = _dot(h_lo, wr[:, :LANES])
    logits = hh[:, :LANES] + hh[:, LANES:] + lh + br_ref[...]
    lt = logits.T

    neg = jnp.float32(-jnp.inf)
    row8 = lax.broadcasted_iota(jnp.int32, (SUBLANES, n_tok), 0).astype(F32)
    lg = jnp.where(row8 < N_EXPERT_GROUPS, lt[_R_GROUP_ROW:_R_GROUP_ROW + SUBLANES], neg)
    gmax = jnp.max(lg, axis=0, keepdims=True)
    gi = jnp.min(jnp.where(lg == gmax, row8, float(SUBLANES)), axis=0, keepdims=True)
    pg_sel = 1.0 / jnp.sum(jnp.exp(lg - gmax), axis=0, keepdims=True)

    le = jnp.zeros((EXPERTS_PER_GROUP, n_tok), F32)
    for g in range(N_EXPERT_GROUPS):
        le = jnp.where(gi == float(g), lt[g * EXPERTS_PER_GROUP:(g + 1) * EXPERTS_PER_GROUP], le)
    v1 = jnp.max(le, axis=0, keepdims=True)
    i1 = jnp.min(jnp.where(le == v1, row8, float(SUBLANES)), axis=0, keepdims=True)
    le2 = jnp.where(row8 == i1, neg, le)
    v2 = jnp.max(le2, axis=0, keepdims=True)
    i2 = jnp.min(jnp.where(le2 == v2, row8, float(SUBLANES)), axis=0, keepdims=True)
    r = jnp.exp(v2 - v1)
    w1 = pg_sel / (1.0 + r)
    w2 = w1 * r
    e1 = gi * float(EXPERTS_PER_GROUP) + i1
    e2 = gi * float(EXPERTS_PER_GROUP) + i2
    eid_ref[0] = jnp.concatenate([e1, e2], axis=0).astype(jnp.int32)

    row32 = lax.broadcasted_iota(jnp.int32, (N_EXPERTS, n_tok), 0).astype(F32)
    hit1 = row32 == e1
    hit2 = row32 == e2
    onehot = jnp.where(jnp.logical_or(hit1, hit2), 1.0, 0.0)
    run = run_ref[...][:, 0:1]
    ranks1, ranks2 = [], []
    for c in range(n_tok // RANK_CHUNK):
        sl = slice(c * RANK_CHUNK, (c + 1) * RANK_CHUNK)
        oh = onehot[:, sl]
        before = _dot(oh.astype(BF16), tri_ref[...]) + run
        ranks1.append(jnp.sum(jnp.where(hit1[:, sl], before, 0.0), axis=0, keepdims=True))
        ranks2.append(jnp.sum(jnp.where(hit2[:, sl], before, 0.0), axis=0, keepdims=True))
        run = run + jnp.sum(oh, axis=1, keepdims=True)
    rank_ref[0] = jnp.concatenate(
        [jnp.concatenate(ranks1, axis=1), jnp.concatenate(ranks2, axis=1)], axis=0).astype(jnp.int32)
    run_b = jnp.broadcast_to(run, run_ref.shape)
    run_ref[...] = run_b
    cnt_ref[...] = run_b

    row128 = lax.broadcasted_iota(jnp.int32, (LANES, n_tok), 0)
    wt = jnp.where(row128 == 0, w1, jnp.where(row128 == 1, w2, 0.0))
    wcol_ref[0] = wt.T


def _postmix(a, sn, x, mod, P):
    nb, seq, _ = x.shape
    tm = TOKEN_TILE
    const = lambda b, i: (0, 0)
    return pl.pallas_call(
        _postmix_kernel,
        grid=(nb, seq // tm),
        in_specs=[
            pl.BlockSpec((1, N_HEADS // 2, tm, LANES), lambda b, i: (b, 0, i, 0)),
            pl.BlockSpec((1, tm, GMLP_WIDTH), lambda b, i: (b, i, 0)),
            pl.BlockSpec((1, tm, D_MODEL), lambda b, i: (b, i, 0)),
            pl.BlockSpec((1, SUBLANES, D_MODEL), lambda b, i: (b, 0, 0)),
            pl.BlockSpec((1, N_HEADS * V_HEAD), const),
            pl.BlockSpec((D_MODEL, D_MODEL), const),
            pl.BlockSpec((1, D_MODEL), const),
            pl.BlockSpec((1, D_MODEL), const),
            pl.BlockSpec((D_MODEL, 2 * LANES), const),
            pl.BlockSpec((1, LANES), const),
            pl.BlockSpec((RANK_CHUNK, RANK_CHUNK), const),
        ],
        out_specs=[
            pl.BlockSpec((1, tm, D_MODEL), lambda b, i: (b, i, 0)),
            pl.BlockSpec((tm * ROW_TILES, LANES), lambda b, i: (b * (seq // tm) + i, 0)),
            pl.BlockSpec((1, 2, tm), lambda b, i: (b, 0, i)),
            pl.BlockSpec((1, 2, tm), lambda b, i: (b, 0, i)),
            pl.BlockSpec((1, tm, LANES), lambda b, i: (b, i, 0)),
            pl.BlockSpec((N_EXPERTS, LANES), const),
        ],
        out_shape=[
            jax.ShapeDtypeStruct((nb, seq, D_MODEL), F32),
            jax.ShapeDtypeStruct((nb * seq * ROW_TILES, LANES), U32),
            jax.ShapeDtypeStruct((nb, 2, seq), jnp.int32),
            jax.ShapeDtypeStruct((nb, 2, seq), jnp.int32),
            jax.ShapeDtypeStruct((nb, seq, LANES), F32),
            jax.ShapeDtypeStruct((N_EXPERTS, LANES), F32),
        ],
        scratch_shapes=[pltpu.VMEM((N_EXPERTS, LANES), F32)],
        compiler_params=_params(("arbitrary", "arbitrary")),
        name="postmix",
    )(a, sn, x, mod, P["g_attn_out"], P["w_out"], P["g_post1"], P["g_pre2"], P["w_router"], P["b_router"], P["tri"])


def _sc_mesh():
    return plsc.VectorSubcoreMesh(core_axis_name="c", subcore_axis_name="s")


def _sc_worker():
    return lax.axis_index("s") * SC_CORES + lax.axis_index("c")


def _sc_gather_rows(table, idx):
    n = idx.shape[0]
    per_w = n // SC_WORKERS
    steps = per_w // SC_ROWS
    idx3 = idx.reshape(SC_WORKERS, steps, SC_ROWS)

    @functools.partial(
        pl.kernel, mesh=_sc_mesh(),
        out_type=jax.ShapeDtypeStruct((n, ROW_TILES, LANES), U32),
        scratch_types=[pltpu.VMEM((steps, SC_ROWS), jnp.int32), pltpu.VMEM((SC_ROWS, ROW_TILES, LANES), U32),
                       pltpu.SemaphoreType.DMA],
        name="sc_gather_rows",
    )
    def gather(table_hbm, idx_hbm, out_hbm, idx_v, rows_v, sem):
        wid = _sc_worker()
        pltpu.sync_copy(idx_hbm.at[wid], idx_v)

        @pl.loop(0, steps)
        def _(j):
            pltpu.async_copy(table_hbm.at[idx_v.at[j]], rows_v, sem).wait()
            pltpu.sync_copy(rows_v, out_hbm.at[pl.ds(wid * per_w + j * SC_ROWS, SC_ROWS)])

    return gather(table, idx3)


def _sc_scatter_rows(src, idx, n_dst):
    n_dup, n = idx.shape
    per_w = n // SC_WORKERS
    steps = per_w // SC_ROWS
    idx4 = idx.reshape(n_dup, SC_WORKERS, steps, SC_ROWS).transpose(1, 0, 2, 3)

    @functools.partial(
        pl.kernel, mesh=_sc_mesh(),
        out_type=jax.ShapeDtypeStruct((n_dst, ROW_TILES, LANES), U32),
        scratch_types=[pltpu.VMEM((n_dup, steps, SC_ROWS), jnp.int32), pltpu.VMEM((SC_ROWS, ROW_TILES, LANES), U32),
                       pltpu.SemaphoreType.DMA],
        name="sc_scatter_rows",
    )
    def scatter(src_hbm, idx_hbm, dst_hbm, idx_v, rows_v, sem):
        wid = _sc_worker()
        pltpu.sync_copy(idx_hbm.at[wid], idx_v)

        @pl.loop(0, steps)
        def _(j):
            pltpu.sync_copy(src_hbm.at[pl.ds(wid * per_w + j * SC_ROWS, SC_ROWS)], rows_v)
            for k in range(n_dup):
                pltpu.async_copy(rows_v, dst_hbm.at[idx_v.at[k, j]], sem).wait()

    return scatter(src, idx4)


def _expert_kernel(te_ref, nv_ref, xs_ref, wg_ref, wu_ref, wd_ref, ys_ref, wgu_bf, wd_bf):
    i = pl.program_id(0)
    valid = i < nv_ref[0]
    new_expert = jnp.logical_or(i == 0, te_ref[i] != te_ref[jnp.maximum(i - 1, 0)])

    @pl.when(jnp.logical_and(valid, new_expert))
    def _():
        wgu_bf[:, :EXPERT_FF] = wg_ref[0].astype(BF16)
        wgu_bf[:, EXPERT_FF:] = wu_ref[0].astype(BF16)
        wd_bf[...] = wd_ref[0].astype(BF16)

    @pl.when(valid)
    def _():
        x = _load_token_rows(xs_ref, EXPERT_TILE).astype(BF16)
        gu = _dot(x, wgu_bf[...])
        g, u = gu[:, :EXPERT_FF], gu[:, EXPERT_FF:]
        act = (g * jax.nn.sigmoid(g) * u).astype(BF16)
        _store_token_rows(ys_ref, _dot(act, wd_bf[...]))

    @pl.when(jnp.logical_not(valid))
    def _():
        ys_ref[...] = jnp.zeros_like(ys_ref)


def _experts(xs, tile_expert, n_valid, P):
    rows_blk = EXPERT_TILE * ROW_TILES
    n_tiles = xs.shape[0] // rows_blk

    def row_map(i, te, nv):
        return (jnp.minimum(i, nv[0] - 1), 0)

    def out_map(i, te, nv):
        return (i, 0)

    def w_map(i, te, nv):
        return (te[jnp.minimum(i, nv[0] - 1)], 0, 0)

    return pl.pallas_call(
        _expert_kernel,
        grid_spec=pltpu.PrefetchScalarGridSpec(
            num_scalar_prefetch=2,
            grid=(n_tiles,),
            in_specs=[
                pl.BlockSpec((rows_blk, LANES), row_map),
                pl.BlockSpec((1, D_MODEL, EXPERT_FF), w_map),
                pl.BlockSpec((1, D_MODEL, EXPERT_FF), w_map),
                pl.BlockSpec((1, EXPERT_FF, D_MODEL), w_map),
            ],
            out_specs=pl.BlockSpec((rows_blk, LANES), out_map),
            scratch_shapes=[pltpu.VMEM((D_MODEL, 2 * EXPERT_FF), BF16), pltpu.VMEM((EXPERT_FF, D_MODEL), BF16)],
        ),
        out_shape=jax.ShapeDtypeStruct(xs.shape, U32),
        compiler_params=_params(("arbitrary",)),
        name="experts",
    )(tile_expert, n_valid, xs, P["w_gate"], P["w_up"], P["w_down"])


def _final_kernel(y0_ref, y1_ref, wcol_ref, x1_ref, mod_ref, gpost_ref, o_ref):
    w = wcol_ref[0]
    w0, w1 = w[:, 0:1], w[:, 1:2]
    n_tok = w.shape[0]
    m = w0 * _load_token_rows(y0_ref, n_tok) + w1 * _load_token_rows(y1_ref, n_tok)
    gate2 = mod_ref[0][5:6]
    o_ref[0] = x1_ref[0] + gate2 * (_rms(m) * gpost_ref[...])


def _final(yg, wcol, x1, mod, P):
    nb, seq, _ = x1.shape
    tm = TOKEN_TILE
    nt = seq // tm
    n_tok_tiles = nb * nt
    return pl.pallas_call(
        _final_kernel,
        grid=(nb, nt),
        in_specs=[
            pl.BlockSpec((tm * ROW_TILES, LANES), lambda b, i: (b * nt + i, 0)),
            pl.BlockSpec((tm * ROW_TILES, LANES), lambda b, i: (n_tok_tiles + b * nt + i, 0)),
            pl.BlockSpec((1, tm, LANES), lambda b, i: (b, i, 0)),
            pl.BlockSpec((1, tm, D_MODEL), lambda b, i: (b, i, 0)),
            pl.BlockSpec((1, SUBLANES, D_MODEL), lambda b, i: (b, 0, 0)),
            pl.BlockSpec((1, D_MODEL), lambda b, i: (0, 0)),
        ],
        out_specs=pl.BlockSpec((1, tm, D_MODEL), lambda b, i: (b, i, 0)),
        out_shape=jax.ShapeDtypeStruct((nb, seq, D_MODEL), F32),
        compiler_params=_params(("arbitrary", "arbitrary")),
        name="final",
    )(yg, yg, wcol, x1, mod, P["g_post2"])


def _prepare(w):
    f = lambda a: a.astype(F32)
    P = {}
    for k in ("g_pre1", "g_post1", "g_pre2", "g_post2", "g_q", "g_kv", "g_attn_out", "g_gmlp_out"):
        P[k] = f(w[k]).reshape(1, -1)
    P["g_v"] = f(w["g_v_gmlp"]).reshape(1, -1)

    w_in = f(w["w_in"])
    o0, o1, o2, o3 = Q_LORA, Q_LORA + KV_LORA, Q_LORA + KV_LORA + QK_ROPE, Q_LORA + KV_LORA + QK_ROPE + GMLP_WIDTH
    w_kr = w_in[:, o1:o2]
    kr_partner = jnp.concatenate([-w_kr[:, ROPE_HALF:], w_kr[:, :ROPE_HALF]], axis=1)
    rope_blk = jnp.concatenate([jnp.zeros((D_MODEL, QK_NOPE), F32), w_kr, kr_partner], axis=1)
    P["w_in"] = jnp.concatenate([w_in[:, :o1], rope_blk, w_in[:, o2:o3], w_in[:, o3:]], axis=1).astype(BF16)

    w_uq = f(w["w_uq"]).reshape(Q_LORA, N_HEADS, QK_NOPE + QK_ROPE)
    q_rope = w_uq[:, :, QK_NOPE:]
    q_partner = jnp.concatenate([-q_rope[:, :, ROPE_HALF:], q_rope[:, :, :ROPE_HALF]], axis=2)
    P["w_uq"] = jnp.concatenate([w_uq, q_partner], axis=2).reshape(Q_LORA, N_HEADS * HEAD_PAD).astype(BF16)

    w_ukv = f(w["w_ukv"]).reshape(KV_LORA, N_HEADS, QK_NOPE + V_HEAD)
    zeros = jnp.zeros((KV_LORA, N_HEADS, HEAD_PAD - QK_NOPE), F32)
    w_k = jnp.concatenate([w_ukv[:, :, :QK_NOPE], zeros], axis=2)
    w_v = w_ukv[:, :, QK_NOPE:]
    even = (jnp.arange(N_HEADS) % 2 == 0)[None, :, None]
    zv = jnp.zeros_like(w_v)
    w_v = jnp.concatenate([jnp.where(even, w_v, zv), jnp.where(even, zv, w_v)], axis=2)
    P["w_ukv"] = jnp.concatenate([w_k.reshape(KV_LORA, -1), w_v.reshape(KV_LORA, -1)], axis=1).astype(BF16)

    P["w_sp"] = f(w["w_spatial"]).reshape(GMLP_GROUPS // 2, 2 * CHUNK, CHUNK).astype(BF16)
    P["b_sp"] = jnp.repeat(f(w["b_spatial"]).T, GMLP_GROUP_DIM, axis=1)

    P["w_out"] = f(w["w_out"]).astype(BF16)

    pad = jnp.zeros((D_MODEL, LANES - N_EXPERTS - N_EXPERT_GROUPS), F32)
    wr = jnp.concatenate([f(w["w_router_expert"]), f(w["w_router_group"]), pad], axis=1)
    wr_hi = wr.astype(BF16)
    wr_lo = (wr - wr_hi.astype(F32)).astype(BF16)
    P["w_router"] = jnp.concatenate([wr_hi, wr_lo], axis=1)
    P["b_router"] = jnp.concatenate(
        [f(w["b_router_expert"]), f(w["b_router_group"]), jnp.zeros((LANES - N_EXPERTS - N_EXPERT_GROUPS,), F32)]
    ).reshape(1, LANES)
    P["tri"] = jnp.triu(jnp.ones((RANK_CHUNK, RANK_CHUNK), F32), k=1).astype(BF16)

    P["w_gate"], P["w_up"], P["w_down"] = f(w["w_gate"]), f(w["w_up"]), f(w["w_down"])
    return P


def _rope_tables(seq):
    inv = ROPE_THETA ** (-jnp.arange(ROPE_HALF, dtype=F32) / ROPE_HALF)
    ang = jnp.arange(seq, dtype=F32)[:, None] * inv[None, :]
    z_lo = jnp.zeros((seq, _ROPE_LO), F32)
    z_hi = jnp.zeros((seq, LANES - _ROPE_LO - QK_ROPE), F32)
    cos = jnp.concatenate([z_lo, jnp.cos(ang), jnp.cos(ang), z_hi], axis=1)
    sin = jnp.concatenate([z_lo, jnp.sin(ang), jnp.sin(ang), z_hi], axis=1)
    return cos, sin


def _layer(x, mod, P):
    nb, seq, _ = x.shape
    n_tok = nb * seq
    q, kt, v, sn = _premix(x, mod, P)
    a = _attention(q, kt, v)
    x1, h2rows, eid, rank, wcol, counts = _postmix(a, sn, x, mod, P)

    cnt = counts[:, 0].astype(jnp.int32)
    padded = ((cnt + EXPERT_TILE - 1) // EXPERT_TILE) * EXPERT_TILE
    ends = jnp.cumsum(padded)
    starts = ends - padded
    eflat = jnp.transpose(eid, (1, 0, 2)).reshape(2, n_tok)
    rflat = jnp.transpose(rank, (1, 0, 2)).reshape(2, n_tok)
    onehot = eflat[:, :, None] == jnp.arange(N_EXPERTS, dtype=jnp.int32)[None, None, :]
    pos = rflat + jnp.sum(jnp.where(onehot, starts[None, None, :], 0), axis=2)
    n_rows = 2 * n_tok + N_EXPERTS * EXPERT_TILE
    n_tiles = n_rows // EXPERT_TILE
    tile_start = jnp.arange(n_tiles, dtype=jnp.int32) * EXPERT_TILE
    tile_expert = jnp.minimum(
        jnp.sum((tile_start[:, None] >= ends[None, :]).astype(jnp.int32), axis=1), N_EXPERTS - 1).astype(jnp.int32)
    n_valid = (ends[-1:] // EXPERT_TILE).astype(jnp.int32)

    as_tiles = lambda a: a.reshape(-1, ROW_TILES, LANES)
    as_rows = lambda a: a.reshape(-1, LANES)
    xs = _sc_scatter_rows(as_tiles(h2rows), pos, n_rows)
    ys = _experts(as_rows(xs), tile_expert, n_valid, P)
    yg = as_rows(_sc_gather_rows(as_tiles(ys), pos.reshape(2 * n_tok)))
    return _final(yg, wcol, x1, mod, P)


def kernel(x_prompt, x_sample, c_prompt, c_sample, w_ada, b_ada, g_pre1, g_post1, g_pre2, g_post2, w_in, g_q, w_uq,
           g_kv, w_ukv, g_v_gmlp, w_spatial, b_spatial, g_attn_out, g_gmlp_out, w_out, w_router_group,
           b_router_group, w_router_expert, b_router_expert, w_gate, w_up, w_down):
    P = _prepare(dict(
        g_pre1=g_pre1, g_post1=g_post1, g_pre2=g_pre2, g_post2=g_post2, w_in=w_in, g_q=g_q, w_uq=w_uq, g_kv=g_kv,
        w_ukv=w_ukv, g_v_gmlp=g_v_gmlp, w_spatial=w_spatial, b_spatial=b_spatial, g_attn_out=g_attn_out,
        g_gmlp_out=g_gmlp_out, w_out=w_out, w_router_group=w_router_group, b_router_group=b_router_group,
        w_router_expert=w_router_expert, b_router_expert=b_router_expert, w_gate=w_gate, w_up=w_up, w_down=w_down))
    P["cos"], P["sin"] = _rope_tables(max(x_prompt.shape[1], x_sample.shape[1]))

    nbp = c_prompt.shape[0]
    c_all = jnp.concatenate([c_prompt, c_sample], axis=0).astype(F32)
    mod = _ada(c_all, w_ada.astype(F32), b_ada.astype(F32))
    mod = mod.reshape(c_all.shape[0], 6, D_MODEL)
    mod = jnp.concatenate([mod, jnp.zeros((c_all.shape[0], SUBLANES - 6, D_MODEL), F32)], axis=1)

    y_prompt = _layer(x_prompt, mod[:nbp], P)
    y_sample = _layer(x_sample, mod[nbp:], P)
    return (y_prompt, y_sample)
```

```python
import functools
import math

import jax
import jax.numpy as jnp
from jax import lax
from jax.experimental import pallas as pl
from jax.experimental.pallas import tpu as pltpu
from jax.experimental.pallas import tpu_sc as plsc

F32 = jnp.float32
BF16 = jnp.bfloat16

D_MODEL = 1024
N_HEADS = 8
QK_NOPE = 64
QK_ROPE = 32
ROPE_HALF = QK_ROPE // 2
V_HEAD = 64
Q_LORA = 256
KV_LORA = 128
GMLP_WIDTH = 512
GMLP_GROUPS = 8
GMLP_GROUP_DIM = 64
CHUNK = 128
N_EXPERTS = 32
N_EXPERT_GROUPS = 4
EXPERTS_PER_GROUP = 8
EXPERT_FF = 256
ROPE_THETA = 10000.0
EPS = 1e-6

LANES = 128
SUBLANES = 8
HEAD_PAD = LANES

TOKEN_TILE = 1024
POST_TILE = 1024
Q_TILE = 1024
K_TILE = 512
EXPERT_TILE = 512
SC_CORES = 2
SC_WORKERS = 32
SC_ROWS = 64
RANK_CHUNK = 256
VMEM_LIMIT = 56 * 1024 * 1024

U32 = jnp.uint32
PACKED_WIDTH = D_MODEL // 2
ROW_TILES = PACKED_WIDTH // LANES
_HI_MASK = 0xFFFF0000

_SQRT_2_OVER_PI = math.sqrt(2.0 / math.pi)


def _rms(x):
    return x * lax.rsqrt(jnp.mean(x * x, axis=-1, keepdims=True) + EPS)


def _gelu_tanh(x):
    return 0.5 * x * (1.0 + jnp.tanh(_SQRT_2_OVER_PI * (x + 0.044715 * (x * x * x))))


def _split_bf16(x):
    hi = x.astype(BF16)
    lo = (x - hi.astype(F32)).astype(BF16)
    return hi, lo


def _dot(a, b):
    return jnp.dot(a, b, preferred_element_type=F32)


def _bf16_bits(x):
    return lax.bitcast_convert_type(x.astype(BF16).astype(F32), U32)


def _load_token_rows(ref, n):
    w = jnp.concatenate([ref[pl.ds(c, n, stride=ROW_TILES), :] for c in range(ROW_TILES)], axis=1)
    lo = lax.bitcast_convert_type(w << 16, F32)
    hi = lax.bitcast_convert_type(w & jnp.uint32(_HI_MASK), F32)
    return jnp.concatenate([lo, hi], axis=1)


def _store_token_rows(ref, val):
    n = val.shape[0]
    w = (_bf16_bits(val[:, :PACKED_WIDTH]) >> 16) | (_bf16_bits(val[:, PACKED_WIDTH:]) & jnp.uint32(_HI_MASK))
    for c in range(ROW_TILES):
        ref[pl.ds(c, n, stride=ROW_TILES), :] = w[:, c * LANES:(c + 1) * LANES]


def _params(sem, vmem=VMEM_LIMIT):
    return pltpu.CompilerParams(dimension_semantics=sem, vmem_limit_bytes=vmem)


def _ada_kernel(c_ref, w_ref, b_ref, o_ref):
    c = c_ref[...]
    a = c * jax.nn.sigmoid(c)
    a_hi, a_lo = _split_bf16(a)
    w_hi, w_lo = _split_bf16(w_ref[...])
    o_ref[...] = _dot(a_hi, w_hi) + _dot(a_hi, w_lo) + _dot(a_lo, w_hi) + b_ref[...]


def _ada(c, w_ada, b_ada):
    nb = c.shape[0]
    n_out = w_ada.shape[1]
    blk = D_MODEL
    return pl.pallas_call(
        _ada_kernel,
        grid=(n_out // blk,),
        in_specs=[
            pl.BlockSpec((nb, D_MODEL), lambda j: (0, 0)),
            pl.BlockSpec((D_MODEL, blk), lambda j: (0, j)),
            pl.BlockSpec((1, blk), lambda j: (0, j)),
        ],
        out_specs=pl.BlockSpec((nb, blk), lambda j: (0, j)),
        out_shape=jax.ShapeDtypeStruct((nb, n_out), F32),
        compiler_params=_params(("arbitrary",)),
        name="ada",
    )(c, w_ada, b_ada.reshape(1, n_out))


_C_CQ = 0
_C_CKV = _C_CQ + Q_LORA
_C_KR = _C_CKV + KV_LORA
_C_U = _C_KR + LANES
_C_V = _C_U + GMLP_WIDTH
_C_END = _C_V + GMLP_WIDTH
_ROPE_LO = QK_NOPE
_ROLL_PARTNER = LANES - QK_ROPE


def _premix_kernel(x_ref, mod_ref, cos_ref, sin_ref, gpre_ref, win_ref, gq_ref, wuq_ref, gkv_ref, wukv_ref,
                   gv_ref, wsp_ref, bsp_ref, ggo_ref, q_ref, kt_ref, v_ref, sn_ref):
    x = x_ref[0]
    mod = mod_ref[0]
    shift1, scale1 = mod[0:1], mod[1:2]
    h = _rms(x) * (gpre_ref[...] * (1.0 + scale1)) + shift1
    z = _dot(h.astype(BF16), win_ref[...])

    cosb = cos_ref[...]
    sinb = sin_ref[...]
    lane = lax.broadcasted_iota(jnp.int32, (1, LANES), 1)
    nope_mask = jnp.where(lane < QK_NOPE, 1.0, 0.0).astype(F32)

    qscale = (QK_NOPE + QK_ROPE) ** -0.5 * math.log2(math.e)
    cq_tab = (nope_mask + cosb) * qscale
    sq_tab = sinb * qscale
    cqn = (_rms(z[:, _C_CQ:_C_CKV]) * gq_ref[...]).astype(BF16)
    qb = _dot(cqn, wuq_ref[...])
    for hd in range(N_HEADS):
        blk = qb[:, hd * HEAD_PAD:(hd + 1) * HEAD_PAD]
        qh = blk * cq_tab + pltpu.roll(blk, _ROLL_PARTNER, 1) * sq_tab
        q_ref[0, hd] = qh.astype(BF16)

    ckvn = (_rms(z[:, _C_CKV:_C_KR]) * gkv_ref[...]).astype(BF16)
    kvb = _dot(ckvn, wukv_ref[...])
    krb = z[:, _C_KR:_C_U]
    krope = krb * cosb + pltpu.roll(krb, _ROLL_PARTNER, 1) * sinb
    v_off = N_HEADS * HEAD_PAD
    for hd in range(N_HEADS):
        kh = kvb[:, hd * HEAD_PAD:(hd + 1) * HEAD_PAD] + krope
        for c in range(kt_ref.shape[2]):
            kt_ref[0, hd, c] = kh[c * K_TILE:(c + 1) * K_TILE].T.astype(BF16)
        ones_lane = V_HEAD if hd % 2 == 0 else 0
        vh = kvb[:, v_off + hd * HEAD_PAD:v_off + (hd + 1) * HEAD_PAD] + jnp.where(lane == ones_lane, 1.0, 0.0)
        v_ref[0, hd] = vh.astype(BF16)

    ua = _gelu_tanh(z[:, _C_U:_C_V])
    vn = (_rms(_gelu_tanh(z[:, _C_V:_C_END])) * gv_ref[...]).astype(BF16)
    n_tok = x.shape[0]
    bsp = bsp_ref[...]
    rows = []
    for n in range(n_tok // CHUNK):
        cols = []
        for j in range(GMLP_GROUPS // 2):
            rhs = vn[n * CHUNK:(n + 1) * CHUNK, j * LANES:(j + 1) * LANES]
            ab = _dot(wsp_ref[j], rhs)
            cols.append(jnp.where(lane < GMLP_GROUP_DIM, ab[:CHUNK], ab[CHUNK:]))
        rows.append(jnp.concatenate(cols, axis=1) + bsp)
    s = ua * jnp.concatenate(rows, axis=0)
    sn_ref[0] = (_rms(s) * ggo_ref[...]).astype(BF16)


def _premix(x, mod, P):
    nb, seq, _ = x.shape
    tm = TOKEN_TILE
    nck = seq // K_TILE
    const = lambda i, b: (0, 0)
    return pl.pallas_call(
        _premix_kernel,
        grid=(seq // tm, nb),
        in_specs=[
            pl.BlockSpec((1, tm, D_MODEL), lambda i, b: (b, i, 0)),
            pl.BlockSpec((1, SUBLANES, D_MODEL), lambda i, b: (b, 0, 0)),
            pl.BlockSpec((tm, LANES), lambda i, b: (i, 0)),
            pl.BlockSpec((tm, LANES), lambda i, b: (i, 0)),
            pl.BlockSpec((1, D_MODEL), const),
            pl.BlockSpec((D_MODEL, _C_END), const),
            pl.BlockSpec((1, Q_LORA), const),
            pl.BlockSpec((Q_LORA, N_HEADS * HEAD_PAD), const),
            pl.BlockSpec((1, KV_LORA), const),
            pl.BlockSpec((KV_LORA, 2 * N_HEADS * HEAD_PAD), const),
            pl.BlockSpec((1, GMLP_WIDTH), const),
            pl.BlockSpec((GMLP_GROUPS // 2, 2 * CHUNK, CHUNK), lambda i, b: (0, 0, 0)),
            pl.BlockSpec((CHUNK, GMLP_WIDTH), const),
            pl.BlockSpec((1, GMLP_WIDTH), const),
        ],
        out_specs=[
            pl.BlockSpec((1, N_HEADS, tm, HEAD_PAD), lambda i, b: (b, 0, i, 0)),
            pl.BlockSpec((1, N_HEADS, tm // K_TILE, HEAD_PAD, K_TILE), lambda i, b: (b, 0, i, 0, 0)),
            pl.BlockSpec((1, N_HEADS, tm, HEAD_PAD), lambda i, b: (b, 0, i, 0)),
            pl.BlockSpec((1, tm, GMLP_WIDTH), lambda i, b: (b, i, 0)),
        ],
        out_shape=[
            jax.ShapeDtypeStruct((nb, N_HEADS, seq, HEAD_PAD), BF16),
            jax.ShapeDtypeStruct((nb, N_HEADS, nck, HEAD_PAD, K_TILE), BF16),
            jax.ShapeDtypeStruct((nb, N_HEADS, seq, HEAD_PAD), BF16),
            jax.ShapeDtypeStruct((nb, seq, GMLP_WIDTH), BF16),
        ],
        compiler_params=_params(("arbitrary", "arbitrary")),
        name="premix",
    )(x, mod, P["cos"][:seq], P["sin"][:seq], P["g_pre1"], P["w_in"], P["g_q"], P["w_uq"], P["g_kv"], P["w_ukv"],
      P["g_v"], P["w_sp"], P["b_sp"], P["g_gmlp_out"])


def _attn_kernel(q_ref, kt_ref, v_ref, o_ref):
    n_chunks = kt_ref.shape[2]
    tk = kt_ref.shape[4]
    lane = lax.broadcasted_iota(jnp.int32, (1, LANES), 1)

    def one_head(hd, ones_lane):
        q = q_ref[0, hd]
        m = None
        acc = None
        for c in range(n_chunks):
            s = _dot(q, kt_ref[0, hd, c])
            v = v_ref[0, hd, c * tk:(c + 1) * tk, :]
            smax = jnp.max(s, axis=1, keepdims=True)
            if c == 0:
                m = smax
                acc = _dot(jnp.exp2((s - m).astype(BF16)), v)
            else:
                m_new = jnp.maximum(m, smax)
                acc = acc * jnp.exp2(m - m_new) + _dot(jnp.exp2((s - m_new).astype(BF16)), v)
                m = m_new
        row_sum = acc[:, ones_lane:ones_lane + 1]
        return acc * (1.0 / row_sum)

    def pair(j, carry):
        even = one_head(2 * j, V_HEAD)
        odd = one_head(2 * j + 1, 0)
        o_ref[0, j] = jnp.where(lane < V_HEAD, even, odd).astype(BF16)
        return carry

    lax.fori_loop(0, N_HEADS // 2, pair, 0)


def _attention(q, kt, v):
    nb, _, seq, _ = q.shape
    nck = kt.shape[2]
    return pl.pallas_call(
        _attn_kernel,
        grid=(nb, seq // Q_TILE),
        in_specs=[
            pl.BlockSpec((1, N_HEADS, Q_TILE, HEAD_PAD), lambda b, i: (b, 0, i, 0)),
            pl.BlockSpec((1, N_HEADS, nck, HEAD_PAD, K_TILE), lambda b, i: (b, 0, 0, 0, 0)),
            pl.BlockSpec((1, N_HEADS, seq, HEAD_PAD), lambda b, i: (b, 0, 0, 0)),
        ],
        out_specs=pl.BlockSpec((1, N_HEADS // 2, Q_TILE, LANES), lambda b, i: (b, 0, i, 0)),
        out_shape=jax.ShapeDtypeStruct((nb, N_HEADS // 2, seq, LANES), BF16),
        compiler_params=_params(("arbitrary", "arbitrary")),
        name="attn",
    )(q, kt, v)


_R_GROUP_ROW = N_EXPERTS


def _postmix_kernel(a_ref, sn_ref, x_ref, mod_ref, gao_ref, wout_ref, gpost_ref, gpre_ref, wr_ref, br_ref, tri_ref,
                    x1_ref, h2_ref, eid_ref, rank_ref, wcol_ref, cnt_ref, run_ref):
    first = jnp.logical_and(pl.program_id(0) == 0, pl.program_id(1) == 0)

    @pl.when(first)
    def _():
        run_ref[...] = jnp.zeros_like(run_ref)

    mod = mod_ref[0]
    gate1, shift2, scale2 = mod[2:3], mod[3:4], mod[4:5]
    a = jnp.concatenate([a_ref[0, j] for j in range(N_HEADS // 2)], axis=1).astype(F32)
    an = (_rms(a) * gao_ref[...]).astype(BF16)
    merged = jnp.concatenate([an, sn_ref[0]], axis=1)
    o = _dot(merged, wout_ref[...])
    x1 = x_ref[0] + gate1 * (_rms(o) * gpost_ref[...])
    x1_ref[0] = x1
    h2 = _rms(x1) * (gpre_ref[...] * (1.0 + scale2)) + shift2
    n_tok = h2.shape[0]
    _store_token_rows(h2_ref, h2)

    h_hi, h_lo = _split_bf16(h2)
    wr = wr_ref[...]
    hh = _dot(h_hi, wr)
    lh = _dot(h_lo, wr[:, :LANES])
    logits = hh[:, :LANES] + hh[:, LANES:] + lh + br_ref[...]
    lt = logits.T

    neg = jnp.float32(-jnp.inf)
    row8 = lax.broadcasted_iota(jnp.int32, (SUBLANES, n_tok), 0).astype(F32)
    lg = jnp.where(row8 < N_EXPERT_GROUPS, lt[_R_GROUP_ROW:_R_GROUP_ROW + SUBLANES], neg)
    gmax = jnp.max(lg, axis=0, keepdims=True)
    gi = jnp.min(jnp.where(lg == gmax, row8, float(SUBLANES)), axis=0, keepdims=True)
    pg_sel = 1.0 / jnp.sum(jnp.exp(lg - gmax), axis=0, keepdims=True)

    le = jnp.zeros((EXPERTS_PER_GROUP, n_tok), F32)
    for g in range(N_EXPERT_GROUPS):
        le = jnp.where(gi == float(g), lt[g * EXPERTS_PER_GROUP:(g + 1) * EXPERTS_PER_GROUP], le)
    v1 = jnp.max(le, axis=0, keepdims=True)
    i1 = jnp.min(jnp.where(le == v1, row8, float(SUBLANES)), axis=0, keepdims=True)
    le2 = jnp.where(row8 == i1, neg, le)
    v2 = jnp.max(le2, axis=0, keepdims=True)
    i2 = jnp.min(jnp.where(le2 == v2, row8, float(SUBLANES)), axis=0, keepdims=True)
    r = jnp.exp(v2 - v1)
    w1 = pg_sel / (1.0 + r)
    w2 = w1 * r
    e1 = gi * float(EXPERTS_PER_GROUP) + i1
    e2 = gi * float(EXPERTS_PER_GROUP) + i2
    eid_ref[0] = jnp.concatenate([e1, e2], axis=0).astype(jnp.int32)

    row32 = lax.broadcasted_iota(jnp.int32, (N_EXPERTS, n_tok), 0).astype(F32)
    hit1 = row32 == e1
    hit2 = row32 == e2
    onehot = jnp.where(jnp.logical_or(hit1, hit2), 1.0, 0.0)
    run = run_ref[...][:, 0:1]
    ranks1, ranks2 = [], []
    for c in range(n_tok // RANK_CHUNK):
        sl = slice(c * RANK_CHUNK, (c + 1) * RANK_CHUNK)
        oh = onehot[:, sl]
        before = _dot(oh.astype(BF16), tri_ref[...]) + run
        ranks1.append(jnp.sum(jnp.where(hit1[:, sl], before, 0.0), axis=0, keepdims=True))
        ranks2.append(jnp.sum(jnp.where(hit2[:, sl], before, 0.0), axis=0, keepdims=True))
        run = run + jnp.sum(oh, axis=1, keepdims=True)
    rank_ref[0] = jnp.concatenate(
        [jnp.concatenate(ranks1, axis=1), jnp.concatenate(ranks2, axis=1)], axis=0).astype(jnp.int32)
    run_b = jnp.broadcast_to(run, run_ref.shape)
    run_ref[...] = run_b
    cnt_ref[...] = run_b

    row128 = lax.broadcasted_iota(jnp.int32, (LANES, n_tok), 0)
    wt = jnp.where(row128 == 0, w1, jnp.where(row128 == 1, w2, 0.0))
    wcol_ref[0] = wt.T


def _postmix(a, sn, x, mod, P):
    nb, seq, _ = x.shape
    tm = POST_TILE
    const = lambda b, i: (0, 0)
    return pl.pallas_call(
        _postmix_kernel,
        grid=(nb, seq // tm),
        in_specs=[
            pl.BlockSpec((1, N_HEADS // 2, tm, LANES), lambda b, i: (b, 0, i, 0)),
            pl.BlockSpec((1, tm, GMLP_WIDTH), lambda b, i: (b, i, 0)),
            pl.BlockSpec((1, tm, D_MODEL), lambda b, i: (b, i, 0)),
            pl.BlockSpec((1, SUBLANES, D_MODEL), lambda b, i: (b, 0, 0)),
            pl.BlockSpec((1, N_HEADS * V_HEAD), const),
            pl.BlockSpec((D_MODEL, D_MODEL), const),
            pl.BlockSpec((1, D_MODEL), const),
            pl.BlockSpec((1, D_MODEL), const),
            pl.BlockSpec((D_MODEL, 2 * LANES), const),
            pl.BlockSpec((1, LANES), const),
            pl.BlockSpec((RANK_CHUNK, RANK_CHUNK), const),
        ],
        out_specs=[
            pl.BlockSpec((1, tm, D_MODEL), lambda b, i: (b, i, 0)),
            pl.BlockSpec((tm * ROW_TILES, LANES), lambda b, i: (b * (seq // tm) + i, 0)),
            pl.BlockSpec((1, 2, tm), lambda b, i: (b, 0, i)),
            pl.BlockSpec((1, 2, tm), lambda b, i: (b, 0, i)),
            pl.BlockSpec((1, tm, LANES), lambda b, i: (b, i, 0)),
            pl.BlockSpec((N_EXPERTS, LANES), const),
        ],
        out_shape=[
            jax.ShapeDtypeStruct((nb, seq, D_MODEL), F32),
            jax.ShapeDtypeStruct((nb * seq * ROW_TILES, LANES), U32),
            jax.ShapeDtypeStruct((nb, 2, seq), jnp.int32),
            jax.ShapeDtypeStruct((nb, 2, seq), jnp.int32),
            jax.ShapeDtypeStruct((nb, seq, LANES), F32),
            jax.ShapeDtypeStruct((N_EXPERTS, LANES), F32),
        ],
        scratch_shapes=[pltpu.VMEM((N_EXPERTS, LANES), F32)],
        compiler_params=_params(("arbitrary", "arbitrary")),
        name="postmix",
    )(a, sn, x, mod, P["g_attn_out"], P["w_out"], P["g_post1"], P["g_pre2"], P["w_router"], P["b_router"], P["tri"])


def _sc_mesh():
    return plsc.VectorSubcoreMesh(core_axis_name="c", subcore_axis_name="s")


def _sc_worker():
    return lax.axis_index("s") * SC_CORES + lax.axis_index("c")


def _sc_gather_rows(table, idx):
    n = idx.shape[0]
    per_w = n // SC_WORKERS
    steps = per_w // SC_ROWS
    idx3 = idx.reshape(SC_WORKERS, steps, SC_ROWS)

    @functools.partial(
        pl.kernel, mesh=_sc_mesh(),
        out_type=jax.ShapeDtypeStruct((n, ROW_TILES, LANES), U32),
        scratch_types=[pltpu.VMEM((steps, SC_ROWS), jnp.int32), pltpu.VMEM((SC_ROWS, ROW_TILES, LANES), U32),
                       pltpu.SemaphoreType.DMA],
        name="sc_gather_rows",
    )
    def gather(table_hbm, idx_hbm, out_hbm, idx_v, rows_v, sem):
        wid = _sc_worker()
        pltpu.sync_copy(idx_hbm.at[wid], idx_v)

        @pl.loop(0, steps)
        def _(j):
            pltpu.async_copy(table_hbm.at[idx_v.at[j]], rows_v, sem).wait()
            pltpu.sync_copy(rows_v, out_hbm.at[pl.ds(wid * per_w + j * SC_ROWS, SC_ROWS)])

    return gather(table, idx3)


def _sc_scatter_rows(src, idx, n_dst):
    n_dup, n = idx.shape
    per_w = n // SC_WORKERS
    steps = per_w // SC_ROWS
    idx4 = idx.reshape(n_dup, SC_WORKERS, steps, SC_ROWS).transpose(1, 0, 2, 3)

    @functools.partial(
        pl.kernel, mesh=_sc_mesh(),
        out_type=jax.ShapeDtypeStruct((n_dst, ROW_TILES, LANES), U32),
        scratch_types=[pltpu.VMEM((n_dup, steps, SC_ROWS), jnp.int32), pltpu.VMEM((SC_ROWS, ROW_TILES, LANES), U32),
                       pltpu.SemaphoreType.DMA],
        name="sc_scatter_rows",
    )
    def scatter(src_hbm, idx_hbm, dst_hbm, idx_v, rows_v, sem):
        wid = _sc_worker()
        pltpu.sync_copy(idx_hbm.at[wid], idx_v)

        @pl.loop(0, steps)
        def _(j):
            pltpu.sync_copy(src_hbm.at[pl.ds(wid * per_w + j * SC_ROWS, SC_ROWS)], rows_v)
            for k in range(n_dup):
                pltpu.async_copy(rows_v, dst_hbm.at[idx_v.at[k, j]], sem).wait()

    return scatter(src, idx4)


def _expert_kernel(te_ref, nv_ref, xs_ref, wg_ref, wu_ref, wd_ref, ys_ref, wgu_bf, wd_bf):
    i = pl.program_id(0)
    valid = i < nv_ref[0]
    new_expert = jnp.logical_or(i == 0, te_ref[i] != te_ref[jnp.maximum(i - 1, 0)])

    @pl.when(jnp.logical_and(valid, new_expert))
    def _():
        wgu_bf[:, :EXPERT_FF] = wg_ref[0].astype(BF16)
        wgu_bf[:, EXPERT_FF:] = wu_ref[0].astype(BF16)
        wd_bf[...] = wd_ref[0].astype(BF16)

    @pl.when(valid)
    def _():
        x = _load_token_rows(xs_ref, EXPERT_TILE).astype(BF16)
        gu = _dot(x, wgu_bf[...])
        g, u = gu[:, :EXPERT_FF], gu[:, EXPERT_FF:]
        act = (g * jax.nn.sigmoid(g) * u).astype(BF16)
        _store_token_rows(ys_ref, _dot(act, wd_bf[...]))

    @pl.when(jnp.logical_not(valid))
    def _():
        ys_ref[...] = jnp.zeros_like(ys_ref)


def _experts(xs, tile_expert, n_valid, P):
    rows_blk = EXPERT_TILE * ROW_TILES
    n_tiles = xs.shape[0] // rows_blk

    def row_map(i, te, nv):
        return (jnp.minimum(i, nv[0] - 1), 0)

    def out_map(i, te, nv):
        return (i, 0)

    def w_map(i, te, nv):
        return (te[jnp.minimum(i, nv[0] - 1)], 0, 0)

    return pl.pallas_call(
        _expert_kernel,
        grid_spec=pltpu.PrefetchScalarGridSpec(
            num_scalar_prefetch=2,
            grid=(n_tiles,),
            in_specs=[
                pl.BlockSpec((rows_blk, LANES), row_map),
                pl.BlockSpec((1, D_MODEL, EXPERT_FF), w_map),
                pl.BlockSpec((1, D_MODEL, EXPERT_FF), w_map),
                pl.BlockSpec((1, EXPERT_FF, D_MODEL), w_map),
            ],
            out_specs=pl.BlockSpec((rows_blk, LANES), out_map),
            scratch_shapes=[pltpu.VMEM((D_MODEL, 2 * EXPERT_FF), BF16), pltpu.VMEM((EXPERT_FF, D_MODEL), BF16)],
        ),
        out_shape=jax.ShapeDtypeStruct(xs.shape, U32),
        compiler_params=_params(("arbitrary",)),
        name="experts",
    )(tile_expert, n_valid, xs, P["w_gate"], P["w_up"], P["w_down"])


def _final_kernel(y0_ref, y1_ref, wcol_ref, x1_ref, mod_ref, gpost_ref, o_ref):
    w = wcol_ref[0]
    w0, w1 = w[:, 0:1], w[:, 1:2]
    n_tok = w.shape[0]
    m = w0 * _load_token_rows(y0_ref, n_tok) + w1 * _load_token_rows(y1_ref, n_tok)
    gate2 = mod_ref[0][5:6]
    o_ref[0] = x1_ref[0] + gate2 * (_rms(m) * gpost_ref[...])


def _final(yg, wcol, x1, mod, P):
    nb, seq, _ = x1.shape
    tm = POST_TILE
    nt = seq // tm
    n_tok_tiles = nb * nt
    return pl.pallas_call(
        _final_kernel,
        grid=(nb, nt),
        in_specs=[
            pl.BlockSpec((tm * ROW_TILES, LANES), lambda b, i: (b * nt + i, 0)),
            pl.BlockSpec((tm * ROW_TILES, LANES), lambda b, i: (n_tok_tiles + b * nt + i, 0)),
            pl.BlockSpec((1, tm, LANES), lambda b, i: (b, i, 0)),
            pl.BlockSpec((1, tm, D_MODEL), lambda b, i: (b, i, 0)),
            pl.BlockSpec((1, SUBLANES, D_MODEL), lambda b, i: (b, 0, 0)),
            pl.BlockSpec((1, D_MODEL), lambda b, i: (0, 0)),
        ],
        out_specs=pl.BlockSpec((1, tm, D_MODEL), lambda b, i: (b, i, 0)),
        out_shape=jax.ShapeDtypeStruct((nb, seq, D_MODEL), F32),
        compiler_params=_params(("arbitrary", "arbitrary")),
        name="final",
    )(yg, yg, wcol, x1, mod, P["g_post2"])


def _prepare(w):
    f = lambda a: a.astype(F32)
    P = {}
    for k in ("g_pre1", "g_post1", "g_pre2", "g_post2", "g_q", "g_kv", "g_attn_out", "g_gmlp_out"):
        P[k] = f(w[k]).reshape(1, -1)
    P["g_v"] = f(w["g_v_gmlp"]).reshape(1, -1)

    w_in = f(w["w_in"])
    o0, o1, o2, o3 = Q_LORA, Q_LORA + KV_LORA, Q_LORA + KV_LORA + QK_ROPE, Q_LORA + KV_LORA + QK_ROPE + GMLP_WIDTH
    w_kr = w_in[:, o1:o2]
    kr_partner = jnp.concatenate([-w_kr[:, ROPE_HALF:], w_kr[:, :ROPE_HALF]], axis=1)
    rope_blk = jnp.concatenate([jnp.zeros((D_MODEL, QK_NOPE), F32), w_kr, kr_partner], axis=1)
    P["w_in"] = jnp.concatenate([w_in[:, :o1], rope_blk, w_in[:, o2:o3], w_in[:, o3:]], axis=1).astype(BF16)

    w_uq = f(w["w_uq"]).reshape(Q_LORA, N_HEADS, QK_NOPE + QK_ROPE)
    q_rope = w_uq[:, :, QK_NOPE:]
    q_partner = jnp.concatenate([-q_rope[:, :, ROPE_HALF:], q_rope[:, :, :ROPE_HALF]], axis=2)
    P["w_uq"] = jnp.concatenate([w_uq, q_partner], axis=2).reshape(Q_LORA, N_HEADS * HEAD_PAD).astype(BF16)

    w_ukv = f(w["w_ukv"]).reshape(KV_LORA, N_HEADS, QK_NOPE + V_HEAD)
    zeros = jnp.zeros((KV_LORA, N_HEADS, HEAD_PAD - QK_NOPE), F32)
    w_k = jnp.concatenate([w_ukv[:, :, :QK_NOPE], zeros], axis=2)
    w_v = w_ukv[:, :, QK_NOPE:]
    even = (jnp.arange(N_HEADS) % 2 == 0)[None, :, None]
    zv = jnp.zeros_like(w_v)
    w_v = jnp.concatenate([jnp.where(even, w_v, zv), jnp.where(even, zv, w_v)], axis=2)
    P["w_ukv"] = jnp.concatenate([w_k.reshape(KV_LORA, -1), w_v.reshape(KV_LORA, -1)], axis=1).astype(BF16)

    P["w_sp"] = f(w["w_spatial"]).reshape(GMLP_GROUPS // 2, 2 * CHUNK, CHUNK).astype(BF16)
    P["b_sp"] = jnp.repeat(f(w["b_spatial"]).T, GMLP_GROUP_DIM, axis=1)

    P["w_out"] = f(w["w_out"]).astype(BF16)

    pad = jnp.zeros((D_MODEL, LANES - N_EXPERTS - N_EXPERT_GROUPS), F32)
    wr = jnp.concatenate([f(w["w_router_expert"]), f(w["w_router_group"]), pad], axis=1)
    wr_hi = wr.astype(BF16)
    wr_lo = (wr - wr_hi.astype(F32)).astype(BF16)
    P["w_router"] = jnp.concatenate([wr_hi, wr_lo], axis=1)
    P["b_router"] = jnp.concatenate(
        [f(w["b_router_expert"]), f(w["b_router_group"]), jnp.zeros((LANES - N_EXPERTS - N_EXPERT_GROUPS,), F32)]
    ).reshape(1, LANES)
    P["tri"] = jnp.triu(jnp.ones((RANK_CHUNK, RANK_CHUNK), F32), k=1).astype(BF16)

    P["w_gate"], P["w_up"], P["w_down"] = f(w["w_gate"]), f(w["w_up"]), f(w["w_down"])
    return P


def _rope_tables(seq):
    inv = ROPE_THETA ** (-jnp.arange(ROPE_HALF, dtype=F32) / ROPE_HALF)
    ang = jnp.arange(seq, dtype=F32)[:, None] * inv[None, :]
    z_lo = jnp.zeros((seq, _ROPE_LO), F32)
    z_hi = jnp.zeros((seq, LANES - _ROPE_LO - QK_ROPE), F32)
    cos = jnp.concatenate([z_lo, jnp.cos(ang), jnp.cos(ang), z_hi], axis=1)
    sin = jnp.concatenate([z_lo, jnp.sin(ang), jnp.sin(ang), z_hi], axis=1)
    return cos, sin


def _layer(x, mod, P):
    nb, seq, _ = x.shape
    n_tok = nb * seq
    q, kt, v, sn = _premix(x, mod, P)
    a = _attention(q, kt, v)
    x1, h2rows, eid, rank, wcol, counts = _postmix(a, sn, x, mod, P)

    cnt = counts[:, 0].astype(jnp.int32)
    padded = ((cnt + EXPERT_TILE - 1) // EXPERT_TILE) * EXPERT_TILE
    ends = jnp.cumsum(padded)
    starts = ends - padded
    eflat = jnp.transpose(eid, (1, 0, 2)).reshape(2, n_tok)
    rflat = jnp.transpose(rank, (1, 0, 2)).reshape(2, n_tok)
    onehot = eflat[:, :, None] == jnp.arange(N_EXPERTS, dtype=jnp.int32)[None, None, :]
    pos = rflat + jnp.sum(jnp.where(onehot, starts[None, None, :], 0), axis=2)
    n_rows = 2 * n_tok + N_EXPERTS * EXPERT_TILE
    n_tiles = n_rows // EXPERT_TILE
    tile_start = jnp.arange(n_tiles, dtype=jnp.int32) * EXPERT_TILE
    tile_expert = jnp.minimum(
        jnp.sum((tile_start[:, None] >= ends[None, :]).astype(jnp.int32), axis=1), N_EXPERTS - 1).astype(jnp.int32)
    n_valid = (ends[-1:] // EXPERT_TILE).astype(jnp.int32)

    as_tiles = lambda a: a.reshape(-1, ROW_TILES, LANES)
    as_rows = lambda a: a.reshape(-1, LANES)
    xs = _sc_scatter_rows(as_tiles(h2rows), pos, n_rows)
    ys = _experts(as_rows(xs), tile_expert, n_valid, P)
    yg = as_rows(_sc_gather_rows(as_tiles(ys), pos.reshape(2 * n_tok)))
    return _final(yg, wcol, x1, mod, P)


def kernel(x_prompt, x_sample, c_prompt, c_sample, w_ada, b_ada, g_pre1, g_post1, g_pre2, g_post2, w_in, g_q, w_uq,
           g_kv, w_ukv, g_v_gmlp, w_spatial, b_spatial, g_attn_out, g_gmlp_out, w_out, w_router_group,
           b_router_group, w_router_expert, b_router_expert, w_gate, w_up, w_down):
    P = _prepare(dict(
        g_pre1=g_pre1, g_post1=g_post1, g_pre2=g_pre2, g_post2=g_post2, w_in=w_in, g_q=g_q, w_uq=w_uq, g_kv=g_kv,
        w_ukv=w_ukv, g_v_gmlp=g_v_gmlp, w_spatial=w_spatial, b_spatial=b_spatial, g_attn_out=g_attn_out,
        g_gmlp_out=g_gmlp_out, w_out=w_out, w_router_group=w_router_group, b_router_group=b_router_group,
        w_router_expert=w_router_expert, b_router_expert=b_router_expert, w_gate=w_gate, w_up=w_up, w_down=w_down))
    P["cos"], P["sin"] = _rope_tables(max(x_prompt.shape[1], x_sample.shape[1]))

    nbp = c_prompt.shape[0]
    c_all = jnp.concatenate([c_prompt, c_sample], axis=0).astype(F32)
    mod = _ada(c_all, w_ada.astype(F32), b_ada.astype(F32))
    mod = mod.reshape(c_all.shape[0], 6, D_MODEL)
    mod = jnp.concatenate([mod, jnp.zeros((c_all.shape[0], SUBLANES - 6, D_MODEL), F32)], axis=1)

    y_prompt = _layer(x_prompt, mod[:nbp], P)
    y_sample = _layer(x_sample, mod[nbp:], P)
    return (y_prompt, y_sample)
```

```python
import functools
import math

import jax
import jax.numpy as jnp
from jax import lax
from jax.experimental import pallas as pl
from jax.experimental.pallas import tpu as pltpu
from jax.experimental.pallas import tpu_sc as plsc

F32 = jnp.float32
BF16 = jnp.bfloat16

D_MODEL = 1024
N_HEADS = 8
QK_NOPE = 64
QK_ROPE = 32
ROPE_HALF = QK_ROPE // 2
V_HEAD = 64
Q_LORA = 256
KV_LORA = 128
GMLP_WIDTH = 512
GMLP_GROUPS = 8
GMLP_GROUP_DIM = 64
CHUNK = 128
N_EXPERTS = 32
N_EXPERT_GROUPS = 4
EXPERTS_PER_GROUP = 8
EXPERT_FF = 256
ROPE_THETA = 10000.0
EPS = 1e-6

LANES = 128
SUBLANES = 8
HEAD_PAD = LANES

TOKEN_TILE = 1024
POST_TILE = 1024
Q_TILE = 1024
K_TILE = 512
EXPERT_TILE_SMALL = 512
EXPERT_TILE_LARGE = 1024
SC_CORES = 2
SC_WORKERS = 32
SC_ROWS = 64
RANK_CHUNK = 256
VMEM_LIMIT = 56 * 1024 * 1024

U32 = jnp.uint32
PACKED_WIDTH = D_MODEL // 2
ROW_TILES = PACKED_WIDTH // LANES
_HI_MASK = 0xFFFF0000

_SQRT_2_OVER_PI = math.sqrt(2.0 / math.pi)


def _rms(x):
    return x * lax.rsqrt(jnp.mean(x * x, axis=-1, keepdims=True) + EPS)


def _gelu_tanh(x):
    return 0.5 * x * (1.0 + jnp.tanh(_SQRT_2_OVER_PI * (x + 0.044715 * (x * x * x))))


def _split_bf16(x):
    hi = x.astype(BF16)
    lo = (x - hi.astype(F32)).astype(BF16)
    return hi, lo


def _dot(a, b):
    return jnp.dot(a, b, preferred_element_type=F32)


def _bf16_bits(x):
    return lax.bitcast_convert_type(x.astype(BF16).astype(F32), U32)


def _load_token_rows(ref, n):
    w = jnp.concatenate([ref[pl.ds(c, n, stride=ROW_TILES), :] for c in range(ROW_TILES)], axis=1)
    lo = lax.bitcast_convert_type(w << 16, F32)
    hi = lax.bitcast_convert_type(w & jnp.uint32(_HI_MASK), F32)
    return jnp.concatenate([lo, hi], axis=1)


def _store_token_rows(ref, val):
    n = val.shape[0]
    w = (_bf16_bits(val[:, :PACKED_WIDTH]) >> 16) | (_bf16_bits(val[:, PACKED_WIDTH:]) & jnp.uint32(_HI_MASK))
    for c in range(ROW_TILES):
        ref[pl.ds(c, n, stride=ROW_TILES), :] = w[:, c * LANES:(c + 1) * LANES]


def _params(sem, vmem=VMEM_LIMIT):
    return pltpu.CompilerParams(dimension_semantics=sem, vmem_limit_bytes=vmem)


def _ada_kernel(c_ref, w_ref, b_ref, o_ref):
    c = c_ref[...]
    a = c * jax.nn.sigmoid(c)
    a_hi, a_lo = _split_bf16(a)
    w_hi, w_lo = _split_bf16(w_ref[...])
    o_ref[...] = _dot(a_hi, w_hi) + _dot(a_hi, w_lo) + _dot(a_lo, w_hi) + b_ref[...]


def _ada(c, w_ada, b_ada):
    nb = c.shape[0]
    n_out = w_ada.shape[1]
    blk = D_MODEL
    return pl.pallas_call(
        _ada_kernel,
        grid=(n_out // blk,),
        in_specs=[
            pl.BlockSpec((nb, D_MODEL), lambda j: (0, 0)),
            pl.BlockSpec((D_MODEL, blk), lambda j: (0, j)),
            pl.BlockSpec((1, blk), lambda j: (0, j)),
        ],
        out_specs=pl.BlockSpec((nb, blk), lambda j: (0, j)),
        out_shape=jax.ShapeDtypeStruct((nb, n_out), F32),
        compiler_params=_params(("arbitrary",)),
        name="ada",
    )(c, w_ada, b_ada.reshape(1, n_out))


_C_CQ = 0
_C_CKV = _C_CQ + Q_LORA
_C_KR = _C_CKV + KV_LORA
_C_U = _C_KR + LANES
_C_V = _C_U + GMLP_WIDTH
_C_END = _C_V + GMLP_WIDTH
_ROPE_LO = QK_NOPE
_ROLL_PARTNER = LANES - QK_ROPE


def _premix_kernel(x_ref, mod_ref, cos_ref, sin_ref, gpre_ref, win_ref, gq_ref, wuq_ref, gkv_ref, wukv_ref,
                   gv_ref, wsp_ref, bsp_ref, ggo_ref, q_ref, kt_ref, v_ref, sn_ref):
    x = x_ref[0]
    mod = mod_ref[0]
    shift1, scale1 = mod[0:1], mod[1:2]
    h = _rms(x) * (gpre_ref[...] * (1.0 + scale1)) + shift1
    z = _dot(h.astype(BF16), win_ref[...])

    cosb = cos_ref[...]
    sinb = sin_ref[...]
    lane = lax.broadcasted_iota(jnp.int32, (1, LANES), 1)
    nope_mask = jnp.where(lane < QK_NOPE, 1.0, 0.0).astype(F32)

    qscale = (QK_NOPE + QK_ROPE) ** -0.5 * math.log2(math.e)
    cq_tab = (nope_mask + cosb) * qscale
    sq_tab = sinb * qscale
    cqn = (_rms(z[:, _C_CQ:_C_CKV]) * gq_ref[...]).astype(BF16)
    qb = _dot(cqn, wuq_ref[...])
    for hd in range(N_HEADS):
        blk = qb[:, hd * HEAD_PAD:(hd + 1) * HEAD_PAD]
        qh = blk * cq_tab + pltpu.roll(blk, _ROLL_PARTNER, 1) * sq_tab
        q_ref[0, hd] = qh.astype(BF16)

    ckvn = (_rms(z[:, _C_CKV:_C_KR]) * gkv_ref[...]).astype(BF16)
    kvb = _dot(ckvn, wukv_ref[...])
    krb = z[:, _C_KR:_C_U]
    krope = krb * cosb + pltpu.roll(krb, _ROLL_PARTNER, 1) * sinb
    v_off = N_HEADS * HEAD_PAD
    for hd in range(N_HEADS):
        kh = kvb[:, hd * HEAD_PAD:(hd + 1) * HEAD_PAD] + krope
        for c in range(kt_ref.shape[2]):
            kt_ref[0, hd, c] = kh[c * K_TILE:(c + 1) * K_TILE].T.astype(BF16)
        ones_lane = V_HEAD if hd % 2 == 0 else 0
        vh = kvb[:, v_off + hd * HEAD_PAD:v_off + (hd + 1) * HEAD_PAD] + jnp.where(lane == ones_lane, 1.0, 0.0)
        v_ref[0, hd] = vh.astype(BF16)

    ua = _gelu_tanh(z[:, _C_U:_C_V])
    vn = (_rms(_gelu_tanh(z[:, _C_V:_C_END])) * gv_ref[...]).astype(BF16)
    n_tok = x.shape[0]
    bsp = bsp_ref[...]
    rows = []
    for n in range(n_tok // CHUNK):
        cols = []
        for j in range(GMLP_GROUPS // 2):
            rhs = vn[n * CHUNK:(n + 1) * CHUNK, j * LANES:(j + 1) * LANES]
            ab = _dot(wsp_ref[j], rhs)
            cols.append(jnp.where(lane < GMLP_GROUP_DIM, ab[:CHUNK], ab[CHUNK:]))
        rows.append(jnp.concatenate(cols, axis=1) + bsp)
    s = ua * jnp.concatenate(rows, axis=0)
    sn_ref[0] = (_rms(s) * ggo_ref[...]).astype(BF16)


def _premix(x, mod, P):
    nb, seq, _ = x.shape
    tm = TOKEN_TILE
    nck = seq // K_TILE
    const = lambda i, b: (0, 0)
    return pl.pallas_call(
        _premix_kernel,
        grid=(seq // tm, nb),
        in_specs=[
            pl.BlockSpec((1, tm, D_MODEL), lambda i, b: (b, i, 0)),
            pl.BlockSpec((1, SUBLANES, D_MODEL), lambda i, b: (b, 0, 0)),
            pl.BlockSpec((tm, LANES), lambda i, b: (i, 0)),
            pl.BlockSpec((tm, LANES), lambda i, b: (i, 0)),
            pl.BlockSpec((1, D_MODEL), const),
            pl.BlockSpec((D_MODEL, _C_END), const),
            pl.BlockSpec((1, Q_LORA), const),
            pl.BlockSpec((Q_LORA, N_HEADS * HEAD_PAD), const),
            pl.BlockSpec((1, KV_LORA), const),
            pl.BlockSpec((KV_LORA, 2 * N_HEADS * HEAD_PAD), const),
            pl.BlockSpec((1, GMLP_WIDTH), const),
            pl.BlockSpec((GMLP_GROUPS // 2, 2 * CHUNK, CHUNK), lambda i, b: (0, 0, 0)),
            pl.BlockSpec((CHUNK, GMLP_WIDTH), const),
            pl.BlockSpec((1, GMLP_WIDTH), const),
        ],
        out_specs=[
            pl.BlockSpec((1, N_HEADS, tm, HEAD_PAD), lambda i, b: (b, 0, i, 0)),
            pl.BlockSpec((1, N_HEADS, tm // K_TILE, HEAD_PAD, K_TILE), lambda i, b: (b, 0, i, 0, 0)),
            pl.BlockSpec((1, N_HEADS, tm, HEAD_PAD), lambda i, b: (b, 0, i, 0)),
            pl.BlockSpec((1, tm, GMLP_WIDTH), lambda i, b: (b, i, 0)),
        ],
        out_shape=[
            jax.ShapeDtypeStruct((nb, N_HEADS, seq, HEAD_PAD), BF16),
            jax.ShapeDtypeStruct((nb, N_HEADS, nck, HEAD_PAD, K_TILE), BF16),
            jax.ShapeDtypeStruct((nb, N_HEADS, seq, HEAD_PAD), BF16),
            jax.ShapeDtypeStruct((nb, seq, GMLP_WIDTH), BF16),
        ],
        compiler_params=_params(("arbitrary", "arbitrary")),
        name="premix",
    )(x, mod, P["cos"][:seq], P["sin"][:seq], P["g_pre1"], P["w_in"], P["g_q"], P["w_uq"], P["g_kv"], P["w_ukv"],
      P["g_v"], P["w_sp"], P["b_sp"], P["g_gmlp_out"])


def _attn_kernel(q_ref, kt_ref, v_ref, o_ref):
    n_chunks = kt_ref.shape[2]
    tk = kt_ref.shape[4]
    lane = lax.broadcasted_iota(jnp.int32, (1, LANES), 1)

    def one_head(hd, ones_lane):
        q = q_ref[0, hd]
        m = None
        acc = None
        for c in range(n_chunks):
            s = _dot(q, kt_ref[0, hd, c])
            v = v_ref[0, hd, c * tk:(c + 1) * tk, :]
            smax = jnp.max(s, axis=1, keepdims=True)
            if c == 0:
                m = smax
                acc = _dot(jnp.exp2((s - m).astype(BF16)), v)
            else:
                m_new = jnp.maximum(m, smax)
                acc = acc * jnp.exp2(m - m_new) + _dot(jnp.exp2((s - m_new).astype(BF16)), v)
                m = m_new
        row_sum = acc[:, ones_lane:ones_lane + 1]
        return acc * (1.0 / row_sum)

    def pair(j, carry):
        even = one_head(2 * j, V_HEAD)
        odd = one_head(2 * j + 1, 0)
        o_ref[0, j] = jnp.where(lane < V_HEAD, even, odd).astype(BF16)
        return carry

    lax.fori_loop(0, N_HEADS // 2, pair, 0)


def _attention(q, kt, v):
    nb, _, seq, _ = q.shape
    nck = kt.shape[2]
    return pl.pallas_call(
        _attn_kernel,
        grid=(nb, seq // Q_TILE),
        in_specs=[
            pl.BlockSpec((1, N_HEADS, Q_TILE, HEAD_PAD), lambda b, i: (b, 0, i, 0)),
            pl.BlockSpec((1, N_HEADS, nck, HEAD_PAD, K_TILE), lambda b, i: (b, 0, 0, 0, 0)),
            pl.BlockSpec((1, N_HEADS, seq, HEAD_PAD), lambda b, i: (b, 0, 0, 0)),
        ],
        out_specs=pl.BlockSpec((1, N_HEADS // 2, Q_TILE, LANES), lambda b, i: (b, 0, i, 0)),
        out_shape=jax.ShapeDtypeStruct((nb, N_HEADS // 2, seq, LANES), BF16),
        compiler_params=_params(("arbitrary", "arbitrary")),
        name="attn",
    )(q, kt, v)


_R_GROUP_ROW = N_EXPERTS


def _postmix_kernel(a_ref, sn_ref, x_ref, mod_ref, gao_ref, wout_ref, gpost_ref, gpre_ref, wr_ref, br_ref, tri_ref,
                    x1_ref, h2_ref, eid_ref, rank_ref, wcol_ref, cnt_ref, run_ref):
    first = jnp.logical_and(pl.program_id(0) == 0, pl.program_id(1) == 0)

    @pl.when(first)
    def _():
        run_ref[...] = jnp.zeros_like(run_ref)

    mod = mod_ref[0]
    gate1, shift2, scale2 = mod[2:3], mod[3:4], mod[4:5]
    a = jnp.concatenate([a_ref[0, j] for j in range(N_HEADS // 2)], axis=1).astype(F32)
    an = (_rms(a) * gao_ref[...]).astype(BF16)
    merged = jnp.concatenate([an, sn_ref[0]], axis=1)
    o = _dot(merged, wout_ref[...])
    x1 = x_ref[0] + gate1 * (_rms(o) * gpost_ref[...])
    x1_ref[0] = x1
    h2 = _rms(x1) * (gpre_ref[...] * (1.0 + scale2)) + shift2
    n_tok = h2.shape[0]
    _store_token_rows(h2_ref, h2)

    h_hi, h_lo = _split_bf16(h2)
    wr = wr_ref[...]
    hh = _dot(h_hi, wr)
    lh = _dot(h_lo, wr[:, :LANES])
    logits = hh[:, :LANES] + hh[:, LANES:] + lh + br_ref[...]
    lt = logits.T

    neg = jnp.float32(-jnp.inf)
    row8 = lax.broadcasted_iota(jnp.int32, (SUBLANES, n_tok), 0).astype(F32)
    lg = jnp.where(row8 < N_EXPERT_GROUPS, lt[_R_GROUP_ROW:_R_GROUP_ROW + SUBLANES], neg)
    gmax = jnp.max(lg, axis=0, keepdims=True)
    gi = jnp.min(jnp.where(lg == gmax, row8, float(SUBLANES)), axis=0, keepdims=True)
    pg_sel = 1.0 / jnp.sum(jnp.exp(lg - gmax), axis=0, keepdims=True)

    le = jnp.zeros((EXPERTS_PER_GROUP, n_tok), F32)
    for g in range(N_EXPERT_GROUPS):
        le = jnp.where(gi == float(g), lt[g * EXPERTS_PER_GROUP:(g + 1) * EXPERTS_PER_GROUP], le)
    v1 = jnp.max(le, axis=0, keepdims=True)
    i1 = jnp.min(jnp.where(le == v1, row8, float(SUBLANES)), axis=0, keepdims=True)
    le2 = jnp.where(row8 == i1, neg, le)
    v2 = jnp.max(le2, axis=0, keepdims=True)
    i2 = jnp.min(jnp.where(le2 == v2, row8, float(SUBLANES)), axis=0, keepdims=True)
    r = jnp.exp(v2 - v1)
    w1 = pg_sel / (1.0 + r)
    w2 = w1 * r
    e1 = gi * float(EXPERTS_PER_GROUP) + i1
    e2 = gi * float(EXPERTS_PER_GROUP) + i2
    eid_ref[0] = jnp.concatenate([e1, e2], axis=0).astype(jnp.int32)

    row32 = lax.broadcasted_iota(jnp.int32, (N_EXPERTS, n_tok), 0).astype(F32)
    hit1 = row32 == e1
    hit2 = row32 == e2
    onehot = jnp.where(jnp.logical_or(hit1, hit2), 1.0, 0.0)
    run = run_ref[...][:, 0:1]
    ranks1, ranks2 = [], []
    for c in range(n_tok // RANK_CHUNK):
        sl = slice(c * RANK_CHUNK, (c + 1) * RANK_CHUNK)
        oh = onehot[:, sl]
        before = _dot(oh.astype(BF16), tri_ref[...]) + run
        ranks1.append(jnp.sum(jnp.where(hit1[:, sl], before, 0.0), axis=0, keepdims=True))
        ranks2.append(jnp.sum(jnp.where(hit2[:, sl], before, 0.0), axis=0, keepdims=True))
        run = run + jnp.sum(oh, axis=1, keepdims=True)
    rank_ref[0] = jnp.concatenate(
        [jnp.concatenate(ranks1, axis=1), jnp.concatenate(ranks2, axis=1)], axis=0).astype(jnp.int32)
    run_b = jnp.broadcast_to(run, run_ref.shape)
    run_ref[...] = run_b
    cnt_ref[...] = run_b

    row128 = lax.broadcasted_iota(jnp.int32, (LANES, n_tok), 0)
    wt = jnp.where(row128 == 0, w1, jnp.where(row128 == 1, w2, 0.0))
    wcol_ref[0] = wt.T


def _postmix(a, sn, x, mod, P):
    nb, seq, _ = x.shape
    tm = POST_TILE
    const = lambda b, i: (0, 0)
    return pl.pallas_call(
        _postmix_kernel,
        grid=(nb, seq // tm),
        in_specs=[
            pl.BlockSpec((1, N_HEADS // 2, tm, LANES), lambda b, i: (b, 0, i, 0)),
            pl.BlockSpec((1, tm, GMLP_WIDTH), lambda b, i: (b, i, 0)),
            pl.BlockSpec((1, tm, D_MODEL), lambda b, i: (b, i, 0)),
            pl.BlockSpec((1, SUBLANES, D_MODEL), lambda b, i: (b, 0, 0)),
            pl.BlockSpec((1, N_HEADS * V_HEAD), const),
            pl.BlockSpec((D_MODEL, D_MODEL), const),
            pl.BlockSpec((1, D_MODEL), const),
            pl.BlockSpec((1, D_MODEL), const),
            pl.BlockSpec((D_MODEL, 2 * LANES), const),
            pl.BlockSpec((1, LANES), const),
            pl.BlockSpec((RANK_CHUNK, RANK_CHUNK), const),
        ],
        out_specs=[
            pl.BlockSpec((1, tm, D_MODEL), lambda b, i: (b, i, 0)),
            pl.BlockSpec((tm * ROW_TILES, LANES), lambda b, i: (b * (seq // tm) + i, 0)),
            pl.BlockSpec((1, 2, tm), lambda b, i: (b, 0, i)),
            pl.BlockSpec((1, 2, tm), lambda b, i: (b, 0, i)),
            pl.BlockSpec((1, tm, LANES), lambda b, i: (b, i, 0)),
            pl.BlockSpec((N_EXPERTS, LANES), const),
        ],
        out_shape=[
            jax.ShapeDtypeStruct((nb, seq, D_MODEL), F32),
            jax.ShapeDtypeStruct((nb * seq * ROW_TILES, LANES), U32),
            jax.ShapeDtypeStruct((nb, 2, seq), jnp.int32),
            jax.ShapeDtypeStruct((nb, 2, seq), jnp.int32),
            jax.ShapeDtypeStruct((nb, seq, LANES), F32),
            jax.ShapeDtypeStruct((N_EXPERTS, LANES), F32),
        ],
        scratch_shapes=[pltpu.VMEM((N_EXPERTS, LANES), F32)],
        compiler_params=_params(("arbitrary", "arbitrary")),
        name="postmix",
    )(a, sn, x, mod, P["g_attn_out"], P["w_out"], P["g_post1"], P["g_pre2"], P["w_router"], P["b_router"], P["tri"])


def _sc_mesh():
    return plsc.VectorSubcoreMesh(core_axis_name="c", subcore_axis_name="s")


def _sc_worker():
    return lax.axis_index("s") * SC_CORES + lax.axis_index("c")


def _sc_gather_rows(table, idx):
    n = idx.shape[0]
    per_w = n // SC_WORKERS
    steps = per_w // SC_ROWS
    idx3 = idx.reshape(SC_WORKERS, steps, SC_ROWS)

    @functools.partial(
        pl.kernel, mesh=_sc_mesh(),
        out_type=jax.ShapeDtypeStruct((n, ROW_TILES, LANES), U32),
        scratch_types=[pltpu.VMEM((steps, SC_ROWS), jnp.int32), pltpu.VMEM((SC_ROWS, ROW_TILES, LANES), U32),
                       pltpu.SemaphoreType.DMA],
        name="sc_gather_rows",
    )
    def gather(table_hbm, idx_hbm, out_hbm, idx_v, rows_v, sem):
        wid = _sc_worker()
        pltpu.sync_copy(idx_hbm.at[wid], idx_v)

        @pl.loop(0, steps)
        def _(j):
            pltpu.async_copy(table_hbm.at[idx_v.at[j]], rows_v, sem).wait()
            pltpu.sync_copy(rows_v, out_hbm.at[pl.ds(wid * per_w + j * SC_ROWS, SC_ROWS)])

    return gather(table, idx3)


def _sc_scatter_rows(src, idx, n_dst):
    n_dup, n = idx.shape
    per_w = n // SC_WORKERS
    steps = per_w // SC_ROWS
    idx4 = idx.reshape(n_dup, SC_WORKERS, steps, SC_ROWS).transpose(1, 0, 2, 3)

    @functools.partial(
        pl.kernel, mesh=_sc_mesh(),
        out_type=jax.ShapeDtypeStruct((n_dst, ROW_TILES, LANES), U32),
        scratch_types=[pltpu.VMEM((n_dup, steps, SC_ROWS), jnp.int32), pltpu.VMEM((SC_ROWS, ROW_TILES, LANES), U32),
                       pltpu.SemaphoreType.DMA],
        name="sc_scatter_rows",
    )
    def scatter(src_hbm, idx_hbm, dst_hbm, idx_v, rows_v, sem):
        wid = _sc_worker()
        pltpu.sync_copy(idx_hbm.at[wid], idx_v)

        @pl.loop(0, steps)
        def _(j):
            pltpu.sync_copy(src_hbm.at[pl.ds(wid * per_w + j * SC_ROWS, SC_ROWS)], rows_v)
            for k in range(n_dup):
                pltpu.async_copy(rows_v, dst_hbm.at[idx_v.at[k, j]], sem).wait()

    return scatter(src, idx4)


def _expert_kernel(te_ref, nv_ref, xs_ref, wg_ref, wu_ref, wd_ref, ys_ref, wgu_bf, wd_bf):
    i = pl.program_id(0)
    valid = i < nv_ref[0]
    new_expert = jnp.logical_or(i == 0, te_ref[i] != te_ref[jnp.maximum(i - 1, 0)])

    @pl.when(jnp.logical_and(valid, new_expert))
    def _():
        wgu_bf[:, :EXPERT_FF] = wg_ref[0].astype(BF16)
        wgu_bf[:, EXPERT_FF:] = wu_ref[0].astype(BF16)
        wd_bf[...] = wd_ref[0].astype(BF16)

    @pl.when(valid)
    def _():
        x = _load_token_rows(xs_ref, xs_ref.shape[0] // ROW_TILES).astype(BF16)
        gu = _dot(x, wgu_bf[...])
        g, u = gu[:, :EXPERT_FF], gu[:, EXPERT_FF:]
        act = (g * jax.nn.sigmoid(g) * u).astype(BF16)
        _store_token_rows(ys_ref, _dot(act, wd_bf[...]))

    @pl.when(jnp.logical_not(valid))
    def _():
        ys_ref[...] = jnp.zeros_like(ys_ref)


def _expert_tile(n_tok):
    return EXPERT_TILE_LARGE if 2 * n_tok // N_EXPERTS >= 2 * EXPERT_TILE_LARGE else EXPERT_TILE_SMALL


def _experts(xs, tile_expert, n_valid, tile, P):
    rows_blk = tile * ROW_TILES
    n_tiles = xs.shape[0] // rows_blk

    def row_map(i, te, nv):
        return (jnp.minimum(i, nv[0] - 1), 0)

    def out_map(i, te, nv):
        return (i, 0)

    def w_map(i, te, nv):
        return (te[jnp.minimum(i, nv[0] - 1)], 0, 0)

    return pl.pallas_call(
        _expert_kernel,
        grid_spec=pltpu.PrefetchScalarGridSpec(
            num_scalar_prefetch=2,
            grid=(n_tiles,),
            in_specs=[
                pl.BlockSpec((rows_blk, LANES), row_map),
                pl.BlockSpec((1, D_MODEL, EXPERT_FF), w_map),
                pl.BlockSpec((1, D_MODEL, EXPERT_FF), w_map),
                pl.BlockSpec((1, EXPERT_FF, D_MODEL), w_map),
            ],
            out_specs=pl.BlockSpec((rows_blk, LANES), out_map),
            scratch_shapes=[pltpu.VMEM((D_MODEL, 2 * EXPERT_FF), BF16), pltpu.VMEM((EXPERT_FF, D_MODEL), BF16)],
        ),
        out_shape=jax.ShapeDtypeStruct(xs.shape, U32),
        compiler_params=_params(("arbitrary",)),
        name="experts",
    )(tile_expert, n_valid, xs, P["w_gate"], P["w_up"], P["w_down"])


def _final_kernel(y0_ref, y1_ref, wcol_ref, x1_ref, mod_ref, gpost_ref, o_ref):
    w = wcol_ref[0]
    w0, w1 = w[:, 0:1], w[:, 1:2]
    n_tok = w.shape[0]
    m = w0 * _load_token_rows(y0_ref, n_tok) + w1 * _load_token_rows(y1_ref, n_tok)
    gate2 = mod_ref[0][5:6]
    o_ref[0] = x1_ref[0] + gate2 * (_rms(m) * gpost_ref[...])


def _final(yg, wcol, x1, mod, P):
    nb, seq, _ = x1.shape
    tm = POST_TILE
    nt = seq // tm
    n_tok_tiles = nb * nt
    return pl.pallas_call(
        _final_kernel,
        grid=(nb, nt),
        in_specs=[
            pl.BlockSpec((tm * ROW_TILES, LANES), lambda b, i: (b * nt + i, 0)),
            pl.BlockSpec((tm * ROW_TILES, LANES), lambda b, i: (n_tok_tiles + b * nt + i, 0)),
            pl.BlockSpec((1, tm, LANES), lambda b, i: (b, i, 0)),
            pl.BlockSpec((1, tm, D_MODEL), lambda b, i: (b, i, 0)),
            pl.BlockSpec((1, SUBLANES, D_MODEL), lambda b, i: (b, 0, 0)),
            pl.BlockSpec((1, D_MODEL), lambda b, i: (0, 0)),
        ],
        out_specs=pl.BlockSpec((1, tm, D_MODEL), lambda b, i: (b, i, 0)),
        out_shape=jax.ShapeDtypeStruct((nb, seq, D_MODEL), F32),
        compiler_params=_params(("arbitrary", "arbitrary")),
        name="final",
    )(yg, yg, wcol, x1, mod, P["g_post2"])


def _prepare(w):
    f = lambda a: a.astype(F32)
    P = {}
    for k in ("g_pre1", "g_post1", "g_pre2", "g_post2", "g_q", "g_kv", "g_attn_out", "g_gmlp_out"):
        P[k] = f(w[k]).reshape(1, -1)
    P["g_v"] = f(w["g_v_gmlp"]).reshape(1, -1)

    w_in = f(w["w_in"])
    o0, o1, o2, o3 = Q_LORA, Q_LORA + KV_LORA, Q_LORA + KV_LORA + QK_ROPE, Q_LORA + KV_LORA + QK_ROPE + GMLP_WIDTH
    w_kr = w_in[:, o1:o2]
    kr_partner = jnp.concatenate([-w_kr[:, ROPE_HALF:], w_kr[:, :ROPE_HALF]], axis=1)
    rope_blk = jnp.concatenate([jnp.zeros((D_MODEL, QK_NOPE), F32), w_kr, kr_partner], axis=1)
    P["w_in"] = jnp.concatenate([w_in[:, :o1], rope_blk, w_in[:, o2:o3], w_in[:, o3:]], axis=1).astype(BF16)

    w_uq = f(w["w_uq"]).reshape(Q_LORA, N_HEADS, QK_NOPE + QK_ROPE)
    q_rope = w_uq[:, :, QK_NOPE:]
    q_partner = jnp.concatenate([-q_rope[:, :, ROPE_HALF:], q_rope[:, :, :ROPE_HALF]], axis=2)
    P["w_uq"] = jnp.concatenate([w_uq, q_partner], axis=2).reshape(Q_LORA, N_HEADS * HEAD_PAD).astype(BF16)

    w_ukv = f(w["w_ukv"]).reshape(KV_LORA, N_HEADS, QK_NOPE + V_HEAD)
    zeros = jnp.zeros((KV_LORA, N_HEADS, HEAD_PAD - QK_NOPE), F32)
    w_k = jnp.concatenate([w_ukv[:, :, :QK_NOPE], zeros], axis=2)
    w_v = w_ukv[:, :, QK_NOPE:]
    even = (jnp.arange(N_HEADS) % 2 == 0)[None, :, None]
    zv = jnp.zeros_like(w_v)
    w_v = jnp.concatenate([jnp.where(even, w_v, zv), jnp.where(even, zv, w_v)], axis=2)
    P["w_ukv"] = jnp.concatenate([w_k.reshape(KV_LORA, -1), w_v.reshape(KV_LORA, -1)], axis=1).astype(BF16)

    P["w_sp"] = f(w["w_spatial"]).reshape(GMLP_GROUPS // 2, 2 * CHUNK, CHUNK).astype(BF16)
    P["b_sp"] = jnp.repeat(f(w["b_spatial"]).T, GMLP_GROUP_DIM, axis=1)

    P["w_out"] = f(w["w_out"]).astype(BF16)

    pad = jnp.zeros((D_MODEL, LANES - N_EXPERTS - N_EXPERT_GROUPS), F32)
    wr = jnp.concatenate([f(w["w_router_expert"]), f(w["w_router_group"]), pad], axis=1)
    wr_hi = wr.astype(BF16)
    wr_lo = (wr - wr_hi.astype(F32)).astype(BF16)
    P["w_router"] = jnp.concatenate([wr_hi, wr_lo], axis=1)
    P["b_router"] = jnp.concatenate(
        [f(w["b_router_expert"]), f(w["b_router_group"]), jnp.zeros((LANES - N_EXPERTS - N_EXPERT_GROUPS,), F32)]
    ).reshape(1, LANES)
    P["tri"] = jnp.triu(jnp.ones((RANK_CHUNK, RANK_CHUNK), F32), k=1).astype(BF16)

    P["w_gate"], P["w_up"], P["w_down"] = f(w["w_gate"]), f(w["w_up"]), f(w["w_down"])
    return P


def _rope_tables(seq):
    inv = ROPE_THETA ** (-jnp.arange(ROPE_HALF, dtype=F32) / ROPE_HALF)
    ang = jnp.arange(seq, dtype=F32)[:, None] * inv[None, :]
    z_lo = jnp.zeros((seq, _ROPE_LO), F32)
    z_hi = jnp.zeros((seq, LANES - _ROPE_LO - QK_ROPE), F32)
    cos = jnp.concatenate([z_lo, jnp.cos(ang), jnp.cos(ang), z_hi], axis=1)
    sin = jnp.concatenate([z_lo, jnp.sin(ang), jnp.sin(ang), z_hi], axis=1)
    return cos, sin


def _layer(x, mod, P):
    nb, seq, _ = x.shape
    n_tok = nb * seq
    q, kt, v, sn = _premix(x, mod, P)
    a = _attention(q, kt, v)
    x1, h2rows, eid, rank, wcol, counts = _postmix(a, sn, x, mod, P)

    tile = _expert_tile(n_tok)
    cnt = counts[:, 0].astype(jnp.int32)
    padded = ((cnt + tile - 1) // tile) * tile
    ends = jnp.cumsum(padded)
    starts = ends - padded
    eflat = jnp.transpose(eid, (1, 0, 2)).reshape(2, n_tok)
    rflat = jnp.transpose(rank, (1, 0, 2)).reshape(2, n_tok)
    onehot = eflat[:, :, None] == jnp.arange(N_EXPERTS, dtype=jnp.int32)[None, None, :]
    pos = rflat + jnp.sum(jnp.where(onehot, starts[None, None, :], 0), axis=2)
    n_rows = 2 * n_tok + N_EXPERTS * tile
    n_tiles = n_rows // tile
    tile_start = jnp.arange(n_tiles, dtype=jnp.int32) * tile
    tile_expert = jnp.minimum(
        jnp.sum((tile_start[:, None] >= ends[None, :]).astype(jnp.int32), axis=1), N_EXPERTS - 1).astype(jnp.int32)
    n_valid = (ends[-1:] // tile).astype(jnp.int32)

    as_tiles = lambda a: a.reshape(-1, ROW_TILES, LANES)
    as_rows = lambda a: a.reshape(-1, LANES)
    xs = _sc_scatter_rows(as_tiles(h2rows), pos, n_rows)
    ys = _experts(as_rows(xs), tile_expert, n_valid, tile, P)
    yg = as_rows(_sc_gather_rows(as_tiles(ys), pos.reshape(2 * n_tok)))
    return _final(yg, wcol, x1, mod, P)


def kernel(x_prompt, x_sample, c_prompt, c_sample, w_ada, b_ada, g_pre1, g_post1, g_pre2, g_post2, w_in, g_q, w_uq,
           g_kv, w_ukv, g_v_gmlp, w_spatial, b_spatial, g_attn_out, g_gmlp_out, w_out, w_router_group,
           b_router_group, w_router_expert, b_router_expert, w_gate, w_up, w_down):
    P = _prepare(dict(
        g_pre1=g_pre1, g_post1=g_post1, g_pre2=g_pre2, g_post2=g_post2, w_in=w_in, g_q=g_q, w_uq=w_uq, g_kv=g_kv,
        w_ukv=w_ukv, g_v_gmlp=g_v_gmlp, w_spatial=w_spatial, b_spatial=b_spatial, g_attn_out=g_attn_out,
        g_gmlp_out=g_gmlp_out, w_out=w_out, w_router_group=w_router_group, b_router_group=b_router_group,
        w_router_expert=w_router_expert, b_router_expert=b_router_expert, w_gate=w_gate, w_up=w_up, w_down=w_down))
    P["cos"], P["sin"] = _rope_tables(max(x_prompt.shape[1], x_sample.shape[1]))

    nbp = c_prompt.shape[0]
    c_all = jnp.concatenate([c_prompt, c_sample], axis=0).astype(F32)
    mod = _ada(c_all, w_ada.astype(F32), b_ada.astype(F32))
    mod = mod.reshape(c_all.shape[0], 6, D_MODEL)
    mod = jnp.concatenate([mod, jnp.zeros((c_all.shape[0], SUBLANES - 6, D_MODEL), F32)], axis=1)

    y_prompt = _layer(x_prompt, mod[:nbp], P)
    y_sample = _layer(x_sample, mod[nbp:], P)
    return (y_prompt, y_sample)
```

```python
import functools
import math

import jax
import jax.numpy as jnp
from jax import lax
from jax.experimental import pallas as pl
from jax.experimental.pallas import tpu as pltpu
from jax.experimental.pallas import tpu_sc as plsc

F32 = jnp.float32
BF16 = jnp.bfloat16

D_MODEL = 1024
N_HEADS = 8
QK_NOPE = 64
QK_ROPE = 32
ROPE_HALF = QK_ROPE // 2
V_HEAD = 64
Q_LORA = 256
KV_LORA = 128
GMLP_WIDTH = 512
GMLP_GROUPS = 8
GMLP_GROUP_DIM = 64
CHUNK = 128
N_EXPERTS = 32
N_EXPERT_GROUPS = 4
EXPERTS_PER_GROUP = 8
EXPERT_FF = 256
ROPE_THETA = 10000.0
EPS = 1e-6

LANES = 128
SUBLANES = 8
HEAD_PAD = LANES

TOKEN_TILE = 1024
POST_TILE = 1024
Q_TILE = 1024
K_TILE = 512
EXPERT_TILE_SMALL = 512
EXPERT_TILE_LARGE = 1024
SC_CORES = 2
SC_WORKERS = 32
SC_ROWS = 128
RANK_CHUNK = 256
VMEM_LIMIT = 56 * 1024 * 1024

U32 = jnp.uint32
PACKED_WIDTH = D_MODEL // 2
ROW_TILES = PACKED_WIDTH // LANES
_HI_MASK = 0xFFFF0000

_SQRT_2_OVER_PI = math.sqrt(2.0 / math.pi)


def _rms(x):
    return x * lax.rsqrt(jnp.mean(x * x, axis=-1, keepdims=True) + EPS)


def _gelu_tanh(x):
    return 0.5 * x * (1.0 + jnp.tanh(_SQRT_2_OVER_PI * (x + 0.044715 * (x * x * x))))


def _split_bf16(x):
    hi = x.astype(BF16)
    lo = (x - hi.astype(F32)).astype(BF16)
    return hi, lo


def _dot(a, b):
    return jnp.dot(a, b, preferred_element_type=F32)


def _bf16_bits(x):
    return lax.bitcast_convert_type(x.astype(BF16).astype(F32), U32)


def _load_token_rows(ref, n):
    w = jnp.concatenate([ref[pl.ds(c, n, stride=ROW_TILES), :] for c in range(ROW_TILES)], axis=1)
    lo = lax.bitcast_convert_type(w << 16, F32)
    hi = lax.bitcast_convert_type(w & jnp.uint32(_HI_MASK), F32)
    return jnp.concatenate([lo, hi], axis=1)


def _store_token_rows(ref, val):
    n = val.shape[0]
    w = (_bf16_bits(val[:, :PACKED_WIDTH]) >> 16) | (_bf16_bits(val[:, PACKED_WIDTH:]) & jnp.uint32(_HI_MASK))
    for c in range(ROW_TILES):
        ref[pl.ds(c, n, stride=ROW_TILES), :] = w[:, c * LANES:(c + 1) * LANES]


def _params(sem, vmem=VMEM_LIMIT):
    return pltpu.CompilerParams(dimension_semantics=sem, vmem_limit_bytes=vmem)


def _ada_kernel(c_ref, w_ref, b_ref, o_ref):
    c = c_ref[...]
    a = c * jax.nn.sigmoid(c)
    a_hi, a_lo = _split_bf16(a)
    w_hi, w_lo = _split_bf16(w_ref[...])
    o_ref[...] = _dot(a_hi, w_hi) + _dot(a_hi, w_lo) + _dot(a_lo, w_hi) + b_ref[...]


def _ada(c, w_ada, b_ada):
    nb = c.shape[0]
    n_out = w_ada.shape[1]
    blk = D_MODEL
    return pl.pallas_call(
        _ada_kernel,
        grid=(n_out // blk,),
        in_specs=[
            pl.BlockSpec((nb, D_MODEL), lambda j: (0, 0)),
            pl.BlockSpec((D_MODEL, blk), lambda j: (0, j)),
            pl.BlockSpec((1, blk), lambda j: (0, j)),
        ],
        out_specs=pl.BlockSpec((nb, blk), lambda j: (0, j)),
        out_shape=jax.ShapeDtypeStruct((nb, n_out), F32),
        compiler_params=_params(("arbitrary",)),
        name="ada",
    )(c, w_ada, b_ada.reshape(1, n_out))


_C_CQ = 0
_C_CKV = _C_CQ + Q_LORA
_C_KR = _C_CKV + KV_LORA
_C_U = _C_KR + LANES
_C_V = _C_U + GMLP_WIDTH
_C_END = _C_V + GMLP_WIDTH
_ROPE_LO = QK_NOPE
_ROLL_PARTNER = LANES - QK_ROPE


def _premix_kernel(x_ref, mod_ref, cos_ref, sin_ref, gpre_ref, win_ref, gq_ref, wuq_ref, gkv_ref, wukv_ref,
                   gv_ref, wsp_ref, bsp_ref, ggo_ref, q_ref, kt_ref, v_ref, sn_ref):
    x = x_ref[0]
    mod = mod_ref[0]
    shift1, scale1 = mod[0:1], mod[1:2]
    h = _rms(x) * (gpre_ref[...] * (1.0 + scale1)) + shift1
    z = _dot(h.astype(BF16), win_ref[...])

    cosb = cos_ref[...]
    sinb = sin_ref[...]
    lane = lax.broadcasted_iota(jnp.int32, (1, LANES), 1)
    nope_mask = jnp.where(lane < QK_NOPE, 1.0, 0.0).astype(F32)

    qscale = (QK_NOPE + QK_ROPE) ** -0.5 * math.log2(math.e)
    cq_tab = (nope_mask + cosb) * qscale
    sq_tab = sinb * qscale
    cqn = (_rms(z[:, _C_CQ:_C_CKV]) * gq_ref[...]).astype(BF16)
    qb = _dot(cqn, wuq_ref[...])
    for hd in range(N_HEADS):
        blk = qb[:, hd * HEAD_PAD:(hd + 1) * HEAD_PAD]
        qh = blk * cq_tab + pltpu.roll(blk, _ROLL_PARTNER, 1) * sq_tab
        q_ref[0, hd] = qh.astype(BF16)

    ckvn = (_rms(z[:, _C_CKV:_C_KR]) * gkv_ref[...]).astype(BF16)
    kvb = _dot(ckvn, wukv_ref[...])
    krb = z[:, _C_KR:_C_U]
    krope = krb * cosb + pltpu.roll(krb, _ROLL_PARTNER, 1) * sinb
    v_off = N_HEADS * HEAD_PAD
    for hd in range(N_HEADS):
        kh = kvb[:, hd * HEAD_PAD:(hd + 1) * HEAD_PAD] + krope
        for c in range(kt_ref.shape[2]):
            kt_ref[0, hd, c] = kh[c * K_TILE:(c + 1) * K_TILE].T.astype(BF16)
        ones_lane = V_HEAD if hd % 2 == 0 else 0
        vh = kvb[:, v_off + hd * HEAD_PAD:v_off + (hd + 1) * HEAD_PAD] + jnp.where(lane == ones_lane, 1.0, 0.0)
        v_ref[0, hd] = vh.astype(BF16)

    ua = _gelu_tanh(z[:, _C_U:_C_V])
    vn = (_rms(_gelu_tanh(z[:, _C_V:_C_END])) * gv_ref[...]).astype(BF16)
    n_tok = x.shape[0]
    bsp = bsp_ref[...]
    rows = []
    for n in range(n_tok // CHUNK):
        cols = []
        for j in range(GMLP_GROUPS // 2):
            rhs = vn[n * CHUNK:(n + 1) * CHUNK, j * LANES:(j + 1) * LANES]
            ab = _dot(wsp_ref[j], rhs)
            cols.append(jnp.where(lane < GMLP_GROUP_DIM, ab[:CHUNK], ab[CHUNK:]))
        rows.append(jnp.concatenate(cols, axis=1) + bsp)
    s = ua * jnp.concatenate(rows, axis=0)
    sn_ref[0] = (_rms(s) * ggo_ref[...]).astype(BF16)


def _premix(x, mod, P):
    nb, seq, _ = x.shape
    tm = TOKEN_TILE
    nck = seq // K_TILE
    const = lambda i, b: (0, 0)
    return pl.pallas_call(
        _premix_kernel,
        grid=(seq // tm, nb),
        in_specs=[
            pl.BlockSpec((1, tm, D_MODEL), lambda i, b: (b, i, 0)),
            pl.BlockSpec((1, SUBLANES, D_MODEL), lambda i, b: (b, 0, 0)),
            pl.BlockSpec((tm, LANES), lambda i, b: (i, 0)),
            pl.BlockSpec((tm, LANES), lambda i, b: (i, 0)),
            pl.BlockSpec((1, D_MODEL), const),
            pl.BlockSpec((D_MODEL, _C_END), const),
            pl.BlockSpec((1, Q_LORA), const),
            pl.BlockSpec((Q_LORA, N_HEADS * HEAD_PAD), const),
            pl.BlockSpec((1, KV_LORA), const),
            pl.BlockSpec((KV_LORA, 2 * N_HEADS * HEAD_PAD), const),
            pl.BlockSpec((1, GMLP_WIDTH), const),
            pl.BlockSpec((GMLP_GROUPS // 2, 2 * CHUNK, CHUNK), lambda i, b: (0, 0, 0)),
            pl.BlockSpec((CHUNK, GMLP_WIDTH), const),
            pl.BlockSpec((1, GMLP_WIDTH), const),
        ],
        out_specs=[
            pl.BlockSpec((1, N_HEADS, tm, HEAD_PAD), lambda i, b: (b, 0, i, 0)),
            pl.BlockSpec((1, N_HEADS, tm // K_TILE, HEAD_PAD, K_TILE), lambda i, b: (b, 0, i, 0, 0)),
            pl.BlockSpec((1, N_HEADS, tm, HEAD_PAD), lambda i, b: (b, 0, i, 0)),
            pl.BlockSpec((1, tm, GMLP_WIDTH), lambda i, b: (b, i, 0)),
        ],
        out_shape=[
            jax.ShapeDtypeStruct((nb, N_HEADS, seq, HEAD_PAD), BF16),
            jax.ShapeDtypeStruct((nb, N_HEADS, nck, HEAD_PAD, K_TILE), BF16),
            jax.ShapeDtypeStruct((nb, N_HEADS, seq, HEAD_PAD), BF16),
            jax.ShapeDtypeStruct((nb, seq, GMLP_WIDTH), BF16),
        ],
        compiler_params=_params(("arbitrary", "arbitrary")),
        name="premix",
    )(x, mod, P["cos"][:seq], P["sin"][:seq], P["g_pre1"], P["w_in"], P["g_q"], P["w_uq"], P["g_kv"], P["w_ukv"],
      P["g_v"], P["w_sp"], P["b_sp"], P["g_gmlp_out"])


def _attn_kernel(q_ref, kt_ref, v_ref, o_ref):
    n_chunks = kt_ref.shape[2]
    tk = kt_ref.shape[4]
    lane = lax.broadcasted_iota(jnp.int32, (1, LANES), 1)

    def one_head(hd, ones_lane):
        q = q_ref[0, hd]
        m = None
        acc = None
        for c in range(n_chunks):
            s = _dot(q, kt_ref[0, hd, c])
            v = v_ref[0, hd, c * tk:(c + 1) * tk, :]
            smax = jnp.max(s, axis=1, keepdims=True)
            if c == 0:
                m = smax
                acc = _dot(jnp.exp2((s - m).astype(BF16)), v)
            else:
                m_new = jnp.maximum(m, smax)
                acc = acc * jnp.exp2(m - m_new) + _dot(jnp.exp2((s - m_new).astype(BF16)), v)
                m = m_new
        row_sum = acc[:, ones_lane:ones_lane + 1]
        return acc * (1.0 / row_sum)

    def pair(j, carry):
        even = one_head(2 * j, V_HEAD)
        odd = one_head(2 * j + 1, 0)
        o_ref[0, j] = jnp.where(lane < V_HEAD, even, odd).astype(BF16)
        return carry

    lax.fori_loop(0, N_HEADS // 2, pair, 0)


def _attention(q, kt, v):
    nb, _, seq, _ = q.shape
    nck = kt.shape[2]
    return pl.pallas_call(
        _attn_kernel,
        grid=(nb, seq // Q_TILE),
        in_specs=[
            pl.BlockSpec((1, N_HEADS, Q_TILE, HEAD_PAD), lambda b, i: (b, 0, i, 0)),
            pl.BlockSpec((1, N_HEADS, nck, HEAD_PAD, K_TILE), lambda b, i: (b, 0, 0, 0, 0)),
            pl.BlockSpec((1, N_HEADS, seq, HEAD_PAD), lambda b, i: (b, 0, 0, 0)),
        ],
        out_specs=pl.BlockSpec((1, N_HEADS // 2, Q_TILE, LANES), lambda b, i: (b, 0, i, 0)),
        out_shape=jax.ShapeDtypeStruct((nb, N_HEADS // 2, seq, LANES), BF16),
        compiler_params=_params(("arbitrary", "arbitrary")),
        name="attn",
    )(q, kt, v)


_R_GROUP_ROW = N_EXPERTS


def _postmix_kernel(a_ref, sn_ref, x_ref, mod_ref, gao_ref, wout_ref, gpost_ref, gpre_ref, wr_ref, br_ref, tri_ref,
                    x1_ref, h2_ref, eid_ref, rank_ref, wcol_ref, cnt_ref, run_ref):
    first = jnp.logical_and(pl.program_id(0) == 0, pl.program_id(1) == 0)

    @pl.when(first)
    def _():
        run_ref[...] = jnp.zeros_like(run_ref)

    mod = mod_ref[0]
    gate1, shift2, scale2 = mod[2:3], mod[3:4], mod[4:5]
    a = jnp.concatenate([a_ref[0, j] for j in range(N_HEADS // 2)], axis=1).astype(F32)
    an = (_rms(a) * gao_ref[...]).astype(BF16)
    merged = jnp.concatenate([an, sn_ref[0]], axis=1)
    o = _dot(merged, wout_ref[...])
    x1 = x_ref[0] + gate1 * (_rms(o) * gpost_ref[...])
    x1_ref[0] = x1
    h2 = _rms(x1) * (gpre_ref[...] * (1.0 + scale2)) + shift2
    n_tok = h2.shape[0]
    _store_token_rows(h2_ref, h2)

    h_hi, h_lo = _split_bf16(h2)
    wr = wr_ref[...]
    hh = _dot(h_hi, wr)
    lh = _dot(h_lo, wr[:, :LANES])
    logits = hh[:, :LANES] + hh[:, LANES:] + lh + br_ref[...]
    lt = logits.T

    neg = jnp.float32(-jnp.inf)
    row8 = lax.broadcasted_iota(jnp.int32, (SUBLANES, n_tok), 0).astype(F32)
    lg = jnp.where(row8 < N_EXPERT_GROUPS, lt[_R_GROUP_ROW:_R_GROUP_ROW + SUBLANES], neg)
    gmax = jnp.max(lg, axis=0, keepdims=True)
    gi = jnp.min(jnp.where(lg == gmax, row8, float(SUBLANES)), axis=0, keepdims=True)
    pg_sel = 1.0 / jnp.sum(jnp.exp(lg - gmax), axis=0, keepdims=True)

    le = jnp.zeros((EXPERTS_PER_GROUP, n_tok), F32)
    for g in range(N_EXPERT_GROUPS):
        le = jnp.where(gi == float(g), lt[g * EXPERTS_PER_GROUP:(g + 1) * EXPERTS_PER_GROUP], le)
    v1 = jnp.max(le, axis=0, keepdims=True)
    i1 = jnp.min(jnp.where(le == v1, row8, float(SUBLANES)), axis=0, keepdims=True)
    le2 = jnp.where(row8 == i1, neg, le)
    v2 = jnp.max(le2, axis=0, keepdims=True)
    i2 = jnp.min(jnp.where(le2 == v2, row8, float(SUBLANES)), axis=0, keepdims=True)
    r = jnp.exp(v2 - v1)
    w1 = pg_sel / (1.0 + r)
    w2 = w1 * r
    e1 = gi * float(EXPERTS_PER_GROUP) + i1
    e2 = gi * float(EXPERTS_PER_GROUP) + i2
    eid_ref[0] = jnp.concatenate([e1, e2], axis=0).astype(jnp.int32)

    row32 = lax.broadcasted_iota(jnp.int32, (N_EXPERTS, n_tok), 0).astype(F32)
    hit1 = row32 == e1
    hit2 = row32 == e2
    onehot = jnp.where(jnp.logical_or(hit1, hit2), 1.0, 0.0)
    run = run_ref[...][:, 0:1]
    ranks1, ranks2 = [], []
    for c in range(n_tok // RANK_CHUNK):
        sl = slice(c * RANK_CHUNK, (c + 1) * RANK_CHUNK)
        oh = onehot[:, sl]
        before = _dot(oh.astype(BF16), tri_ref[...]) + run
        ranks1.append(jnp.sum(jnp.where(hit1[:, sl], before, 0.0), axis=0, keepdims=True))
        ranks2.append(jnp.sum(jnp.where(hit2[:, sl], before, 0.0), axis=0, keepdims=True))
        run = run + jnp.sum(oh, axis=1, keepdims=True)
    rank_ref[0] = jnp.concatenate(
        [jnp.concatenate(ranks1, axis=1), jnp.concatenate(ranks2, axis=1)], axis=0).astype(jnp.int32)
    run_b = jnp.broadcast_to(run, run_ref.shape)
    run_ref[...] = run_b
    cnt_ref[...] = run_b

    row128 = lax.broadcasted_iota(jnp.int32, (LANES, n_tok), 0)
    wt = jnp.where(row128 == 0, w1, jnp.where(row128 == 1, w2, 0.0))
    wcol_ref[0] = wt.T


def _postmix(a, sn, x, mod, P):
    nb, seq, _ = x.shape
    tm = POST_TILE
    const = lambda b, i: (0, 0)
    return pl.pallas_call(
        _postmix_kernel,
        grid=(nb, seq // tm),
        in_specs=[
            pl.BlockSpec((1, N_HEADS // 2, tm, LANES), lambda b, i: (b, 0, i, 0)),
            pl.BlockSpec((1, tm, GMLP_WIDTH), lambda b, i: (b, i, 0)),
            pl.BlockSpec((1, tm, D_MODEL), lambda b, i: (b, i, 0)),
            pl.BlockSpec((1, SUBLANES, D_MODEL), lambda b, i: (b, 0, 0)),
            pl.BlockSpec((1, N_HEADS * V_HEAD), const),
            pl.BlockSpec((D_MODEL, D_MODEL), const),
            pl.BlockSpec((1, D_MODEL), const),
            pl.BlockSpec((1, D_MODEL), const),
            pl.BlockSpec((D_MODEL, 2 * LANES), const),
            pl.BlockSpec((1, LANES), const),
            pl.BlockSpec((RANK_CHUNK, RANK_CHUNK), const),
        ],
        out_specs=[
            pl.BlockSpec((1, tm, D_MODEL), lambda b, i: (b, i, 0)),
            pl.BlockSpec((tm * ROW_TILES, LANES), lambda b, i: (b * (seq // tm) + i, 0)),
            pl.BlockSpec((1, 2, tm), lambda b, i: (b, 0, i)),
            pl.BlockSpec((1, 2, tm), lambda b, i: (b, 0, i)),
            pl.BlockSpec((1, tm, LANES), lambda b, i: (b, i, 0)),
            pl.BlockSpec((N_EXPERTS, LANES), const),
        ],
        out_shape=[
            jax.ShapeDtypeStruct((nb, seq, D_MODEL), F32),
            jax.ShapeDtypeStruct((nb * seq * ROW_TILES, LANES), U32),
            jax.ShapeDtypeStruct((nb, 2, seq), jnp.int32),
            jax.ShapeDtypeStruct((nb, 2, seq), jnp.int32),
            jax.ShapeDtypeStruct((nb, seq, LANES), F32),
            jax.ShapeDtypeStruct((N_EXPERTS, LANES), F32),
        ],
        scratch_shapes=[pltpu.VMEM((N_EXPERTS, LANES), F32)],
        compiler_params=_params(("arbitrary", "arbitrary")),
        name="postmix",
    )(a, sn, x, mod, P["g_attn_out"], P["w_out"], P["g_post1"], P["g_pre2"], P["w_router"], P["b_router"], P["tri"])


def _sc_mesh():
    return plsc.VectorSubcoreMesh(core_axis_name="c", subcore_axis_name="s")


def _sc_worker():
    return lax.axis_index("s") * SC_CORES + lax.axis_index("c")


def _sc_gather_rows(table, idx):
    n = idx.shape[0]
    per_w = n // SC_WORKERS
    steps = per_w // SC_ROWS
    idx3 = idx.reshape(SC_WORKERS, steps, SC_ROWS)

    @functools.partial(
        pl.kernel, mesh=_sc_mesh(),
        out_type=jax.ShapeDtypeStruct((n, ROW_TILES, LANES), U32),
        scratch_types=[pltpu.VMEM((steps, SC_ROWS), jnp.int32), pltpu.VMEM((SC_ROWS, ROW_TILES, LANES), U32),
                       pltpu.SemaphoreType.DMA],
        name="sc_gather_rows",
    )
    def gather(table_hbm, idx_hbm, out_hbm, idx_v, rows_v, sem):
        wid = _sc_worker()
        pltpu.sync_copy(idx_hbm.at[wid], idx_v)

        @pl.loop(0, steps)
        def _(j):
            pltpu.async_copy(table_hbm.at[idx_v.at[j]], rows_v, sem).wait()
            pltpu.sync_copy(rows_v, out_hbm.at[pl.ds(wid * per_w + j * SC_ROWS, SC_ROWS)])

    return gather(table, idx3)


def _sc_scatter_rows(src, idx, n_dst):
    n_dup, n = idx.shape
    per_w = n // SC_WORKERS
    steps = per_w // SC_ROWS
    idx4 = idx.reshape(n_dup, SC_WORKERS, steps, SC_ROWS).transpose(1, 0, 2, 3)

    @functools.partial(
        pl.kernel, mesh=_sc_mesh(),
        out_type=jax.ShapeDtypeStruct((n_dst, ROW_TILES, LANES), U32),
        scratch_types=[pltpu.VMEM((n_dup, steps, SC_ROWS), jnp.int32), pltpu.VMEM((SC_ROWS, ROW_TILES, LANES), U32),
                       pltpu.SemaphoreType.DMA],
        name="sc_scatter_rows",
    )
    def scatter(src_hbm, idx_hbm, dst_hbm, idx_v, rows_v, sem):
        wid = _sc_worker()
        pltpu.sync_copy(idx_hbm.at[wid], idx_v)

        @pl.loop(0, steps)
        def _(j):
            pltpu.sync_copy(src_hbm.at[pl.ds(wid * per_w + j * SC_ROWS, SC_ROWS)], rows_v)
            copies = [pltpu.async_copy(rows_v, dst_hbm.at[idx_v.at[k, j]], sem) for k in range(n_dup)]
            for copy in copies:
                copy.wait()

    return scatter(src, idx4)


def _expert_kernel(te_ref, nv_ref, xs_ref, wg_ref, wu_ref, wd_ref, ys_ref, wgu_bf, wd_bf):
    i = pl.program_id(0)
    valid = i < nv_ref[0]
    new_expert = jnp.logical_or(i == 0, te_ref[i] != te_ref[jnp.maximum(i - 1, 0)])

    @pl.when(jnp.logical_and(valid, new_expert))
    def _():
        wgu_bf[:, :EXPERT_FF] = wg_ref[0].astype(BF16)
        wgu_bf[:, EXPERT_FF:] = wu_ref[0].astype(BF16)
        wd_bf[...] = wd_ref[0].astype(BF16)

    @pl.when(valid)
    def _():
        x = _load_token_rows(xs_ref, xs_ref.shape[0] // ROW_TILES).astype(BF16)
        gu = _dot(x, wgu_bf[...])
        g, u = gu[:, :EXPERT_FF], gu[:, EXPERT_FF:]
        act = (g * jax.nn.sigmoid(g) * u).astype(BF16)
        _store_token_rows(ys_ref, _dot(act, wd_bf[...]))

    @pl.when(jnp.logical_not(valid))
    def _():
        ys_ref[...] = jnp.zeros_like(ys_ref)


def _expert_tile(n_tok):
    return EXPERT_TILE_LARGE if 2 * n_tok // N_EXPERTS >= 2 * EXPERT_TILE_LARGE else EXPERT_TILE_SMALL


def _experts(xs, tile_expert, n_valid, tile, P):
    rows_blk = tile * ROW_TILES
    n_tiles = xs.shape[0] // rows_blk

    def row_map(i, te, nv):
        return (jnp.minimum(i, nv[0] - 1), 0)

    def out_map(i, te, nv):
        return (i, 0)

    def w_map(i, te, nv):
        return (te[jnp.minimum(i, nv[0] - 1)], 0, 0)

    return pl.pallas_call(
        _expert_kernel,
        grid_spec=pltpu.PrefetchScalarGridSpec(
            num_scalar_prefetch=2,
            grid=(n_tiles,),
            in_specs=[
                pl.BlockSpec((rows_blk, LANES), row_map),
                pl.BlockSpec((1, D_MODEL, EXPERT_FF), w_map),
                pl.BlockSpec((1, D_MODEL, EXPERT_FF), w_map),
                pl.BlockSpec((1, EXPERT_FF, D_MODEL), w_map),
            ],
            out_specs=pl.BlockSpec((rows_blk, LANES), out_map),
            scratch_shapes=[pltpu.VMEM((D_MODEL, 2 * EXPERT_FF), BF16), pltpu.VMEM((EXPERT_FF, D_MODEL), BF16)],
        ),
        out_shape=jax.ShapeDtypeStruct(xs.shape, U32),
        compiler_params=_params(("arbitrary",)),
        name="experts",
    )(tile_expert, n_valid, xs, P["w_gate"], P["w_up"], P["w_down"])


def _final_kernel(y0_ref, y1_ref, wcol_ref, x1_ref, mod_ref, gpost_ref, o_ref):
    w = wcol_ref[0]
    w0, w1 = w[:, 0:1], w[:, 1:2]
    n_tok = w.shape[0]
    m = w0 * _load_token_rows(y0_ref, n_tok) + w1 * _load_token_rows(y1_ref, n_tok)
    gate2 = mod_ref[0][5:6]
    o_ref[0] = x1_ref[0] + gate2 * (_rms(m) * gpost_ref[...])


def _final(yg, wcol, x1, mod, P):
    nb, seq, _ = x1.shape
    tm = POST_TILE
    nt = seq // tm
    n_tok_tiles = nb * nt
    return pl.pallas_call(
        _final_kernel,
        grid=(nb, nt),
        in_specs=[
            pl.BlockSpec((tm * ROW_TILES, LANES), lambda b, i: (b * nt + i, 0)),
            pl.BlockSpec((tm * ROW_TILES, LANES), lambda b, i: (n_tok_tiles + b * nt + i, 0)),
            pl.BlockSpec((1, tm, LANES), lambda b, i: (b, i, 0)),
            pl.BlockSpec((1, tm, D_MODEL), lambda b, i: (b, i, 0)),
            pl.BlockSpec((1, SUBLANES, D_MODEL), lambda b, i: (b, 0, 0)),
            pl.BlockSpec((1, D_MODEL), lambda b, i: (0, 0)),
        ],
        out_specs=pl.BlockSpec((1, tm, D_MODEL), lambda b, i: (b, i, 0)),
        out_shape=jax.ShapeDtypeStruct((nb, seq, D_MODEL), F32),
        compiler_params=_params(("arbitrary", "arbitrary")),
        name="final",
    )(yg, yg, wcol, x1, mod, P["g_post2"])


def _prepare(w):
    f = lambda a: a.astype(F32)
    P = {}
    for k in ("g_pre1", "g_post1", "g_pre2", "g_post2", "g_q", "g_kv", "g_attn_out", "g_gmlp_out"):
        P[k] = f(w[k]).reshape(1, -1)
    P["g_v"] = f(w["g_v_gmlp"]).reshape(1, -1)

    w_in = f(w["w_in"])
    o0, o1, o2, o3 = Q_LORA, Q_LORA + KV_LORA, Q_LORA + KV_LORA + QK_ROPE, Q_LORA + KV_LORA + QK_ROPE + GMLP_WIDTH
    w_kr = w_in[:, o1:o2]
    kr_partner = jnp.concatenate([-w_kr[:, ROPE_HALF:], w_kr[:, :ROPE_HALF]], axis=1)
    rope_blk = jnp.concatenate([jnp.zeros((D_MODEL, QK_NOPE), F32), w_kr, kr_partner], axis=1)
    P["w_in"] = jnp.concatenate([w_in[:, :o1], rope_blk, w_in[:, o2:o3], w_in[:, o3:]], axis=1).astype(BF16)

    w_uq = f(w["w_uq"]).reshape(Q_LORA, N_HEADS, QK_NOPE + QK_ROPE)
    q_rope = w_uq[:, :, QK_NOPE:]
    q_partner = jnp.concatenate([-q_rope[:, :, ROPE_HALF:], q_rope[:, :, :ROPE_HALF]], axis=2)
    P["w_uq"] = jnp.concatenate([w_uq, q_partner], axis=2).reshape(Q_LORA, N_HEADS * HEAD_PAD).astype(BF16)

    w_ukv = f(w["w_ukv"]).reshape(KV_LORA, N_HEADS, QK_NOPE + V_HEAD)
    zeros = jnp.zeros((KV_LORA, N_HEADS, HEAD_PAD - QK_NOPE), F32)
    w_k = jnp.concatenate([w_ukv[:, :, :QK_NOPE], zeros], axis=2)
    w_v = w_ukv[:, :, QK_NOPE:]
    even = (jnp.arange(N_HEADS) % 2 == 0)[None, :, None]
    zv = jnp.zeros_like(w_v)
    w_v = jnp.concatenate([jnp.where(even, w_v, zv), jnp.where(even, zv, w_v)], axis=2)
    P["w_ukv"] = jnp.concatenate([w_k.reshape(KV_LORA, -1), w_v.reshape(KV_LORA, -1)], axis=1).astype(BF16)

    P["w_sp"] = f(w["w_spatial"]).reshape(GMLP_GROUPS // 2, 2 * CHUNK, CHUNK).astype(BF16)
    P["b_sp"] = jnp.repeat(f(w["b_spatial"]).T, GMLP_GROUP_DIM, axis=1)

    P["w_out"] = f(w["w_out"]).astype(BF16)

    pad = jnp.zeros((D_MODEL, LANES - N_EXPERTS - N_EXPERT_GROUPS), F32)
    wr = jnp.concatenate([f(w["w_router_expert"]), f(w["w_router_group"]), pad], axis=1)
    wr_hi = wr.astype(BF16)
    wr_lo = (wr - wr_hi.astype(F32)).astype(BF16)
    P["w_router"] = jnp.concatenate([wr_hi, wr_lo], axis=1)
    P["b_router"] = jnp.concatenate(
        [f(w["b_router_expert"]), f(w["b_router_group"]), jnp.zeros((LANES - N_EXPERTS - N_EXPERT_GROUPS,), F32)]
    ).reshape(1, LANES)
    P["tri"] = jnp.triu(jnp.ones((RANK_CHUNK, RANK_CHUNK), F32), k=1).astype(BF16)

    P["w_gate"], P["w_up"], P["w_down"] = f(w["w_gate"]), f(w["w_up"]), f(w["w_down"])
    return P


def _rope_tables(seq):
    inv = ROPE_THETA ** (-jnp.arange(ROPE_HALF, dtype=F32) / ROPE_HALF)
    ang = jnp.arange(seq, dtype=F32)[:, None] * inv[None, :]
    z_lo = jnp.zeros((seq, _ROPE_LO), F32)
    z_hi = jnp.zeros((seq, LANES - _ROPE_LO - QK_ROPE), F32)
    cos = jnp.concatenate([z_lo, jnp.cos(ang), jnp.cos(ang), z_hi], axis=1)
    sin = jnp.concatenate([z_lo, jnp.sin(ang), jnp.sin(ang), z_hi], axis=1)
    return cos, sin


def _layer(x, mod, P):
    nb, seq, _ = x.shape
    n_tok = nb * seq
    q, kt, v, sn = _premix(x, mod, P)
    a = _attention(q, kt, v)
    x1, h2rows, eid, rank, wcol, counts = _postmix(a, sn, x, mod, P)

    tile = _expert_tile(n_tok)
    cnt = counts[:, 0].astype(jnp.int32)
    padded = ((cnt + tile - 1) // tile) * tile
    ends = jnp.cumsum(padded)
    starts = ends - padded
    eflat = jnp.transpose(eid, (1, 0, 2)).reshape(2, n_tok)
    rflat = jnp.transpose(rank, (1, 0, 2)).reshape(2, n_tok)
    onehot = eflat[:, :, None] == jnp.arange(N_EXPERTS, dtype=jnp.int32)[None, None, :]
    pos = rflat + jnp.sum(jnp.where(onehot, starts[None, None, :], 0), axis=2)
    n_rows = 2 * n_tok + N_EXPERTS * tile
    n_tiles = n_rows // tile
    tile_start = jnp.arange(n_tiles, dtype=jnp.int32) * tile
    tile_expert = jnp.minimum(
        jnp.sum((tile_start[:, None] >= ends[None, :]).astype(jnp.int32), axis=1), N_EXPERTS - 1).astype(jnp.int32)
    n_valid = (ends[-1:] // tile).astype(jnp.int32)

    as_tiles = lambda a: a.reshape(-1, ROW_TILES, LANES)
    as_rows = lambda a: a.reshape(-1, LANES)
    xs = _sc_scatter_rows(as_tiles(h2rows), pos, n_rows)
    ys = _experts(as_rows(xs), tile_expert, n_valid, tile, P)
    yg = as_rows(_sc_gather_rows(as_tiles(ys), pos.reshape(2 * n_tok)))
    return _final(yg, wcol, x1, mod, P)


def kernel(x_prompt, x_sample, c_prompt, c_sample, w_ada, b_ada, g_pre1, g_post1, g_pre2, g_post2, w_in, g_q, w_uq,
           g_kv, w_ukv, g_v_gmlp, w_spatial, b_spatial, g_attn_out, g_gmlp_out, w_out, w_router_group,
           b_router_group, w_router_expert, b_router_expert, w_gate, w_up, w_down):
    P = _prepare(dict(
        g_pre1=g_pre1, g_post1=g_post1, g_pre2=g_pre2, g_post2=g_post2, w_in=w_in, g_q=g_q, w_uq=w_uq, g_kv=g_kv,
        w_ukv=w_ukv, g_v_gmlp=g_v_gmlp, w_spatial=w_spatial, b_spatial=b_spatial, g_attn_out=g_attn_out,
        g_gmlp_out=g_gmlp_out, w_out=w_out, w_router_group=w_router_group, b_router_group=b_router_group,
        w_router_expert=w_router_expert, b_router_expert=b_router_expert, w_gate=w_gate, w_up=w_up, w_down=w_down))
    P["cos"], P["sin"] = _rope_tables(max(x_prompt.shape[1], x_sample.shape[1]))

    nbp = c_prompt.shape[0]
    c_all = jnp.concatenate([c_prompt, c_sample], axis=0).astype(F32)
    mod = _ada(c_all, w_ada.astype(F32), b_ada.astype(F32))
    mod = mod.reshape(c_all.shape[0], 6, D_MODEL)
    mod = jnp.concatenate([mod, jnp.zeros((c_all.shape[0], SUBLANES - 6, D_MODEL), F32)], axis=1)

    y_prompt = _layer(x_prompt, mod[:nbp], P)
    y_sample = _layer(x_sample, mod[nbp:], P)
    return (y_prompt, y_sample)
```

```python
import functools
import math

import jax
import jax.numpy as jnp
from jax import lax
from jax.experimental import pallas as pl
from jax.experimental.pallas import tpu as pltpu
from jax.experimental.pallas import tpu_sc as plsc

F32 = jnp.float32
BF16 = jnp.bfloat16

D_MODEL = 1024
N_HEADS = 8
QK_NOPE = 64
QK_ROPE = 32
ROPE_HALF = QK_ROPE // 2
V_HEAD = 64
Q_LORA = 256
KV_LORA = 128
GMLP_WIDTH = 512
GMLP_GROUPS = 8
GMLP_GROUP_DIM = 64
CHUNK = 128
N_EXPERTS = 32
N_EXPERT_GROUPS = 4
EXPERTS_PER_GROUP = 8
EXPERT_FF = 256
ROPE_THETA = 10000.0
EPS = 1e-6

LANES = 128
SUBLANES = 8
HEAD_PAD = LANES

TOKEN_TILE = 1024
POST_TILE = 1024
Q_TILE = 1024
K_TILE = 512
EXPERT_TILE_SMALL = 512
EXPERT_TILE_LARGE = 1024
SC_CORES = 2
SC_WORKERS = 32
SC_ROWS = 128
RANK_CHUNK = 256
VMEM_LIMIT = 56 * 1024 * 1024

U32 = jnp.uint32
PACKED_WIDTH = D_MODEL // 2
ROW_TILES = PACKED_WIDTH // LANES
_HI_MASK = 0xFFFF0000

_SQRT_2_OVER_PI = math.sqrt(2.0 / math.pi)


def _rms(x):
    return x * lax.rsqrt(jnp.mean(x * x, axis=-1, keepdims=True) + EPS)


def _gelu_tanh(x):
    return 0.5 * x * (1.0 + jnp.tanh(_SQRT_2_OVER_PI * (x + 0.044715 * (x * x * x))))


def _split_bf16(x):
    hi = x.astype(BF16)
    lo = (x - hi.astype(F32)).astype(BF16)
    return hi, lo


def _dot(a, b):
    return jnp.dot(a, b, preferred_element_type=F32)


def _bf16_bits(x):
    return lax.bitcast_convert_type(x.astype(BF16).astype(F32), U32)


def _load_token_rows(ref, n):
    w = jnp.concatenate([ref[pl.ds(c, n, stride=ROW_TILES), :] for c in range(ROW_TILES)], axis=1)
    lo = lax.bitcast_convert_type(w << 16, F32)
    hi = lax.bitcast_convert_type(w & jnp.uint32(_HI_MASK), F32)
    return jnp.concatenate([lo, hi], axis=1)


def _store_token_rows(ref, val):
    n = val.shape[0]
    w = (_bf16_bits(val[:, :PACKED_WIDTH]) >> 16) | (_bf16_bits(val[:, PACKED_WIDTH:]) & jnp.uint32(_HI_MASK))
    for c in range(ROW_TILES):
        ref[pl.ds(c, n, stride=ROW_TILES), :] = w[:, c * LANES:(c + 1) * LANES]


def _params(sem, vmem=VMEM_LIMIT):
    return pltpu.CompilerParams(dimension_semantics=sem, vmem_limit_bytes=vmem)


def _ada_kernel(c_ref, w_ref, b_ref, o_ref):
    c = c_ref[...]
    a = c * jax.nn.sigmoid(c)
    a_hi, a_lo = _split_bf16(a)
    w_hi, w_lo = _split_bf16(w_ref[...])
    o_ref[...] = _dot(a_hi, w_hi) + _dot(a_hi, w_lo) + _dot(a_lo, w_hi) + b_ref[...]


def _ada(c, w_ada, b_ada):
    nb = c.shape[0]
    n_out = w_ada.shape[1]
    blk = D_MODEL
    return pl.pallas_call(
        _ada_kernel,
        grid=(n_out // blk,),
        in_specs=[
            pl.BlockSpec((nb, D_MODEL), lambda j: (0, 0)),
            pl.BlockSpec((D_MODEL, blk), lambda j: (0, j)),
            pl.BlockSpec((1, blk), lambda j: (0, j)),
        ],
        out_specs=pl.BlockSpec((nb, blk), lambda j: (0, j)),
        out_shape=jax.ShapeDtypeStruct((nb, n_out), F32),
        compiler_params=_params(("arbitrary",)),
        name="ada",
    )(c, w_ada, b_ada.reshape(1, n_out))


_C_CQ = 0
_C_CKV = _C_CQ + Q_LORA
_C_KR = _C_CKV + KV_LORA
_C_U = _C_KR + LANES
_C_V = _C_U + GMLP_WIDTH
_C_END = _C_V + GMLP_WIDTH
_ROPE_LO = QK_NOPE
_ROLL_PARTNER = LANES - QK_ROPE


def _premix_kernel(x_ref, mod_ref, cos_ref, sin_ref, gpre_ref, win_ref, gq_ref, wuq_ref, gkv_ref, wukv_ref,
                   gv_ref, wsp_ref, bsp_ref, ggo_ref, q_ref, kt_ref, v_ref, sn_ref):
    x = x_ref[0]
    mod = mod_ref[0]
    shift1, scale1 = mod[0:1], mod[1:2]
    h = _rms(x) * (gpre_ref[...] * (1.0 + scale1)) + shift1
    z = _dot(h.astype(BF16), win_ref[...])

    cosb = cos_ref[...]
    sinb = sin_ref[...]
    lane = lax.broadcasted_iota(jnp.int32, (1, LANES), 1)
    nope_mask = jnp.where(lane < QK_NOPE, 1.0, 0.0).astype(F32)

    qscale = (QK_NOPE + QK_ROPE) ** -0.5 * math.log2(math.e)
    cq_tab = (nope_mask + cosb) * qscale
    sq_tab = sinb * qscale
    cqn = (_rms(z[:, _C_CQ:_C_CKV]) * gq_ref[...]).astype(BF16)
    qb = _dot(cqn, wuq_ref[...])
    for hd in range(N_HEADS):
        blk = qb[:, hd * HEAD_PAD:(hd + 1) * HEAD_PAD]
        qh = blk * cq_tab + pltpu.roll(blk, _ROLL_PARTNER, 1) * sq_tab
        q_ref[0, hd] = qh.astype(BF16)

    ckvn = (_rms(z[:, _C_CKV:_C_KR]) * gkv_ref[...]).astype(BF16)
    kvb = _dot(ckvn, wukv_ref[...])
    krb = z[:, _C_KR:_C_U]
    krope = krb * cosb + pltpu.roll(krb, _ROLL_PARTNER, 1) * sinb
    v_off = N_HEADS * HEAD_PAD
    for hd in range(N_HEADS):
        kh = kvb[:, hd * HEAD_PAD:(hd + 1) * HEAD_PAD] + krope
        for c in range(kt_ref.shape[2]):
            kt_ref[0, hd, c] = kh[c * K_TILE:(c + 1) * K_TILE].T.astype(BF16)
        ones_lane = V_HEAD if hd % 2 == 0 else 0
        vh = kvb[:, v_off + hd * HEAD_PAD:v_off + (hd + 1) * HEAD_PAD] + jnp.where(lane == ones_lane, 1.0, 0.0)
        v_ref[0, hd] = vh.astype(BF16)

    ua = _gelu_tanh(z[:, _C_U:_C_V])
    vn = (_rms(_gelu_tanh(z[:, _C_V:_C_END])) * gv_ref[...]).astype(BF16)
    n_tok = x.shape[0]
    bsp = bsp_ref[...]
    rows = []
    for n in range(n_tok // CHUNK):
        cols = []
        for j in range(GMLP_GROUPS // 2):
            rhs = vn[n * CHUNK:(n + 1) * CHUNK, j * LANES:(j + 1) * LANES]
            ab = _dot(wsp_ref[j], rhs)
            cols.append(jnp.where(lane < GMLP_GROUP_DIM, ab[:CHUNK], ab[CHUNK:]))
        rows.append(jnp.concatenate(cols, axis=1) + bsp)
    s = ua * jnp.concatenate(rows, axis=0)
    sn_ref[0] = (_rms(s) * ggo_ref[...]).astype(BF16)


def _premix(x, mod, P):
    nb, seq, _ = x.shape
    tm = TOKEN_TILE
    nck = seq // K_TILE
    const = lambda i, b: (0, 0)
    return pl.pallas_call(
        _premix_kernel,
        grid=(seq // tm, nb),
        in_specs=[
            pl.BlockSpec((1, tm, D_MODEL), lambda i, b: (b, i, 0)),
            pl.BlockSpec((1, SUBLANES, D_MODEL), lambda i, b: (b, 0, 0)),
            pl.BlockSpec((tm, LANES), lambda i, b: (i, 0)),
            pl.BlockSpec((tm, LANES), lambda i, b: (i, 0)),
            pl.BlockSpec((1, D_MODEL), const),
            pl.BlockSpec((D_MODEL, _C_END), const),
            pl.BlockSpec((1, Q_LORA), const),
            pl.BlockSpec((Q_LORA, N_HEADS * HEAD_PAD), const),
            pl.BlockSpec((1, KV_LORA), const),
            pl.BlockSpec((KV_LORA, 2 * N_HEADS * HEAD_PAD), const),
            pl.BlockSpec((1, GMLP_WIDTH), const),
            pl.BlockSpec((GMLP_GROUPS // 2, 2 * CHUNK, CHUNK), lambda i, b: (0, 0, 0)),
            pl.BlockSpec((CHUNK, GMLP_WIDTH), const),
            pl.BlockSpec((1, GMLP_WIDTH), const),
        ],
        out_specs=[
            pl.BlockSpec((1, N_HEADS, tm, HEAD_PAD), lambda i, b: (b, 0, i, 0)),
            pl.BlockSpec((1, N_HEADS, tm // K_TILE, HEAD_PAD, K_TILE), lambda i, b: (b, 0, i, 0, 0)),
            pl.BlockSpec((1, N_HEADS, tm, HEAD_PAD), lambda i, b: (b, 0, i, 0)),
            pl.BlockSpec((1, tm, GMLP_WIDTH), lambda i, b: (b, i, 0)),
        ],
        out_shape=[
            jax.ShapeDtypeStruct((nb, N_HEADS, seq, HEAD_PAD), BF16),
            jax.ShapeDtypeStruct((nb, N_HEADS, nck, HEAD_PAD, K_TILE), BF16),
            jax.ShapeDtypeStruct((nb, N_HEADS, seq, HEAD_PAD), BF16),
            jax.ShapeDtypeStruct((nb, seq, GMLP_WIDTH), BF16),
        ],
        compiler_params=_params(("arbitrary", "arbitrary")),
        name="premix",
    )(x, mod, P["cos"][:seq], P["sin"][:seq], P["g_pre1"], P["w_in"], P["g_q"], P["w_uq"], P["g_kv"], P["w_ukv"],
      P["g_v"], P["w_sp"], P["b_sp"], P["g_gmlp_out"])


def _attn_kernel(q_ref, kt_ref, v_ref, o_ref):
    n_chunks = kt_ref.shape[2]
    tk = kt_ref.shape[4]
    lane = lax.broadcasted_iota(jnp.int32, (1, LANES), 1)

    def one_head(hd, ones_lane):
        q = q_ref[0, hd]
        m = None
        acc = None
        for c in range(n_chunks):
            s = _dot(q, kt_ref[0, hd, c])
            v = v_ref[0, hd, c * tk:(c + 1) * tk, :]
            smax = jnp.max(s, axis=1, keepdims=True)
            if c == 0:
                m = smax
                acc = _dot(jnp.exp2((s - m).astype(BF16)), v)
            else:
                m_new = jnp.maximum(m, smax)
                acc = acc * jnp.exp2(m - m_new) + _dot(jnp.exp2((s - m_new).astype(BF16)), v)
                m = m_new
        row_sum = acc[:, ones_lane:ones_lane + 1]
        return acc * (1.0 / row_sum)

    def pair(j, carry):
        even = one_head(2 * j, V_HEAD)
        odd = one_head(2 * j + 1, 0)
        o_ref[0, j] = jnp.where(lane < V_HEAD, even, odd).astype(BF16)
        return carry

    lax.fori_loop(0, N_HEADS // 2, pair, 0)


def _attention(q, kt, v):
    nb, _, seq, _ = q.shape
    nck = kt.shape[2]
    return pl.pallas_call(
        _attn_kernel,
        grid=(nb, seq // Q_TILE),
        in_specs=[
            pl.BlockSpec((1, N_HEADS, Q_TILE, HEAD_PAD), lambda b, i: (b, 0, i, 0)),
            pl.BlockSpec((1, N_HEADS, nck, HEAD_PAD, K_TILE), lambda b, i: (b, 0, 0, 0, 0)),
            pl.BlockSpec((1, N_HEADS, seq, HEAD_PAD), lambda b, i: (b, 0, 0, 0)),
        ],
        out_specs=pl.BlockSpec((1, N_HEADS // 2, Q_TILE, LANES), lambda b, i: (b, 0, i, 0)),
        out_shape=jax.ShapeDtypeStruct((nb, N_HEADS // 2, seq, LANES), BF16),
        compiler_params=_params(("arbitrary", "arbitrary")),
        name="attn",
    )(q, kt, v)


_R_GROUP_ROW = N_EXPERTS


def _postmix_kernel(a_ref, sn_ref, x_ref, mod_ref, gao_ref, wout_ref, gpost_ref, gpre_ref, wr_ref, br_ref, tri_ref,
                    x1_ref, h2_ref, eid_ref, rank_ref, wcol_ref, cnt_ref, run_ref):
    first = jnp.logical_and(pl.program_id(0) == 0, pl.program_id(1) == 0)

    @pl.when(first)
    def _():
        run_ref[...] = jnp.zeros_like(run_ref)

    mod = mod_ref[0]
    gate1, shift2, scale2 = mod[2:3], mod[3:4], mod[4:5]
    a = jnp.concatenate([a_ref[0, j] for j in range(N_HEADS // 2)], axis=1).astype(F32)
    an = (_rms(a) * gao_ref[...]).astype(BF16)
    merged = jnp.concatenate([an, sn_ref[0]], axis=1)
    o = _dot(merged, wout_ref[...])
    x1 = x_ref[0] + gate1 * (_rms(o) * gpost_ref[...])
    x1_ref[0] = x1
    h2 = _rms(x1) * (gpre_ref[...] * (1.0 + scale2)) + shift2
    n_tok = h2.shape[0]
    _store_token_rows(h2_ref, h2)

    h_hi, h_lo = _split_bf16(h2)
    wr = wr_ref[...]
    hh = _dot(h_hi, wr)
    lh = _dot(h_lo, wr[:, :LANES])
    logits = hh[:, :LANES] + hh[:, LANES:] + lh + br_ref[...]
    lt = logits.T

    neg = jnp.float32(-jnp.inf)
    row8 = lax.broadcasted_iota(jnp.int32, (SUBLANES, n_tok), 0).astype(F32)
    lg = jnp.where(row8 < N_EXPERT_GROUPS, lt[_R_GROUP_ROW:_R_GROUP_ROW + SUBLANES], neg)
    gmax = jnp.max(lg, axis=0, keepdims=True)
    gi = jnp.min(jnp.where(lg == gmax, row8, float(SUBLANES)), axis=0, keepdims=True)
    pg_sel = 1.0 / jnp.sum(jnp.exp(lg - gmax), axis=0, keepdims=True)

    le = jnp.zeros((EXPERTS_PER_GROUP, n_tok), F32)
    for g in range(N_EXPERT_GROUPS):
        le = jnp.where(gi == float(g), lt[g * EXPERTS_PER_GROUP:(g + 1) * EXPERTS_PER_GROUP], le)
    v1 = jnp.max(le, axis=0, keepdims=True)
    i1 = jnp.min(jnp.where(le == v1, row8, float(SUBLANES)), axis=0, keepdims=True)
    le2 = jnp.where(row8 == i1, neg, le)
    v2 = jnp.max(le2, axis=0, keepdims=True)
    i2 = jnp.min(jnp.where(le2 == v2, row8, float(SUBLANES)), axis=0, keepdims=True)
    r = jnp.exp(v2 - v1)
    w1 = pg_sel / (1.0 + r)
    w2 = w1 * r
    e1 = gi * float(EXPERTS_PER_GROUP) + i1
    e2 = gi * float(EXPERTS_PER_GROUP) + i2
    eid_ref[0] = jnp.concatenate([e1, e2], axis=0).astype(jnp.int32)

    row32 = lax.broadcasted_iota(jnp.int32, (N_EXPERTS, n_tok), 0).astype(F32)
    hit1 = row32 == e1
    hit2 = row32 == e2
    onehot = jnp.where(jnp.logical_or(hit1, hit2), 1.0, 0.0)
    run = run_ref[...][:, 0:1]
    ranks1, ranks2 = [], []
    for c in range(n_tok // RANK_CHUNK):
        sl = slice(c * RANK_CHUNK, (c + 1) * RANK_CHUNK)
        oh = onehot[:, sl]
        before = _dot(oh.astype(BF16), tri_ref[...]) + run
        ranks1.append(jnp.sum(jnp.where(hit1[:, sl], before, 0.0), axis=0, keepdims=True))
        ranks2.append(jnp.sum(jnp.where(hit2[:, sl], before, 0.0), axis=0, keepdims=True))
        run = run + jnp.sum(oh, axis=1, keepdims=True)
    rank_ref[0] = jnp.concatenate(
        [jnp.concatenate(ranks1, axis=1), jnp.concatenate(ranks2, axis=1)], axis=0).astype(jnp.int32)
    run_b = jnp.broadcast_to(run, run_ref.shape)
    run_ref[...] = run_b
    cnt_ref[...] = run_b

    row128 = lax.broadcasted_iota(jnp.int32, (LANES, n_tok), 0)
    wt = jnp.where(row128 == 0, w1, jnp.where(row128 == 1, w2, 0.0))
    wcol_ref[0] = wt.T


def _postmix(a, sn, x, mod, P):
    nb, seq, _ = x.shape
    tm = POST_TILE
    const = lambda b, i: (0, 0)
    return pl.pallas_call(
        _postmix_kernel,
        grid=(nb, seq // tm),
        in_specs=[
            pl.BlockSpec((1, N_HEADS // 2, tm, LANES), lambda b, i: (b, 0, i, 0)),
            pl.BlockSpec((1, tm, GMLP_WIDTH), lambda b, i: (b, i, 0)),
            pl.BlockSpec((1, tm, D_MODEL), lambda b, i: (b, i, 0)),
            pl.BlockSpec((1, SUBLANES, D_MODEL), lambda b, i: (b, 0, 0)),
            pl.BlockSpec((1, N_HEADS * V_HEAD), const),
            pl.BlockSpec((D_MODEL, D_MODEL), const),
            pl.BlockSpec((1, D_MODEL), const),
            pl.BlockSpec((1, D_MODEL), const),
            pl.BlockSpec((D_MODEL, 2 * LANES), const),
            pl.BlockSpec((1, LANES), const),
            pl.BlockSpec((RANK_CHUNK, RANK_CHUNK), const),
        ],
        out_specs=[
            pl.BlockSpec((1, tm, D_MODEL), lambda b, i: (b, i, 0)),
            pl.BlockSpec((tm * ROW_TILES, LANES), lambda b, i: (b * (seq // tm) + i, 0)),
            pl.BlockSpec((1, 2, tm), lambda b, i: (b, 0, i)),
            pl.BlockSpec((1, 2, tm), lambda b, i: (b, 0, i)),
            pl.BlockSpec((1, tm, LANES), lambda b, i: (b, i, 0)),
            pl.BlockSpec((N_EXPERTS, LANES), const),
        ],
        out_shape=[
            jax.ShapeDtypeStruct((nb, seq, D_MODEL), F32),
            jax.ShapeDtypeStruct((nb * seq * ROW_TILES, LANES), U32),
            jax.ShapeDtypeStruct((nb, 2, seq), jnp.int32),
            jax.ShapeDtypeStruct((nb, 2, seq), jnp.int32),
            jax.ShapeDtypeStruct((nb, seq, LANES), F32),
            jax.ShapeDtypeStruct((N_EXPERTS, LANES), F32),
        ],
        scratch_shapes=[pltpu.VMEM((N_EXPERTS, LANES), F32)],
        compiler_params=_params(("arbitrary", "arbitrary")),
        name="postmix",
    )(a, sn, x, mod, P["g_attn_out"], P["w_out"], P["g_post1"], P["g_pre2"], P["w_router"], P["b_router"], P["tri"])


def _sc_mesh():
    return plsc.VectorSubcoreMesh(core_axis_name="c", subcore_axis_name="s")


def _sc_worker():
    return lax.axis_index("s") * SC_CORES + lax.axis_index("c")


def _sc_gather_rows(table, idx):
    n = idx.shape[0]
    per_w = n // SC_WORKERS
    rows = SC_ROWS // 2
    steps = per_w // rows
    idx3 = idx.reshape(SC_WORKERS, steps, rows)
    buf = pltpu.VMEM((rows, ROW_TILES, LANES), U32)
    dma = pltpu.SemaphoreType.DMA

    @functools.partial(
        pl.kernel, mesh=_sc_mesh(),
        out_type=jax.ShapeDtypeStruct((n, ROW_TILES, LANES), U32),
        scratch_types=[pltpu.VMEM((steps, rows), jnp.int32), buf, buf, dma, dma, dma, dma],
        name="sc_gather_rows",
    )
    def gather(table_hbm, idx_hbm, out_hbm, idx_v, rows_a, rows_b, gsem_a, gsem_b, wsem_a, wsem_b):
        wid = _sc_worker()
        pltpu.sync_copy(idx_hbm.at[wid], idx_v)

        @pl.loop(0, steps, step=2)
        def _(j):
            base = wid * per_w + j * rows
            gather_a = pltpu.async_copy(table_hbm.at[idx_v.at[j]], rows_a, gsem_a)
            gather_b = pltpu.async_copy(table_hbm.at[idx_v.at[j + 1]], rows_b, gsem_b)
            gather_a.wait()
            write_a = pltpu.async_copy(rows_a, out_hbm.at[pl.ds(base, rows)], wsem_a)
            gather_b.wait()
            write_b = pltpu.async_copy(rows_b, out_hbm.at[pl.ds(base + rows, rows)], wsem_b)
            write_a.wait()
            write_b.wait()

    return gather(table, idx3)


def _sc_scatter_rows(src, idx, n_dst):
    n_dup, n = idx.shape
    per_w = n // SC_WORKERS
    steps = per_w // SC_ROWS
    idx4 = idx.reshape(n_dup, SC_WORKERS, steps, SC_ROWS).transpose(1, 0, 2, 3)

    @functools.partial(
        pl.kernel, mesh=_sc_mesh(),
        out_type=jax.ShapeDtypeStruct((n_dst, ROW_TILES, LANES), U32),
        scratch_types=[pltpu.VMEM((n_dup, steps, SC_ROWS), jnp.int32), pltpu.VMEM((SC_ROWS, ROW_TILES, LANES), U32),
                       pltpu.SemaphoreType.DMA],
        name="sc_scatter_rows",
    )
    def scatter(src_hbm, idx_hbm, dst_hbm, idx_v, rows_v, sem):
        wid = _sc_worker()
        pltpu.sync_copy(idx_hbm.at[wid], idx_v)

        @pl.loop(0, steps)
        def _(j):
            pltpu.sync_copy(src_hbm.at[pl.ds(wid * per_w + j * SC_ROWS, SC_ROWS)], rows_v)
            copies = [pltpu.async_copy(rows_v, dst_hbm.at[idx_v.at[k, j]], sem) for k in range(n_dup)]
            for copy in copies:
                copy.wait()

    return scatter(src, idx4)


def _expert_kernel(te_ref, nv_ref, xs_ref, wg_ref, wu_ref, wd_ref, ys_ref, wgu_bf, wd_bf):
    i = pl.program_id(0)
    valid = i < nv_ref[0]
    new_expert = jnp.logical_or(i == 0, te_ref[i] != te_ref[jnp.maximum(i - 1, 0)])

    @pl.when(jnp.logical_and(valid, new_expert))
    def _():
        wgu_bf[:, :EXPERT_FF] = wg_ref[0].astype(BF16)
        wgu_bf[:, EXPERT_FF:] = wu_ref[0].astype(BF16)
        wd_bf[...] = wd_ref[0].astype(BF16)

    @pl.when(valid)
    def _():
        x = _load_token_rows(xs_ref, xs_ref.shape[0] // ROW_TILES).astype(BF16)
        gu = _dot(x, wgu_bf[...])
        g, u = gu[:, :EXPERT_FF], gu[:, EXPERT_FF:]
        act = (g * jax.nn.sigmoid(g) * u).astype(BF16)
        _store_token_rows(ys_ref, _dot(act, wd_bf[...]))

    @pl.when(jnp.logical_not(valid))
    def _():
        ys_ref[...] = jnp.zeros_like(ys_ref)


def _expert_tile(n_tok):
    return EXPERT_TILE_LARGE if 2 * n_tok // N_EXPERTS >= 2 * EXPERT_TILE_LARGE else EXPERT_TILE_SMALL


def _experts(xs, tile_expert, n_valid, tile, P):
    rows_blk = tile * ROW_TILES
    n_tiles = xs.shape[0] // rows_blk

    def row_map(i, te, nv):
        return (jnp.minimum(i, nv[0] - 1), 0)

    def out_map(i, te, nv):
        return (i, 0)

    def w_map(i, te, nv):
        return (te[jnp.minimum(i, nv[0] - 1)], 0, 0)

    return pl.pallas_call(
        _expert_kernel,
        grid_spec=pltpu.PrefetchScalarGridSpec(
            num_scalar_prefetch=2,
            grid=(n_tiles,),
            in_specs=[
                pl.BlockSpec((rows_blk, LANES), row_map),
                pl.BlockSpec((1, D_MODEL, EXPERT_FF), w_map),
                pl.BlockSpec((1, D_MODEL, EXPERT_FF), w_map),
                pl.BlockSpec((1, EXPERT_FF, D_MODEL), w_map),
            ],
            out_specs=pl.BlockSpec((rows_blk, LANES), out_map),
            scratch_shapes=[pltpu.VMEM((D_MODEL, 2 * EXPERT_FF), BF16), pltpu.VMEM((EXPERT_FF, D_MODEL), BF16)],
        ),
        out_shape=jax.ShapeDtypeStruct(xs.shape, U32),
        compiler_params=_params(("arbitrary",)),
        name="experts",
    )(tile_expert, n_valid, xs, P["w_gate"], P["w_up"], P["w_down"])


def _final_kernel(y0_ref, y1_ref, wcol_ref, x1_ref, mod_ref, gpost_ref, o_ref):
    w = wcol_ref[0]
    w0, w1 = w[:, 0:1], w[:, 1:2]
    n_tok = w.shape[0]
    m = w0 * _load_token_rows(y0_ref, n_tok) + w1 * _load_token_rows(y1_ref, n_tok)
    gate2 = mod_ref[0][5:6]
    o_ref[0] = x1_ref[0] + gate2 * (_rms(m) * gpost_ref[...])


def _final(yg, wcol, x1, mod, P):
    nb, seq, _ = x1.shape
    tm = POST_TILE
    nt = seq // tm
    n_tok_tiles = nb * nt
    return pl.pallas_call(
        _final_kernel,
        grid=(nb, nt),
        in_specs=[
            pl.BlockSpec((tm * ROW_TILES, LANES), lambda b, i: (b * nt + i, 0)),
            pl.BlockSpec((tm * ROW_TILES, LANES), lambda b, i: (n_tok_tiles + b * nt + i, 0)),
            pl.BlockSpec((1, tm, LANES), lambda b, i: (b, i, 0)),
            pl.BlockSpec((1, tm, D_MODEL), lambda b, i: (b, i, 0)),
            pl.BlockSpec((1, SUBLANES, D_MODEL), lambda b, i: (b, 0, 0)),
            pl.BlockSpec((1, D_MODEL), lambda b, i: (0, 0)),
        ],
        out_specs=pl.BlockSpec((1, tm, D_MODEL), lambda b, i: (b, i, 0)),
        out_shape=jax.ShapeDtypeStruct((nb, seq, D_MODEL), F32),
        compiler_params=_params(("arbitrary", "arbitrary")),
        name="final",
    )(yg, yg, wcol, x1, mod, P["g_post2"])


def _prepare(w):
    f = lambda a: a.astype(F32)
    P = {}
    for k in ("g_pre1", "g_post1", "g_pre2", "g_post2", "g_q", "g_kv", "g_attn_out", "g_gmlp_out"):
        P[k] = f(w[k]).reshape(1, -1)
    P["g_v"] = f(w["g_v_gmlp"]).reshape(1, -1)

    w_in = f(w["w_in"])
    o0, o1, o2, o3 = Q_LORA, Q_LORA + KV_LORA, Q_LORA + KV_LORA + QK_ROPE, Q_LORA + KV_LORA + QK_ROPE + GMLP_WIDTH
    w_kr = w_in[:, o1:o2]
    kr_partner = jnp.concatenate([-w_kr[:, ROPE_HALF:], w_kr[:, :ROPE_HALF]], axis=1)
    rope_blk = jnp.concatenate([jnp.zeros((D_MODEL, QK_NOPE), F32), w_kr, kr_partner], axis=1)
    P["w_in"] = jnp.concatenate([w_in[:, :o1], rope_blk, w_in[:, o2:o3], w_in[:, o3:]], axis=1).astype(BF16)

    w_uq = f(w["w_uq"]).reshape(Q_LORA, N_HEADS, QK_NOPE + QK_ROPE)
    q_rope = w_uq[:, :, QK_NOPE:]
    q_partner = jnp.concatenate([-q_rope[:, :, ROPE_HALF:], q_rope[:, :, :ROPE_HALF]], axis=2)
    P["w_uq"] = jnp.concatenate([w_uq, q_partner], axis=2).reshape(Q_LORA, N_HEADS * HEAD_PAD).astype(BF16)

    w_ukv = f(w["w_ukv"]).reshape(KV_LORA, N_HEADS, QK_NOPE + V_HEAD)
    zeros = jnp.zeros((KV_LORA, N_HEADS, HEAD_PAD - QK_NOPE), F32)
    w_k = jnp.concatenate([w_ukv[:, :, :QK_NOPE], zeros], axis=2)
    w_v = w_ukv[:, :, QK_NOPE:]
    even = (jnp.arange(N_HEADS) % 2 == 0)[None, :, None]
    zv = jnp.zeros_like(w_v)
    w_v = jnp.concatenate([jnp.where(even, w_v, zv), jnp.where(even, zv, w_v)], axis=2)
    P["w_ukv"] = jnp.concatenate([w_k.reshape(KV_LORA, -1), w_v.reshape(KV_LORA, -1)], axis=1).astype(BF16)

    P["w_sp"] = f(w["w_spatial"]).reshape(GMLP_GROUPS // 2, 2 * CHUNK, CHUNK).astype(BF16)
    P["b_sp"] = jnp.repeat(f(w["b_spatial"]).T, GMLP_GROUP_DIM, axis=1)

    P["w_out"] = f(w["w_out"]).astype(BF16)

    pad = jnp.zeros((D_MODEL, LANES - N_EXPERTS - N_EXPERT_GROUPS), F32)
    wr = jnp.concatenate([f(w["w_router_expert"]), f(w["w_router_group"]), pad], axis=1)
    wr_hi = wr.astype(BF16)
    wr_lo = (wr - wr_hi.astype(F32)).astype(BF16)
    P["w_router"] = jnp.concatenate([wr_hi, wr_lo], axis=1)
    P["b_router"] = jnp.concatenate(
        [f(w["b_router_expert"]), f(w["b_router_group"]), jnp.zeros((LANES - N_EXPERTS - N_EXPERT_GROUPS,), F32)]
    ).reshape(1, LANES)
    P["tri"] = jnp.triu(jnp.ones((RANK_CHUNK, RANK_CHUNK), F32), k=1).astype(BF16)

    P["w_gate"], P["w_up"], P["w_down"] = f(w["w_gate"]), f(w["w_up"]), f(w["w_down"])
    return P


def _rope_tables(seq):
    inv = ROPE_THETA ** (-jnp.arange(ROPE_HALF, dtype=F32) / ROPE_HALF)
    ang = jnp.arange(seq, dtype=F32)[:, None] * inv[None, :]
    z_lo = jnp.zeros((seq, _ROPE_LO), F32)
    z_hi = jnp.zeros((seq, LANES - _ROPE_LO - QK_ROPE), F32)
    cos = jnp.concatenate([z_lo, jnp.cos(ang), jnp.cos(ang), z_hi], axis=1)
    sin = jnp.concatenate([z_lo, jnp.sin(ang), jnp.sin(ang), z_hi], axis=1)
    return cos, sin


def _layer(x, mod, P):
    nb, seq, _ = x.shape
    n_tok = nb * seq
    q, kt, v, sn = _premix(x, mod, P)
    a = _attention(q, kt, v)
    x1, h2rows, eid, rank, wcol, counts = _postmix(a, sn, x, mod, P)

    tile = _expert_tile(n_tok)
    cnt = counts[:, 0].astype(jnp.int32)
    padded = ((cnt + tile - 1) // tile) * tile
    ends = jnp.cumsum(padded)
    starts = ends - padded
    eflat = jnp.transpose(eid, (1, 0, 2)).reshape(2, n_tok)
    rflat = jnp.transpose(rank, (1, 0, 2)).reshape(2, n_tok)
    onehot = eflat[:, :, None] == jnp.arange(N_EXPERTS, dtype=jnp.int32)[None, None, :]
    pos = rflat + jnp.sum(jnp.where(onehot, starts[None, None, :], 0), axis=2)
    n_rows = 2 * n_tok + N_EXPERTS * tile
    n_tiles = n_rows // tile
    tile_start = jnp.arange(n_tiles, dtype=jnp.int32) * tile
    tile_expert = jnp.minimum(
        jnp.sum((tile_start[:, None] >= ends[None, :]).astype(jnp.int32), axis=1), N_EXPERTS - 1).astype(jnp.int32)
    n_valid = (ends[-1:] // tile).astype(jnp.int32)

    as_tiles = lambda a: a.reshape(-1, ROW_TILES, LANES)
    as_rows = lambda a: a.reshape(-1, LANES)
    xs = _sc_scatter_rows(as_tiles(h2rows), pos, n_rows)
    ys = _experts(as_rows(xs), tile_expert, n_valid, tile, P)
    yg = as_rows(_sc_gather_rows(as_tiles(ys), pos.reshape(2 * n_tok)))
    return _final(yg, wcol, x1, mod, P)


def kernel(x_prompt, x_sample, c_prompt, c_sample, w_ada, b_ada, g_pre1, g_post1, g_pre2, g_post2, w_in, g_q, w_uq,
           g_kv, w_ukv, g_v_gmlp, w_spatial, b_spatial, g_attn_out, g_gmlp_out, w_out, w_router_group,
           b_router_group, w_router_expert, b_router_expert, w_gate, w_up, w_down):
    P = _prepare(dict(
        g_pre1=g_pre1, g_post1=g_post1, g_pre2=g_pre2, g_post2=g_post2, w_in=w_in, g_q=g_q, w_uq=w_uq, g_kv=g_kv,
        w_ukv=w_ukv, g_v_gmlp=g_v_gmlp, w_spatial=w_spatial, b_spatial=b_spatial, g_attn_out=g_attn_out,
        g_gmlp_out=g_gmlp_out, w_out=w_out, w_router_group=w_router_group, b_router_group=b_router_group,
        w_router_expert=w_router_expert, b_router_expert=b_router_expert, w_gate=w_gate, w_up=w_up, w_down=w_down))
    P["cos"], P["sin"] = _rope_tables(max(x_prompt.shape[1], x_sample.shape[1]))

    nbp = c_prompt.shape[0]
    c_all = jnp.concatenate([c_prompt, c_sample], axis=0).astype(F32)
    mod = _ada(c_all, w_ada.astype(F32), b_ada.astype(F32))
    mod = mod.reshape(c_all.shape[0], 6, D_MODEL)
    mod = jnp.concatenate([mod, jnp.zeros((c_all.shape[0], SUBLANES - 6, D_MODEL), F32)], axis=1)

    y_prompt = _layer(x_prompt, mod[:nbp], P)
    y_sample = _layer(x_sample, mod[nbp:], P)
    return (y_prompt, y_sample)
```

```python
import functools
import math

import jax
import jax.numpy as jnp
from jax import lax
from jax.experimental import pallas as pl
from jax.experimental.pallas import tpu as pltpu
from jax.experimental.pallas import tpu_sc as plsc

F32 = jnp.float32
BF16 = jnp.bfloat16

D_MODEL = 1024
N_HEADS = 8
QK_NOPE = 64
QK_ROPE = 32
ROPE_HALF = QK_ROPE // 2
V_HEAD = 64
Q_LORA = 256
KV_LORA = 128
GMLP_WIDTH = 512
GMLP_GROUPS = 8
GMLP_GROUP_DIM = 64
CHUNK = 128
N_EXPERTS = 32
N_EXPERT_GROUPS = 4
EXPERTS_PER_GROUP = 8
EXPERT_FF = 256
ROPE_THETA = 10000.0
EPS = 1e-6

LANES = 128
SUBLANES = 8
HEAD_PAD = LANES

TOKEN_TILE = 1024
POST_TILE = 1024
Q_TILE = 1024
K_TILE = 512
EXPERT_TILE_SMALL = 512
EXPERT_TILE_LARGE = 1024
SC_CORES = 2
SC_WORKERS = 32
SC_ROWS = 128
RANK_CHUNK = 256
VMEM_LIMIT = 56 * 1024 * 1024

U32 = jnp.uint32
PACKED_WIDTH = D_MODEL // 2
ROW_TILES = PACKED_WIDTH // LANES
_HI_MASK = 0xFFFF0000

_SQRT_2_OVER_PI = math.sqrt(2.0 / math.pi)


def _rms(x):
    return x * lax.rsqrt(jnp.mean(x * x, axis=-1, keepdims=True) + EPS)


def _gelu_tanh(x):
    return 0.5 * x * (1.0 + jnp.tanh(_SQRT_2_OVER_PI * (x + 0.044715 * (x * x * x))))


def _split_bf16(x):
    hi = x.astype(BF16)
    lo = (x - hi.astype(F32)).astype(BF16)
    return hi, lo


def _dot(a, b):
    return jnp.dot(a, b, preferred_element_type=F32)


def _bf16_bits(x):
    return lax.bitcast_convert_type(x.astype(BF16).astype(F32), U32)


def _load_token_rows(ref, n):
    w = jnp.concatenate([ref[pl.ds(c, n, stride=ROW_TILES), :] for c in range(ROW_TILES)], axis=1)
    lo = lax.bitcast_convert_type(w << 16, F32)
    hi = lax.bitcast_convert_type(w & jnp.uint32(_HI_MASK), F32)
    return jnp.concatenate([lo, hi], axis=1)


def _store_token_rows(ref, val):
    n = val.shape[0]
    w = (_bf16_bits(val[:, :PACKED_WIDTH]) >> 16) | (_bf16_bits(val[:, PACKED_WIDTH:]) & jnp.uint32(_HI_MASK))
    for c in range(ROW_TILES):
        ref[pl.ds(c, n, stride=ROW_TILES), :] = w[:, c * LANES:(c + 1) * LANES]


def _params(sem, vmem=VMEM_LIMIT):
    return pltpu.CompilerParams(dimension_semantics=sem, vmem_limit_bytes=vmem)


def _ada_kernel(c_ref, w_ref, b_ref, o_ref):
    c = c_ref[...]
    a = c * jax.nn.sigmoid(c)
    a_hi, a_lo = _split_bf16(a)
    w_hi, w_lo = _split_bf16(w_ref[...])
    o_ref[...] = _dot(a_hi, w_hi) + _dot(a_hi, w_lo) + _dot(a_lo, w_hi) + b_ref[...]


def _ada(c, w_ada, b_ada):
    nb = c.shape[0]
    n_out = w_ada.shape[1]
    blk = D_MODEL
    return pl.pallas_call(
        _ada_kernel,
        grid=(n_out // blk,),
        in_specs=[
            pl.BlockSpec((nb, D_MODEL), lambda j: (0, 0)),
            pl.BlockSpec((D_MODEL, blk), lambda j: (0, j)),
            pl.BlockSpec((1, blk), lambda j: (0, j)),
        ],
        out_specs=pl.BlockSpec((nb, blk), lambda j: (0, j)),
        out_shape=jax.ShapeDtypeStruct((nb, n_out), F32),
        compiler_params=_params(("arbitrary",)),
        name="ada",
    )(c, w_ada, b_ada.reshape(1, n_out))


_C_CQ = 0
_C_CKV = _C_CQ + Q_LORA
_C_KR = _C_CKV + KV_LORA
_C_U = _C_KR + LANES
_C_V = _C_U + GMLP_WIDTH
_C_END = _C_V + GMLP_WIDTH
_ROPE_LO = QK_NOPE
_ROLL_PARTNER = LANES - QK_ROPE


def _premix_kernel(x_ref, mod_ref, cos_ref, sin_ref, gpre_ref, win_ref, gq_ref, wuq_ref, gkv_ref, wukv_ref,
                   gv_ref, wsp_ref, bsp_ref, ggo_ref, q_ref, kt_ref, v_ref, sn_ref):
    x = x_ref[0]
    mod = mod_ref[0]
    shift1, scale1 = mod[0:1], mod[1:2]
    h = _rms(x) * (gpre_ref[...] * (1.0 + scale1)) + shift1
    z = _dot(h.astype(BF16), win_ref[...])

    cosb = cos_ref[...]
    sinb = sin_ref[...]
    lane = lax.broadcasted_iota(jnp.int32, (1, LANES), 1)
    nope_mask = jnp.where(lane < QK_NOPE, 1.0, 0.0).astype(F32)

    qscale = (QK_NOPE + QK_ROPE) ** -0.5 * math.log2(math.e)
    cq_tab = (nope_mask + cosb) * qscale
    sq_tab = sinb * qscale
    cqn = (_rms(z[:, _C_CQ:_C_CKV]) * gq_ref[...]).astype(BF16)
    qb = _dot(cqn, wuq_ref[...])
    for hd in range(N_HEADS):
        blk = qb[:, hd * HEAD_PAD:(hd + 1) * HEAD_PAD]
        qh = blk * cq_tab + pltpu.roll(blk, _ROLL_PARTNER, 1) * sq_tab
        q_ref[0, hd] = qh.astype(BF16)

    ckvn = (_rms(z[:, _C_CKV:_C_KR]) * gkv_ref[...]).astype(BF16)
    kvb = _dot(ckvn, wukv_ref[...])
    krb = z[:, _C_KR:_C_U]
    krope = krb * cosb + pltpu.roll(krb, _ROLL_PARTNER, 1) * sinb
    v_off = N_HEADS * HEAD_PAD
    for hd in range(N_HEADS):
        kh = kvb[:, hd * HEAD_PAD:(hd + 1) * HEAD_PAD] + krope
        for c in range(kt_ref.shape[2]):
            kt_ref[0, hd, c] = kh[c * K_TILE:(c + 1) * K_TILE].T.astype(BF16)
        ones_lane = V_HEAD if hd % 2 == 0 else 0
        vh = kvb[:, v_off + hd * HEAD_PAD:v_off + (hd + 1) * HEAD_PAD] + jnp.where(lane == ones_lane, 1.0, 0.0)
        v_ref[0, hd] = vh.astype(BF16)

    ua = _gelu_tanh(z[:, _C_U:_C_V])
    vn = (_rms(_gelu_tanh(z[:, _C_V:_C_END])) * gv_ref[...]).astype(BF16)
    n_tok = x.shape[0]
    bsp = bsp_ref[...]
    rows = []
    for n in range(n_tok // CHUNK):
        cols = []
        for j in range(GMLP_GROUPS // 2):
            rhs = vn[n * CHUNK:(n + 1) * CHUNK, j * LANES:(j + 1) * LANES]
            ab = _dot(wsp_ref[j], rhs)
            cols.append(jnp.where(lane < GMLP_GROUP_DIM, ab[:CHUNK], ab[CHUNK:]))
        rows.append(jnp.concatenate(cols, axis=1) + bsp)
    s = ua * jnp.concatenate(rows, axis=0)
    sn_ref[0] = (_rms(s) * ggo_ref[...]).astype(BF16)


def _premix(x, mod, P):
    nb, seq, _ = x.shape
    tm = TOKEN_TILE
    nck = seq // K_TILE
    const = lambda i, b: (0, 0)
    return pl.pallas_call(
        _premix_kernel,
        grid=(seq // tm, nb),
        in_specs=[
            pl.BlockSpec((1, tm, D_MODEL), lambda i, b: (b, i, 0)),
            pl.BlockSpec((1, SUBLANES, D_MODEL), lambda i, b: (b, 0, 0)),
            pl.BlockSpec((tm, LANES), lambda i, b: (i, 0)),
            pl.BlockSpec((tm, LANES), lambda i, b: (i, 0)),
            pl.BlockSpec((1, D_MODEL), const),
            pl.BlockSpec((D_MODEL, _C_END), const),
            pl.BlockSpec((1, Q_LORA), const),
            pl.BlockSpec((Q_LORA, N_HEADS * HEAD_PAD), const),
            pl.BlockSpec((1, KV_LORA), const),
            pl.BlockSpec((KV_LORA, 2 * N_HEADS * HEAD_PAD), const),
            pl.BlockSpec((1, GMLP_WIDTH), const),
            pl.BlockSpec((GMLP_GROUPS // 2, 2 * CHUNK, CHUNK), lambda i, b: (0, 0, 0)),
            pl.BlockSpec((CHUNK, GMLP_WIDTH), const),
            pl.BlockSpec((1, GMLP_WIDTH), const),
        ],
        out_specs=[
            pl.BlockSpec((1, N_HEADS, tm, HEAD_PAD), lambda i, b: (b, 0, i, 0)),
            pl.BlockSpec((1, N_HEADS, tm // K_TILE, HEAD_PAD, K_TILE), lambda i, b: (b, 0, i, 0, 0)),
            pl.BlockSpec((1, N_HEADS, tm, HEAD_PAD), lambda i, b: (b, 0, i, 0)),
            pl.BlockSpec((1, tm, GMLP_WIDTH), lambda i, b: (b, i, 0)),
        ],
        out_shape=[
            jax.ShapeDtypeStruct((nb, N_HEADS, seq, HEAD_PAD), BF16),
            jax.ShapeDtypeStruct((nb, N_HEADS, nck, HEAD_PAD, K_TILE), BF16),
            jax.ShapeDtypeStruct((nb, N_HEADS, seq, HEAD_PAD), BF16),
            jax.ShapeDtypeStruct((nb, seq, GMLP_WIDTH), BF16),
        ],
        compiler_params=_params(("arbitrary", "arbitrary")),
        name="premix",
    )(x, mod, P["cos"][:seq], P["sin"][:seq], P["g_pre1"], P["w_in"], P["g_q"], P["w_uq"], P["g_kv"], P["w_ukv"],
      P["g_v"], P["w_sp"], P["b_sp"], P["g_gmlp_out"])


def _attn_kernel(q_ref, kt_ref, v_ref, o_ref):
    n_chunks = kt_ref.shape[2]
    tk = kt_ref.shape[4]
    lane = lax.broadcasted_iota(jnp.int32, (1, LANES), 1)

    def one_head(hd, ones_lane):
        q = q_ref[0, hd]
        m = None
        acc = None
        for c in range(n_chunks):
            s = _dot(q, kt_ref[0, hd, c])
            v = v_ref[0, hd, c * tk:(c + 1) * tk, :]
            smax = jnp.max(s, axis=1, keepdims=True)
            if c == 0:
                m = smax
                acc = _dot(jnp.exp2((s - m).astype(BF16)), v)
            else:
                m_new = jnp.maximum(m, smax)
                acc = acc * jnp.exp2(m - m_new) + _dot(jnp.exp2((s - m_new).astype(BF16)), v)
                m = m_new
        row_sum = acc[:, ones_lane:ones_lane + 1]
        return acc * (1.0 / row_sum)

    def pair(j, carry):
        even = one_head(2 * j, V_HEAD)
        odd = one_head(2 * j + 1, 0)
        o_ref[0, j] = jnp.where(lane < V_HEAD, even, odd).astype(BF16)
        return carry

    lax.fori_loop(0, N_HEADS // 2, pair, 0)


def _attention(q, kt, v):
    nb, _, seq, _ = q.shape
    nck = kt.shape[2]
    return pl.pallas_call(
        _attn_kernel,
        grid=(nb, seq // Q_TILE),
        in_specs=[
            pl.BlockSpec((1, N_HEADS, Q_TILE, HEAD_PAD), lambda b, i: (b, 0, i, 0)),
            pl.BlockSpec((1, N_HEADS, nck, HEAD_PAD, K_TILE), lambda b, i: (b, 0, 0, 0, 0)),
            pl.BlockSpec((1, N_HEADS, seq, HEAD_PAD), lambda b, i: (b, 0, 0, 0)),
        ],
        out_specs=pl.BlockSpec((1, N_HEADS // 2, Q_TILE, LANES), lambda b, i: (b, 0, i, 0)),
        out_shape=jax.ShapeDtypeStruct((nb, N_HEADS // 2, seq, LANES), BF16),
        compiler_params=_params(("arbitrary", "arbitrary")),
        name="attn",
    )(q, kt, v)


_R_GROUP_ROW = N_EXPERTS


def _postmix_kernel(a_ref, sn_ref, x_ref, mod_ref, gao_ref, wout_ref, gpost_ref, gpre_ref, wr_ref, br_ref, tri_ref,
                    x1_ref, h2_ref, eid_ref, rank_ref, wcol_ref, cnt_ref, run_ref):
    first = jnp.logical_and(pl.program_id(0) == 0, pl.program_id(1) == 0)

    @pl.when(first)
    def _():
        run_ref[...] = jnp.zeros_like(run_ref)

    mod = mod_ref[0]
    gate1, shift2, scale2 = mod[2:3], mod[3:4], mod[4:5]
    a = jnp.concatenate([a_ref[0, j] for j in range(N_HEADS // 2)], axis=1).astype(F32)
    an = (_rms(a) * gao_ref[...]).astype(BF16)
    merged = jnp.concatenate([an, sn_ref[0]], axis=1)
    o = _dot(merged, wout_ref[...])
    x1 = x_ref[0] + gate1 * (_rms(o) * gpost_ref[...])
    x1_ref[0] = x1
    h2 = _rms(x1) * (gpre_ref[...] * (1.0 + scale2)) + shift2
    n_tok = h2.shape[0]
    _store_token_rows(h2_ref, h2)

    h_hi, h_lo = _split_bf16(h2)
    wr = wr_ref[...]
    hh = _dot(h_hi, wr)
    lh = _dot(h_lo, wr[:, :LANES])
    logits = hh[:, :LANES] + hh[:, LANES:] + lh + br_ref[...]
    lt = logits.T

    neg = jnp.float32(-jnp.inf)
    row8 = lax.broadcasted_iota(jnp.int32, (SUBLANES, n_tok), 0).astype(F32)
    lg = jnp.where(row8 < N_EXPERT_GROUPS, lt[_R_GROUP_ROW:_R_GROUP_ROW + SUBLANES], neg)
    gmax = jnp.max(lg, axis=0, keepdims=True)
    gi = jnp.min(jnp.where(lg == gmax, row8, float(SUBLANES)), axis=0, keepdims=True)
    pg_sel = 1.0 / jnp.sum(jnp.exp(lg - gmax), axis=0, keepdims=True)

    le = jnp.zeros((EXPERTS_PER_GROUP, n_tok), F32)
    for g in range(N_EXPERT_GROUPS):
        le = jnp.where(gi == float(g), lt[g * EXPERTS_PER_GROUP:(g + 1) * EXPERTS_PER_GROUP], le)
    v1 = jnp.max(le, axis=0, keepdims=True)
    i1 = jnp.min(jnp.where(le == v1, row8, float(SUBLANES)), axis=0, keepdims=True)
    le2 = jnp.where(row8 == i1, neg, le)
    v2 = jnp.max(le2, axis=0, keepdims=True)
    i2 = jnp.min(jnp.where(le2 == v2, row8, float(SUBLANES)), axis=0, keepdims=True)
    r = jnp.exp(v2 - v1)
    w1 = pg_sel / (1.0 + r)
    w2 = w1 * r
    e1 = gi * float(EXPERTS_PER_GROUP) + i1
    e2 = gi * float(EXPERTS_PER_GROUP) + i2
    eid_ref[0] = jnp.concatenate([e1, e2], axis=0).astype(jnp.int32)

    row32 = lax.broadcasted_iota(jnp.int32, (N_EXPERTS, n_tok), 0).astype(F32)
    hit1 = row32 == e1
    hit2 = row32 == e2
    onehot = jnp.where(jnp.logical_or(hit1, hit2), 1.0, 0.0)
    run = run_ref[...][:, 0:1]
    ranks1, ranks2 = [], []
    for c in range(n_tok // RANK_CHUNK):
        sl = slice(c * RANK_CHUNK, (c + 1) * RANK_CHUNK)
        oh = onehot[:, sl]
        before = _dot(oh.astype(BF16), tri_ref[...]) + run
        ranks1.append(jnp.sum(jnp.where(hit1[:, sl], before, 0.0), axis=0, keepdims=True))
        ranks2.append(jnp.sum(jnp.where(hit2[:, sl], before, 0.0), axis=0, keepdims=True))
        run = run + jnp.sum(oh, axis=1, keepdims=True)
    rank_ref[0] = jnp.concatenate(
        [jnp.concatenate(ranks1, axis=1), jnp.concatenate(ranks2, axis=1)], axis=0).astype(jnp.int32)
    run_b = jnp.broadcast_to(run, run_ref.shape)
    run_ref[...] = run_b
    cnt_ref[...] = run_b

    row128 = lax.broadcasted_iota(jnp.int32, (LANES, n_tok), 0)
    wt = jnp.where(row128 == 0, w1, jnp.where(row128 == 1, w2, 0.0))
    wcol_ref[0] = wt.T


def _postmix(a, sn, x, mod, P):
    nb, seq, _ = x.shape
    tm = POST_TILE
    const = lambda b, i: (0, 0)
    return pl.pallas_call(
        _postmix_kernel,
        grid=(nb, seq // tm),
        in_specs=[
            pl.BlockSpec((1, N_HEADS // 2, tm, LANES), lambda b, i: (b, 0, i, 0)),
            pl.BlockSpec((1, tm, GMLP_WIDTH), lambda b, i: (b, i, 0)),
            pl.BlockSpec((1, tm, D_MODEL), lambda b, i: (b, i, 0)),
            pl.BlockSpec((1, SUBLANES, D_MODEL), lambda b, i: (b, 0, 0)),
            pl.BlockSpec((1, N_HEADS * V_HEAD), const),
            pl.BlockSpec((D_MODEL, D_MODEL), const),
            pl.BlockSpec((1, D_MODEL), const),
            pl.BlockSpec((1, D_MODEL), const),
            pl.BlockSpec((D_MODEL, 2 * LANES), const),
            pl.BlockSpec((1, LANES), const),
            pl.BlockSpec((RANK_CHUNK, RANK_CHUNK), const),
        ],
        out_specs=[
            pl.BlockSpec((1, tm, D_MODEL), lambda b, i: (b, i, 0)),
            pl.BlockSpec((tm * ROW_TILES, LANES), lambda b, i: (b * (seq // tm) + i, 0)),
            pl.BlockSpec((1, 2, tm), lambda b, i: (b, 0, i)),
            pl.BlockSpec((1, 2, tm), lambda b, i: (b, 0, i)),
            pl.BlockSpec((1, tm, LANES), lambda b, i: (b, i, 0)),
            pl.BlockSpec((N_EXPERTS, LANES), const),
        ],
        out_shape=[
            jax.ShapeDtypeStruct((nb, seq, D_MODEL), F32),
            jax.ShapeDtypeStruct((nb * seq * ROW_TILES, LANES), U32),
            jax.ShapeDtypeStruct((nb, 2, seq), jnp.int32),
            jax.ShapeDtypeStruct((nb, 2, seq), jnp.int32),
            jax.ShapeDtypeStruct((nb, seq, LANES), F32),
            jax.ShapeDtypeStruct((N_EXPERTS, LANES), F32),
        ],
        scratch_shapes=[pltpu.VMEM((N_EXPERTS, LANES), F32)],
        compiler_params=_params(("arbitrary", "arbitrary")),
        name="postmix",
    )(a, sn, x, mod, P["g_attn_out"], P["w_out"], P["g_post1"], P["g_pre2"], P["w_router"], P["b_router"], P["tri"])


def _sc_mesh():
    return plsc.VectorSubcoreMesh(core_axis_name="c", subcore_axis_name="s")


def _sc_worker():
    return lax.axis_index("s") * SC_CORES + lax.axis_index("c")


def _sc_gather_rows(table, idx):
    n = idx.shape[0]
    assert n % (SC_WORKERS * SC_ROWS) == 0
    per_w = n // SC_WORKERS
    steps = per_w // SC_ROWS
    idx3 = idx.reshape(SC_WORKERS, steps, SC_ROWS)

    @functools.partial(
        pl.kernel, mesh=_sc_mesh(),
        out_type=jax.ShapeDtypeStruct((n, ROW_TILES, LANES), U32),
        scratch_types=[pltpu.VMEM((steps, SC_ROWS), jnp.int32), pltpu.VMEM((SC_ROWS, ROW_TILES, LANES), U32),
                       pltpu.SemaphoreType.DMA],
        name="sc_gather_rows",
    )
    def gather(table_hbm, idx_hbm, out_hbm, idx_v, rows_v, sem):
        wid = _sc_worker()
        pltpu.sync_copy(idx_hbm.at[wid], idx_v)

        @pl.loop(0, steps)
        def _(j):
            pltpu.async_copy(table_hbm.at[idx_v.at[j]], rows_v, sem).wait()
            pltpu.sync_copy(rows_v, out_hbm.at[pl.ds(wid * per_w + j * SC_ROWS, SC_ROWS)])

    return gather(table, idx3)


def _sc_scatter_rows(src, idx, n_dst):
    n_dup, n = idx.shape
    assert n % (SC_WORKERS * SC_ROWS) == 0
    per_w = n // SC_WORKERS
    steps = per_w // SC_ROWS
    idx4 = idx.reshape(n_dup, SC_WORKERS, steps, SC_ROWS).transpose(1, 0, 2, 3)

    @functools.partial(
        pl.kernel, mesh=_sc_mesh(),
        out_type=jax.ShapeDtypeStruct((n_dst, ROW_TILES, LANES), U32),
        scratch_types=[pltpu.VMEM((n_dup, steps, SC_ROWS), jnp.int32), pltpu.VMEM((SC_ROWS, ROW_TILES, LANES), U32),
                       pltpu.SemaphoreType.DMA],
        name="sc_scatter_rows",
    )
    def scatter(src_hbm, idx_hbm, dst_hbm, idx_v, rows_v, sem):
        wid = _sc_worker()
        pltpu.sync_copy(idx_hbm.at[wid], idx_v)

        @pl.loop(0, steps)
        def _(j):
            pltpu.sync_copy(src_hbm.at[pl.ds(wid * per_w + j * SC_ROWS, SC_ROWS)], rows_v)
            copies = [pltpu.async_copy(rows_v, dst_hbm.at[idx_v.at[k, j]], sem) for k in range(n_dup)]
            for copy in copies:
                copy.wait()

    return scatter(src, idx4)


def _expert_kernel(te_ref, tr_ref, nv_ref, xs_ref, wg_ref, wu_ref, wd_ref, ys_ref, wgu_bf, wd_bf):
    i = pl.program_id(0)
    valid = i < nv_ref[0]
    new_expert = jnp.logical_or(i == 0, te_ref[i] != te_ref[jnp.maximum(i - 1, 0)])

    @pl.when(jnp.logical_and(valid, new_expert))
    def _():
        wgu_bf[:, :EXPERT_FF] = wg_ref[0].astype(BF16)
        wgu_bf[:, EXPERT_FF:] = wu_ref[0].astype(BF16)
        wd_bf[...] = wd_ref[0].astype(BF16)

    @pl.when(valid)
    def _():
        n_rows = xs_ref.shape[0] // ROW_TILES
        x = _load_token_rows(xs_ref, n_rows)
        row = lax.broadcasted_iota(jnp.int32, (n_rows, 1), 0)
        x = jnp.where(row < tr_ref[i], x, 0.0).astype(BF16)
        gu = _dot(x, wgu_bf[...])
        g, u = gu[:, :EXPERT_FF], gu[:, EXPERT_FF:]
        act = (g * jax.nn.sigmoid(g) * u).astype(BF16)
        _store_token_rows(ys_ref, _dot(act, wd_bf[...]))

    @pl.when(jnp.logical_not(valid))
    def _():
        ys_ref[...] = jnp.zeros_like(ys_ref)


def _expert_tile(n_tok):
    return EXPERT_TILE_LARGE if 2 * n_tok // N_EXPERTS >= 2 * EXPERT_TILE_LARGE else EXPERT_TILE_SMALL


def _experts(xs, tile_expert, tile_rows, n_valid, tile, P):
    rows_blk = tile * ROW_TILES
    assert xs.shape[0] % rows_blk == 0
    n_tiles = xs.shape[0] // rows_blk

    def row_map(i, te, tr, nv):
        return (jnp.minimum(i, nv[0] - 1), 0)

    def out_map(i, te, tr, nv):
        return (i, 0)

    def w_map(i, te, tr, nv):
        return (te[jnp.minimum(i, nv[0] - 1)], 0, 0)

    return pl.pallas_call(
        _expert_kernel,
        grid_spec=pltpu.PrefetchScalarGridSpec(
            num_scalar_prefetch=3,
            grid=(n_tiles,),
            in_specs=[
                pl.BlockSpec((rows_blk, LANES), row_map),
                pl.BlockSpec((1, D_MODEL, EXPERT_FF), w_map),
                pl.BlockSpec((1, D_MODEL, EXPERT_FF), w_map),
                pl.BlockSpec((1, EXPERT_FF, D_MODEL), w_map),
            ],
            out_specs=pl.BlockSpec((rows_blk, LANES), out_map),
            scratch_shapes=[pltpu.VMEM((D_MODEL, 2 * EXPERT_FF), BF16), pltpu.VMEM((EXPERT_FF, D_MODEL), BF16)],
        ),
        out_shape=jax.ShapeDtypeStruct(xs.shape, U32),
        compiler_params=_params(("arbitrary",)),
        name="experts",
    )(tile_expert, tile_rows, n_valid, xs, P["w_gate"], P["w_up"], P["w_down"])


def _final_kernel(y0_ref, y1_ref, wcol_ref, x1_ref, mod_ref, gpost_ref, o_ref):
    w = wcol_ref[0]
    w0, w1 = w[:, 0:1], w[:, 1:2]
    n_tok = w.shape[0]
    m = w0 * _load_token_rows(y0_ref, n_tok) + w1 * _load_token_rows(y1_ref, n_tok)
    gate2 = mod_ref[0][5:6]
    o_ref[0] = x1_ref[0] + gate2 * (_rms(m) * gpost_ref[...])


def _final(yg, wcol, x1, mod, P):
    nb, seq, _ = x1.shape
    tm = POST_TILE
    nt = seq // tm
    n_tok_tiles = nb * nt
    return pl.pallas_call(
        _final_kernel,
        grid=(nb, nt),
        in_specs=[
            pl.BlockSpec((tm * ROW_TILES, LANES), lambda b, i: (b * nt + i, 0)),
            pl.BlockSpec((tm * ROW_TILES, LANES), lambda b, i: (n_tok_tiles + b * nt + i, 0)),
            pl.BlockSpec((1, tm, LANES), lambda b, i: (b, i, 0)),
            pl.BlockSpec((1, tm, D_MODEL), lambda b, i: (b, i, 0)),
            pl.BlockSpec((1, SUBLANES, D_MODEL), lambda b, i: (b, 0, 0)),
            pl.BlockSpec((1, D_MODEL), lambda b, i: (0, 0)),
        ],
        out_specs=pl.BlockSpec((1, tm, D_MODEL), lambda b, i: (b, i, 0)),
        out_shape=jax.ShapeDtypeStruct((nb, seq, D_MODEL), F32),
        compiler_params=_params(("arbitrary", "arbitrary")),
        name="final",
    )(yg, yg, wcol, x1, mod, P["g_post2"])


def _prepare(w):
    f = lambda a: a.astype(F32)
    P = {}
    for k in ("g_pre1", "g_post1", "g_pre2", "g_post2", "g_q", "g_kv", "g_attn_out", "g_gmlp_out"):
        P[k] = f(w[k]).reshape(1, -1)
    P["g_v"] = f(w["g_v_gmlp"]).reshape(1, -1)

    w_in = f(w["w_in"])
    o0, o1, o2, o3 = Q_LORA, Q_LORA + KV_LORA, Q_LORA + KV_LORA + QK_ROPE, Q_LORA + KV_LORA + QK_ROPE + GMLP_WIDTH
    w_kr = w_in[:, o1:o2]
    kr_partner = jnp.concatenate([-w_kr[:, ROPE_HALF:], w_kr[:, :ROPE_HALF]], axis=1)
    rope_blk = jnp.concatenate([jnp.zeros((D_MODEL, QK_NOPE), F32), w_kr, kr_partner], axis=1)
    P["w_in"] = jnp.concatenate([w_in[:, :o1], rope_blk, w_in[:, o2:o3], w_in[:, o3:]], axis=1).astype(BF16)

    w_uq = f(w["w_uq"]).reshape(Q_LORA, N_HEADS, QK_NOPE + QK_ROPE)
    q_rope = w_uq[:, :, QK_NOPE:]
    q_partner = jnp.concatenate([-q_rope[:, :, ROPE_HALF:], q_rope[:, :, :ROPE_HALF]], axis=2)
    P["w_uq"] = jnp.concatenate([w_uq, q_partner], axis=2).reshape(Q_LORA, N_HEADS * HEAD_PAD).astype(BF16)

    w_ukv = f(w["w_ukv"]).reshape(KV_LORA, N_HEADS, QK_NOPE + V_HEAD)
    zeros = jnp.zeros((KV_LORA, N_HEADS, HEAD_PAD - QK_NOPE), F32)
    w_k = jnp.concatenate([w_ukv[:, :, :QK_NOPE], zeros], axis=2)
    w_v = w_ukv[:, :, QK_NOPE:]
    even = (jnp.arange(N_HEADS) % 2 == 0)[None, :, None]
    zv = jnp.zeros_like(w_v)
    w_v = jnp.concatenate([jnp.where(even, w_v, zv), jnp.where(even, zv, w_v)], axis=2)
    P["w_ukv"] = jnp.concatenate([w_k.reshape(KV_LORA, -1), w_v.reshape(KV_LORA, -1)], axis=1).astype(BF16)

    P["w_sp"] = f(w["w_spatial"]).reshape(GMLP_GROUPS // 2, 2 * CHUNK, CHUNK).astype(BF16)
    P["b_sp"] = jnp.repeat(f(w["b_spatial"]).T, GMLP_GROUP_DIM, axis=1)

    P["w_out"] = f(w["w_out"]).astype(BF16)

    pad = jnp.zeros((D_MODEL, LANES - N_EXPERTS - N_EXPERT_GROUPS), F32)
    wr = jnp.concatenate([f(w["w_router_expert"]), f(w["w_router_group"]), pad], axis=1)
    wr_hi = wr.astype(BF16)
    wr_lo = (wr - wr_hi.astype(F32)).astype(BF16)
    P["w_router"] = jnp.concatenate([wr_hi, wr_lo], axis=1)
    P["b_router"] = jnp.concatenate(
        [f(w["b_router_expert"]), f(w["b_router_group"]), jnp.zeros((LANES - N_EXPERTS - N_EXPERT_GROUPS,), F32)]
    ).reshape(1, LANES)
    P["tri"] = jnp.triu(jnp.ones((RANK_CHUNK, RANK_CHUNK), F32), k=1).astype(BF16)

    P["w_gate"], P["w_up"], P["w_down"] = f(w["w_gate"]), f(w["w_up"]), f(w["w_down"])
    return P


def _rope_tables(seq):
    inv = ROPE_THETA ** (-jnp.arange(ROPE_HALF, dtype=F32) / ROPE_HALF)
    ang = jnp.arange(seq, dtype=F32)[:, None] * inv[None, :]
    z_lo = jnp.zeros((seq, _ROPE_LO), F32)
    z_hi = jnp.zeros((seq, LANES - _ROPE_LO - QK_ROPE), F32)
    cos = jnp.concatenate([z_lo, jnp.cos(ang), jnp.cos(ang), z_hi], axis=1)
    sin = jnp.concatenate([z_lo, jnp.sin(ang), jnp.sin(ang), z_hi], axis=1)
    return cos, sin


def _layer(x, mod, P):
    nb, seq, _ = x.shape
    n_tok = nb * seq
    q, kt, v, sn = _premix(x, mod, P)
    a = _attention(q, kt, v)
    x1, h2rows, eid, rank, wcol, counts = _postmix(a, sn, x, mod, P)

    tile = _expert_tile(n_tok)
    cnt = counts[:, 0].astype(jnp.int32)
    padded = ((cnt + tile - 1) // tile) * tile
    ends = jnp.cumsum(padded)
    starts = ends - padded
    eflat = jnp.transpose(eid, (1, 0, 2)).reshape(2, n_tok)
    rflat = jnp.transpose(rank, (1, 0, 2)).reshape(2, n_tok)
    onehot = eflat[:, :, None] == jnp.arange(N_EXPERTS, dtype=jnp.int32)[None, None, :]
    pos = rflat + jnp.sum(jnp.where(onehot, starts[None, None, :], 0), axis=2)
    n_rows = 2 * n_tok + N_EXPERTS * tile
    n_tiles = n_rows // tile
    tile_start = jnp.arange(n_tiles, dtype=jnp.int32) * tile
    tile_expert = jnp.minimum(
        jnp.sum((tile_start[:, None] >= ends[None, :]).astype(jnp.int32), axis=1), N_EXPERTS - 1).astype(jnp.int32)
    live_end = (starts + cnt)[tile_expert]
    tile_rows = jnp.clip(live_end - tile_start, 0, tile).astype(jnp.int32)
    n_valid = (ends[-1:] // tile).astype(jnp.int32)

    as_tiles = lambda a: a.reshape(-1, ROW_TILES, LANES)
    as_rows = lambda a: a.reshape(-1, LANES)
    xs = _sc_scatter_rows(as_tiles(h2rows), pos, n_rows)
    ys = _experts(as_rows(xs), tile_expert, tile_rows, n_valid, tile, P)
    yg = as_rows(_sc_gather_rows(as_tiles(ys), pos.reshape(2 * n_tok)))
    return _final(yg, wcol, x1, mod, P)


def kernel(x_prompt, x_sample, c_prompt, c_sample, w_ada, b_ada, g_pre1, g_post1, g_pre2, g_post2, w_in, g_q, w_uq,
           g_kv, w_ukv, g_v_gmlp, w_spatial, b_spatial, g_attn_out, g_gmlp_out, w_out, w_router_group,
           b_router_group, w_router_expert, b_router_expert, w_gate, w_up, w_down):
    P = _prepare(dict(
        g_pre1=g_pre1, g_post1=g_post1, g_pre2=g_pre2, g_post2=g_post2, w_in=w_in, g_q=g_q, w_uq=w_uq, g_kv=g_kv,
        w_ukv=w_ukv, g_v_gmlp=g_v_gmlp, w_spatial=w_spatial, b_spatial=b_spatial, g_attn_out=g_attn_out,
        g_gmlp_out=g_gmlp_out, w_out=w_out, w_router_group=w_router_group, b_router_group=b_router_group,
        w_router_expert=w_router_expert, b_router_expert=b_router_expert, w_gate=w_gate, w_up=w_up, w_down=w_down))
    P["cos"], P["sin"] = _rope_tables(max(x_prompt.shape[1], x_sample.shape[1]))

    nbp = c_prompt.shape[0]
    c_all = jnp.concatenate([c_prompt, c_sample], axis=0).astype(F32)
    mod = _ada(c_all, w_ada.astype(F32), b_ada.astype(F32))
    mod = mod.reshape(c_all.shape[0], 6, D_MODEL)
    mod = jnp.concatenate([mod, jnp.zeros((c_all.shape[0], SUBLANES - 6, D_MODEL), F32)], axis=1)

    y_prompt = _layer(x_prompt, mod[:nbp], P)
    y_sample = _layer(x_sample, mod[nbp:], P)
    return (y_prompt, y_sample)
```

```python
import functools
import math

import jax
import jax.numpy as jnp
from jax import lax
from jax.experimental import pallas as pl
from jax.experimental.pallas import tpu as pltpu
from jax.experimental.pallas import tpu_sc as plsc

F32 = jnp.float32
BF16 = jnp.bfloat16

D_MODEL = 1024
N_HEADS = 8
QK_NOPE = 64
QK_ROPE = 32
ROPE_HALF = QK_ROPE // 2
V_HEAD = 64
Q_LORA = 256
KV_LORA = 128
GMLP_WIDTH = 512
GMLP_GROUPS = 8
GMLP_GROUP_DIM = 64
CHUNK = 128
N_EXPERTS = 32
N_EXPERT_GROUPS = 4
EXPERTS_PER_GROUP = 8
EXPERT_FF = 256
ROPE_THETA = 10000.0
EPS = 1e-6

LANES = 128
SUBLANES = 8
HEAD_PAD = LANES

TOKEN_TILE = 1024
POST_TILE = 1024
Q_TILE = 1024
K_TILE = 512
EXPERT_TILE_SMALL = 512
EXPERT_TILE_LARGE = 1024
SC_CORES = 2
SC_WORKERS = 32
SC_ROWS = 128
RANK_CHUNK = 256
VMEM_LIMIT = 56 * 1024 * 1024

U32 = jnp.uint32
PACKED_WIDTH = D_MODEL // 2
ROW_TILES = PACKED_WIDTH // LANES
_HI_MASK = 0xFFFF0000

_SQRT_2_OVER_PI = math.sqrt(2.0 / math.pi)


def _rms(x):
    return x * lax.rsqrt(jnp.mean(x * x, axis=-1, keepdims=True) + EPS)


def _gelu_tanh(x):
    return 0.5 * x * (1.0 + jnp.tanh(_SQRT_2_OVER_PI * (x + 0.044715 * (x * x * x))))


def _split_bf16(x):
    hi = x.astype(BF16)
    lo = (x - hi.astype(F32)).astype(BF16)
    return hi, lo


def _dot(a, b):
    return jnp.dot(a, b, preferred_element_type=F32)


def _bf16_bits(x):
    return lax.bitcast_convert_type(x.astype(BF16).astype(F32), U32)


def _load_token_rows(ref, n):
    w = jnp.concatenate([ref[pl.ds(c, n, stride=ROW_TILES), :] for c in range(ROW_TILES)], axis=1)
    lo = lax.bitcast_convert_type(w << 16, F32)
    hi = lax.bitcast_convert_type(w & jnp.uint32(_HI_MASK), F32)
    return jnp.concatenate([lo, hi], axis=1)


def _store_token_rows(ref, val):
    n = val.shape[0]
    w = (_bf16_bits(val[:, :PACKED_WIDTH]) >> 16) | (_bf16_bits(val[:, PACKED_WIDTH:]) & jnp.uint32(_HI_MASK))
    for c in range(ROW_TILES):
        ref[pl.ds(c, n, stride=ROW_TILES), :] = w[:, c * LANES:(c + 1) * LANES]


def _params(sem, vmem=VMEM_LIMIT):
    return pltpu.CompilerParams(dimension_semantics=sem, vmem_limit_bytes=vmem)


def _ada_kernel(c_ref, w_ref, b_ref, o_ref):
    c = c_ref[...]
    a = c * jax.nn.sigmoid(c)
    a_hi, a_lo = _split_bf16(a)
    w_hi, w_lo = _split_bf16(w_ref[...])
    o_ref[...] = _dot(a_hi, w_hi) + _dot(a_hi, w_lo) + _dot(a_lo, w_hi) + b_ref[...]


def _ada(c, w_ada, b_ada):
    nb = c.shape[0]
    n_out = w_ada.shape[1]
    blk = D_MODEL
    return pl.pallas_call(
        _ada_kernel,
        grid=(n_out // blk,),
        in_specs=[
            pl.BlockSpec((nb, D_MODEL), lambda j: (0, 0)),
            pl.BlockSpec((D_MODEL, blk), lambda j: (0, j)),
            pl.BlockSpec((1, blk), lambda j: (0, j)),
        ],
        out_specs=pl.BlockSpec((nb, blk), lambda j: (0, j)),
        out_shape=jax.ShapeDtypeStruct((nb, n_out), F32),
        compiler_params=_params(("arbitrary",)),
        name="ada",
    )(c, w_ada, b_ada.reshape(1, n_out))


_C_CQ = 0
_C_CKV = _C_CQ + Q_LORA
_C_KR = _C_CKV + KV_LORA
_C_U = _C_KR + LANES
_C_V = _C_U + GMLP_WIDTH
_C_END = _C_V + GMLP_WIDTH
_ROPE_LO = QK_NOPE
_ROLL_PARTNER = LANES - QK_ROPE


def _premix_kernel(x_ref, mod_ref, cos_ref, sin_ref, gpre_ref, win_ref, gq_ref, wuq_ref, gkv_ref, wukv_ref,
                   gv_ref, wsp_ref, bsp_ref, ggo_ref, q_ref, kt_ref, v_ref, sn_ref):
    x = x_ref[0]
    mod = mod_ref[0]
    shift1, scale1 = mod[0:1], mod[1:2]
    h = _rms(x) * (gpre_ref[...] * (1.0 + scale1)) + shift1
    z = _dot(h.astype(BF16), win_ref[...])

    cosb = cos_ref[...]
    sinb = sin_ref[...]
    lane = lax.broadcasted_iota(jnp.int32, (1, LANES), 1)
    nope_mask = jnp.where(lane < QK_NOPE, 1.0, 0.0).astype(F32)

    qscale = (QK_NOPE + QK_ROPE) ** -0.5 * math.log2(math.e)
    cq_tab = (nope_mask + cosb) * qscale
    sq_tab = sinb * qscale
    cqn = (_rms(z[:, _C_CQ:_C_CKV]) * gq_ref[...]).astype(BF16)
    qb = _dot(cqn, wuq_ref[...])
    for hd in range(N_HEADS):
        blk = qb[:, hd * HEAD_PAD:(hd + 1) * HEAD_PAD]
        qh = blk * cq_tab + pltpu.roll(blk, _ROLL_PARTNER, 1) * sq_tab
        q_ref[0, hd] = qh.astype(BF16)

    ckvn = (_rms(z[:, _C_CKV:_C_KR]) * gkv_ref[...]).astype(BF16)
    kvb = _dot(ckvn, wukv_ref[...])
    krb = z[:, _C_KR:_C_U]
    krope = krb * cosb + pltpu.roll(krb, _ROLL_PARTNER, 1) * sinb
    v_off = N_HEADS * HEAD_PAD
    for hd in range(N_HEADS):
        kh = kvb[:, hd * HEAD_PAD:(hd + 1) * HEAD_PAD] + krope
        for c in range(kt_ref.shape[2]):
            kt_ref[0, hd, c] = kh[c * K_TILE:(c + 1) * K_TILE].T.astype(BF16)
        ones_lane = V_HEAD if hd % 2 == 0 else 0
        vh = kvb[:, v_off + hd * HEAD_PAD:v_off + (hd + 1) * HEAD_PAD] + jnp.where(lane == ones_lane, 1.0, 0.0)
        v_ref[0, hd] = vh.astype(BF16)

    ua = _gelu_tanh(z[:, _C_U:_C_V])
    vn = (_rms(_gelu_tanh(z[:, _C_V:_C_END])) * gv_ref[...]).astype(BF16)
    n_tok = x.shape[0]
    bsp = bsp_ref[...]
    rows = []
    for n in range(n_tok // CHUNK):
        cols = []
        for j in range(GMLP_GROUPS // 2):
            rhs = vn[n * CHUNK:(n + 1) * CHUNK, j * LANES:(j + 1) * LANES]
            ab = _dot(wsp_ref[j], rhs)
            cols.append(jnp.where(lane < GMLP_GROUP_DIM, ab[:CHUNK], ab[CHUNK:]))
        rows.append(jnp.concatenate(cols, axis=1) + bsp)
    s = ua * jnp.concatenate(rows, axis=0)
    sn_ref[0] = (_rms(s) * ggo_ref[...]).astype(BF16)


def _premix(x, mod, P):
    nb, seq, _ = x.shape
    tm = TOKEN_TILE
    nck = seq // K_TILE
    const = lambda i, b: (0, 0)
    return pl.pallas_call(
        _premix_kernel,
        grid=(seq // tm, nb),
        in_specs=[
            pl.BlockSpec((1, tm, D_MODEL), lambda i, b: (b, i, 0)),
            pl.BlockSpec((1, SUBLANES, D_MODEL), lambda i, b: (b, 0, 0)),
            pl.BlockSpec((tm, LANES), lambda i, b: (i, 0)),
            pl.BlockSpec((tm, LANES), lambda i, b: (i, 0)),
            pl.BlockSpec((1, D_MODEL), const),
            pl.BlockSpec((D_MODEL, _C_END), const),
            pl.BlockSpec((1, Q_LORA), const),
            pl.BlockSpec((Q_LORA, N_HEADS * HEAD_PAD), const),
            pl.BlockSpec((1, KV_LORA), const),
            pl.BlockSpec((KV_LORA, 2 * N_HEADS * HEAD_PAD), const),
            pl.BlockSpec((1, GMLP_WIDTH), const),
            pl.BlockSpec((GMLP_GROUPS // 2, 2 * CHUNK, CHUNK), lambda i, b: (0, 0, 0)),
            pl.BlockSpec((CHUNK, GMLP_WIDTH), const),
            pl.BlockSpec((1, GMLP_WIDTH), const),
        ],
        out_specs=[
            pl.BlockSpec((1, N_HEADS, tm, HEAD_PAD), lambda i, b: (b, 0, i, 0)),
            pl.BlockSpec((1, N_HEADS, tm // K_TILE, HEAD_PAD, K_TILE), lambda i, b: (b, 0, i, 0, 0)),
            pl.BlockSpec((1, N_HEADS, tm, HEAD_PAD), lambda i, b: (b, 0, i, 0)),
            pl.BlockSpec((1, tm, GMLP_WIDTH), lambda i, b: (b, i, 0)),
        ],
        out_shape=[
            jax.ShapeDtypeStruct((nb, N_HEADS, seq, HEAD_PAD), BF16),
            jax.ShapeDtypeStruct((nb, N_HEADS, nck, HEAD_PAD, K_TILE), BF16),
            jax.ShapeDtypeStruct((nb, N_HEADS, seq, HEAD_PAD), BF16),
            jax.ShapeDtypeStruct((nb, seq, GMLP_WIDTH), BF16),
        ],
        compiler_params=_params(("arbitrary", "arbitrary")),
        name="premix",
    )(x, mod, *P["rope"][seq], P["g_pre1"], P["w_in"], P["g_q"], P["w_uq"], P["g_kv"], P["w_ukv"],
      P["g_v"], P["w_sp"], P["b_sp"], P["g_gmlp_out"])


def _attn_kernel(q_ref, kt_ref, v_ref, o_ref):
    n_chunks = kt_ref.shape[2]
    tk = kt_ref.shape[4]
    lane = lax.broadcasted_iota(jnp.int32, (1, LANES), 1)

    def one_head(hd, ones_lane):
        q = q_ref[0, hd]
        m = None
        acc = None
        for c in range(n_chunks):
            s = _dot(q, kt_ref[0, hd, c])
            v = v_ref[0, hd, c * tk:(c + 1) * tk, :]
            smax = jnp.max(s, axis=1, keepdims=True)
            if c == 0:
                m = smax
                acc = _dot(jnp.exp2((s - m).astype(BF16)), v)
            else:
                m_new = jnp.maximum(m, smax)
                acc = acc * jnp.exp2(m - m_new) + _dot(jnp.exp2((s - m_new).astype(BF16)), v)
                m = m_new
        row_sum = acc[:, ones_lane:ones_lane + 1]
        return acc * (1.0 / row_sum)

    def pair(j, carry):
        even = one_head(2 * j, V_HEAD)
        odd = one_head(2 * j + 1, 0)
        o_ref[0, j] = jnp.where(lane < V_HEAD, even, odd).astype(BF16)
        return carry

    lax.fori_loop(0, N_HEADS // 2, pair, 0)


def _attention(q, kt, v):
    nb, _, seq, _ = q.shape
    nck = kt.shape[2]
    return pl.pallas_call(
        _attn_kernel,
        grid=(nb, seq // Q_TILE),
        in_specs=[
            pl.BlockSpec((1, N_HEADS, Q_TILE, HEAD_PAD), lambda b, i: (b, 0, i, 0)),
            pl.BlockSpec((1, N_HEADS, nck, HEAD_PAD, K_TILE), lambda b, i: (b, 0, 0, 0, 0)),
            pl.BlockSpec((1, N_HEADS, seq, HEAD_PAD), lambda b, i: (b, 0, 0, 0)),
        ],
        out_specs=pl.BlockSpec((1, N_HEADS // 2, Q_TILE, LANES), lambda b, i: (b, 0, i, 0)),
        out_shape=jax.ShapeDtypeStruct((nb, N_HEADS // 2, seq, LANES), BF16),
        compiler_params=_params(("arbitrary", "arbitrary")),
        name="attn",
    )(q, kt, v)


_R_GROUP_ROW = N_EXPERTS


def _postmix_kernel(a_ref, sn_ref, x_ref, mod_ref, gao_ref, wout_ref, gpost_ref, gpre_ref, wr_ref, br_ref, tri_ref,
                    x1_ref, h2_ref, eid_ref, rank_ref, wcol_ref, cnt_ref, run_ref):
    first = jnp.logical_and(pl.program_id(0) == 0, pl.program_id(1) == 0)

    @pl.when(first)
    def _():
        run_ref[...] = jnp.zeros_like(run_ref)

    mod = mod_ref[0]
    gate1, shift2, scale2 = mod[2:3], mod[3:4], mod[4:5]
    a = jnp.concatenate([a_ref[0, j] for j in range(N_HEADS // 2)], axis=1).astype(F32)
    an = (_rms(a) * gao_ref[...]).astype(BF16)
    merged = jnp.concatenate([an, sn_ref[0]], axis=1)
    o = _dot(merged, wout_ref[...])
    x1 = x_ref[0] + gate1 * (_rms(o) * gpost_ref[...])
    x1_ref[0] = x1
    h2 = _rms(x1) * (gpre_ref[...] * (1.0 + scale2)) + shift2
    n_tok = h2.shape[0]
    _store_token_rows(h2_ref, h2)

    h_hi, h_lo = _split_bf16(h2)
    wr = wr_ref[...]
    hh = _dot(h_hi, wr)
    lh = _dot(h_lo, wr[:, :LANES])
    logits = hh[:, :LANES] + hh[:, LANES:] + lh + br_ref[...]
    lt = logits.T

    neg = jnp.float32(-jnp.inf)
    row8 = lax.broadcasted_iota(jnp.int32, (SUBLANES, n_tok), 0).astype(F32)
    lg = jnp.where(row8 < N_EXPERT_GROUPS, lt[_R_GROUP_ROW:_R_GROUP_ROW + SUBLANES], neg)
    gmax = jnp.max(lg, axis=0, keepdims=True)
    gi = jnp.min(jnp.where(lg == gmax, row8, float(SUBLANES)), axis=0, keepdims=True)
    pg_sel = 1.0 / jnp.sum(jnp.exp(lg - gmax), axis=0, keepdims=True)

    le = jnp.zeros((EXPERTS_PER_GROUP, n_tok), F32)
    for g in range(N_EXPERT_GROUPS):
        le = jnp.where(gi == float(g), lt[g * EXPERTS_PER_GROUP:(g + 1) * EXPERTS_PER_GROUP], le)
    v1 = jnp.max(le, axis=0, keepdims=True)
    i1 = jnp.min(jnp.where(le == v1, row8, float(SUBLANES)), axis=0, keepdims=True)
    le2 = jnp.where(row8 == i1, neg, le)
    v2 = jnp.max(le2, axis=0, keepdims=True)
    i2 = jnp.min(jnp.where(le2 == v2, row8, float(SUBLANES)), axis=0, keepdims=True)
    r = jnp.exp(v2 - v1)
    w1 = pg_sel / (1.0 + r)
    w2 = w1 * r
    e1 = gi * float(EXPERTS_PER_GROUP) + i1
    e2 = gi * float(EXPERTS_PER_GROUP) + i2
    eid_ref[0] = jnp.concatenate([e1, e2], axis=0).astype(jnp.int32)

    row32 = lax.broadcasted_iota(jnp.int32, (N_EXPERTS, n_tok), 0).astype(F32)
    hit1 = row32 == e1
    hit2 = row32 == e2
    onehot = jnp.where(jnp.logical_or(hit1, hit2), 1.0, 0.0)
    run = run_ref[...][:, 0:1]
    ranks1, ranks2 = [], []
    for c in range(n_tok // RANK_CHUNK):
        sl = slice(c * RANK_CHUNK, (c + 1) * RANK_CHUNK)
        oh = onehot[:, sl]
        before = _dot(oh.astype(BF16), tri_ref[...]) + run
        ranks1.append(jnp.sum(jnp.where(hit1[:, sl], before, 0.0), axis=0, keepdims=True))
        ranks2.append(jnp.sum(jnp.where(hit2[:, sl], before, 0.0), axis=0, keepdims=True))
        run = run + jnp.sum(oh, axis=1, keepdims=True)
    rank_ref[0] = jnp.concatenate(
        [jnp.concatenate(ranks1, axis=1), jnp.concatenate(ranks2, axis=1)], axis=0).astype(jnp.int32)
    run_b = jnp.broadcast_to(run, run_ref.shape)
    run_ref[...] = run_b
    cnt_ref[...] = run_b

    row128 = lax.broadcasted_iota(jnp.int32, (LANES, n_tok), 0)
    wt = jnp.where(row128 == 0, w1, jnp.where(row128 == 1, w2, 0.0))
    wcol_ref[0] = wt.T


def _postmix(a, sn, x, mod, P):
    nb, seq, _ = x.shape
    tm = POST_TILE
    const = lambda b, i: (0, 0)
    return pl.pallas_call(
        _postmix_kernel,
        grid=(nb, seq // tm),
        in_specs=[
            pl.BlockSpec((1, N_HEADS // 2, tm, LANES), lambda b, i: (b, 0, i, 0)),
            pl.BlockSpec((1, tm, GMLP_WIDTH), lambda b, i: (b, i, 0)),
            pl.BlockSpec((1, tm, D_MODEL), lambda b, i: (b, i, 0)),
            pl.BlockSpec((1, SUBLANES, D_MODEL), lambda b, i: (b, 0, 0)),
            pl.BlockSpec((1, N_HEADS * V_HEAD), const),
            pl.BlockSpec((D_MODEL, D_MODEL), const),
            pl.BlockSpec((1, D_MODEL), const),
            pl.BlockSpec((1, D_MODEL), const),
            pl.BlockSpec((D_MODEL, 2 * LANES), const),
            pl.BlockSpec((1, LANES), const),
            pl.BlockSpec((RANK_CHUNK, RANK_CHUNK), const),
        ],
        out_specs=[
            pl.BlockSpec((1, tm, D_MODEL), lambda b, i: (b, i, 0)),
            pl.BlockSpec((tm * ROW_TILES, LANES), lambda b, i: (b * (seq // tm) + i, 0)),
            pl.BlockSpec((1, 2, tm), lambda b, i: (b, 0, i)),
            pl.BlockSpec((1, 2, tm), lambda b, i: (b, 0, i)),
            pl.BlockSpec((1, tm, LANES), lambda b, i: (b, i, 0)),
            pl.BlockSpec((N_EXPERTS, LANES), const),
        ],
        out_shape=[
            jax.ShapeDtypeStruct((nb, seq, D_MODEL), F32),
            jax.ShapeDtypeStruct((nb * seq * ROW_TILES, LANES), U32),
            jax.ShapeDtypeStruct((nb, 2, seq), jnp.int32),
            jax.ShapeDtypeStruct((nb, 2, seq), jnp.int32),
            jax.ShapeDtypeStruct((nb, seq, LANES), F32),
            jax.ShapeDtypeStruct((N_EXPERTS, LANES), F32),
        ],
        scratch_shapes=[pltpu.VMEM((N_EXPERTS, LANES), F32)],
        compiler_params=_params(("arbitrary", "arbitrary")),
        name="postmix",
    )(a, sn, x, mod, P["g_attn_out"], P["w_out"], P["g_post1"], P["g_pre2"], P["w_router"], P["b_router"], P["tri"])


def _sc_mesh():
    return plsc.VectorSubcoreMesh(core_axis_name="c", subcore_axis_name="s")


def _sc_worker():
    return lax.axis_index("s") * SC_CORES + lax.axis_index("c")


def _sc_gather_rows(table, idx):
    n = idx.shape[0]
    assert n % (SC_WORKERS * SC_ROWS) == 0
    per_w = n // SC_WORKERS
    steps = per_w // SC_ROWS
    idx3 = idx.reshape(SC_WORKERS, steps, SC_ROWS)

    @functools.partial(
        pl.kernel, mesh=_sc_mesh(),
        out_type=jax.ShapeDtypeStruct((n, ROW_TILES, LANES), U32),
        scratch_types=[pltpu.VMEM((steps, SC_ROWS), jnp.int32), pltpu.VMEM((SC_ROWS, ROW_TILES, LANES), U32),
                       pltpu.SemaphoreType.DMA],
        name="sc_gather_rows",
    )
    def gather(table_hbm, idx_hbm, out_hbm, idx_v, rows_v, sem):
        wid = _sc_worker()
        pltpu.sync_copy(idx_hbm.at[wid], idx_v)

        @pl.loop(0, steps)
        def _(j):
            pltpu.async_copy(table_hbm.at[idx_v.at[j]], rows_v, sem).wait()
            pltpu.sync_copy(rows_v, out_hbm.at[pl.ds(wid * per_w + j * SC_ROWS, SC_ROWS)])

    return gather(table, idx3)


def _sc_scatter_rows(src, idx, n_dst):
    n_dup, n = idx.shape
    assert n % (SC_WORKERS * SC_ROWS) == 0
    per_w = n // SC_WORKERS
    steps = per_w // SC_ROWS
    idx4 = idx.reshape(n_dup, SC_WORKERS, steps, SC_ROWS).transpose(1, 0, 2, 3)

    @functools.partial(
        pl.kernel, mesh=_sc_mesh(),
        out_type=jax.ShapeDtypeStruct((n_dst, ROW_TILES, LANES), U32),
        scratch_types=[pltpu.VMEM((n_dup, steps, SC_ROWS), jnp.int32), pltpu.VMEM((SC_ROWS, ROW_TILES, LANES), U32),
                       pltpu.SemaphoreType.DMA],
        name="sc_scatter_rows",
    )
    def scatter(src_hbm, idx_hbm, dst_hbm, idx_v, rows_v, sem):
        wid = _sc_worker()
        pltpu.sync_copy(idx_hbm.at[wid], idx_v)

        @pl.loop(0, steps)
        def _(j):
            pltpu.sync_copy(src_hbm.at[pl.ds(wid * per_w + j * SC_ROWS, SC_ROWS)], rows_v)
            copies = [pltpu.async_copy(rows_v, dst_hbm.at[idx_v.at[k, j]], sem) for k in range(n_dup)]
            for copy in copies:
                copy.wait()

    return scatter(src, idx4)


def _expert_kernel(te_ref, tr_ref, nv_ref, xs_ref, wg_ref, wu_ref, wd_ref, ys_ref, wgu_bf, wd_bf):
    i = pl.program_id(0)
    valid = i < nv_ref[0]
    new_expert = jnp.logical_or(i == 0, te_ref[i] != te_ref[jnp.maximum(i - 1, 0)])

    @pl.when(jnp.logical_and(valid, new_expert))
    def _():
        wgu_bf[:, :EXPERT_FF] = wg_ref[0].astype(BF16)
        wgu_bf[:, EXPERT_FF:] = wu_ref[0].astype(BF16)
        wd_bf[...] = wd_ref[0].astype(BF16)

    @pl.when(valid)
    def _():
        n_rows = xs_ref.shape[0] // ROW_TILES
        x = _load_token_rows(xs_ref, n_rows)
        row = lax.broadcasted_iota(jnp.int32, (n_rows, 1), 0)
        x = jnp.where(row < tr_ref[i], x, 0.0).astype(BF16)
        gu = _dot(x, wgu_bf[...])
        g, u = gu[:, :EXPERT_FF], gu[:, EXPERT_FF:]
        act = (g * jax.nn.sigmoid(g) * u).astype(BF16)
        _store_token_rows(ys_ref, _dot(act, wd_bf[...]))

    @pl.when(jnp.logical_not(valid))
    def _():
        ys_ref[...] = jnp.zeros_like(ys_ref)


def _expert_tile(n_tok):
    return EXPERT_TILE_LARGE if 2 * n_tok // N_EXPERTS >= 2 * EXPERT_TILE_LARGE else EXPERT_TILE_SMALL


def _experts(xs, tile_expert, tile_rows, n_valid, tile, P):
    rows_blk = tile * ROW_TILES
    assert xs.shape[0] % rows_blk == 0
    n_tiles = xs.shape[0] // rows_blk

    def row_map(i, te, tr, nv):
        return (jnp.minimum(i, nv[0] - 1), 0)

    def out_map(i, te, tr, nv):
        return (i, 0)

    def w_map(i, te, tr, nv):
        return (te[jnp.minimum(i, nv[0] - 1)], 0, 0)

    return pl.pallas_call(
        _expert_kernel,
        grid_spec=pltpu.PrefetchScalarGridSpec(
            num_scalar_prefetch=3,
            grid=(n_tiles,),
            in_specs=[
                pl.BlockSpec((rows_blk, LANES), row_map),
                pl.BlockSpec((1, D_MODEL, EXPERT_FF), w_map),
                pl.BlockSpec((1, D_MODEL, EXPERT_FF), w_map),
                pl.BlockSpec((1, EXPERT_FF, D_MODEL), w_map),
            ],
            out_specs=pl.BlockSpec((rows_blk, LANES), out_map),
            scratch_shapes=[pltpu.VMEM((D_MODEL, 2 * EXPERT_FF), BF16), pltpu.VMEM((EXPERT_FF, D_MODEL), BF16)],
        ),
        out_shape=jax.ShapeDtypeStruct(xs.shape, U32),
        compiler_params=_params(("arbitrary",)),
        name="experts",
    )(tile_expert, tile_rows, n_valid, xs, P["w_gate"], P["w_up"], P["w_down"])


def _final_kernel(y0_ref, y1_ref, wcol_ref, x1_ref, mod_ref, gpost_ref, o_ref):
    w = wcol_ref[0]
    w0, w1 = w[:, 0:1], w[:, 1:2]
    n_tok = w.shape[0]
    m = w0 * _load_token_rows(y0_ref, n_tok) + w1 * _load_token_rows(y1_ref, n_tok)
    gate2 = mod_ref[0][5:6]
    o_ref[0] = x1_ref[0] + gate2 * (_rms(m) * gpost_ref[...])


def _final(yg, wcol, x1, mod, P):
    nb, seq, _ = x1.shape
    tm = POST_TILE
    nt = seq // tm
    n_tok_tiles = nb * nt
    return pl.pallas_call(
        _final_kernel,
        grid=(nb, nt),
        in_specs=[
            pl.BlockSpec((tm * ROW_TILES, LANES), lambda b, i: (b * nt + i, 0)),
            pl.BlockSpec((tm * ROW_TILES, LANES), lambda b, i: (n_tok_tiles + b * nt + i, 0)),
            pl.BlockSpec((1, tm, LANES), lambda b, i: (b, i, 0)),
            pl.BlockSpec((1, tm, D_MODEL), lambda b, i: (b, i, 0)),
            pl.BlockSpec((1, SUBLANES, D_MODEL), lambda b, i: (b, 0, 0)),
            pl.BlockSpec((1, D_MODEL), lambda b, i: (0, 0)),
        ],
        out_specs=pl.BlockSpec((1, tm, D_MODEL), lambda b, i: (b, i, 0)),
        out_shape=jax.ShapeDtypeStruct((nb, seq, D_MODEL), F32),
        compiler_params=_params(("arbitrary", "arbitrary")),
        name="final",
    )(yg, yg, wcol, x1, mod, P["g_post2"])


def _prepare(w):
    f = lambda a: a.astype(F32)
    P = {}
    for k in ("g_pre1", "g_post1", "g_pre2", "g_post2", "g_q", "g_kv", "g_attn_out", "g_gmlp_out"):
        P[k] = f(w[k]).reshape(1, -1)
    P["g_v"] = f(w["g_v_gmlp"]).reshape(1, -1)

    w_in = f(w["w_in"])
    o0, o1, o2, o3 = Q_LORA, Q_LORA + KV_LORA, Q_LORA + KV_LORA + QK_ROPE, Q_LORA + KV_LORA + QK_ROPE + GMLP_WIDTH
    w_kr = w_in[:, o1:o2]
    kr_partner = jnp.concatenate([-w_kr[:, ROPE_HALF:], w_kr[:, :ROPE_HALF]], axis=1)
    rope_blk = jnp.concatenate([jnp.zeros((D_MODEL, QK_NOPE), F32), w_kr, kr_partner], axis=1)
    P["w_in"] = jnp.concatenate([w_in[:, :o1], rope_blk, w_in[:, o2:o3], w_in[:, o3:]], axis=1).astype(BF16)

    w_uq = f(w["w_uq"]).reshape(Q_LORA, N_HEADS, QK_NOPE + QK_ROPE)
    q_rope = w_uq[:, :, QK_NOPE:]
    q_partner = jnp.concatenate([-q_rope[:, :, ROPE_HALF:], q_rope[:, :, :ROPE_HALF]], axis=2)
    P["w_uq"] = jnp.concatenate([w_uq, q_partner], axis=2).reshape(Q_LORA, N_HEADS * HEAD_PAD).astype(BF16)

    w_ukv = f(w["w_ukv"]).reshape(KV_LORA, N_HEADS, QK_NOPE + V_HEAD)
    zeros = jnp.zeros((KV_LORA, N_HEADS, HEAD_PAD - QK_NOPE), F32)
    w_k = jnp.concatenate([w_ukv[:, :, :QK_NOPE], zeros], axis=2)
    w_v = w_ukv[:, :, QK_NOPE:]
    even = (jnp.arange(N_HEADS) % 2 == 0)[None, :, None]
    zv = jnp.zeros_like(w_v)
    w_v = jnp.concatenate([jnp.where(even, w_v, zv), jnp.where(even, zv, w_v)], axis=2)
    P["w_ukv"] = jnp.concatenate([w_k.reshape(KV_LORA, -1), w_v.reshape(KV_LORA, -1)], axis=1).astype(BF16)

    P["w_sp"] = f(w["w_spatial"]).reshape(GMLP_GROUPS // 2, 2 * CHUNK, CHUNK).astype(BF16)
    P["b_sp"] = jnp.repeat(f(w["b_spatial"]).T, GMLP_GROUP_DIM, axis=1)

    P["w_out"] = f(w["w_out"]).astype(BF16)

    pad = jnp.zeros((D_MODEL, LANES - N_EXPERTS - N_EXPERT_GROUPS), F32)
    wr = jnp.concatenate([f(w["w_router_expert"]), f(w["w_router_group"]), pad], axis=1)
    wr_hi = wr.astype(BF16)
    wr_lo = (wr - wr_hi.astype(F32)).astype(BF16)
    P["w_router"] = jnp.concatenate([wr_hi, wr_lo], axis=1)
    P["b_router"] = jnp.concatenate(
        [f(w["b_router_expert"]), f(w["b_router_group"]), jnp.zeros((LANES - N_EXPERTS - N_EXPERT_GROUPS,), F32)]
    ).reshape(1, LANES)
    P["tri"] = jnp.triu(jnp.ones((RANK_CHUNK, RANK_CHUNK), F32), k=1).astype(BF16)

    P["w_gate"], P["w_up"], P["w_down"] = f(w["w_gate"]), f(w["w_up"]), f(w["w_down"])
    return P


def _rope_tables(seq):
    inv = ROPE_THETA ** (-jnp.arange(ROPE_HALF, dtype=F32) / ROPE_HALF)
    ang = jnp.arange(seq, dtype=F32)[:, None] * inv[None, :]
    z_lo = jnp.zeros((seq, _ROPE_LO), F32)
    z_hi = jnp.zeros((seq, LANES - _ROPE_LO - QK_ROPE), F32)
    cos = jnp.concatenate([z_lo, jnp.cos(ang), jnp.cos(ang), z_hi], axis=1)
    sin = jnp.concatenate([z_lo, jnp.sin(ang), jnp.sin(ang), z_hi], axis=1)
    return cos, sin


def _layer(x, mod, P):
    nb, seq, _ = x.shape
    n_tok = nb * seq
    q, kt, v, sn = _premix(x, mod, P)
    a = _attention(q, kt, v)
    x1, h2rows, eid, rank, wcol, counts = _postmix(a, sn, x, mod, P)

    tile = _expert_tile(n_tok)
    cnt = counts[:, 0].astype(jnp.int32)
    padded = ((cnt + tile - 1) // tile) * tile
    ends = jnp.cumsum(padded)
    starts = ends - padded
    eflat = jnp.transpose(eid, (1, 0, 2)).reshape(2, n_tok)
    rflat = jnp.transpose(rank, (1, 0, 2)).reshape(2, n_tok)
    onehot = eflat[:, :, None] == jnp.arange(N_EXPERTS, dtype=jnp.int32)[None, None, :]
    pos = rflat + jnp.sum(jnp.where(onehot, starts[None, None, :], 0), axis=2)
    n_rows = 2 * n_tok + N_EXPERTS * tile
    n_tiles = n_rows // tile
    tile_start = jnp.arange(n_tiles, dtype=jnp.int32) * tile
    tile_expert = jnp.minimum(
        jnp.sum((tile_start[:, None] >= ends[None, :]).astype(jnp.int32), axis=1), N_EXPERTS - 1).astype(jnp.int32)
    expert_ids = jnp.arange(N_EXPERTS, dtype=jnp.int32)
    live_end = jnp.sum(jnp.where(tile_expert[:, None] == expert_ids[None, :], (starts + cnt)[None, :], 0), axis=1)
    tile_rows = jnp.clip(live_end - tile_start, 0, tile).astype(jnp.int32)
    n_valid = (ends[-1:] // tile).astype(jnp.int32)

    as_tiles = lambda a: a.reshape(-1, ROW_TILES, LANES)
    as_rows = lambda a: a.reshape(-1, LANES)
    xs = _sc_scatter_rows(as_tiles(h2rows), pos, n_rows)
    ys = _experts(as_rows(xs), tile_expert, tile_rows, n_valid, tile, P)
    yg = as_rows(_sc_gather_rows(as_tiles(ys), pos.reshape(2 * n_tok)))
    return _final(yg, wcol, x1, mod, P)


def kernel(x_prompt, x_sample, c_prompt, c_sample, w_ada, b_ada, g_pre1, g_post1, g_pre2, g_post2, w_in, g_q, w_uq,
           g_kv, w_ukv, g_v_gmlp, w_spatial, b_spatial, g_attn_out, g_gmlp_out, w_out, w_router_group,
           b_router_group, w_router_expert, b_router_expert, w_gate, w_up, w_down):
    P = _prepare(dict(
        g_pre1=g_pre1, g_post1=g_post1, g_pre2=g_pre2, g_post2=g_post2, w_in=w_in, g_q=g_q, w_uq=w_uq, g_kv=g_kv,
        w_ukv=w_ukv, g_v_gmlp=g_v_gmlp, w_spatial=w_spatial, b_spatial=b_spatial, g_attn_out=g_attn_out,
        g_gmlp_out=g_gmlp_out, w_out=w_out, w_router_group=w_router_group, b_router_group=b_router_group,
        w_router_expert=w_router_expert, b_router_expert=b_router_expert, w_gate=w_gate, w_up=w_up, w_down=w_down))
    P["rope"] = {seq: _rope_tables(seq) for seq in {x_prompt.shape[1], x_sample.shape[1]}}

    nbp = c_prompt.shape[0]
    c_all = jnp.concatenate([c_prompt, c_sample], axis=0).astype(F32)
    mod = _ada(c_all, w_ada.astype(F32), b_ada.astype(F32))
    mod = mod.reshape(c_all.shape[0], 6, D_MODEL)
    mod = jnp.concatenate([mod, jnp.zeros((c_all.shape[0], SUBLANES - 6, D_MODEL), F32)], axis=1)

    y_prompt = _layer(x_prompt, mod[:nbp], P)
    y_sample = _layer(x_sample, mod[nbp:], P)
    return (y_prompt, y_sample)
```

```python
import functools
import math

import jax
import jax.numpy as jnp
from jax import lax
from jax.experimental import pallas as pl
from jax.experimental.pallas import tpu as pltpu
from jax.experimental.pallas import tpu_sc as plsc

F32 = jnp.float32
BF16 = jnp.bfloat16

D_MODEL = 1024
N_HEADS = 8
QK_NOPE = 64
QK_ROPE = 32
ROPE_HALF = QK_ROPE // 2
V_HEAD = 64
Q_LORA = 256
KV_LORA = 128
GMLP_WIDTH = 512
GMLP_GROUPS = 8
GMLP_GROUP_DIM = 64
CHUNK = 128
N_EXPERTS = 32
N_EXPERT_GROUPS = 4
EXPERTS_PER_GROUP = 8
EXPERT_FF = 256
ROPE_THETA = 10000.0
EPS = 1e-6

LANES = 128
SUBLANES = 8
HEAD_PAD = LANES

TOKEN_TILE = 1024
POST_TILE = 1024
Q_TILE = 1024
K_TILE = 512
EXPERT_TILE_SMALL = 512
EXPERT_TILE_LARGE = 1024
SC_CORES = 2
SC_WORKERS = 32
SC_ROWS = 128
RANK_CHUNK = 256
VMEM_LIMIT = 56 * 1024 * 1024

U32 = jnp.uint32
PACKED_WIDTH = D_MODEL // 2
ROW_TILES = PACKED_WIDTH // LANES
_HI_MASK = 0xFFFF0000

_SQRT_2_OVER_PI = math.sqrt(2.0 / math.pi)


def _rms(x):
    return x * lax.rsqrt(jnp.mean(x * x, axis=-1, keepdims=True) + EPS)


def _gelu_tanh(x):
    return 0.5 * x * (1.0 + jnp.tanh(_SQRT_2_OVER_PI * (x + 0.044715 * (x * x * x))))


def _split_bf16(x):
    hi = x.astype(BF16)
    lo = (x - hi.astype(F32)).astype(BF16)
    return hi, lo


def _dot(a, b):
    return jnp.dot(a, b, preferred_element_type=F32)


def _bf16_bits(x):
    return lax.bitcast_convert_type(x.astype(BF16).astype(F32), U32)


def _load_token_rows(ref, n):
    w = jnp.concatenate([ref[pl.ds(c, n, stride=ROW_TILES), :] for c in range(ROW_TILES)], axis=1)
    lo = lax.bitcast_convert_type(w << 16, F32)
    hi = lax.bitcast_convert_type(w & jnp.uint32(_HI_MASK), F32)
    return jnp.concatenate([lo, hi], axis=1)


def _store_token_rows(ref, val):
    n = val.shape[0]
    w = (_bf16_bits(val[:, :PACKED_WIDTH]) >> 16) | (_bf16_bits(val[:, PACKED_WIDTH:]) & jnp.uint32(_HI_MASK))
    for c in range(ROW_TILES):
        ref[pl.ds(c, n, stride=ROW_TILES), :] = w[:, c * LANES:(c + 1) * LANES]


def _params(sem, vmem=VMEM_LIMIT):
    return pltpu.CompilerParams(dimension_semantics=sem, vmem_limit_bytes=vmem)


def _ada_kernel(c_ref, w_ref, b_ref, o_ref):
    c = c_ref[...]
    a = c * jax.nn.sigmoid(c)
    a_hi, a_lo = _split_bf16(a)
    w_hi, w_lo = _split_bf16(w_ref[...])
    o_ref[...] = _dot(a_hi, w_hi) + _dot(a_hi, w_lo) + _dot(a_lo, w_hi) + b_ref[...]


def _ada(c, w_ada, b_ada):
    nb = c.shape[0]
    n_out = w_ada.shape[1]
    blk = D_MODEL
    return pl.pallas_call(
        _ada_kernel,
        grid=(n_out // blk,),
        in_specs=[
            pl.BlockSpec((nb, D_MODEL), lambda j: (0, 0)),
            pl.BlockSpec((D_MODEL, blk), lambda j: (0, j)),
            pl.BlockSpec((1, blk), lambda j: (0, j)),
        ],
        out_specs=pl.BlockSpec((nb, blk), lambda j: (0, j)),
        out_shape=jax.ShapeDtypeStruct((nb, n_out), F32),
        compiler_params=_params(("arbitrary",)),
        name="ada",
    )(c, w_ada, b_ada.reshape(1, n_out))


_C_CQ = 0
_C_CKV = _C_CQ + Q_LORA
_C_KR = _C_CKV + KV_LORA
_C_U = _C_KR + LANES
_C_V = _C_U + GMLP_WIDTH
_C_END = _C_V + GMLP_WIDTH
_ROPE_LO = QK_NOPE
_ROLL_PARTNER = LANES - QK_ROPE


def _premix_kernel(x_ref, mod_ref, cos_ref, sin_ref, gpre_ref, win_ref, gq_ref, wuq_ref, gkv_ref, wukv_ref,
                   gv_ref, wsp_ref, bsp_ref, ggo_ref, q_ref, kt_ref, v_ref, sn_ref):
    x = x_ref[0]
    mod = mod_ref[0]
    shift1, scale1 = mod[0:1], mod[1:2]
    h = _rms(x) * (gpre_ref[...] * (1.0 + scale1)) + shift1
    z = _dot(h.astype(BF16), win_ref[...])

    cosb = cos_ref[...]
    sinb = sin_ref[...]
    lane = lax.broadcasted_iota(jnp.int32, (1, LANES), 1)
    nope_mask = jnp.where(lane < QK_NOPE, 1.0, 0.0).astype(F32)

    qscale = (QK_NOPE + QK_ROPE) ** -0.5 * math.log2(math.e)
    cq_tab = (nope_mask + cosb) * qscale
    sq_tab = sinb * qscale
    cqn = (_rms(z[:, _C_CQ:_C_CKV]) * gq_ref[...]).astype(BF16)
    qb = _dot(cqn, wuq_ref[...])
    for hd in range(N_HEADS):
        blk = qb[:, hd * HEAD_PAD:(hd + 1) * HEAD_PAD]
        qh = blk * cq_tab + pltpu.roll(blk, _ROLL_PARTNER, 1) * sq_tab
        q_ref[0, hd] = qh.astype(BF16)

    ckvn = (_rms(z[:, _C_CKV:_C_KR]) * gkv_ref[...]).astype(BF16)
    kvb = _dot(ckvn, wukv_ref[...])
    krb = z[:, _C_KR:_C_U]
    krope = krb * cosb + pltpu.roll(krb, _ROLL_PARTNER, 1) * sinb
    v_off = N_HEADS * HEAD_PAD
    for hd in range(N_HEADS):
        kh = kvb[:, hd * HEAD_PAD:(hd + 1) * HEAD_PAD] + krope
        for c in range(kt_ref.shape[2]):
            kt_ref[0, hd, c] = kh[c * K_TILE:(c + 1) * K_TILE].T.astype(BF16)
        ones_lane = V_HEAD if hd % 2 == 0 else 0
        vh = kvb[:, v_off + hd * HEAD_PAD:v_off + (hd + 1) * HEAD_PAD] + jnp.where(lane == ones_lane, 1.0, 0.0)
        v_ref[0, hd] = vh.astype(BF16)

    ua = _gelu_tanh(z[:, _C_U:_C_V])
    vn = (_rms(_gelu_tanh(z[:, _C_V:_C_END])) * gv_ref[...]).astype(BF16)
    n_tok = x.shape[0]
    bsp = bsp_ref[...]
    rows = []
    for n in range(n_tok // CHUNK):
        cols = []
        for j in range(GMLP_GROUPS // 2):
            rhs = vn[n * CHUNK:(n + 1) * CHUNK, j * LANES:(j + 1) * LANES]
            ab = _dot(wsp_ref[j], rhs)
            cols.append(jnp.where(lane < GMLP_GROUP_DIM, ab[:CHUNK], ab[CHUNK:]))
        rows.append(jnp.concatenate(cols, axis=1) + bsp)
    s = ua * jnp.concatenate(rows, axis=0)
    sn_ref[0] = (_rms(s) * ggo_ref[...]).astype(BF16)


def _premix(x, b0, mod, P):
    nb, seq = mod.shape[0], x.shape[1]
    tm = TOKEN_TILE
    nck = seq // K_TILE
    const = lambda i, b: (0, 0)
    return pl.pallas_call(
        _premix_kernel,
        grid=(seq // tm, nb),
        in_specs=[
            pl.BlockSpec((1, tm, D_MODEL), lambda i, b: (b0 + b, i, 0)),
            pl.BlockSpec((1, SUBLANES, D_MODEL), lambda i, b: (b, 0, 0)),
            pl.BlockSpec((tm, LANES), lambda i, b: (i, 0)),
            pl.BlockSpec((tm, LANES), lambda i, b: (i, 0)),
            pl.BlockSpec((1, D_MODEL), const),
            pl.BlockSpec((D_MODEL, _C_END), const),
            pl.BlockSpec((1, Q_LORA), const),
            pl.BlockSpec((Q_LORA, N_HEADS * HEAD_PAD), const),
            pl.BlockSpec((1, KV_LORA), const),
            pl.BlockSpec((KV_LORA, 2 * N_HEADS * HEAD_PAD), const),
            pl.BlockSpec((1, GMLP_WIDTH), const),
            pl.BlockSpec((GMLP_GROUPS // 2, 2 * CHUNK, CHUNK), lambda i, b: (0, 0, 0)),
            pl.BlockSpec((CHUNK, GMLP_WIDTH), const),
            pl.BlockSpec((1, GMLP_WIDTH), const),
        ],
        out_specs=[
            pl.BlockSpec((1, N_HEADS, tm, HEAD_PAD), lambda i, b: (b, 0, i, 0)),
            pl.BlockSpec((1, N_HEADS, tm // K_TILE, HEAD_PAD, K_TILE), lambda i, b: (b, 0, i, 0, 0)),
            pl.BlockSpec((1, N_HEADS, tm, HEAD_PAD), lambda i, b: (b, 0, i, 0)),
            pl.BlockSpec((1, tm, GMLP_WIDTH), lambda i, b: (b, i, 0)),
        ],
        out_shape=[
            jax.ShapeDtypeStruct((nb, N_HEADS, seq, HEAD_PAD), BF16),
            jax.ShapeDtypeStruct((nb, N_HEADS, nck, HEAD_PAD, K_TILE), BF16),
            jax.ShapeDtypeStruct((nb, N_HEADS, seq, HEAD_PAD), BF16),
            jax.ShapeDtypeStruct((nb, seq, GMLP_WIDTH), BF16),
        ],
        compiler_params=_params(("arbitrary", "arbitrary")),
        name="premix",
    )(x, mod, *P["rope"][seq], P["g_pre1"], P["w_in"], P["g_q"], P["w_uq"], P["g_kv"], P["w_ukv"],
      P["g_v"], P["w_sp"], P["b_sp"], P["g_gmlp_out"])


def _attn_kernel(q_ref, kt_ref, v_ref, o_ref):
    n_chunks = kt_ref.shape[2]
    tk = kt_ref.shape[4]
    lane = lax.broadcasted_iota(jnp.int32, (1, LANES), 1)

    def one_head(hd, ones_lane):
        q = q_ref[0, hd]
        m = None
        acc = None
        for c in range(n_chunks):
            s = _dot(q, kt_ref[0, hd, c])
            v = v_ref[0, hd, c * tk:(c + 1) * tk, :]
            smax = jnp.max(s, axis=1, keepdims=True)
            if c == 0:
                m = smax
                acc = _dot(jnp.exp2((s - m).astype(BF16)), v)
            else:
                m_new = jnp.maximum(m, smax)
                acc = acc * jnp.exp2(m - m_new) + _dot(jnp.exp2((s - m_new).astype(BF16)), v)
                m = m_new
        row_sum = acc[:, ones_lane:ones_lane + 1]
        return acc * (1.0 / row_sum)

    def pair(j, carry):
        even = one_head(2 * j, V_HEAD)
        odd = one_head(2 * j + 1, 0)
        o_ref[0, j] = jnp.where(lane < V_HEAD, even, odd).astype(BF16)
        return carry

    lax.fori_loop(0, N_HEADS // 2, pair, 0)


def _attention(q, kt, v):
    nb, _, seq, _ = q.shape
    nck = kt.shape[2]
    return pl.pallas_call(
        _attn_kernel,
        grid=(nb, seq // Q_TILE),
        in_specs=[
            pl.BlockSpec((1, N_HEADS, Q_TILE, HEAD_PAD), lambda b, i: (b, 0, i, 0)),
            pl.BlockSpec((1, N_HEADS, nck, HEAD_PAD, K_TILE), lambda b, i: (b, 0, 0, 0, 0)),
            pl.BlockSpec((1, N_HEADS, seq, HEAD_PAD), lambda b, i: (b, 0, 0, 0)),
        ],
        out_specs=pl.BlockSpec((1, N_HEADS // 2, Q_TILE, LANES), lambda b, i: (b, 0, i, 0)),
        out_shape=jax.ShapeDtypeStruct((nb, N_HEADS // 2, seq, LANES), BF16),
        compiler_params=_params(("arbitrary", "arbitrary")),
        name="attn",
    )(q, kt, v)


_R_GROUP_ROW = N_EXPERTS


def _postmix_kernel(a_ref, sn_ref, x_ref, mod_ref, gao_ref, wout_ref, gpost_ref, gpre_ref, wr_ref, br_ref, tri_ref,
                    x1_ref, h2_ref, eid_ref, rank_ref, wcol_ref, cnt_ref, run_ref):
    first = jnp.logical_and(pl.program_id(0) == 0, pl.program_id(1) == 0)

    @pl.when(first)
    def _():
        run_ref[...] = jnp.zeros_like(run_ref)

    mod = mod_ref[0]
    gate1, shift2, scale2 = mod[2:3], mod[3:4], mod[4:5]
    a = jnp.concatenate([a_ref[0, j] for j in range(N_HEADS // 2)], axis=1).astype(F32)
    an = (_rms(a) * gao_ref[...]).astype(BF16)
    merged = jnp.concatenate([an, sn_ref[0]], axis=1)
    o = _dot(merged, wout_ref[...])
    x1 = x_ref[0] + gate1 * (_rms(o) * gpost_ref[...])
    x1_ref[0] = x1
    h2 = _rms(x1) * (gpre_ref[...] * (1.0 + scale2)) + shift2
    n_tok = h2.shape[0]
    _store_token_rows(h2_ref, h2)

    h_hi, h_lo = _split_bf16(h2)
    wr = wr_ref[...]
    hh = _dot(h_hi, wr)
    lh = _dot(h_lo, wr[:, :LANES])
    logits = hh[:, :LANES] + hh[:, LANES:] + lh + br_ref[...]
    lt = logits.T

    neg = jnp.float32(-jnp.inf)
    row8 = lax.broadcasted_iota(jnp.int32, (SUBLANES, n_tok), 0).astype(F32)
    lg = jnp.where(row8 < N_EXPERT_GROUPS, lt[_R_GROUP_ROW:_R_GROUP_ROW + SUBLANES], neg)
    gmax = jnp.max(lg, axis=0, keepdims=True)
    gi = jnp.min(jnp.where(lg == gmax, row8, float(SUBLANES)), axis=0, keepdims=True)
    pg_sel = 1.0 / jnp.sum(jnp.exp(lg - gmax), axis=0, keepdims=True)

    le = jnp.zeros((EXPERTS_PER_GROUP, n_tok), F32)
    for g in range(N_EXPERT_GROUPS):
        le = jnp.where(gi == float(g), lt[g * EXPERTS_PER_GROUP:(g + 1) * EXPERTS_PER_GROUP], le)
    v1 = jnp.max(le, axis=0, keepdims=True)
    i1 = jnp.min(jnp.where(le == v1, row8, float(SUBLANES)), axis=0, keepdims=True)
    le2 = jnp.where(row8 == i1, neg, le)
    v2 = jnp.max(le2, axis=0, keepdims=True)
    i2 = jnp.min(jnp.where(le2 == v2, row8, float(SUBLANES)), axis=0, keepdims=True)
    r = jnp.exp(v2 - v1)
    w1 = pg_sel / (1.0 + r)
    w2 = w1 * r
    e1 = gi * float(EXPERTS_PER_GROUP) + i1
    e2 = gi * float(EXPERTS_PER_GROUP) + i2
    eid_ref[0] = jnp.concatenate([e1, e2], axis=0).astype(jnp.int32)

    row32 = lax.broadcasted_iota(jnp.int32, (N_EXPERTS, n_tok), 0).astype(F32)
    hit1 = row32 == e1
    hit2 = row32 == e2
    onehot = jnp.where(jnp.logical_or(hit1, hit2), 1.0, 0.0)
    run = run_ref[...][:, 0:1]
    ranks1, ranks2 = [], []
    for c in range(n_tok // RANK_CHUNK):
        sl = slice(c * RANK_CHUNK, (c + 1) * RANK_CHUNK)
        oh = onehot[:, sl]
        before = _dot(oh.astype(BF16), tri_ref[...]) + run
        ranks1.append(jnp.sum(jnp.where(hit1[:, sl], before, 0.0), axis=0, keepdims=True))
        ranks2.append(jnp.sum(jnp.where(hit2[:, sl], before, 0.0), axis=0, keepdims=True))
        run = run + jnp.sum(oh, axis=1, keepdims=True)
    rank_ref[0] = jnp.concatenate(
        [jnp.concatenate(ranks1, axis=1), jnp.concatenate(ranks2, axis=1)], axis=0).astype(jnp.int32)
    run_b = jnp.broadcast_to(run, run_ref.shape)
    run_ref[...] = run_b
    cnt_ref[...] = run_b

    row128 = lax.broadcasted_iota(jnp.int32, (LANES, n_tok), 0)
    wt = jnp.where(row128 == 0, w1, jnp.where(row128 == 1, w2, 0.0))
    wcol_ref[0] = wt.T


def _postmix(a, sn, x, b0, mod, P):
    nb, seq = mod.shape[0], x.shape[1]
    tm = POST_TILE
    const = lambda b, i: (0, 0)
    return pl.pallas_call(
        _postmix_kernel,
        grid=(nb, seq // tm),
        in_specs=[
            pl.BlockSpec((1, N_HEADS // 2, tm, LANES), lambda b, i: (b, 0, i, 0)),
            pl.BlockSpec((1, tm, GMLP_WIDTH), lambda b, i: (b, i, 0)),
            pl.BlockSpec((1, tm, D_MODEL), lambda b, i: (b0 + b, i, 0)),
            pl.BlockSpec((1, SUBLANES, D_MODEL), lambda b, i: (b, 0, 0)),
            pl.BlockSpec((1, N_HEADS * V_HEAD), const),
            pl.BlockSpec((D_MODEL, D_MODEL), const),
            pl.BlockSpec((1, D_MODEL), const),
            pl.BlockSpec((1, D_MODEL), const),
            pl.BlockSpec((D_MODEL, 2 * LANES), const),
            pl.BlockSpec((1, LANES), const),
            pl.BlockSpec((RANK_CHUNK, RANK_CHUNK), const),
        ],
        out_specs=[
            pl.BlockSpec((1, tm, D_MODEL), lambda b, i: (b, i, 0)),
            pl.BlockSpec((tm * ROW_TILES, LANES), lambda b, i: (b * (seq // tm) + i, 0)),
            pl.BlockSpec((1, 2, tm), lambda b, i: (b, 0, i)),
            pl.BlockSpec((1, 2, tm), lambda b, i: (b, 0, i)),
            pl.BlockSpec((1, tm, LANES), lambda b, i: (b, i, 0)),
            pl.BlockSpec((N_EXPERTS, LANES), const),
        ],
        out_shape=[
            jax.ShapeDtypeStruct((nb, seq, D_MODEL), F32),
            jax.ShapeDtypeStruct((nb * seq * ROW_TILES, LANES), U32),
            jax.ShapeDtypeStruct((nb, 2, seq), jnp.int32),
            jax.ShapeDtypeStruct((nb, 2, seq), jnp.int32),
            jax.ShapeDtypeStruct((nb, seq, LANES), F32),
            jax.ShapeDtypeStruct((N_EXPERTS, LANES), F32),
        ],
        scratch_shapes=[pltpu.VMEM((N_EXPERTS, LANES), F32)],
        compiler_params=_params(("arbitrary", "arbitrary")),
        name="postmix",
    )(a, sn, x, mod, P["g_attn_out"], P["w_out"], P["g_post1"], P["g_pre2"], P["w_router"], P["b_router"], P["tri"])


def _sc_mesh():
    return plsc.VectorSubcoreMesh(core_axis_name="c", subcore_axis_name="s")


def _sc_worker():
    return lax.axis_index("s") * SC_CORES + lax.axis_index("c")


def _sc_gather_rows(table, idx):
    n = idx.shape[0]
    assert n % (SC_WORKERS * SC_ROWS) == 0
    per_w = n // SC_WORKERS
    steps = per_w // SC_ROWS
    idx3 = idx.reshape(SC_WORKERS, steps, SC_ROWS)

    @functools.partial(
        pl.kernel, mesh=_sc_mesh(),
        out_type=jax.ShapeDtypeStruct((n, ROW_TILES, LANES), U32),
        scratch_types=[pltpu.VMEM((steps, SC_ROWS), jnp.int32), pltpu.VMEM((SC_ROWS, ROW_TILES, LANES), U32),
                       pltpu.SemaphoreType.DMA],
        name="sc_gather_rows",
    )
    def gather(table_hbm, idx_hbm, out_hbm, idx_v, rows_v, sem):
        wid = _sc_worker()
        pltpu.sync_copy(idx_hbm.at[wid], idx_v)

        @pl.loop(0, steps)
        def _(j):
            pltpu.async_copy(table_hbm.at[idx_v.at[j]], rows_v, sem).wait()
            pltpu.sync_copy(rows_v, out_hbm.at[pl.ds(wid * per_w + j * SC_ROWS, SC_ROWS)])

    return gather(table, idx3)


def _sc_scatter_rows(src, idx, n_dst):
    n_dup, n = idx.shape
    assert n % (SC_WORKERS * SC_ROWS) == 0
    per_w = n // SC_WORKERS
    steps = per_w // SC_ROWS
    idx4 = idx.reshape(n_dup, SC_WORKERS, steps, SC_ROWS).transpose(1, 0, 2, 3)

    @functools.partial(
        pl.kernel, mesh=_sc_mesh(),
        out_type=jax.ShapeDtypeStruct((n_dst, ROW_TILES, LANES), U32),
        scratch_types=[pltpu.VMEM((n_dup, steps, SC_ROWS), jnp.int32), pltpu.VMEM((SC_ROWS, ROW_TILES, LANES), U32),
                       pltpu.SemaphoreType.DMA],
        name="sc_scatter_rows",
    )
    def scatter(src_hbm, idx_hbm, dst_hbm, idx_v, rows_v, sem):
        wid = _sc_worker()
        pltpu.sync_copy(idx_hbm.at[wid], idx_v)

        @pl.loop(0, steps)
        def _(j):
            pltpu.sync_copy(src_hbm.at[pl.ds(wid * per_w + j * SC_ROWS, SC_ROWS)], rows_v)
            copies = [pltpu.async_copy(rows_v, dst_hbm.at[idx_v.at[k, j]], sem) for k in range(n_dup)]
            for copy in copies:
                copy.wait()

    return scatter(src, idx4)


def _expert_kernel(te_ref, tr_ref, nv_ref, xs_ref, wg_ref, wu_ref, wd_ref, ys_ref, wgu_bf, wd_bf):
    i = pl.program_id(0)
    valid = i < nv_ref[0]
    new_expert = jnp.logical_or(i == 0, te_ref[i] != te_ref[jnp.maximum(i - 1, 0)])

    @pl.when(jnp.logical_and(valid, new_expert))
    def _():
        wgu_bf[:, :EXPERT_FF] = wg_ref[0].astype(BF16)
        wgu_bf[:, EXPERT_FF:] = wu_ref[0].astype(BF16)
        wd_bf[...] = wd_ref[0].astype(BF16)

    @pl.when(valid)
    def _():
        n_rows = xs_ref.shape[0] // ROW_TILES
        x = _load_token_rows(xs_ref, n_rows)
        row = lax.broadcasted_iota(jnp.int32, (n_rows, 1), 0)
        x = jnp.where(row < tr_ref[i], x, 0.0).astype(BF16)
        gu = _dot(x, wgu_bf[...])
        g, u = gu[:, :EXPERT_FF], gu[:, EXPERT_FF:]
        act = (g * jax.nn.sigmoid(g) * u).astype(BF16)
        _store_token_rows(ys_ref, _dot(act, wd_bf[...]))

    @pl.when(jnp.logical_not(valid))
    def _():
        ys_ref[...] = jnp.zeros_like(ys_ref)


def _expert_tile(n_tok):
    return EXPERT_TILE_LARGE if 2 * n_tok // N_EXPERTS >= 4 * EXPERT_TILE_LARGE else EXPERT_TILE_SMALL


def _experts(xs, tile_expert, tile_rows, n_valid, tile, P):
    rows_blk = tile * ROW_TILES
    assert xs.shape[0] % rows_blk == 0
    n_tiles = xs.shape[0] // rows_blk

    def row_map(i, te, tr, nv):
        return (jnp.minimum(i, nv[0] - 1), 0)

    def out_map(i, te, tr, nv):
        return (i, 0)

    def w_map(i, te, tr, nv):
        return (te[jnp.minimum(i, nv[0] - 1)], 0, 0)

    return pl.pallas_call(
        _expert_kernel,
        grid_spec=pltpu.PrefetchScalarGridSpec(
            num_scalar_prefetch=3,
            grid=(n_tiles,),
            in_specs=[
                pl.BlockSpec((rows_blk, LANES), row_map),
                pl.BlockSpec((1, D_MODEL, EXPERT_FF), w_map),
                pl.BlockSpec((1, D_MODEL, EXPERT_FF), w_map),
                pl.BlockSpec((1, EXPERT_FF, D_MODEL), w_map),
            ],
            out_specs=pl.BlockSpec((rows_blk, LANES), out_map),
            scratch_shapes=[pltpu.VMEM((D_MODEL, 2 * EXPERT_FF), BF16), pltpu.VMEM((EXPERT_FF, D_MODEL), BF16)],
        ),
        out_shape=jax.ShapeDtypeStruct(xs.shape, U32),
        compiler_params=_params(("arbitrary",)),
        name="experts",
    )(tile_expert, tile_rows, n_valid, xs, P["w_gate"], P["w_up"], P["w_down"])


def _final_kernel(y0_ref, y1_ref, wcol_ref, x1_ref, mod_ref, gpost_ref, *rest):
    o_ref = rest[-1]
    w = wcol_ref[0]
    w0, w1 = w[:, 0:1], w[:, 1:2]
    n_tok = w.shape[0]
    m = w0 * _load_token_rows(y0_ref, n_tok) + w1 * _load_token_rows(y1_ref, n_tok)
    gate2 = mod_ref[0][5:6]
    o_ref[0] = x1_ref[0] + gate2 * (_rms(m) * gpost_ref[...])


def _final(yg, wcol, x1, mod, P, nb_out, b0, y_prev):
    nb, seq, _ = x1.shape
    tm = POST_TILE
    nt = seq // tm
    n_tok_tiles = nb * nt
    in_specs = [
        pl.BlockSpec((tm * ROW_TILES, LANES), lambda b, i: (b * nt + i, 0)),
        pl.BlockSpec((tm * ROW_TILES, LANES), lambda b, i: (n_tok_tiles + b * nt + i, 0)),
        pl.BlockSpec((1, tm, LANES), lambda b, i: (b, i, 0)),
        pl.BlockSpec((1, tm, D_MODEL), lambda b, i: (b, i, 0)),
        pl.BlockSpec((1, SUBLANES, D_MODEL), lambda b, i: (b, 0, 0)),
        pl.BlockSpec((1, D_MODEL), lambda b, i: (0, 0)),
    ]
    args = [yg, yg, wcol, x1, mod, P["g_post2"]]
    aliases = {}
    if y_prev is not None:
        in_specs.append(pl.BlockSpec(memory_space=pl.ANY))
        args.append(y_prev)
        aliases = {len(args) - 1: 0}
    return pl.pallas_call(
        _final_kernel,
        grid=(nb, nt),
        in_specs=in_specs,
        out_specs=pl.BlockSpec((1, tm, D_MODEL), lambda b, i: (b0 + b, i, 0)),
        out_shape=jax.ShapeDtypeStruct((nb_out, seq, D_MODEL), F32),
        input_output_aliases=aliases,
        compiler_params=_params(("arbitrary", "arbitrary")),
        name="final",
    )(*args)


def _prepare(w):
    f = lambda a: a.astype(F32)
    P = {}
    for k in ("g_pre1", "g_post1", "g_pre2", "g_post2", "g_q", "g_kv", "g_attn_out", "g_gmlp_out"):
        P[k] = f(w[k]).reshape(1, -1)
    P["g_v"] = f(w["g_v_gmlp"]).reshape(1, -1)

    w_in = f(w["w_in"])
    o0, o1, o2, o3 = Q_LORA, Q_LORA + KV_LORA, Q_LORA + KV_LORA + QK_ROPE, Q_LORA + KV_LORA + QK_ROPE + GMLP_WIDTH
    w_kr = w_in[:, o1:o2]
    kr_partner = jnp.concatenate([-w_kr[:, ROPE_HALF:], w_kr[:, :ROPE_HALF]], axis=1)
    rope_blk = jnp.concatenate([jnp.zeros((D_MODEL, QK_NOPE), F32), w_kr, kr_partner], axis=1)
    P["w_in"] = jnp.concatenate([w_in[:, :o1], rope_blk, w_in[:, o2:o3], w_in[:, o3:]], axis=1).astype(BF16)

    w_uq = f(w["w_uq"]).reshape(Q_LORA, N_HEADS, QK_NOPE + QK_ROPE)
    q_rope = w_uq[:, :, QK_NOPE:]
    q_partner = jnp.concatenate([-q_rope[:, :, ROPE_HALF:], q_rope[:, :, :ROPE_HALF]], axis=2)
    P["w_uq"] = jnp.concatenate([w_uq, q_partner], axis=2).reshape(Q_LORA, N_HEADS * HEAD_PAD).astype(BF16)

    w_ukv = f(w["w_ukv"]).reshape(KV_LORA, N_HEADS, QK_NOPE + V_HEAD)
    zeros = jnp.zeros((KV_LORA, N_HEADS, HEAD_PAD - QK_NOPE), F32)
    w_k = jnp.concatenate([w_ukv[:, :, :QK_NOPE], zeros], axis=2)
    w_v = w_ukv[:, :, QK_NOPE:]
    even = (jnp.arange(N_HEADS) % 2 == 0)[None, :, None]
    zv = jnp.zeros_like(w_v)
    w_v = jnp.concatenate([jnp.where(even, w_v, zv), jnp.where(even, zv, w_v)], axis=2)
    P["w_ukv"] = jnp.concatenate([w_k.reshape(KV_LORA, -1), w_v.reshape(KV_LORA, -1)], axis=1).astype(BF16)

    P["w_sp"] = f(w["w_spatial"]).reshape(GMLP_GROUPS // 2, 2 * CHUNK, CHUNK).astype(BF16)
    P["b_sp"] = jnp.repeat(f(w["b_spatial"]).T, GMLP_GROUP_DIM, axis=1)

    P["w_out"] = f(w["w_out"]).astype(BF16)

    pad = jnp.zeros((D_MODEL, LANES - N_EXPERTS - N_EXPERT_GROUPS), F32)
    wr = jnp.concatenate([f(w["w_router_expert"]), f(w["w_router_group"]), pad], axis=1)
    wr_hi = wr.astype(BF16)
    wr_lo = (wr - wr_hi.astype(F32)).astype(BF16)
    P["w_router"] = jnp.concatenate([wr_hi, wr_lo], axis=1)
    P["b_router"] = jnp.concatenate(
        [f(w["b_router_expert"]), f(w["b_router_group"]), jnp.zeros((LANES - N_EXPERTS - N_EXPERT_GROUPS,), F32)]
    ).reshape(1, LANES)
    P["tri"] = jnp.triu(jnp.ones((RANK_CHUNK, RANK_CHUNK), F32), k=1).astype(BF16)

    P["w_gate"], P["w_up"], P["w_down"] = f(w["w_gate"]), f(w["w_up"]), f(w["w_down"])
    return P


def _rope_tables(seq):
    inv = ROPE_THETA ** (-jnp.arange(ROPE_HALF, dtype=F32) / ROPE_HALF)
    ang = jnp.arange(seq, dtype=F32)[:, None] * inv[None, :]
    z_lo = jnp.zeros((seq, _ROPE_LO), F32)
    z_hi = jnp.zeros((seq, LANES - _ROPE_LO - QK_ROPE), F32)
    cos = jnp.concatenate([z_lo, jnp.cos(ang), jnp.cos(ang), z_hi], axis=1)
    sin = jnp.concatenate([z_lo, jnp.sin(ang), jnp.sin(ang), z_hi], axis=1)
    return cos, sin


def _layer(x, b0, mod, P, y_prev=None):
    nb_out, seq, d_model = x.shape
    nb = mod.shape[0]
    assert d_model == D_MODEL and TOKEN_TILE % K_TILE == 0
    assert seq % TOKEN_TILE == 0 and seq % Q_TILE == 0 and seq % POST_TILE == 0 and POST_TILE % RANK_CHUNK == 0
    n_tok = nb * seq
    q, kt, v, sn = _premix(x, b0, mod, P)
    a = _attention(q, kt, v)
    x1, h2rows, eid, rank, wcol, counts = _postmix(a, sn, x, b0, mod, P)

    tile = _expert_tile(n_tok)
    cnt = counts[:, 0].astype(jnp.int32)
    padded = ((cnt + tile - 1) // tile) * tile
    ends = jnp.cumsum(padded)
    starts = ends - padded
    eflat = jnp.transpose(eid, (1, 0, 2)).reshape(2, n_tok)
    rflat = jnp.transpose(rank, (1, 0, 2)).reshape(2, n_tok)
    onehot = eflat[:, :, None] == jnp.arange(N_EXPERTS, dtype=jnp.int32)[None, None, :]
    pos = rflat + jnp.sum(jnp.where(onehot, starts[None, None, :], 0), axis=2)
    n_rows = 2 * n_tok + N_EXPERTS * tile
    n_tiles = n_rows // tile
    tile_start = jnp.arange(n_tiles, dtype=jnp.int32) * tile
    tile_expert = jnp.minimum(
        jnp.sum((tile_start[:, None] >= ends[None, :]).astype(jnp.int32), axis=1), N_EXPERTS - 1).astype(jnp.int32)
    expert_ids = jnp.arange(N_EXPERTS, dtype=jnp.int32)
    live_end = jnp.sum(jnp.where(tile_expert[:, None] == expert_ids[None, :], (starts + cnt)[None, :], 0), axis=1)
    tile_rows = jnp.clip(live_end - tile_start, 0, tile).astype(jnp.int32)
    n_valid = (ends[-1:] // tile).astype(jnp.int32)

    as_tiles = lambda a: a.reshape(-1, ROW_TILES, LANES)
    as_rows = lambda a: a.reshape(-1, LANES)
    xs = _sc_scatter_rows(as_tiles(h2rows), pos, n_rows)
    ys = _experts(as_rows(xs), tile_expert, tile_rows, n_valid, tile, P)
    yg = as_rows(_sc_gather_rows(as_tiles(ys), pos.reshape(2 * n_tok)))
    return _final(yg, wcol, x1, mod, P, nb_out, b0, y_prev)


def kernel(x_prompt, x_sample, c_prompt, c_sample, w_ada, b_ada, g_pre1, g_post1, g_pre2, g_post2, w_in, g_q, w_uq,
           g_kv, w_ukv, g_v_gmlp, w_spatial, b_spatial, g_attn_out, g_gmlp_out, w_out, w_router_group,
           b_router_group, w_router_expert, b_router_expert, w_gate, w_up, w_down):
    P = _prepare(dict(
        g_pre1=g_pre1, g_post1=g_post1, g_pre2=g_pre2, g_post2=g_post2, w_in=w_in, g_q=g_q, w_uq=w_uq, g_kv=g_kv,
        w_ukv=w_ukv, g_v_gmlp=g_v_gmlp, w_spatial=w_spatial, b_spatial=b_spatial, g_attn_out=g_attn_out,
        g_gmlp_out=g_gmlp_out, w_out=w_out, w_router_group=w_router_group, b_router_group=b_router_group,
        w_router_expert=w_router_expert, b_router_expert=b_router_expert, w_gate=w_gate, w_up=w_up, w_down=w_down))
    P["rope"] = {seq: _rope_tables(seq) for seq in {x_prompt.shape[1], x_sample.shape[1]}}

    nbp = c_prompt.shape[0]
    c_all = jnp.concatenate([c_prompt, c_sample], axis=0).astype(F32)
    mod = _ada(c_all, w_ada.astype(F32), b_ada.astype(F32))
    mod = mod.reshape(c_all.shape[0], 6, D_MODEL)
    mod = jnp.concatenate([mod, jnp.zeros((c_all.shape[0], SUBLANES - 6, D_MODEL), F32)], axis=1)

    y_prompt = _layer(x_prompt, 0, mod[:nbp], P)
    nbs = c_sample.shape[0]
    half = nbs // 2 if nbs % 2 == 0 and nbs >= 2 * nbp else nbs
    y_sample = None
    for b0 in range(0, nbs, half):
        y_sample = _layer(x_sample, b0, mod[nbp + b0:nbp + b0 + half], P, y_sample)
    return (y_prompt, y_sample)
```

```python
import functools
import math

import jax
import jax.numpy as jnp
from jax import lax
from jax.experimental import pallas as pl
from jax.experimental.pallas import tpu as pltpu
from jax.experimental.pallas import tpu_sc as plsc

F32 = jnp.float32
BF16 = jnp.bfloat16

D_MODEL = 1024
N_HEADS = 8
QK_NOPE = 64
QK_ROPE = 32
ROPE_HALF = QK_ROPE // 2
V_HEAD = 64
Q_LORA = 256
KV_LORA = 128
GMLP_WIDTH = 512
GMLP_GROUPS = 8
GMLP_GROUP_DIM = 64
CHUNK = 128
N_EXPERTS = 32
N_EXPERT_GROUPS = 4
EXPERTS_PER_GROUP = 8
EXPERT_FF = 256
ROPE_THETA = 10000.0
EPS = 1e-6

LANES = 128
SUBLANES = 8
HEAD_PAD = LANES

TOKEN_TILE = 1024
POST_TILE = 1024
Q_TILE = 1024
K_TILE = 512
EXPERT_TILE_SMALL = 512
EXPERT_TILE_LARGE = 1024
SC_CORES = 2
SC_WORKERS = 32
SC_ROWS = 128
RANK_CHUNK = 256
FINAL_INPUT_BUFFERS = 3
VMEM_LIMIT = 56 * 1024 * 1024

U32 = jnp.uint32
PACKED_WIDTH = D_MODEL // 2
ROW_TILES = PACKED_WIDTH // LANES
_HI_MASK = 0xFFFF0000

_SQRT_2_OVER_PI = math.sqrt(2.0 / math.pi)


def _rms(x):
    return x * lax.rsqrt(jnp.mean(x * x, axis=-1, keepdims=True) + EPS)


def _gelu_tanh(x):
    return 0.5 * x * (1.0 + jnp.tanh(_SQRT_2_OVER_PI * (x + 0.044715 * (x * x * x))))


def _split_bf16(x):
    hi = x.astype(BF16)
    lo = (x - hi.astype(F32)).astype(BF16)
    return hi, lo


def _dot(a, b):
    return jnp.dot(a, b, preferred_element_type=F32)


def _bf16_bits(x):
    return lax.bitcast_convert_type(x.astype(BF16).astype(F32), U32)


def _load_token_rows(ref, n):
    w = jnp.concatenate([ref[pl.ds(c, n, stride=ROW_TILES), :] for c in range(ROW_TILES)], axis=1)
    lo = lax.bitcast_convert_type(w << 16, F32)
    hi = lax.bitcast_convert_type(w & jnp.uint32(_HI_MASK), F32)
    return jnp.concatenate([lo, hi], axis=1)


def _store_token_rows(ref, val):
    n = val.shape[0]
    w = (_bf16_bits(val[:, :PACKED_WIDTH]) >> 16) | (_bf16_bits(val[:, PACKED_WIDTH:]) & jnp.uint32(_HI_MASK))
    for c in range(ROW_TILES):
        ref[pl.ds(c, n, stride=ROW_TILES), :] = w[:, c * LANES:(c + 1) * LANES]


def _params(sem, vmem=VMEM_LIMIT):
    return pltpu.CompilerParams(dimension_semantics=sem, vmem_limit_bytes=vmem)


def _ada_kernel(c_ref, w_ref, b_ref, o_ref):
    c = c_ref[...]
    a = c * jax.nn.sigmoid(c)
    a_hi, a_lo = _split_bf16(a)
    w_hi, w_lo = _split_bf16(w_ref[...])
    o_ref[...] = _dot(a_hi, w_hi) + _dot(a_hi, w_lo) + _dot(a_lo, w_hi) + b_ref[...]


def _ada(c, w_ada, b_ada):
    nb = c.shape[0]
    n_out = w_ada.shape[1]
    blk = D_MODEL
    return pl.pallas_call(
        _ada_kernel,
        grid=(n_out // blk,),
        in_specs=[
            pl.BlockSpec((nb, D_MODEL), lambda j: (0, 0)),
            pl.BlockSpec((D_MODEL, blk), lambda j: (0, j)),
            pl.BlockSpec((1, blk), lambda j: (0, j)),
        ],
        out_specs=pl.BlockSpec((nb, blk), lambda j: (0, j)),
        out_shape=jax.ShapeDtypeStruct((nb, n_out), F32),
        compiler_params=_params(("arbitrary",)),
        name="ada",
    )(c, w_ada, b_ada.reshape(1, n_out))


_C_CQ = 0
_C_CKV = _C_CQ + Q_LORA
_C_KR = _C_CKV + KV_LORA
_C_U = _C_KR + LANES
_C_V = _C_U + GMLP_WIDTH
_C_END = _C_V + GMLP_WIDTH
_ROPE_LO = QK_NOPE
_ROLL_PARTNER = LANES - QK_ROPE


def _premix_kernel(x_ref, mod_ref, cos_ref, sin_ref, gpre_ref, win_ref, gq_ref, wuq_ref, gkv_ref, wukv_ref,
                   gv_ref, wsp_ref, bsp_ref, ggo_ref, q_ref, kt_ref, v_ref, sn_ref):
    x = x_ref[0]
    mod = mod_ref[0]
    shift1, scale1 = mod[0:1], mod[1:2]
    h = _rms(x) * (gpre_ref[...] * (1.0 + scale1)) + shift1
    z = _dot(h.astype(BF16), win_ref[...])

    cosb = cos_ref[...]
    sinb = sin_ref[...]
    lane = lax.broadcasted_iota(jnp.int32, (1, LANES), 1)
    nope_mask = jnp.where(lane < QK_NOPE, 1.0, 0.0).astype(F32)

    qscale = (QK_NOPE + QK_ROPE) ** -0.5 * math.log2(math.e)
    cq_tab = (nope_mask + cosb) * qscale
    sq_tab = sinb * qscale
    cqn = (_rms(z[:, _C_CQ:_C_CKV]) * gq_ref[...]).astype(BF16)
    qb = _dot(cqn, wuq_ref[...])
    for hd in range(N_HEADS):
        blk = qb[:, hd * HEAD_PAD:(hd + 1) * HEAD_PAD]
        qh = blk * cq_tab + pltpu.roll(blk, _ROLL_PARTNER, 1) * sq_tab
        q_ref[0, hd] = qh.astype(BF16)

    ckvn = (_rms(z[:, _C_CKV:_C_KR]) * gkv_ref[...]).astype(BF16)
    kvb = _dot(ckvn, wukv_ref[...])
    krb = z[:, _C_KR:_C_U]
    krope = krb * cosb + pltpu.roll(krb, _ROLL_PARTNER, 1) * sinb
    v_off = N_HEADS * HEAD_PAD
    for hd in range(N_HEADS):
        kh = kvb[:, hd * HEAD_PAD:(hd + 1) * HEAD_PAD] + krope
        for c in range(kt_ref.shape[2]):
            kt_ref[0, hd, c] = kh[c * K_TILE:(c + 1) * K_TILE].T.astype(BF16)
        ones_lane = V_HEAD if hd % 2 == 0 else 0
        vh = kvb[:, v_off + hd * HEAD_PAD:v_off + (hd + 1) * HEAD_PAD] + jnp.where(lane == ones_lane, 1.0, 0.0)
        v_ref[0, hd] = vh.astype(BF16)

    ua = _gelu_tanh(z[:, _C_U:_C_V])
    vn = (_rms(_gelu_tanh(z[:, _C_V:_C_END])) * gv_ref[...]).astype(BF16)
    n_tok = x.shape[0]
    bsp = bsp_ref[...]
    rows = []
    for n in range(n_tok // CHUNK):
        cols = []
        for j in range(GMLP_GROUPS // 2):
            rhs = vn[n * CHUNK:(n + 1) * CHUNK, j * LANES:(j + 1) * LANES]
            ab = _dot(wsp_ref[j], rhs)
            cols.append(jnp.where(lane < GMLP_GROUP_DIM, ab[:CHUNK], ab[CHUNK:]))
        rows.append(jnp.concatenate(cols, axis=1) + bsp)
    s = ua * jnp.concatenate(rows, axis=0)
    sn_ref[0] = (_rms(s) * ggo_ref[...]).astype(BF16)


def _premix(x, mod, P):
    nb, seq, _ = x.shape
    tm = TOKEN_TILE
    nck = seq // K_TILE
    const = lambda i, b: (0, 0)
    return pl.pallas_call(
        _premix_kernel,
        grid=(seq // tm, nb),
        in_specs=[
            pl.BlockSpec((1, tm, D_MODEL), lambda i, b: (b, i, 0)),
            pl.BlockSpec((1, SUBLANES, D_MODEL), lambda i, b: (b, 0, 0)),
            pl.BlockSpec((tm, LANES), lambda i, b: (i, 0)),
            pl.BlockSpec((tm, LANES), lambda i, b: (i, 0)),
            pl.BlockSpec((1, D_MODEL), const),
            pl.BlockSpec((D_MODEL, _C_END), const),
            pl.BlockSpec((1, Q_LORA), const),
            pl.BlockSpec((Q_LORA, N_HEADS * HEAD_PAD), const),
            pl.BlockSpec((1, KV_LORA), const),
            pl.BlockSpec((KV_LORA, 2 * N_HEADS * HEAD_PAD), const),
            pl.BlockSpec((1, GMLP_WIDTH), const),
            pl.BlockSpec((GMLP_GROUPS // 2, 2 * CHUNK, CHUNK), lambda i, b: (0, 0, 0)),
            pl.BlockSpec((CHUNK, GMLP_WIDTH), const),
            pl.BlockSpec((1, GMLP_WIDTH), const),
        ],
        out_specs=[
            pl.BlockSpec((1, N_HEADS, tm, HEAD_PAD), lambda i, b: (b, 0, i, 0)),
            pl.BlockSpec((1, N_HEADS, tm // K_TILE, HEAD_PAD, K_TILE), lambda i, b: (b, 0, i, 0, 0)),
            pl.BlockSpec((1, N_HEADS, tm, HEAD_PAD), lambda i, b: (b, 0, i, 0)),
            pl.BlockSpec((1, tm, GMLP_WIDTH), lambda i, b: (b, i, 0)),
        ],
        out_shape=[
            jax.ShapeDtypeStruct((nb, N_HEADS, seq, HEAD_PAD), BF16),
            jax.ShapeDtypeStruct((nb, N_HEADS, nck, HEAD_PAD, K_TILE), BF16),
            jax.ShapeDtypeStruct((nb, N_HEADS, seq, HEAD_PAD), BF16),
            jax.ShapeDtypeStruct((nb, seq, GMLP_WIDTH), BF16),
        ],
        compiler_params=_params(("arbitrary", "arbitrary")),
        name="premix",
    )(x, mod, *P["rope"][seq], P["g_pre1"], P["w_in"], P["g_q"], P["w_uq"], P["g_kv"], P["w_ukv"],
      P["g_v"], P["w_sp"], P["b_sp"], P["g_gmlp_out"])


def _attn_kernel(q_ref, kt_ref, v_ref, o_ref):
    n_chunks = kt_ref.shape[2]
    tk = kt_ref.shape[4]
    lane = lax.broadcasted_iota(jnp.int32, (1, LANES), 1)

    def one_head(hd, ones_lane):
        q = q_ref[0, hd]
        m = None
        acc = None
        for c in range(n_chunks):
            s = _dot(q, kt_ref[0, hd, c])
            v = v_ref[0, hd, c * tk:(c + 1) * tk, :]
            smax = jnp.max(s, axis=1, keepdims=True)
            if c == 0:
                m = smax
                acc = _dot(jnp.exp2((s - m).astype(BF16)), v)
            else:
                m_new = jnp.maximum(m, smax)
                acc = acc * jnp.exp2(m - m_new) + _dot(jnp.exp2((s - m_new).astype(BF16)), v)
                m = m_new
        row_sum = acc[:, ones_lane:ones_lane + 1]
        return acc * (1.0 / row_sum)

    def pair(j, carry):
        even = one_head(2 * j, V_HEAD)
        odd = one_head(2 * j + 1, 0)
        o_ref[0, j] = jnp.where(lane < V_HEAD, even, odd).astype(BF16)
        return carry

    lax.fori_loop(0, N_HEADS // 2, pair, 0)


def _attention(q, kt, v):
    nb, _, seq, _ = q.shape
    nck = kt.shape[2]
    return pl.pallas_call(
        _attn_kernel,
        grid=(nb, seq // Q_TILE),
        in_specs=[
            pl.BlockSpec((1, N_HEADS, Q_TILE, HEAD_PAD), lambda b, i: (b, 0, i, 0)),
            pl.BlockSpec((1, N_HEADS, nck, HEAD_PAD, K_TILE), lambda b, i: (b, 0, 0, 0, 0)),
            pl.BlockSpec((1, N_HEADS, seq, HEAD_PAD), lambda b, i: (b, 0, 0, 0)),
        ],
        out_specs=pl.BlockSpec((1, N_HEADS // 2, Q_TILE, LANES), lambda b, i: (b, 0, i, 0)),
        out_shape=jax.ShapeDtypeStruct((nb, N_HEADS // 2, seq, LANES), BF16),
        compiler_params=_params(("arbitrary", "arbitrary")),
        name="attn",
    )(q, kt, v)


_R_GROUP_ROW = N_EXPERTS


def _postmix_kernel(a_ref, sn_ref, x_ref, mod_ref, gao_ref, wout_ref, gpost_ref, gpre_ref, wr_ref, br_ref, tri_ref,
                    x1_ref, h2_ref, eid_ref, rank_ref, wcol_ref, cnt_ref, run_ref):
    first = jnp.logical_and(pl.program_id(0) == 0, pl.program_id(1) == 0)

    @pl.when(first)
    def _():
        run_ref[...] = jnp.zeros_like(run_ref)

    mod = mod_ref[0]
    gate1, shift2, scale2 = mod[2:3], mod[3:4], mod[4:5]
    a = jnp.concatenate([a_ref[0, j] for j in range(N_HEADS // 2)], axis=1).astype(F32)
    an = (_rms(a) * gao_ref[...]).astype(BF16)
    merged = jnp.concatenate([an, sn_ref[0]], axis=1)
    o = _dot(merged, wout_ref[...])
    x1 = x_ref[0] + gate1 * (_rms(o) * gpost_ref[...])
    x1_ref[0] = x1
    h2 = _rms(x1) * (gpre_ref[...] * (1.0 + scale2)) + shift2
    n_tok = h2.shape[0]
    _store_token_rows(h2_ref, h2)

    h_hi, h_lo = _split_bf16(h2)
    wr = wr_ref[...]
    hh = _dot(h_hi, wr)
    lh = _dot(h_lo, wr[:, :LANES])
    logits = hh[:, :LANES] + hh[:, LANES:] + lh + br_ref[...]
    lt = logits.T

    neg = jnp.float32(-jnp.inf)
    row8 = lax.broadcasted_iota(jnp.int32, (SUBLANES, n_tok), 0).astype(F32)
    lg = jnp.where(row8 < N_EXPERT_GROUPS, lt[_R_GROUP_ROW:_R_GROUP_ROW + SUBLANES], neg)
    gmax = jnp.max(lg, axis=0, keepdims=True)
    gi = jnp.min(jnp.where(lg == gmax, row8, float(SUBLANES)), axis=0, keepdims=True)
    pg_sel = 1.0 / jnp.sum(jnp.exp(lg - gmax), axis=0, keepdims=True)

    le = jnp.zeros((EXPERTS_PER_GROUP, n_tok), F32)
    for g in range(N_EXPERT_GROUPS):
        le = jnp.where(gi == float(g), lt[g * EXPERTS_PER_GROUP:(g + 1) * EXPERTS_PER_GROUP], le)
    v1 = jnp.max(le, axis=0, keepdims=True)
    i1 = jnp.min(jnp.where(le == v1, row8, float(SUBLANES)), axis=0, keepdims=True)
    le2 = jnp.where(row8 == i1, neg, le)
    v2 = jnp.max(le2, axis=0, keepdims=True)
    i2 = jnp.min(jnp.where(le2 == v2, row8, float(SUBLANES)), axis=0, keepdims=True)
    r = jnp.exp(v2 - v1)
    w1 = pg_sel / (1.0 + r)
    w2 = w1 * r
    e1 = gi * float(EXPERTS_PER_GROUP) + i1
    e2 = gi * float(EXPERTS_PER_GROUP) + i2
    eid_ref[0] = jnp.concatenate([e1, e2], axis=0).astype(jnp.int32)

    row32 = lax.broadcasted_iota(jnp.int32, (N_EXPERTS, n_tok), 0).astype(F32)
    hit1 = row32 == e1
    hit2 = row32 == e2
    onehot = jnp.where(jnp.logical_or(hit1, hit2), 1.0, 0.0)
    run = run_ref[...][:, 0:1]
    ranks1, ranks2 = [], []
    for c in range(n_tok // RANK_CHUNK):
        sl = slice(c * RANK_CHUNK, (c + 1) * RANK_CHUNK)
        oh = onehot[:, sl]
        before = _dot(oh.astype(BF16), tri_ref[...]) + run
        ranks1.append(jnp.sum(jnp.where(hit1[:, sl], before, 0.0), axis=0, keepdims=True))
        ranks2.append(jnp.sum(jnp.where(hit2[:, sl], before, 0.0), axis=0, keepdims=True))
        run = run + jnp.sum(oh, axis=1, keepdims=True)
    rank_ref[0] = jnp.concatenate(
        [jnp.concatenate(ranks1, axis=1), jnp.concatenate(ranks2, axis=1)], axis=0).astype(jnp.int32)
    run_b = jnp.broadcast_to(run, run_ref.shape)
    run_ref[...] = run_b
    cnt_ref[...] = run_b

    row128 = lax.broadcasted_iota(jnp.int32, (LANES, n_tok), 0)
    wt = jnp.where(row128 == 0, w1, jnp.where(row128 == 1, w2, 0.0))
    wcol_ref[0] = wt.T


def _postmix(a, sn, x, mod, P):
    nb, seq, _ = x.shape
    tm = POST_TILE
    const = lambda b, i: (0, 0)
    return pl.pallas_call(
        _postmix_kernel,
        grid=(nb, seq // tm),
        in_specs=[
            pl.BlockSpec((1, N_HEADS // 2, tm, LANES), lambda b, i: (b, 0, i, 0)),
            pl.BlockSpec((1, tm, GMLP_WIDTH), lambda b, i: (b, i, 0)),
            pl.BlockSpec((1, tm, D_MODEL), lambda b, i: (b, i, 0)),
            pl.BlockSpec((1, SUBLANES, D_MODEL), lambda b, i: (b, 0, 0)),
            pl.BlockSpec((1, N_HEADS * V_HEAD), const),
            pl.BlockSpec((D_MODEL, D_MODEL), const),
            pl.BlockSpec((1, D_MODEL), const),
            pl.BlockSpec((1, D_MODEL), const),
            pl.BlockSpec((D_MODEL, 2 * LANES), const),
            pl.BlockSpec((1, LANES), const),
            pl.BlockSpec((RANK_CHUNK, RANK_CHUNK), const),
        ],
        out_specs=[
            pl.BlockSpec((1, tm, D_MODEL), lambda b, i: (b, i, 0)),
            pl.BlockSpec((tm * ROW_TILES, LANES), lambda b, i: (b * (seq // tm) + i, 0)),
            pl.BlockSpec((1, 2, tm), lambda b, i: (b, 0, i)),
            pl.BlockSpec((1, 2, tm), lambda b, i: (b, 0, i)),
            pl.BlockSpec((1, tm, LANES), lambda b, i: (b, i, 0)),
            pl.BlockSpec((N_EXPERTS, LANES), const),
        ],
        out_shape=[
            jax.ShapeDtypeStruct((nb, seq, D_MODEL), F32),
            jax.ShapeDtypeStruct((nb * seq * ROW_TILES, LANES), U32),
            jax.ShapeDtypeStruct((nb, 2, seq), jnp.int32),
            jax.ShapeDtypeStruct((nb, 2, seq), jnp.int32),
            jax.ShapeDtypeStruct((nb, seq, LANES), F32),
            jax.ShapeDtypeStruct((N_EXPERTS, LANES), F32),
        ],
        scratch_shapes=[pltpu.VMEM((N_EXPERTS, LANES), F32)],
        compiler_params=_params(("arbitrary", "arbitrary")),
        name="postmix",
    )(a, sn, x, mod, P["g_attn_out"], P["w_out"], P["g_post1"], P["g_pre2"], P["w_router"], P["b_router"], P["tri"])


def _sc_mesh():
    return plsc.VectorSubcoreMesh(core_axis_name="c", subcore_axis_name="s")


def _sc_worker():
    return lax.axis_index("s") * SC_CORES + lax.axis_index("c")


def _sc_gather_rows(table, idx):
    n = idx.shape[0]
    assert n % (SC_WORKERS * SC_ROWS) == 0
    per_w = n // SC_WORKERS
    steps = per_w // SC_ROWS
    idx3 = idx.reshape(SC_WORKERS, steps, SC_ROWS)

    @functools.partial(
        pl.kernel, mesh=_sc_mesh(),
        out_type=jax.ShapeDtypeStruct((n, ROW_TILES, LANES), U32),
        scratch_types=[pltpu.VMEM((steps, SC_ROWS), jnp.int32), pltpu.VMEM((SC_ROWS, ROW_TILES, LANES), U32),
                       pltpu.SemaphoreType.DMA],
        name="sc_gather_rows",
    )
    def gather(table_hbm, idx_hbm, out_hbm, idx_v, rows_v, sem):
        wid = _sc_worker()
        pltpu.sync_copy(idx_hbm.at[wid], idx_v)

        @pl.loop(0, steps)
        def _(j):
            pltpu.async_copy(table_hbm.at[idx_v.at[j]], rows_v, sem).wait()
            pltpu.sync_copy(rows_v, out_hbm.at[pl.ds(wid * per_w + j * SC_ROWS, SC_ROWS)])

    return gather(table, idx3)


def _sc_scatter_rows(src, idx, n_dst):
    n_dup, n = idx.shape
    assert n % (SC_WORKERS * SC_ROWS) == 0
    per_w = n // SC_WORKERS
    steps = per_w // SC_ROWS
    idx4 = idx.reshape(n_dup, SC_WORKERS, steps, SC_ROWS).transpose(1, 0, 2, 3)

    @functools.partial(
        pl.kernel, mesh=_sc_mesh(),
        out_type=jax.ShapeDtypeStruct((n_dst, ROW_TILES, LANES), U32),
        scratch_types=[pltpu.VMEM((n_dup, steps, SC_ROWS), jnp.int32), pltpu.VMEM((SC_ROWS, ROW_TILES, LANES), U32),
                       pltpu.SemaphoreType.DMA],
        name="sc_scatter_rows",
    )
    def scatter(src_hbm, idx_hbm, dst_hbm, idx_v, rows_v, sem):
        wid = _sc_worker()
        pltpu.sync_copy(idx_hbm.at[wid], idx_v)

        @pl.loop(0, steps)
        def _(j):
            pltpu.sync_copy(src_hbm.at[pl.ds(wid * per_w + j * SC_ROWS, SC_ROWS)], rows_v)
            copies = [pltpu.async_copy(rows_v, dst_hbm.at[idx_v.at[k, j]], sem) for k in range(n_dup)]
            for copy in copies:
                copy.wait()

    return scatter(src, idx4)


def _expert_kernel(te_ref, tr_ref, nv_ref, xs_ref, wg_ref, wu_ref, wd_ref, ys_ref, wgu_bf, wd_bf):
    i = pl.program_id(0)
    valid = i < nv_ref[0]
    new_expert = jnp.logical_or(i == 0, te_ref[i] != te_ref[jnp.maximum(i - 1, 0)])

    @pl.when(jnp.logical_and(valid, new_expert))
    def _():
        wgu_bf[:, :EXPERT_FF] = wg_ref[0].astype(BF16)
        wgu_bf[:, EXPERT_FF:] = wu_ref[0].astype(BF16)
        wd_bf[...] = wd_ref[0].astype(BF16)

    @pl.when(valid)
    def _():
        n_rows = xs_ref.shape[0] // ROW_TILES
        x = _load_token_rows(xs_ref, n_rows)
        row = lax.broadcasted_iota(jnp.int32, (n_rows, 1), 0)
        x = jnp.where(row < tr_ref[i], x, 0.0).astype(BF16)
        gu = _dot(x, wgu_bf[...])
        g, u = gu[:, :EXPERT_FF], gu[:, EXPERT_FF:]
        act = (g * jax.nn.sigmoid(g) * u).astype(BF16)
        _store_token_rows(ys_ref, _dot(act, wd_bf[...]))

    @pl.when(jnp.logical_not(valid))
    def _():
        ys_ref[...] = jnp.zeros_like(ys_ref)


def _expert_tile(n_tok):
    return EXPERT_TILE_LARGE if 2 * n_tok // N_EXPERTS >= 2 * EXPERT_TILE_LARGE else EXPERT_TILE_SMALL


def _experts(xs, tile_expert, tile_rows, n_valid, tile, P):
    rows_blk = tile * ROW_TILES
    assert xs.shape[0] % rows_blk == 0
    n_tiles = xs.shape[0] // rows_blk

    def row_map(i, te, tr, nv):
        return (jnp.minimum(i, nv[0] - 1), 0)

    def out_map(i, te, tr, nv):
        return (i, 0)

    def w_map(i, te, tr, nv):
        return (te[jnp.minimum(i, nv[0] - 1)], 0, 0)

    return pl.pallas_call(
        _expert_kernel,
        grid_spec=pltpu.PrefetchScalarGridSpec(
            num_scalar_prefetch=3,
            grid=(n_tiles,),
            in_specs=[
                pl.BlockSpec((rows_blk, LANES), row_map),
                pl.BlockSpec((1, D_MODEL, EXPERT_FF), w_map),
                pl.BlockSpec((1, D_MODEL, EXPERT_FF), w_map),
                pl.BlockSpec((1, EXPERT_FF, D_MODEL), w_map),
            ],
            out_specs=pl.BlockSpec((rows_blk, LANES), out_map),
            scratch_shapes=[pltpu.VMEM((D_MODEL, 2 * EXPERT_FF), BF16), pltpu.VMEM((EXPERT_FF, D_MODEL), BF16)],
        ),
        out_shape=jax.ShapeDtypeStruct(xs.shape, U32),
        compiler_params=_params(("arbitrary",)),
        name="experts",
    )(tile_expert, tile_rows, n_valid, xs, P["w_gate"], P["w_up"], P["w_down"])


def _final_kernel(y0_ref, y1_ref, wcol_ref, x1_ref, mod_ref, gpost_ref, o_ref):
    w = wcol_ref[0]
    w0, w1 = w[:, 0:1], w[:, 1:2]
    n_tok = w.shape[0]
    m = w0 * _load_token_rows(y0_ref, n_tok) + w1 * _load_token_rows(y1_ref, n_tok)
    gate2 = mod_ref[0][5:6]
    o_ref[0] = x1_ref[0] + gate2 * (_rms(m) * gpost_ref[...])


def _final(yg, wcol, x1, mod, P):
    nb, seq, _ = x1.shape
    tm = POST_TILE
    nt = seq // tm
    n_tok_tiles = nb * nt
    deep = pl.Buffered(FINAL_INPUT_BUFFERS)

    def pipelined(yg_hbm, wcol_hbm, x1_hbm, mod_hbm, gpost_hbm, o_hbm):
        pltpu.emit_pipeline(
            _final_kernel,
            grid=(nb, nt),
            in_specs=[
                pl.BlockSpec((tm * ROW_TILES, LANES), lambda b, i: (b * nt + i, 0), pipeline_mode=deep),
                pl.BlockSpec((tm * ROW_TILES, LANES), lambda b, i: (n_tok_tiles + b * nt + i, 0), pipeline_mode=deep),
                pl.BlockSpec((1, tm, LANES), lambda b, i: (b, i, 0), pipeline_mode=deep),
                pl.BlockSpec((1, tm, D_MODEL), lambda b, i: (b, i, 0), pipeline_mode=deep),
                pl.BlockSpec((1, SUBLANES, D_MODEL), lambda b, i: (b, 0, 0)),
                pl.BlockSpec((1, D_MODEL), lambda b, i: (0, 0)),
            ],
            out_specs=[pl.BlockSpec((1, tm, D_MODEL), lambda b, i: (b, i, 0))],
        )(yg_hbm, yg_hbm, wcol_hbm, x1_hbm, mod_hbm, gpost_hbm, o_hbm)

    any_spec = pl.BlockSpec(memory_space=pl.ANY)
    return pl.pallas_call(
        pipelined,
        in_specs=[any_spec] * 5,
        out_specs=any_spec,
        out_shape=jax.ShapeDtypeStruct((nb, seq, D_MODEL), F32),
        compiler_params=pltpu.CompilerParams(vmem_limit_bytes=VMEM_LIMIT),
        name="final",
    )(yg, wcol, x1, mod, P["g_post2"])


def _prepare(w):
    f = lambda a: a.astype(F32)
    P = {}
    for k in ("g_pre1", "g_post1", "g_pre2", "g_post2", "g_q", "g_kv", "g_attn_out", "g_gmlp_out"):
        P[k] = f(w[k]).reshape(1, -1)
    P["g_v"] = f(w["g_v_gmlp"]).reshape(1, -1)

    w_in = f(w["w_in"])
    o0, o1, o2, o3 = Q_LORA, Q_LORA + KV_LORA, Q_LORA + KV_LORA + QK_ROPE, Q_LORA + KV_LORA + QK_ROPE + GMLP_WIDTH
    w_kr = w_in[:, o1:o2]
    kr_partner = jnp.concatenate([-w_kr[:, ROPE_HALF:], w_kr[:, :ROPE_HALF]], axis=1)
    rope_blk = jnp.concatenate([jnp.zeros((D_MODEL, QK_NOPE), F32), w_kr, kr_partner], axis=1)
    P["w_in"] = jnp.concatenate([w_in[:, :o1], rope_blk, w_in[:, o2:o3], w_in[:, o3:]], axis=1).astype(BF16)

    w_uq = f(w["w_uq"]).reshape(Q_LORA, N_HEADS, QK_NOPE + QK_ROPE)
    q_rope = w_uq[:, :, QK_NOPE:]
    q_partner = jnp.concatenate([-q_rope[:, :, ROPE_HALF:], q_rope[:, :, :ROPE_HALF]], axis=2)
    P["w_uq"] = jnp.concatenate([w_uq, q_partner], axis=2).reshape(Q_LORA, N_HEADS * HEAD_PAD).astype(BF16)

    w_ukv = f(w["w_ukv"]).reshape(KV_LORA, N_HEADS, QK_NOPE + V_HEAD)
    zeros = jnp.zeros((KV_LORA, N_HEADS, HEAD_PAD - QK_NOPE), F32)
    w_k = jnp.concatenate([w_ukv[:, :, :QK_NOPE], zeros], axis=2)
    w_v = w_ukv[:, :, QK_NOPE:]
    even = (jnp.arange(N_HEADS) % 2 == 0)[None, :, None]
    zv = jnp.zeros_like(w_v)
    w_v = jnp.concatenate([jnp.where(even, w_v, zv), jnp.where(even, zv, w_v)], axis=2)
    P["w_ukv"] = jnp.concatenate([w_k.reshape(KV_LORA, -1), w_v.reshape(KV_LORA, -1)], axis=1).astype(BF16)

    P["w_sp"] = f(w["w_spatial"]).reshape(GMLP_GROUPS // 2, 2 * CHUNK, CHUNK).astype(BF16)
    P["b_sp"] = jnp.repeat(f(w["b_spatial"]).T, GMLP_GROUP_DIM, axis=1)

    P["w_out"] = f(w["w_out"]).astype(BF16)

    pad = jnp.zeros((D_MODEL, LANES - N_EXPERTS - N_EXPERT_GROUPS), F32)
    wr = jnp.concatenate([f(w["w_router_expert"]), f(w["w_router_group"]), pad], axis=1)
    wr_hi = wr.astype(BF16)
    wr_lo = (wr - wr_hi.astype(F32)).astype(BF16)
    P["w_router"] = jnp.concatenate([wr_hi, wr_lo], axis=1)
    P["b_router"] = jnp.concatenate(
        [f(w["b_router_expert"]), f(w["b_router_group"]), jnp.zeros((LANES - N_EXPERTS - N_EXPERT_GROUPS,), F32)]
    ).reshape(1, LANES)
    P["tri"] = jnp.triu(jnp.ones((RANK_CHUNK, RANK_CHUNK), F32), k=1).astype(BF16)

    P["w_gate"], P["w_up"], P["w_down"] = f(w["w_gate"]), f(w["w_up"]), f(w["w_down"])
    return P


def _rope_tables(seq):
    inv = ROPE_THETA ** (-jnp.arange(ROPE_HALF, dtype=F32) / ROPE_HALF)
    ang = jnp.arange(seq, dtype=F32)[:, None] * inv[None, :]
    z_lo = jnp.zeros((seq, _ROPE_LO), F32)
    z_hi = jnp.zeros((seq, LANES - _ROPE_LO - QK_ROPE), F32)
    cos = jnp.concatenate([z_lo, jnp.cos(ang), jnp.cos(ang), z_hi], axis=1)
    sin = jnp.concatenate([z_lo, jnp.sin(ang), jnp.sin(ang), z_hi], axis=1)
    return cos, sin


def _layer(x, mod, P):
    nb, seq, d_model = x.shape
    assert d_model == D_MODEL and TOKEN_TILE % K_TILE == 0
    assert seq % TOKEN_TILE == 0 and seq % Q_TILE == 0 and seq % POST_TILE == 0 and POST_TILE % RANK_CHUNK == 0
    n_tok = nb * seq
    q, kt, v, sn = _premix(x, mod, P)
    a = _attention(q, kt, v)
    x1, h2rows, eid, rank, wcol, counts = _postmix(a, sn, x, mod, P)

    tile = _expert_tile(n_tok)
    cnt = counts[:, 0].astype(jnp.int32)
    padded = ((cnt + tile - 1) // tile) * tile
    ends = jnp.cumsum(padded)
    starts = ends - padded
    eflat = jnp.transpose(eid, (1, 0, 2)).reshape(2, n_tok)
    rflat = jnp.transpose(rank, (1, 0, 2)).reshape(2, n_tok)
    onehot = eflat[:, :, None] == jnp.arange(N_EXPERTS, dtype=jnp.int32)[None, None, :]
    pos = rflat + jnp.sum(jnp.where(onehot, starts[None, None, :], 0), axis=2)
    n_rows = 2 * n_tok + N_EXPERTS * tile
    n_tiles = n_rows // tile
    tile_start = jnp.arange(n_tiles, dtype=jnp.int32) * tile
    tile_expert = jnp.minimum(
        jnp.sum((tile_start[:, None] >= ends[None, :]).astype(jnp.int32), axis=1), N_EXPERTS - 1).astype(jnp.int32)
    expert_ids = jnp.arange(N_EXPERTS, dtype=jnp.int32)
    live_end = jnp.sum(jnp.where(tile_expert[:, None] == expert_ids[None, :], (starts + cnt)[None, :], 0), axis=1)
    tile_rows = jnp.clip(live_end - tile_start, 0, tile).astype(jnp.int32)
    n_valid = (ends[-1:] // tile).astype(jnp.int32)

    as_tiles = lambda a: a.reshape(-1, ROW_TILES, LANES)
    as_rows = lambda a: a.reshape(-1, LANES)
    xs = _sc_scatter_rows(as_tiles(h2rows), pos, n_rows)
    ys = _experts(as_rows(xs), tile_expert, tile_rows, n_valid, tile, P)
    yg = as_rows(_sc_gather_rows(as_tiles(ys), pos.reshape(2 * n_tok)))
    return _final(yg, wcol, x1, mod, P)


def kernel(x_prompt, x_sample, c_prompt, c_sample, w_ada, b_ada, g_pre1, g_post1, g_pre2, g_post2, w_in, g_q, w_uq,
           g_kv, w_ukv, g_v_gmlp, w_spatial, b_spatial, g_attn_out, g_gmlp_out, w_out, w_router_group,
           b_router_group, w_router_expert, b_router_expert, w_gate, w_up, w_down):
    P = _prepare(dict(
        g_pre1=g_pre1, g_post1=g_post1, g_pre2=g_pre2, g_post2=g_post2, w_in=w_in, g_q=g_q, w_uq=w_uq, g_kv=g_kv,
        w_ukv=w_ukv, g_v_gmlp=g_v_gmlp, w_spatial=w_spatial, b_spatial=b_spatial, g_attn_out=g_attn_out,
        g_gmlp_out=g_gmlp_out, w_out=w_out, w_router_group=w_router_group, b_router_group=b_router_group,
        w_router_expert=w_router_expert, b_router_expert=b_router_expert, w_gate=w_gate, w_up=w_up, w_down=w_down))
    P["rope"] = {seq: _rope_tables(seq) for seq in {x_prompt.shape[1], x_sample.shape[1]}}

    nbp = c_prompt.shape[0]
    c_all = jnp.concatenate([c_prompt, c_sample], axis=0).astype(F32)
    mod = _ada(c_all, w_ada.astype(F32), b_ada.astype(F32))
    mod = mod.reshape(c_all.shape[0], 6, D_MODEL)
    mod = jnp.concatenate([mod, jnp.zeros((c_all.shape[0], SUBLANES - 6, D_MODEL), F32)], axis=1)

    y_prompt = _layer(x_prompt, mod[:nbp], P)
    y_sample = _layer(x_sample, mod[nbp:], P)
    return (y_prompt, y_sample)
```

```python
import functools
import math

import jax
import jax.numpy as jnp
from jax import lax
from jax.experimental import pallas as pl
from jax.experimental.pallas import tpu as pltpu
from jax.experimental.pallas import tpu_sc as plsc

F32 = jnp.float32
BF16 = jnp.bfloat16

D_MODEL = 1024
N_HEADS = 8
QK_NOPE = 64
QK_ROPE = 32
ROPE_HALF = QK_ROPE // 2
V_HEAD = 64
Q_LORA = 256
KV_LORA = 128
GMLP_WIDTH = 512
GMLP_GROUPS = 8
GMLP_GROUP_DIM = 64
CHUNK = 128
N_EXPERTS = 32
N_EXPERT_GROUPS = 4
EXPERTS_PER_GROUP = 8
EXPERT_FF = 256
ROPE_THETA = 10000.0
EPS = 1e-6

LANES = 128
SUBLANES = 8
HEAD_PAD = LANES

TOKEN_TILE = 1024
POST_TILE = 1024
Q_TILE = 1024
K_TILE = 512
EXPERT_TILE_SMALL = 512
EXPERT_TILE_LARGE = 1024
SC_CORES = 2
SC_WORKERS = 32
SC_ROWS = 128
RANK_CHUNK = 256
STREAM_BUFFERS = 3
VMEM_LIMIT = 56 * 1024 * 1024

U32 = jnp.uint32
PACKED_WIDTH = D_MODEL // 2
ROW_TILES = PACKED_WIDTH // LANES
_HI_MASK = 0xFFFF0000

_SQRT_2_OVER_PI = math.sqrt(2.0 / math.pi)


def _rms(x):
    return x * lax.rsqrt(jnp.mean(x * x, axis=-1, keepdims=True) + EPS)


def _gelu_tanh(x):
    return 0.5 * x * (1.0 + jnp.tanh(_SQRT_2_OVER_PI * (x + 0.044715 * (x * x * x))))


def _split_bf16(x):
    hi = x.astype(BF16)
    lo = (x - hi.astype(F32)).astype(BF16)
    return hi, lo


def _dot(a, b):
    return jnp.dot(a, b, preferred_element_type=F32)


def _bf16_bits(x):
    return lax.bitcast_convert_type(x.astype(BF16).astype(F32), U32)


def _load_token_rows(ref, n):
    w = jnp.concatenate([ref[pl.ds(c, n, stride=ROW_TILES), :] for c in range(ROW_TILES)], axis=1)
    lo = lax.bitcast_convert_type(w << 16, F32)
    hi = lax.bitcast_convert_type(w & jnp.uint32(_HI_MASK), F32)
    return jnp.concatenate([lo, hi], axis=1)


def _store_token_rows(ref, val):
    n = val.shape[0]
    w = (_bf16_bits(val[:, :PACKED_WIDTH]) >> 16) | (_bf16_bits(val[:, PACKED_WIDTH:]) & jnp.uint32(_HI_MASK))
    for c in range(ROW_TILES):
        ref[pl.ds(c, n, stride=ROW_TILES), :] = w[:, c * LANES:(c + 1) * LANES]


def _params(sem, vmem=VMEM_LIMIT):
    return pltpu.CompilerParams(dimension_semantics=sem, vmem_limit_bytes=vmem)


def _ada_kernel(c_ref, w_ref, b_ref, o_ref):
    c = c_ref[...]
    a = c * jax.nn.sigmoid(c)
    a_hi, a_lo = _split_bf16(a)
    w_hi, w_lo = _split_bf16(w_ref[...])
    o_ref[...] = _dot(a_hi, w_hi) + _dot(a_hi, w_lo) + _dot(a_lo, w_hi) + b_ref[...]


def _ada(c, w_ada, b_ada):
    nb = c.shape[0]
    n_out = w_ada.shape[1]
    blk = D_MODEL
    return pl.pallas_call(
        _ada_kernel,
        grid=(n_out // blk,),
        in_specs=[
            pl.BlockSpec((nb, D_MODEL), lambda j: (0, 0)),
            pl.BlockSpec((D_MODEL, blk), lambda j: (0, j)),
            pl.BlockSpec((1, blk), lambda j: (0, j)),
        ],
        out_specs=pl.BlockSpec((nb, blk), lambda j: (0, j)),
        out_shape=jax.ShapeDtypeStruct((nb, n_out), F32),
        compiler_params=_params(("arbitrary",)),
        name="ada",
    )(c, w_ada, b_ada.reshape(1, n_out))


_C_CQ = 0
_C_CKV = _C_CQ + Q_LORA
_C_KR = _C_CKV + KV_LORA
_C_U = _C_KR + LANES
_C_V = _C_U + GMLP_WIDTH
_C_END = _C_V + GMLP_WIDTH
_ROPE_LO = QK_NOPE
_ROLL_PARTNER = LANES - QK_ROPE


def _premix_kernel(x_ref, mod_ref, cos_ref, sin_ref, gpre_ref, win_ref, gq_ref, wuq_ref, gkv_ref, wukv_ref,
                   gv_ref, wsp_ref, bsp_ref, ggo_ref, q_ref, kt_ref, v_ref, sn_ref):
    x = x_ref[0]
    mod = mod_ref[0]
    shift1, scale1 = mod[0:1], mod[1:2]
    h = _rms(x) * (gpre_ref[...] * (1.0 + scale1)) + shift1
    z = _dot(h.astype(BF16), win_ref[...])

    cosb = cos_ref[...]
    sinb = sin_ref[...]
    lane = lax.broadcasted_iota(jnp.int32, (1, LANES), 1)
    nope_mask = jnp.where(lane < QK_NOPE, 1.0, 0.0).astype(F32)

    qscale = (QK_NOPE + QK_ROPE) ** -0.5 * math.log2(math.e)
    cq_tab = (nope_mask + cosb) * qscale
    sq_tab = sinb * qscale
    cqn = (_rms(z[:, _C_CQ:_C_CKV]) * gq_ref[...]).astype(BF16)
    qb = _dot(cqn, wuq_ref[...])
    for hd in range(N_HEADS):
        blk = qb[:, hd * HEAD_PAD:(hd + 1) * HEAD_PAD]
        qh = blk * cq_tab + pltpu.roll(blk, _ROLL_PARTNER, 1) * sq_tab
        q_ref[0, hd] = qh.astype(BF16)

    ckvn = (_rms(z[:, _C_CKV:_C_KR]) * gkv_ref[...]).astype(BF16)
    kvb = _dot(ckvn, wukv_ref[...])
    krb = z[:, _C_KR:_C_U]
    krope = krb * cosb + pltpu.roll(krb, _ROLL_PARTNER, 1) * sinb
    v_off = N_HEADS * HEAD_PAD
    for hd in range(N_HEADS):
        kh = kvb[:, hd * HEAD_PAD:(hd + 1) * HEAD_PAD] + krope
        for c in range(kt_ref.shape[2]):
            kt_ref[0, hd, c] = kh[c * K_TILE:(c + 1) * K_TILE].T.astype(BF16)
        ones_lane = V_HEAD if hd % 2 == 0 else 0
        vh = kvb[:, v_off + hd * HEAD_PAD:v_off + (hd + 1) * HEAD_PAD] + jnp.where(lane == ones_lane, 1.0, 0.0)
        v_ref[0, hd] = vh.astype(BF16)

    ua = _gelu_tanh(z[:, _C_U:_C_V])
    vn = (_rms(_gelu_tanh(z[:, _C_V:_C_END])) * gv_ref[...]).astype(BF16)
    n_tok = x.shape[0]
    bsp = bsp_ref[...]
    rows = []
    for n in range(n_tok // CHUNK):
        cols = []
        for j in range(GMLP_GROUPS // 2):
            rhs = vn[n * CHUNK:(n + 1) * CHUNK, j * LANES:(j + 1) * LANES]
            ab = _dot(wsp_ref[j], rhs)
            cols.append(jnp.where(lane < GMLP_GROUP_DIM, ab[:CHUNK], ab[CHUNK:]))
        rows.append(jnp.concatenate(cols, axis=1) + bsp)
    s = ua * jnp.concatenate(rows, axis=0)
    sn_ref[0] = (_rms(s) * ggo_ref[...]).astype(BF16)


def _premix(x, mod, P):
    nb, seq, _ = x.shape
    tm = TOKEN_TILE
    nck = seq // K_TILE
    const = lambda i, b: (0, 0)
    return pl.pallas_call(
        _premix_kernel,
        grid=(seq // tm, nb),
        in_specs=[
            pl.BlockSpec((1, tm, D_MODEL), lambda i, b: (b, i, 0)),
            pl.BlockSpec((1, SUBLANES, D_MODEL), lambda i, b: (b, 0, 0)),
            pl.BlockSpec((tm, LANES), lambda i, b: (i, 0)),
            pl.BlockSpec((tm, LANES), lambda i, b: (i, 0)),
            pl.BlockSpec((1, D_MODEL), const),
            pl.BlockSpec((D_MODEL, _C_END), const),
            pl.BlockSpec((1, Q_LORA), const),
            pl.BlockSpec((Q_LORA, N_HEADS * HEAD_PAD), const),
            pl.BlockSpec((1, KV_LORA), const),
            pl.BlockSpec((KV_LORA, 2 * N_HEADS * HEAD_PAD), const),
            pl.BlockSpec((1, GMLP_WIDTH), const),
            pl.BlockSpec((GMLP_GROUPS // 2, 2 * CHUNK, CHUNK), lambda i, b: (0, 0, 0)),
            pl.BlockSpec((CHUNK, GMLP_WIDTH), const),
            pl.BlockSpec((1, GMLP_WIDTH), const),
        ],
        out_specs=[
            pl.BlockSpec((1, N_HEADS, tm, HEAD_PAD), lambda i, b: (b, 0, i, 0)),
            pl.BlockSpec((1, N_HEADS, tm // K_TILE, HEAD_PAD, K_TILE), lambda i, b: (b, 0, i, 0, 0)),
            pl.BlockSpec((1, N_HEADS, tm, HEAD_PAD), lambda i, b: (b, 0, i, 0)),
            pl.BlockSpec((1, tm, GMLP_WIDTH), lambda i, b: (b, i, 0)),
        ],
        out_shape=[
            jax.ShapeDtypeStruct((nb, N_HEADS, seq, HEAD_PAD), BF16),
            jax.ShapeDtypeStruct((nb, N_HEADS, nck, HEAD_PAD, K_TILE), BF16),
            jax.ShapeDtypeStruct((nb, N_HEADS, seq, HEAD_PAD), BF16),
            jax.ShapeDtypeStruct((nb, seq, GMLP_WIDTH), BF16),
        ],
        compiler_params=_params(("arbitrary", "arbitrary")),
        name="premix",
    )(x, mod, *P["rope"][seq], P["g_pre1"], P["w_in"], P["g_q"], P["w_uq"], P["g_kv"], P["w_ukv"],
      P["g_v"], P["w_sp"], P["b_sp"], P["g_gmlp_out"])


def _attn_kernel(q_ref, kt_ref, v_ref, o_ref):
    n_chunks = kt_ref.shape[2]
    tk = kt_ref.shape[4]
    lane = lax.broadcasted_iota(jnp.int32, (1, LANES), 1)

    def one_head(hd, ones_lane):
        q = q_ref[0, hd]
        m = None
        acc = None
        for c in range(n_chunks):
            s = _dot(q, kt_ref[0, hd, c])
            v = v_ref[0, hd, c * tk:(c + 1) * tk, :]
            smax = jnp.max(s, axis=1, keepdims=True)
            if c == 0:
                m = smax
                acc = _dot(jnp.exp2((s - m).astype(BF16)), v)
            else:
                m_new = jnp.maximum(m, smax)
                acc = acc * jnp.exp2(m - m_new) + _dot(jnp.exp2((s - m_new).astype(BF16)), v)
                m = m_new
        row_sum = acc[:, ones_lane:ones_lane + 1]
        return acc * (1.0 / row_sum)

    def pair(j, carry):
        even = one_head(2 * j, V_HEAD)
        odd = one_head(2 * j + 1, 0)
        o_ref[0, j] = jnp.where(lane < V_HEAD, even, odd).astype(BF16)
        return carry

    lax.fori_loop(0, N_HEADS // 2, pair, 0)


def _attention(q, kt, v):
    nb, _, seq, _ = q.shape
    nck = kt.shape[2]
    return pl.pallas_call(
        _attn_kernel,
        grid=(nb, seq // Q_TILE),
        in_specs=[
            pl.BlockSpec((1, N_HEADS, Q_TILE, HEAD_PAD), lambda b, i: (b, 0, i, 0)),
            pl.BlockSpec((1, N_HEADS, nck, HEAD_PAD, K_TILE), lambda b, i: (b, 0, 0, 0, 0)),
            pl.BlockSpec((1, N_HEADS, seq, HEAD_PAD), lambda b, i: (b, 0, 0, 0)),
        ],
        out_specs=pl.BlockSpec((1, N_HEADS // 2, Q_TILE, LANES), lambda b, i: (b, 0, i, 0)),
        out_shape=jax.ShapeDtypeStruct((nb, N_HEADS // 2, seq, LANES), BF16),
        compiler_params=_params(("arbitrary", "arbitrary")),
        name="attn",
    )(q, kt, v)


_R_GROUP_ROW = N_EXPERTS


def _postmix_kernel(a_ref, sn_ref, x_ref, mod_ref, gao_ref, wout_ref, gpost_ref, gpre_ref, wr_ref, br_ref, tri_ref,
                    x1_ref, h2_ref, eid_ref, rank_ref, wcol_ref, cnt_ref, *, run_ref):
    mod = mod_ref[0]
    gate1, shift2, scale2 = mod[2:3], mod[3:4], mod[4:5]
    a = jnp.concatenate([a_ref[0, j] for j in range(N_HEADS // 2)], axis=1).astype(F32)
    an = (_rms(a) * gao_ref[...]).astype(BF16)
    merged = jnp.concatenate([an, sn_ref[0]], axis=1)
    o = _dot(merged, wout_ref[...])
    x1 = x_ref[0] + gate1 * (_rms(o) * gpost_ref[...])
    x1_ref[0] = x1
    h2 = _rms(x1) * (gpre_ref[...] * (1.0 + scale2)) + shift2
    n_tok = h2.shape[0]
    _store_token_rows(h2_ref, h2)

    h_hi, h_lo = _split_bf16(h2)
    wr = wr_ref[...]
    hh = _dot(h_hi, wr)
    lh = _dot(h_lo, wr[:, :LANES])
    logits = hh[:, :LANES] + hh[:, LANES:] + lh + br_ref[...]
    lt = logits.T

    neg = jnp.float32(-jnp.inf)
    row8 = lax.broadcasted_iota(jnp.int32, (SUBLANES, n_tok), 0).astype(F32)
    lg = jnp.where(row8 < N_EXPERT_GROUPS, lt[_R_GROUP_ROW:_R_GROUP_ROW + SUBLANES], neg)
    gmax = jnp.max(lg, axis=0, keepdims=True)
    gi = jnp.min(jnp.where(lg == gmax, row8, float(SUBLANES)), axis=0, keepdims=True)
    pg_sel = 1.0 / jnp.sum(jnp.exp(lg - gmax), axis=0, keepdims=True)

    le = jnp.zeros((EXPERTS_PER_GROUP, n_tok), F32)
    for g in range(N_EXPERT_GROUPS):
        le = jnp.where(gi == float(g), lt[g * EXPERTS_PER_GROUP:(g + 1) * EXPERTS_PER_GROUP], le)
    v1 = jnp.max(le, axis=0, keepdims=True)
    i1 = jnp.min(jnp.where(le == v1, row8, float(SUBLANES)), axis=0, keepdims=True)
    le2 = jnp.where(row8 == i1, neg, le)
    v2 = jnp.max(le2, axis=0, keepdims=True)
    i2 = jnp.min(jnp.where(le2 == v2, row8, float(SUBLANES)), axis=0, keepdims=True)
    r = jnp.exp(v2 - v1)
    w1 = pg_sel / (1.0 + r)
    w2 = w1 * r
    e1 = gi * float(EXPERTS_PER_GROUP) + i1
    e2 = gi * float(EXPERTS_PER_GROUP) + i2
    eid_ref[0] = jnp.concatenate([e1, e2], axis=0).astype(jnp.int32)

    row32 = lax.broadcasted_iota(jnp.int32, (N_EXPERTS, n_tok), 0).astype(F32)
    hit1 = row32 == e1
    hit2 = row32 == e2
    onehot = jnp.where(jnp.logical_or(hit1, hit2), 1.0, 0.0)
    run = run_ref[...][:, 0:1]
    ranks1, ranks2 = [], []
    for c in range(n_tok // RANK_CHUNK):
        sl = slice(c * RANK_CHUNK, (c + 1) * RANK_CHUNK)
        oh = onehot[:, sl]
        before = _dot(oh.astype(BF16), tri_ref[...]) + run
        ranks1.append(jnp.sum(jnp.where(hit1[:, sl], before, 0.0), axis=0, keepdims=True))
        ranks2.append(jnp.sum(jnp.where(hit2[:, sl], before, 0.0), axis=0, keepdims=True))
        run = run + jnp.sum(oh, axis=1, keepdims=True)
    rank_ref[0] = jnp.concatenate(
        [jnp.concatenate(ranks1, axis=1), jnp.concatenate(ranks2, axis=1)], axis=0).astype(jnp.int32)
    run_b = jnp.broadcast_to(run, run_ref.shape)
    run_ref[...] = run_b
    cnt_ref[...] = run_b

    row128 = lax.broadcasted_iota(jnp.int32, (LANES, n_tok), 0)
    wt = jnp.where(row128 == 0, w1, jnp.where(row128 == 1, w2, 0.0))
    wcol_ref[0] = wt.T


def _postmix(a, sn, x, mod, P):
    nb, seq, _ = x.shape
    tm = POST_TILE
    const = lambda b, i: (0, 0)
    deep = pl.Buffered(STREAM_BUFFERS)
    in_specs = [
        pl.BlockSpec((1, N_HEADS // 2, tm, LANES), lambda b, i: (b, 0, i, 0)),
        pl.BlockSpec((1, tm, GMLP_WIDTH), lambda b, i: (b, i, 0)),
        pl.BlockSpec((1, tm, D_MODEL), lambda b, i: (b, i, 0), pipeline_mode=deep),
        pl.BlockSpec((1, SUBLANES, D_MODEL), lambda b, i: (b, 0, 0)),
        pl.BlockSpec((1, N_HEADS * V_HEAD), const),
        pl.BlockSpec((D_MODEL, D_MODEL), const),
        pl.BlockSpec((1, D_MODEL), const),
        pl.BlockSpec((1, D_MODEL), const),
        pl.BlockSpec((D_MODEL, 2 * LANES), const),
        pl.BlockSpec((1, LANES), const),
        pl.BlockSpec((RANK_CHUNK, RANK_CHUNK), const),
    ]
    out_specs = [
        pl.BlockSpec((1, tm, D_MODEL), lambda b, i: (b, i, 0)),
        pl.BlockSpec((tm * ROW_TILES, LANES), lambda b, i: (b * (seq // tm) + i, 0)),
        pl.BlockSpec((1, 2, tm), lambda b, i: (b, 0, i)),
        pl.BlockSpec((1, 2, tm), lambda b, i: (b, 0, i)),
        pl.BlockSpec((1, tm, LANES), lambda b, i: (b, i, 0)),
        pl.BlockSpec((N_EXPERTS, LANES), const),
    ]
    n_in, n_out = len(in_specs), len(out_specs)

    def pipelined(*refs):
        hbm_refs, run_ref = refs[:n_in + n_out], refs[n_in + n_out]
        run_ref[...] = jnp.zeros_like(run_ref)
        pltpu.emit_pipeline(
            functools.partial(_postmix_kernel, run_ref=run_ref),
            grid=(nb, seq // tm), in_specs=in_specs, out_specs=out_specs,
        )(*hbm_refs)

    any_spec = pl.BlockSpec(memory_space=pl.ANY)
    return pl.pallas_call(
        pipelined,
        in_specs=[any_spec] * n_in,
        out_specs=[any_spec] * n_out,
        out_shape=[
            jax.ShapeDtypeStruct((nb, seq, D_MODEL), F32),
            jax.ShapeDtypeStruct((nb * seq * ROW_TILES, LANES), U32),
            jax.ShapeDtypeStruct((nb, 2, seq), jnp.int32),
            jax.ShapeDtypeStruct((nb, 2, seq), jnp.int32),
            jax.ShapeDtypeStruct((nb, seq, LANES), F32),
            jax.ShapeDtypeStruct((N_EXPERTS, LANES), F32),
        ],
        scratch_shapes=[pltpu.VMEM((N_EXPERTS, LANES), F32)],
        compiler_params=pltpu.CompilerParams(vmem_limit_bytes=VMEM_LIMIT),
        name="postmix",
    )(a, sn, x, mod, P["g_attn_out"], P["w_out"], P["g_post1"], P["g_pre2"], P["w_router"], P["b_router"], P["tri"])


def _sc_mesh():
    return plsc.VectorSubcoreMesh(core_axis_name="c", subcore_axis_name="s")


def _sc_worker():
    return lax.axis_index("s") * SC_CORES + lax.axis_index("c")


def _sc_gather_rows(table, idx):
    n = idx.shape[0]
    assert n % (SC_WORKERS * SC_ROWS) == 0
    per_w = n // SC_WORKERS
    steps = per_w // SC_ROWS
    idx3 = idx.reshape(SC_WORKERS, steps, SC_ROWS)

    @functools.partial(
        pl.kernel, mesh=_sc_mesh(),
        out_type=jax.ShapeDtypeStruct((n, ROW_TILES, LANES), U32),
        scratch_types=[pltpu.VMEM((steps, SC_ROWS), jnp.int32), pltpu.VMEM((SC_ROWS, ROW_TILES, LANES), U32),
                       pltpu.SemaphoreType.DMA],
        name="sc_gather_rows",
    )
    def gather(table_hbm, idx_hbm, out_hbm, idx_v, rows_v, sem):
        wid = _sc_worker()
        pltpu.sync_copy(idx_hbm.at[wid], idx_v)

        @pl.loop(0, steps)
        def _(j):
            pltpu.async_copy(table_hbm.at[idx_v.at[j]], rows_v, sem).wait()
            pltpu.sync_copy(rows_v, out_hbm.at[pl.ds(wid * per_w + j * SC_ROWS, SC_ROWS)])

    return gather(table, idx3)


def _sc_scatter_rows(src, idx, n_dst):
    n_dup, n = idx.shape
    assert n % (SC_WORKERS * SC_ROWS) == 0
    per_w = n // SC_WORKERS
    steps = per_w // SC_ROWS
    idx4 = idx.reshape(n_dup, SC_WORKERS, steps, SC_ROWS).transpose(1, 0, 2, 3)

    @functools.partial(
        pl.kernel, mesh=_sc_mesh(),
        out_type=jax.ShapeDtypeStruct((n_dst, ROW_TILES, LANES), U32),
        scratch_types=[pltpu.VMEM((n_dup, steps, SC_ROWS), jnp.int32), pltpu.VMEM((SC_ROWS, ROW_TILES, LANES), U32),
                       pltpu.SemaphoreType.DMA],
        name="sc_scatter_rows",
    )
    def scatter(src_hbm, idx_hbm, dst_hbm, idx_v, rows_v, sem):
        wid = _sc_worker()
        pltpu.sync_copy(idx_hbm.at[wid], idx_v)

        @pl.loop(0, steps)
        def _(j):
            pltpu.sync_copy(src_hbm.at[pl.ds(wid * per_w + j * SC_ROWS, SC_ROWS)], rows_v)
            copies = [pltpu.async_copy(rows_v, dst_hbm.at[idx_v.at[k, j]], sem) for k in range(n_dup)]
            for copy in copies:
                copy.wait()

    return scatter(src, idx4)


def _expert_kernel(te_ref, tr_ref, nv_ref, xs_ref, wg_ref, wu_ref, wd_ref, ys_ref, wgu_bf, wd_bf):
    i = pl.program_id(0)
    valid = i < nv_ref[0]
    new_expert = jnp.logical_or(i == 0, te_ref[i] != te_ref[jnp.maximum(i - 1, 0)])

    @pl.when(jnp.logical_and(valid, new_expert))
    def _():
        wgu_bf[:, :EXPERT_FF] = wg_ref[0].astype(BF16)
        wgu_bf[:, EXPERT_FF:] = wu_ref[0].astype(BF16)
        wd_bf[...] = wd_ref[0].astype(BF16)

    @pl.when(valid)
    def _():
        n_rows = xs_ref.shape[0] // ROW_TILES
        x = _load_token_rows(xs_ref, n_rows)
        row = lax.broadcasted_iota(jnp.int32, (n_rows, 1), 0)
        x = jnp.where(row < tr_ref[i], x, 0.0).astype(BF16)
        gu = _dot(x, wgu_bf[...])
        g, u = gu[:, :EXPERT_FF], gu[:, EXPERT_FF:]
        act = (g * jax.nn.sigmoid(g) * u).astype(BF16)
        _store_token_rows(ys_ref, _dot(act, wd_bf[...]))

    @pl.when(jnp.logical_not(valid))
    def _():
        ys_ref[...] = jnp.zeros_like(ys_ref)


def _expert_tile(n_tok):
    return EXPERT_TILE_LARGE if 2 * n_tok // N_EXPERTS >= 2 * EXPERT_TILE_LARGE else EXPERT_TILE_SMALL


def _experts(xs, tile_expert, tile_rows, n_valid, tile, P):
    rows_blk = tile * ROW_TILES
    assert xs.shape[0] % rows_blk == 0
    n_tiles = xs.shape[0] // rows_blk

    def row_map(i, te, tr, nv):
        return (jnp.minimum(i, nv[0] - 1), 0)

    def out_map(i, te, tr, nv):
        return (i, 0)

    def w_map(i, te, tr, nv):
        return (te[jnp.minimum(i, nv[0] - 1)], 0, 0)

    return pl.pallas_call(
        _expert_kernel,
        grid_spec=pltpu.PrefetchScalarGridSpec(
            num_scalar_prefetch=3,
            grid=(n_tiles,),
            in_specs=[
                pl.BlockSpec((rows_blk, LANES), row_map),
                pl.BlockSpec((1, D_MODEL, EXPERT_FF), w_map),
                pl.BlockSpec((1, D_MODEL, EXPERT_FF), w_map),
                pl.BlockSpec((1, EXPERT_FF, D_MODEL), w_map),
            ],
            out_specs=pl.BlockSpec((rows_blk, LANES), out_map),
            scratch_shapes=[pltpu.VMEM((D_MODEL, 2 * EXPERT_FF), BF16), pltpu.VMEM((EXPERT_FF, D_MODEL), BF16)],
        ),
        out_shape=jax.ShapeDtypeStruct(xs.shape, U32),
        compiler_params=_params(("arbitrary",)),
        name="experts",
    )(tile_expert, tile_rows, n_valid, xs, P["w_gate"], P["w_up"], P["w_down"])


def _final_kernel(y0_ref, y1_ref, wcol_ref, x1_ref, mod_ref, gpost_ref, o_ref):
    w = wcol_ref[0]
    w0, w1 = w[:, 0:1], w[:, 1:2]
    n_tok = w.shape[0]
    m = w0 * _load_token_rows(y0_ref, n_tok) + w1 * _load_token_rows(y1_ref, n_tok)
    gate2 = mod_ref[0][5:6]
    o_ref[0] = x1_ref[0] + gate2 * (_rms(m) * gpost_ref[...])


def _final(yg, wcol, x1, mod, P):
    nb, seq, _ = x1.shape
    tm = POST_TILE
    nt = seq // tm
    n_tok_tiles = nb * nt
    deep = pl.Buffered(STREAM_BUFFERS)

    def pipelined(yg_hbm, wcol_hbm, x1_hbm, mod_hbm, gpost_hbm, o_hbm):
        pltpu.emit_pipeline(
            _final_kernel,
            grid=(nb, nt),
            in_specs=[
                pl.BlockSpec((tm * ROW_TILES, LANES), lambda b, i: (b * nt + i, 0), pipeline_mode=deep),
                pl.BlockSpec((tm * ROW_TILES, LANES), lambda b, i: (n_tok_tiles + b * nt + i, 0), pipeline_mode=deep),
                pl.BlockSpec((1, tm, LANES), lambda b, i: (b, i, 0), pipeline_mode=deep),
                pl.BlockSpec((1, tm, D_MODEL), lambda b, i: (b, i, 0), pipeline_mode=deep),
                pl.BlockSpec((1, SUBLANES, D_MODEL), lambda b, i: (b, 0, 0)),
                pl.BlockSpec((1, D_MODEL), lambda b, i: (0, 0)),
            ],
            out_specs=[pl.BlockSpec((1, tm, D_MODEL), lambda b, i: (b, i, 0))],
        )(yg_hbm, yg_hbm, wcol_hbm, x1_hbm, mod_hbm, gpost_hbm, o_hbm)

    any_spec = pl.BlockSpec(memory_space=pl.ANY)
    return pl.pallas_call(
        pipelined,
        in_specs=[any_spec] * 5,
        out_specs=any_spec,
        out_shape=jax.ShapeDtypeStruct((nb, seq, D_MODEL), F32),
        compiler_params=pltpu.CompilerParams(vmem_limit_bytes=VMEM_LIMIT),
        name="final",
    )(yg, wcol, x1, mod, P["g_post2"])


def _prepare(w):
    f = lambda a: a.astype(F32)
    P = {}
    for k in ("g_pre1", "g_post1", "g_pre2", "g_post2", "g_q", "g_kv", "g_attn_out", "g_gmlp_out"):
        P[k] = f(w[k]).reshape(1, -1)
    P["g_v"] = f(w["g_v_gmlp"]).reshape(1, -1)

    w_in = f(w["w_in"])
    o0, o1, o2, o3 = Q_LORA, Q_LORA + KV_LORA, Q_LORA + KV_LORA + QK_ROPE, Q_LORA + KV_LORA + QK_ROPE + GMLP_WIDTH
    w_kr = w_in[:, o1:o2]
    kr_partner = jnp.concatenate([-w_kr[:, ROPE_HALF:], w_kr[:, :ROPE_HALF]], axis=1)
    rope_blk = jnp.concatenate([jnp.zeros((D_MODEL, QK_NOPE), F32), w_kr, kr_partner], axis=1)
    P["w_in"] = jnp.concatenate([w_in[:, :o1], rope_blk, w_in[:, o2:o3], w_in[:, o3:]], axis=1).astype(BF16)

    w_uq = f(w["w_uq"]).reshape(Q_LORA, N_HEADS, QK_NOPE + QK_ROPE)
    q_rope = w_uq[:, :, QK_NOPE:]
    q_partner = jnp.concatenate([-q_rope[:, :, ROPE_HALF:], q_rope[:, :, :ROPE_HALF]], axis=2)
    P["w_uq"] = jnp.concatenate([w_uq, q_partner], axis=2).reshape(Q_LORA, N_HEADS * HEAD_PAD).astype(BF16)

    w_ukv = f(w["w_ukv"]).reshape(KV_LORA, N_HEADS, QK_NOPE + V_HEAD)
    zeros = jnp.zeros((KV_LORA, N_HEADS, HEAD_PAD - QK_NOPE), F32)
    w_k = jnp.concatenate([w_ukv[:, :, :QK_NOPE], zeros], axis=2)
    w_v = w_ukv[:, :, QK_NOPE:]
    even = (jnp.arange(N_HEADS) % 2 == 0)[None, :, None]
    zv = jnp.zeros_like(w_v)
    w_v = jnp.concatenate([jnp.where(even, w_v, zv), jnp.where(even, zv, w_v)], axis=2)
    P["w_ukv"] = jnp.concatenate([w_k.reshape(KV_LORA, -1), w_v.reshape(KV_LORA, -1)], axis=1).astype(BF16)

    P["w_sp"] = f(w["w_spatial"]).reshape(GMLP_GROUPS // 2, 2 * CHUNK, CHUNK).astype(BF16)
    P["b_sp"] = jnp.repeat(f(w["b_spatial"]).T, GMLP_GROUP_DIM, axis=1)

    P["w_out"] = f(w["w_out"]).astype(BF16)

    pad = jnp.zeros((D_MODEL, LANES - N_EXPERTS - N_EXPERT_GROUPS), F32)
    wr = jnp.concatenate([f(w["w_router_expert"]), f(w["w_router_group"]), pad], axis=1)
    wr_hi = wr.astype(BF16)
    wr_lo = (wr - wr_hi.astype(F32)).astype(BF16)
    P["w_router"] = jnp.concatenate([wr_hi, wr_lo], axis=1)
    P["b_router"] = jnp.concatenate(
        [f(w["b_router_expert"]), f(w["b_router_group"]), jnp.zeros((LANES - N_EXPERTS - N_EXPERT_GROUPS,), F32)]
    ).reshape(1, LANES)
    P["tri"] = jnp.triu(jnp.ones((RANK_CHUNK, RANK_CHUNK), F32), k=1).astype(BF16)

    P["w_gate"], P["w_up"], P["w_down"] = f(w["w_gate"]), f(w["w_up"]), f(w["w_down"])
    return P


def _rope_tables(seq):
    inv = ROPE_THETA ** (-jnp.arange(ROPE_HALF, dtype=F32) / ROPE_HALF)
    ang = jnp.arange(seq, dtype=F32)[:, None] * inv[None, :]
    z_lo = jnp.zeros((seq, _ROPE_LO), F32)
    z_hi = jnp.zeros((seq, LANES - _ROPE_LO - QK_ROPE), F32)
    cos = jnp.concatenate([z_lo, jnp.cos(ang), jnp.cos(ang), z_hi], axis=1)
    sin = jnp.concatenate([z_lo, jnp.sin(ang), jnp.sin(ang), z_hi], axis=1)
    return cos, sin


def _layer(x, mod, P):
    nb, seq, d_model = x.shape
    assert d_model == D_MODEL and TOKEN_TILE % K_TILE == 0
    assert seq % TOKEN_TILE == 0 and seq % Q_TILE == 0 and seq % POST_TILE == 0 and POST_TILE % RANK_CHUNK == 0
    n_tok = nb * seq
    q, kt, v, sn = _premix(x, mod, P)
    a = _attention(q, kt, v)
    x1, h2rows, eid, rank, wcol, counts = _postmix(a, sn, x, mod, P)

    tile = _expert_tile(n_tok)
    cnt = counts[:, 0].astype(jnp.int32)
    padded = ((cnt + tile - 1) // tile) * tile
    ends = jnp.cumsum(padded)
    starts = ends - padded
    eflat = jnp.transpose(eid, (1, 0, 2)).reshape(2, n_tok)
    rflat = jnp.transpose(rank, (1, 0, 2)).reshape(2, n_tok)
    onehot = eflat[:, :, None] == jnp.arange(N_EXPERTS, dtype=jnp.int32)[None, None, :]
    pos = rflat + jnp.sum(jnp.where(onehot, starts[None, None, :], 0), axis=2)
    n_rows = 2 * n_tok + N_EXPERTS * tile
    n_tiles = n_rows // tile
    tile_start = jnp.arange(n_tiles, dtype=jnp.int32) * tile
    tile_expert = jnp.minimum(
        jnp.sum((tile_start[:, None] >= ends[None, :]).astype(jnp.int32), axis=1), N_EXPERTS - 1).astype(jnp.int32)
    expert_ids = jnp.arange(N_EXPERTS, dtype=jnp.int32)
    live_end = jnp.sum(jnp.where(tile_expert[:, None] == expert_ids[None, :], (starts + cnt)[None, :], 0), axis=1)
    tile_rows = jnp.clip(live_end - tile_start, 0, tile).astype(jnp.int32)
    n_valid = (ends[-1:] // tile).astype(jnp.int32)

    as_tiles = lambda a: a.reshape(-1, ROW_TILES, LANES)
    as_rows = lambda a: a.reshape(-1, LANES)
    xs = _sc_scatter_rows(as_tiles(h2rows), pos, n_rows)
    ys = _experts(as_rows(xs), tile_expert, tile_rows, n_valid, tile, P)
    yg = as_rows(_sc_gather_rows(as_tiles(ys), pos.reshape(2 * n_tok)))
    return _final(yg, wcol, x1, mod, P)


def kernel(x_prompt, x_sample, c_prompt, c_sample, w_ada, b_ada, g_pre1, g_post1, g_pre2, g_post2, w_in, g_q, w_uq,
           g_kv, w_ukv, g_v_gmlp, w_spatial, b_spatial, g_attn_out, g_gmlp_out, w_out, w_router_group,
           b_router_group, w_router_expert, b_router_expert, w_gate, w_up, w_down):
    P = _prepare(dict(
        g_pre1=g_pre1, g_post1=g_post1, g_pre2=g_pre2, g_post2=g_post2, w_in=w_in, g_q=g_q, w_uq=w_uq, g_kv=g_kv,
        w_ukv=w_ukv, g_v_gmlp=g_v_gmlp, w_spatial=w_spatial, b_spatial=b_spatial, g_attn_out=g_attn_out,
        g_gmlp_out=g_gmlp_out, w_out=w_out, w_router_group=w_router_group, b_router_group=b_router_group,
        w_router_expert=w_router_expert, b_router_expert=b_router_expert, w_gate=w_gate, w_up=w_up, w_down=w_down))
    P["rope"] = {seq: _rope_tables(seq) for seq in {x_prompt.shape[1], x_sample.shape[1]}}

    nbp = c_prompt.shape[0]
    c_all = jnp.concatenate([c_prompt, c_sample], axis=0).astype(F32)
    mod = _ada(c_all, w_ada.astype(F32), b_ada.astype(F32))
    mod = mod.reshape(c_all.shape[0], 6, D_MODEL)
    mod = jnp.concatenate([mod, jnp.zeros((c_all.shape[0], SUBLANES - 6, D_MODEL), F32)], axis=1)

    y_prompt = _layer(x_prompt, mod[:nbp], P)
    y_sample = _layer(x_sample, mod[nbp:], P)
    return (y_prompt, y_sample)
```

```python
import functools
import math

import jax
import jax.numpy as jnp
from jax import lax
from jax.experimental import pallas as pl
from jax.experimental.pallas import tpu as pltpu
from jax.experimental.pallas import tpu_sc as plsc

F32 = jnp.float32
BF16 = jnp.bfloat16

D_MODEL = 1024
N_HEADS = 8
QK_NOPE = 64
QK_ROPE = 32
ROPE_HALF = QK_ROPE // 2
V_HEAD = 64
Q_LORA = 256
KV_LORA = 128
GMLP_WIDTH = 512
GMLP_GROUPS = 8
GMLP_GROUP_DIM = 64
CHUNK = 128
N_EXPERTS = 32
N_EXPERT_GROUPS = 4
EXPERTS_PER_GROUP = 8
EXPERT_FF = 256
ROPE_THETA = 10000.0
EPS = 1e-6

LANES = 128
SUBLANES = 8
HEAD_PAD = LANES

TOKEN_TILE = 1024
POST_TILE = 1024
Q_TILE = 1024
K_TILE = 512
EXPERT_TILE_SMALL = 512
EXPERT_TILE_LARGE = 1024
SC_CORES = 2
SC_WORKERS = 32
SC_ROWS = 128
RANK_CHUNK = 256
FINAL_INPUT_BUFFERS = 3
VMEM_LIMIT = 56 * 1024 * 1024

U32 = jnp.uint32
PACKED_WIDTH = D_MODEL // 2
ROW_TILES = PACKED_WIDTH // LANES
_HI_MASK = 0xFFFF0000

_SQRT_2_OVER_PI = math.sqrt(2.0 / math.pi)


def _rms(x):
    return x * lax.rsqrt(jnp.mean(x * x, axis=-1, keepdims=True) + EPS)


def _gelu_tanh(x):
    return 0.5 * x * (1.0 + jnp.tanh(_SQRT_2_OVER_PI * (x + 0.044715 * (x * x * x))))


def _split_bf16(x):
    hi = x.astype(BF16)
    lo = (x - hi.astype(F32)).astype(BF16)
    return hi, lo


def _dot(a, b):
    return jnp.dot(a, b, preferred_element_type=F32)


def _bf16_bits(x):
    return lax.bitcast_convert_type(x.astype(BF16).astype(F32), U32)


def _load_token_rows(ref, n):
    w = jnp.concatenate([ref[pl.ds(c, n, stride=ROW_TILES), :] for c in range(ROW_TILES)], axis=1)
    lo = lax.bitcast_convert_type(w << 16, F32)
    hi = lax.bitcast_convert_type(w & jnp.uint32(_HI_MASK), F32)
    return jnp.concatenate([lo, hi], axis=1)


def _store_token_rows(ref, val):
    n = val.shape[0]
    w = (_bf16_bits(val[:, :PACKED_WIDTH]) >> 16) | (_bf16_bits(val[:, PACKED_WIDTH:]) & jnp.uint32(_HI_MASK))
    for c in range(ROW_TILES):
        ref[pl.ds(c, n, stride=ROW_TILES), :] = w[:, c * LANES:(c + 1) * LANES]


def _params(sem, vmem=VMEM_LIMIT):
    return pltpu.CompilerParams(dimension_semantics=sem, vmem_limit_bytes=vmem)


def _ada_kernel(c_ref, w_ref, b_ref, o_ref):
    c = c_ref[...]
    a = c * jax.nn.sigmoid(c)
    a_hi, a_lo = _split_bf16(a)
    w_hi, w_lo = _split_bf16(w_ref[...])
    o_ref[...] = _dot(a_hi, w_hi) + _dot(a_hi, w_lo) + _dot(a_lo, w_hi) + b_ref[...]


def _ada(c, w_ada, b_ada):
    nb = c.shape[0]
    n_out = w_ada.shape[1]
    blk = D_MODEL
    return pl.pallas_call(
        _ada_kernel,
        grid=(n_out // blk,),
        in_specs=[
            pl.BlockSpec((nb, D_MODEL), lambda j: (0, 0)),
            pl.BlockSpec((D_MODEL, blk), lambda j: (0, j)),
            pl.BlockSpec((1, blk), lambda j: (0, j)),
        ],
        out_specs=pl.BlockSpec((nb, blk), lambda j: (0, j)),
        out_shape=jax.ShapeDtypeStruct((nb, n_out), F32),
        compiler_params=_params(("arbitrary",)),
        name="ada",
    )(c, w_ada, b_ada.reshape(1, n_out))


_C_CQ = 0
_C_CKV = _C_CQ + Q_LORA
_C_KR = _C_CKV + KV_LORA
_C_U = _C_KR + LANES
_C_V = _C_U + GMLP_WIDTH
_C_END = _C_V + GMLP_WIDTH
_ROPE_LO = QK_NOPE
_ROLL_PARTNER = LANES - QK_ROPE


def _premix_kernel(x_ref, mod_ref, cos_ref, sin_ref, gpre_ref, win_ref, gq_ref, wuq_ref, gkv_ref, wukv_ref,
                   gv_ref, wsp_ref, bsp_ref, ggo_ref, q_ref, kt_ref, v_ref, sn_ref):
    x = x_ref[0]
    mod = mod_ref[0]
    shift1, scale1 = mod[0:1], mod[1:2]
    h = _rms(x) * (gpre_ref[...] * (1.0 + scale1)) + shift1
    z = _dot(h.astype(BF16), win_ref[...])

    cosb = cos_ref[...]
    sinb = sin_ref[...]
    lane = lax.broadcasted_iota(jnp.int32, (1, LANES), 1)
    nope_mask = jnp.where(lane < QK_NOPE, 1.0, 0.0).astype(F32)

    qscale = (QK_NOPE + QK_ROPE) ** -0.5 * math.log2(math.e)
    cq_tab = (nope_mask + cosb) * qscale
    sq_tab = sinb * qscale
    cqn = (_rms(z[:, _C_CQ:_C_CKV]) * gq_ref[...]).astype(BF16)
    qb = _dot(cqn, wuq_ref[...])
    for hd in range(N_HEADS):
        blk = qb[:, hd * HEAD_PAD:(hd + 1) * HEAD_PAD]
        qh = blk * cq_tab + pltpu.roll(blk, _ROLL_PARTNER, 1) * sq_tab
        q_ref[0, hd] = qh.astype(BF16)

    ckvn = (_rms(z[:, _C_CKV:_C_KR]) * gkv_ref[...]).astype(BF16)
    kvb = _dot(ckvn, wukv_ref[...])
    krb = z[:, _C_KR:_C_U]
    krope = krb * cosb + pltpu.roll(krb, _ROLL_PARTNER, 1) * sinb
    v_off = N_HEADS * HEAD_PAD
    for hd in range(N_HEADS):
        kh = kvb[:, hd * HEAD_PAD:(hd + 1) * HEAD_PAD] + krope
        for c in range(kt_ref.shape[2]):
            kt_ref[0, hd, c] = kh[c * K_TILE:(c + 1) * K_TILE].T.astype(BF16)
        ones_lane = V_HEAD if hd % 2 == 0 else 0
        vh = kvb[:, v_off + hd * HEAD_PAD:v_off + (hd + 1) * HEAD_PAD] + jnp.where(lane == ones_lane, 1.0, 0.0)
        v_ref[0, hd] = vh.astype(BF16)

    ua = _gelu_tanh(z[:, _C_U:_C_V])
    vn = (_rms(_gelu_tanh(z[:, _C_V:_C_END])) * gv_ref[...]).astype(BF16)
    n_tok = x.shape[0]
    bsp = bsp_ref[...]
    rows = []
    for n in range(n_tok // CHUNK):
        cols = []
        for j in range(GMLP_GROUPS // 2):
            rhs = vn[n * CHUNK:(n + 1) * CHUNK, j * LANES:(j + 1) * LANES]
            ab = _dot(wsp_ref[j], rhs)
            cols.append(jnp.where(lane < GMLP_GROUP_DIM, ab[:CHUNK], ab[CHUNK:]))
        rows.append(jnp.concatenate(cols, axis=1) + bsp)
    s = ua * jnp.concatenate(rows, axis=0)
    sn_ref[0] = (_rms(s) * ggo_ref[...]).astype(BF16)


def _premix(x, mod, P):
    nb, seq, _ = x.shape
    tm = TOKEN_TILE
    nck = seq // K_TILE
    const = lambda i, b: (0, 0)
    return pl.pallas_call(
        _premix_kernel,
        grid=(seq // tm, nb),
        in_specs=[
            pl.BlockSpec((1, tm, D_MODEL), lambda i, b: (b, i, 0)),
            pl.BlockSpec((1, SUBLANES, D_MODEL), lambda i, b: (b, 0, 0)),
            pl.BlockSpec((tm, LANES), lambda i, b: (i, 0)),
            pl.BlockSpec((tm, LANES), lambda i, b: (i, 0)),
            pl.BlockSpec((1, D_MODEL), const),
            pl.BlockSpec((D_MODEL, _C_END), const),
            pl.BlockSpec((1, Q_LORA), const),
            pl.BlockSpec((Q_LORA, N_HEADS * HEAD_PAD), const),
            pl.BlockSpec((1, KV_LORA), const),
            pl.BlockSpec((KV_LORA, 2 * N_HEADS * HEAD_PAD), const),
            pl.BlockSpec((1, GMLP_WIDTH), const),
            pl.BlockSpec((GMLP_GROUPS // 2, 2 * CHUNK, CHUNK), lambda i, b: (0, 0, 0)),
            pl.BlockSpec((CHUNK, GMLP_WIDTH), const),
            pl.BlockSpec((1, GMLP_WIDTH), const),
        ],
        out_specs=[
            pl.BlockSpec((1, N_HEADS, tm, HEAD_PAD), lambda i, b: (b, 0, i, 0)),
            pl.BlockSpec((1, N_HEADS, tm // K_TILE, HEAD_PAD, K_TILE), lambda i, b: (b, 0, i, 0, 0)),
            pl.BlockSpec((1, N_HEADS, tm, HEAD_PAD), lambda i, b: (b, 0, i, 0)),
            pl.BlockSpec((1, tm, GMLP_WIDTH), lambda i, b: (b, i, 0)),
        ],
        out_shape=[
            jax.ShapeDtypeStruct((nb, N_HEADS, seq, HEAD_PAD), BF16),
            jax.ShapeDtypeStruct((nb, N_HEADS, nck, HEAD_PAD, K_TILE), BF16),
            jax.ShapeDtypeStruct((nb, N_HEADS, seq, HEAD_PAD), BF16),
            jax.ShapeDtypeStruct((nb, seq, GMLP_WIDTH), BF16),
        ],
        compiler_params=_params(("arbitrary", "arbitrary")),
        name="premix",
    )(x, mod, *P["rope"][seq], P["g_pre1"], P["w_in"], P["g_q"], P["w_uq"], P["g_kv"], P["w_ukv"],
      P["g_v"], P["w_sp"], P["b_sp"], P["g_gmlp_out"])


def _attn_kernel(q_ref, kt_ref, v_ref, o_ref):
    n_chunks = kt_ref.shape[2]
    tk = kt_ref.shape[4]
    lane = lax.broadcasted_iota(jnp.int32, (1, LANES), 1)

    def one_head(hd, ones_lane):
        q = q_ref[0, hd]
        m = None
        acc = None
        for c in range(n_chunks):
            s = _dot(q, kt_ref[0, hd, c])
            v = v_ref[0, hd, c * tk:(c + 1) * tk, :]
            smax = jnp.max(s, axis=1, keepdims=True)
            if c == 0:
                m = smax
                acc = _dot(jnp.exp2((s - m).astype(BF16)), v)
            else:
                m_new = jnp.maximum(m, smax)
                acc = acc * jnp.exp2(m - m_new) + _dot(jnp.exp2((s - m_new).astype(BF16)), v)
                m = m_new
        row_sum = acc[:, ones_lane:ones_lane + 1]
        return acc * (1.0 / row_sum)

    def pair(j, carry):
        even = one_head(2 * j, V_HEAD)
        odd = one_head(2 * j + 1, 0)
        o_ref[0, j] = jnp.where(lane < V_HEAD, even, odd).astype(BF16)
        return carry

    lax.fori_loop(0, N_HEADS // 2, pair, 0)


def _attention(q, kt, v):
    nb, _, seq, _ = q.shape
    nck = kt.shape[2]
    return pl.pallas_call(
        _attn_kernel,
        grid=(nb, seq // Q_TILE),
        in_specs=[
            pl.BlockSpec((1, N_HEADS, Q_TILE, HEAD_PAD), lambda b, i: (b, 0, i, 0)),
            pl.BlockSpec((1, N_HEADS, nck, HEAD_PAD, K_TILE), lambda b, i: (b, 0, 0, 0, 0)),
            pl.BlockSpec((1, N_HEADS, seq, HEAD_PAD), lambda b, i: (b, 0, 0, 0)),
        ],
        out_specs=pl.BlockSpec((1, N_HEADS // 2, Q_TILE, LANES), lambda b, i: (b, 0, i, 0)),
        out_shape=jax.ShapeDtypeStruct((nb, N_HEADS // 2, seq, LANES), BF16),
        compiler_params=_params(("arbitrary", "arbitrary")),
        name="attn",
    )(q, kt, v)


_R_GROUP_ROW = N_EXPERTS


def _postmix_kernel(a_ref, sn_ref, x_ref, mod_ref, gao_ref, wout_ref, gpost_ref, gpre_ref, wr_ref, br_ref, tri_ref,
                    x1_ref, h2_ref, eid_ref, rank_ref, wcol_ref, cnt_ref, run_ref):
    first = jnp.logical_and(pl.program_id(0) == 0, pl.program_id(1) == 0)

    @pl.when(first)
    def _():
        run_ref[...] = jnp.zeros_like(run_ref)

    mod = mod_ref[0]
    gate1, shift2, scale2 = mod[2:3], mod[3:4], mod[4:5]
    a = jnp.concatenate([a_ref[0, j] for j in range(N_HEADS // 2)], axis=1).astype(F32)
    an = (_rms(a) * gao_ref[...]).astype(BF16)
    merged = jnp.concatenate([an, sn_ref[0]], axis=1)
    o = _dot(merged, wout_ref[...])
    x1 = x_ref[0] + gate1 * (_rms(o) * gpost_ref[...])
    x1_ref[0] = x1
    h2 = _rms(x1) * (gpre_ref[...] * (1.0 + scale2)) + shift2
    n_tok = h2.shape[0]
    _store_token_rows(h2_ref, h2)

    h_hi, h_lo = _split_bf16(h2)
    wr = wr_ref[...]
    hh = _dot(h_hi, wr)
    lh = _dot(h_lo, wr[:, :LANES])
    logits = hh[:, :LANES] + hh[:, LANES:] + lh + br_ref[...]
    lt = logits.T

    neg = jnp.float32(-jnp.inf)
    row8 = lax.broadcasted_iota(jnp.int32, (SUBLANES, n_tok), 0).astype(F32)
    lg = jnp.where(row8 < N_EXPERT_GROUPS, lt[_R_GROUP_ROW:_R_GROUP_ROW + SUBLANES], neg)
    gmax = jnp.max(lg, axis=0, keepdims=True)
    gi = jnp.min(jnp.where(lg == gmax, row8, float(SUBLANES)), axis=0, keepdims=True)
    pg_sel = 1.0 / jnp.sum(jnp.exp(lg - gmax), axis=0, keepdims=True)

    le = jnp.zeros((EXPERTS_PER_GROUP, n_tok), F32)
    for g in range(N_EXPERT_GROUPS):
        le = jnp.where(gi == float(g), lt[g * EXPERTS_PER_GROUP:(g + 1) * EXPERTS_PER_GROUP], le)
    v1 = jnp.max(le, axis=0, keepdims=True)
    i1 = jnp.min(jnp.where(le == v1, row8, float(SUBLANES)), axis=0, keepdims=True)
    le2 = jnp.where(row8 == i1, neg, le)
    v2 = jnp.max(le2, axis=0, keepdims=True)
    i2 = jnp.min(jnp.where(le2 == v2, row8, float(SUBLANES)), axis=0, keepdims=True)
    r = jnp.exp(v2 - v1)
    w1 = pg_sel / (1.0 + r)
    w2 = w1 * r
    e1 = gi * float(EXPERTS_PER_GROUP) + i1
    e2 = gi * float(EXPERTS_PER_GROUP) + i2
    eid_ref[0] = jnp.concatenate([e1, e2], axis=0).astype(jnp.int32)

    row32 = lax.broadcasted_iota(jnp.int32, (N_EXPERTS, n_tok), 0).astype(F32)
    hit1 = row32 == e1
    hit2 = row32 == e2
    onehot = jnp.where(jnp.logical_or(hit1, hit2), 1.0, 0.0)
    run = run_ref[...][:, 0:1]
    ranks1, ranks2 = [], []
    for c in range(n_tok // RANK_CHUNK):
        sl = slice(c * RANK_CHUNK, (c + 1) * RANK_CHUNK)
        oh = onehot[:, sl]
        before = _dot(oh.astype(BF16), tri_ref[...]) + run
        ranks1.append(jnp.sum(jnp.where(hit1[:, sl], before, 0.0), axis=0, keepdims=True))
        ranks2.append(jnp.sum(jnp.where(hit2[:, sl], before, 0.0), axis=0, keepdims=True))
        run = run + jnp.sum(oh, axis=1, keepdims=True)
    rank_ref[0] = jnp.concatenate(
        [jnp.concatenate(ranks1, axis=1), jnp.concatenate(ranks2, axis=1)], axis=0).astype(jnp.int32)
    run_b = jnp.broadcast_to(run, run_ref.shape)
    run_ref[...] = run_b
    cnt_ref[...] = run_b

    wcol_ref[0] = jnp.where(row8 == 0.0, w1, jnp.where(row8 == 1.0, w2, 0.0))


def _postmix(a, sn, x, mod, P):
    nb, seq, _ = x.shape
    tm = POST_TILE
    const = lambda b, i: (0, 0)
    return pl.pallas_call(
        _postmix_kernel,
        grid=(nb, seq // tm),
        in_specs=[
            pl.BlockSpec((1, N_HEADS // 2, tm, LANES), lambda b, i: (b, 0, i, 0)),
            pl.BlockSpec((1, tm, GMLP_WIDTH), lambda b, i: (b, i, 0)),
            pl.BlockSpec((1, tm, D_MODEL), lambda b, i: (b, i, 0)),
            pl.BlockSpec((1, SUBLANES, D_MODEL), lambda b, i: (b, 0, 0)),
            pl.BlockSpec((1, N_HEADS * V_HEAD), const),
            pl.BlockSpec((D_MODEL, D_MODEL), const),
            pl.BlockSpec((1, D_MODEL), const),
            pl.BlockSpec((1, D_MODEL), const),
            pl.BlockSpec((D_MODEL, 2 * LANES), const),
            pl.BlockSpec((1, LANES), const),
            pl.BlockSpec((RANK_CHUNK, RANK_CHUNK), const),
        ],
        out_specs=[
            pl.BlockSpec((1, tm, D_MODEL), lambda b, i: (b, i, 0)),
            pl.BlockSpec((tm * ROW_TILES, LANES), lambda b, i: (b * (seq // tm) + i, 0)),
            pl.BlockSpec((1, 2, tm), lambda b, i: (b, 0, i)),
            pl.BlockSpec((1, 2, tm), lambda b, i: (b, 0, i)),
            pl.BlockSpec((1, SUBLANES, tm), lambda b, i: (b, 0, i)),
            pl.BlockSpec((N_EXPERTS, LANES), const),
        ],
        out_shape=[
            jax.ShapeDtypeStruct((nb, seq, D_MODEL), F32),
            jax.ShapeDtypeStruct((nb * seq * ROW_TILES, LANES), U32),
            jax.ShapeDtypeStruct((nb, 2, seq), jnp.int32),
            jax.ShapeDtypeStruct((nb, 2, seq), jnp.int32),
            jax.ShapeDtypeStruct((nb, SUBLANES, seq), F32),
            jax.ShapeDtypeStruct((N_EXPERTS, LANES), F32),
        ],
        scratch_shapes=[pltpu.VMEM((N_EXPERTS, LANES), F32)],
        compiler_params=_params(("arbitrary", "arbitrary")),
        name="postmix",
    )(a, sn, x, mod, P["g_attn_out"], P["w_out"], P["g_post1"], P["g_pre2"], P["w_router"], P["b_router"], P["tri"])


def _sc_mesh():
    return plsc.VectorSubcoreMesh(core_axis_name="c", subcore_axis_name="s")


def _sc_worker():
    return lax.axis_index("s") * SC_CORES + lax.axis_index("c")


def _sc_gather_rows(table, idx):
    n = idx.shape[0]
    assert n % (SC_WORKERS * SC_ROWS) == 0
    per_w = n // SC_WORKERS
    steps = per_w // SC_ROWS
    idx3 = idx.reshape(SC_WORKERS, steps, SC_ROWS)

    @functools.partial(
        pl.kernel, mesh=_sc_mesh(),
        out_type=jax.ShapeDtypeStruct((n, ROW_TILES, LANES), U32),
        scratch_types=[pltpu.VMEM((steps, SC_ROWS), jnp.int32), pltpu.VMEM((SC_ROWS, ROW_TILES, LANES), U32),
                       pltpu.SemaphoreType.DMA],
        name="sc_gather_rows",
    )
    def gather(table_hbm, idx_hbm, out_hbm, idx_v, rows_v, sem):
        wid = _sc_worker()
        pltpu.sync_copy(idx_hbm.at[wid], idx_v)

        @pl.loop(0, steps)
        def _(j):
            pltpu.async_copy(table_hbm.at[idx_v.at[j]], rows_v, sem).wait()
            pltpu.sync_copy(rows_v, out_hbm.at[pl.ds(wid * per_w + j * SC_ROWS, SC_ROWS)])

    return gather(table, idx3)


def _sc_scatter_rows(src, idx, n_dst):
    n_dup, n = idx.shape
    assert n % (SC_WORKERS * SC_ROWS) == 0
    per_w = n // SC_WORKERS
    steps = per_w // SC_ROWS
    idx4 = idx.reshape(n_dup, SC_WORKERS, steps, SC_ROWS).transpose(1, 0, 2, 3)

    @functools.partial(
        pl.kernel, mesh=_sc_mesh(),
        out_type=jax.ShapeDtypeStruct((n_dst, ROW_TILES, LANES), U32),
        scratch_types=[pltpu.VMEM((n_dup, steps, SC_ROWS), jnp.int32), pltpu.VMEM((SC_ROWS, ROW_TILES, LANES), U32),
                       pltpu.SemaphoreType.DMA],
        name="sc_scatter_rows",
    )
    def scatter(src_hbm, idx_hbm, dst_hbm, idx_v, rows_v, sem):
        wid = _sc_worker()
        pltpu.sync_copy(idx_hbm.at[wid], idx_v)

        @pl.loop(0, steps)
        def _(j):
            pltpu.sync_copy(src_hbm.at[pl.ds(wid * per_w + j * SC_ROWS, SC_ROWS)], rows_v)
            copies = [pltpu.async_copy(rows_v, dst_hbm.at[idx_v.at[k, j]], sem) for k in range(n_dup)]
            for copy in copies:
                copy.wait()

    return scatter(src, idx4)


def _expert_kernel(te_ref, tr_ref, nv_ref, xs_ref, wg_ref, wu_ref, wd_ref, ys_ref, wgu_bf, wd_bf):
    i = pl.program_id(0)
    valid = i < nv_ref[0]
    new_expert = jnp.logical_or(i == 0, te_ref[i] != te_ref[jnp.maximum(i - 1, 0)])

    @pl.when(jnp.logical_and(valid, new_expert))
    def _():
        wgu_bf[:, :EXPERT_FF] = wg_ref[0].astype(BF16)
        wgu_bf[:, EXPERT_FF:] = wu_ref[0].astype(BF16)
        wd_bf[...] = wd_ref[0].astype(BF16)

    @pl.when(valid)
    def _():
        n_rows = xs_ref.shape[0] // ROW_TILES
        x = _load_token_rows(xs_ref, n_rows)
        row = lax.broadcasted_iota(jnp.int32, (n_rows, 1), 0)
        x = jnp.where(row < tr_ref[i], x, 0.0).astype(BF16)
        gu = _dot(x, wgu_bf[...])
        g, u = gu[:, :EXPERT_FF], gu[:, EXPERT_FF:]
        act = (g * jax.nn.sigmoid(g) * u).astype(BF16)
        _store_token_rows(ys_ref, _dot(act, wd_bf[...]))

    @pl.when(jnp.logical_not(valid))
    def _():
        ys_ref[...] = jnp.zeros_like(ys_ref)


def _expert_tile(n_tok):
    return EXPERT_TILE_LARGE if 2 * n_tok // N_EXPERTS >= 2 * EXPERT_TILE_LARGE else EXPERT_TILE_SMALL


def _experts(xs, tile_expert, tile_rows, n_valid, tile, P):
    rows_blk = tile * ROW_TILES
    assert xs.shape[0] % rows_blk == 0
    n_tiles = xs.shape[0] // rows_blk

    def row_map(i, te, tr, nv):
        return (jnp.minimum(i, nv[0] - 1), 0)

    def out_map(i, te, tr, nv):
        return (i, 0)

    def w_map(i, te, tr, nv):
        return (te[jnp.minimum(i, nv[0] - 1)], 0, 0)

    return pl.pallas_call(
        _expert_kernel,
        grid_spec=pltpu.PrefetchScalarGridSpec(
            num_scalar_prefetch=3,
            grid=(n_tiles,),
            in_specs=[
                pl.BlockSpec((rows_blk, LANES), row_map),
                pl.BlockSpec((1, D_MODEL, EXPERT_FF), w_map),
                pl.BlockSpec((1, D_MODEL, EXPERT_FF), w_map),
                pl.BlockSpec((1, EXPERT_FF, D_MODEL), w_map),
            ],
            out_specs=pl.BlockSpec((rows_blk, LANES), out_map),
            scratch_shapes=[pltpu.VMEM((D_MODEL, 2 * EXPERT_FF), BF16), pltpu.VMEM((EXPERT_FF, D_MODEL), BF16)],
        ),
        out_shape=jax.ShapeDtypeStruct(xs.shape, U32),
        compiler_params=_params(("arbitrary",)),
        name="experts",
    )(tile_expert, tile_rows, n_valid, xs, P["w_gate"], P["w_up"], P["w_down"])


def _final_kernel(y0_ref, y1_ref, wcol_ref, x1_ref, mod_ref, gpost_ref, o_ref):
    w_rows = wcol_ref[0]
    n_tok = w_rows.shape[1]
    w = jnp.concatenate([w_rows, jnp.zeros((LANES - SUBLANES, n_tok), F32)], axis=0).T
    w0, w1 = w[:, 0:1], w[:, 1:2]
    m = w0 * _load_token_rows(y0_ref, n_tok) + w1 * _load_token_rows(y1_ref, n_tok)
    gate2 = mod_ref[0][5:6]
    o_ref[0] = x1_ref[0] + gate2 * (_rms(m) * gpost_ref[...])


def _final(yg, wcol, x1, mod, P):
    nb, seq, _ = x1.shape
    tm = POST_TILE
    nt = seq // tm
    n_tok_tiles = nb * nt
    deep = pl.Buffered(FINAL_INPUT_BUFFERS)

    def pipelined(yg_hbm, wcol_hbm, x1_hbm, mod_hbm, gpost_hbm, o_hbm):
        pltpu.emit_pipeline(
            _final_kernel,
            grid=(nb, nt),
            in_specs=[
                pl.BlockSpec((tm * ROW_TILES, LANES), lambda b, i: (b * nt + i, 0), pipeline_mode=deep),
                pl.BlockSpec((tm * ROW_TILES, LANES), lambda b, i: (n_tok_tiles + b * nt + i, 0), pipeline_mode=deep),
                pl.BlockSpec((1, SUBLANES, tm), lambda b, i: (b, 0, i), pipeline_mode=deep),
                pl.BlockSpec((1, tm, D_MODEL), lambda b, i: (b, i, 0), pipeline_mode=deep),
                pl.BlockSpec((1, SUBLANES, D_MODEL), lambda b, i: (b, 0, 0)),
                pl.BlockSpec((1, D_MODEL), lambda b, i: (0, 0)),
            ],
            out_specs=[pl.BlockSpec((1, tm, D_MODEL), lambda b, i: (b, i, 0))],
        )(yg_hbm, yg_hbm, wcol_hbm, x1_hbm, mod_hbm, gpost_hbm, o_hbm)

    any_spec = pl.BlockSpec(memory_space=pl.ANY)
    return pl.pallas_call(
        pipelined,
        in_specs=[any_spec] * 5,
        out_specs=any_spec,
        out_shape=jax.ShapeDtypeStruct((nb, seq, D_MODEL), F32),
        compiler_params=pltpu.CompilerParams(vmem_limit_bytes=VMEM_LIMIT),
        name="final",
    )(yg, wcol, x1, mod, P["g_post2"])


def _prepare(w):
    f = lambda a: a.astype(F32)
    P = {}
    for k in ("g_pre1", "g_post1", "g_pre2", "g_post2", "g_q", "g_kv", "g_attn_out", "g_gmlp_out"):
        P[k] = f(w[k]).reshape(1, -1)
    P["g_v"] = f(w["g_v_gmlp"]).reshape(1, -1)

    w_in = f(w["w_in"])
    o0, o1, o2, o3 = Q_LORA, Q_LORA + KV_LORA, Q_LORA + KV_LORA + QK_ROPE, Q_LORA + KV_LORA + QK_ROPE + GMLP_WIDTH
    w_kr = w_in[:, o1:o2]
    kr_partner = jnp.concatenate([-w_kr[:, ROPE_HALF:], w_kr[:, :ROPE_HALF]], axis=1)
    rope_blk = jnp.concatenate([jnp.zeros((D_MODEL, QK_NOPE), F32), w_kr, kr_partner], axis=1)
    P["w_in"] = jnp.concatenate([w_in[:, :o1], rope_blk, w_in[:, o2:o3], w_in[:, o3:]], axis=1).astype(BF16)

    w_uq = f(w["w_uq"]).reshape(Q_LORA, N_HEADS, QK_NOPE + QK_ROPE)
    q_rope = w_uq[:, :, QK_NOPE:]
    q_partner = jnp.concatenate([-q_rope[:, :, ROPE_HALF:], q_rope[:, :, :ROPE_HALF]], axis=2)
    P["w_uq"] = jnp.concatenate([w_uq, q_partner], axis=2).reshape(Q_LORA, N_HEADS * HEAD_PAD).astype(BF16)

    w_ukv = f(w["w_ukv"]).reshape(KV_LORA, N_HEADS, QK_NOPE + V_HEAD)
    zeros = jnp.zeros((KV_LORA, N_HEADS, HEAD_PAD - QK_NOPE), F32)
    w_k = jnp.concatenate([w_ukv[:, :, :QK_NOPE], zeros], axis=2)
    w_v = w_ukv[:, :, QK_NOPE:]
    even = (jnp.arange(N_HEADS) % 2 == 0)[None, :, None]
    zv = jnp.zeros_like(w_v)
    w_v = jnp.concatenate([jnp.where(even, w_v, zv), jnp.where(even, zv, w_v)], axis=2)
    P["w_ukv"] = jnp.concatenate([w_k.reshape(KV_LORA, -1), w_v.reshape(KV_LORA, -1)], axis=1).astype(BF16)

    P["w_sp"] = f(w["w_spatial"]).reshape(GMLP_GROUPS // 2, 2 * CHUNK, CHUNK).astype(BF16)
    P["b_sp"] = jnp.repeat(f(w["b_spatial"]).T, GMLP_GROUP_DIM, axis=1)

    P["w_out"] = f(w["w_out"]).astype(BF16)

    pad = jnp.zeros((D_MODEL, LANES - N_EXPERTS - N_EXPERT_GROUPS), F32)
    wr = jnp.concatenate([f(w["w_router_expert"]), f(w["w_router_group"]), pad], axis=1)
    wr_hi = wr.astype(BF16)
    wr_lo = (wr - wr_hi.astype(F32)).astype(BF16)
    P["w_router"] = jnp.concatenate([wr_hi, wr_lo], axis=1)
    P["b_router"] = jnp.concatenate(
        [f(w["b_router_expert"]), f(w["b_router_group"]), jnp.zeros((LANES - N_EXPERTS - N_EXPERT_GROUPS,), F32)]
    ).reshape(1, LANES)
    P["tri"] = jnp.triu(jnp.ones((RANK_CHUNK, RANK_CHUNK), F32), k=1).astype(BF16)

    P["w_gate"], P["w_up"], P["w_down"] = f(w["w_gate"]), f(w["w_up"]), f(w["w_down"])
    return P


def _rope_tables(seq):
    inv = ROPE_THETA ** (-jnp.arange(ROPE_HALF, dtype=F32) / ROPE_HALF)
    ang = jnp.arange(seq, dtype=F32)[:, None] * inv[None, :]
    z_lo = jnp.zeros((seq, _ROPE_LO), F32)
    z_hi = jnp.zeros((seq, LANES - _ROPE_LO - QK_ROPE), F32)
    cos = jnp.concatenate([z_lo, jnp.cos(ang), jnp.cos(ang), z_hi], axis=1)
    sin = jnp.concatenate([z_lo, jnp.sin(ang), jnp.sin(ang), z_hi], axis=1)
    return cos, sin


def _layer(x, mod, P):
    nb, seq, d_model = x.shape
    assert d_model == D_MODEL and TOKEN_TILE % K_TILE == 0
    assert seq % TOKEN_TILE == 0 and seq % Q_TILE == 0 and seq % POST_TILE == 0 and POST_TILE % RANK_CHUNK == 0
    n_tok = nb * seq
    q, kt, v, sn = _premix(x, mod, P)
    a = _attention(q, kt, v)
    x1, h2rows, eid, rank, wcol, counts = _postmix(a, sn, x, mod, P)

    tile = _expert_tile(n_tok)
    cnt = counts[:, 0].astype(jnp.int32)
    padded = ((cnt + tile - 1) // tile) * tile
    ends = jnp.cumsum(padded)
    starts = ends - padded
    eflat = jnp.transpose(eid, (1, 0, 2)).reshape(2, n_tok)
    rflat = jnp.transpose(rank, (1, 0, 2)).reshape(2, n_tok)
    onehot = eflat[:, :, None] == jnp.arange(N_EXPERTS, dtype=jnp.int32)[None, None, :]
    pos = rflat + jnp.sum(jnp.where(onehot, starts[None, None, :], 0), axis=2)
    n_rows = 2 * n_tok + N_EXPERTS * tile
    n_tiles = n_rows // tile
    tile_start = jnp.arange(n_tiles, dtype=jnp.int32) * tile
    tile_expert = jnp.minimum(
        jnp.sum((tile_start[:, None] >= ends[None, :]).astype(jnp.int32), axis=1), N_EXPERTS - 1).astype(jnp.int32)
    expert_ids = jnp.arange(N_EXPERTS, dtype=jnp.int32)
    live_end = jnp.sum(jnp.where(tile_expert[:, None] == expert_ids[None, :], (starts + cnt)[None, :], 0), axis=1)
    tile_rows = jnp.clip(live_end - tile_start, 0, tile).astype(jnp.int32)
    n_valid = (ends[-1:] // tile).astype(jnp.int32)

    as_tiles = lambda a: a.reshape(-1, ROW_TILES, LANES)
    as_rows = lambda a: a.reshape(-1, LANES)
    xs = _sc_scatter_rows(as_tiles(h2rows), pos, n_rows)
    ys = _experts(as_rows(xs), tile_expert, tile_rows, n_valid, tile, P)
    yg = as_rows(_sc_gather_rows(as_tiles(ys), pos.reshape(2 * n_tok)))
    return _final(yg, wcol, x1, mod, P)


def kernel(x_prompt, x_sample, c_prompt, c_sample, w_ada, b_ada, g_pre1, g_post1, g_pre2, g_post2, w_in, g_q, w_uq,
           g_kv, w_ukv, g_v_gmlp, w_spatial, b_spatial, g_attn_out, g_gmlp_out, w_out, w_router_group,
           b_router_group, w_router_expert, b_router_expert, w_gate, w_up, w_down):
    P = _prepare(dict(
        g_pre1=g_pre1, g_post1=g_post1, g_pre2=g_pre2, g_post2=g_post2, w_in=w_in, g_q=g_q, w_uq=w_uq, g_kv=g_kv,
        w_ukv=w_ukv, g_v_gmlp=g_v_gmlp, w_spatial=w_spatial, b_spatial=b_spatial, g_attn_out=g_attn_out,
        g_gmlp_out=g_gmlp_out, w_out=w_out, w_router_group=w_router_group, b_router_group=b_router_group,
        w_router_expert=w_router_expert, b_router_expert=b_router_expert, w_gate=w_gate, w_up=w_up, w_down=w_down))
    P["rope"] = {seq: _rope_tables(seq) for seq in {x_prompt.shape[1], x_sample.shape[1]}}

    nbp = c_prompt.shape[0]
    c_all = jnp.concatenate([c_prompt, c_sample], axis=0).astype(F32)
    mod = _ada(c_all, w_ada.astype(F32), b_ada.astype(F32))
    mod = mod.reshape(c_all.shape[0], 6, D_MODEL)
    mod = jnp.concatenate([mod, jnp.zeros((c_all.shape[0], SUBLANES - 6, D_MODEL), F32)], axis=1)

    y_prompt = _layer(x_prompt, mod[:nbp], P)
    y_sample = _layer(x_sample, mod[nbp:], P)
    return (y_prompt, y_sample)
```

```python
import functools
import math

import jax
import jax.numpy as jnp
from jax import lax
from jax.experimental import pallas as pl
from jax.experimental.pallas import tpu as pltpu
from jax.experimental.pallas import tpu_sc as plsc

F32 = jnp.float32
BF16 = jnp.bfloat16

D_MODEL = 1024
N_HEADS = 8
QK_NOPE = 64
QK_ROPE = 32
ROPE_HALF = QK_ROPE // 2
V_HEAD = 64
Q_LORA = 256
KV_LORA = 128
GMLP_WIDTH = 512
GMLP_GROUPS = 8
GMLP_GROUP_DIM = 64
CHUNK = 128
N_EXPERTS = 32
N_EXPERT_GROUPS = 4
EXPERTS_PER_GROUP = 8
EXPERT_FF = 256
ROPE_THETA = 10000.0
EPS = 1e-6

LANES = 128
SUBLANES = 8
HEAD_PAD = LANES

TOKEN_TILE = 1024
POST_TILE = 1024
Q_TILE = 1024
K_TILE = 512
EXPERT_TILE_SMALL = 512
EXPERT_TILE_LARGE = 1024
SC_CORES = 2
SC_WORKERS = 32
SC_ROWS = 128
RANK_CHUNK = 256
FINAL_INPUT_BUFFERS = 3
VMEM_LIMIT = 56 * 1024 * 1024

U32 = jnp.uint32
PACKED_WIDTH = D_MODEL // 2
ROW_TILES = PACKED_WIDTH // LANES
_HI_MASK = 0xFFFF0000

_SQRT_2_OVER_PI = math.sqrt(2.0 / math.pi)


def _rms(x):
    return x * lax.rsqrt(jnp.mean(x * x, axis=-1, keepdims=True) + EPS)


def _gelu_tanh(x):
    return 0.5 * x * (1.0 + jnp.tanh(_SQRT_2_OVER_PI * (x + 0.044715 * (x * x * x))))


def _split_bf16(x):
    hi = x.astype(BF16)
    lo = (x - hi.astype(F32)).astype(BF16)
    return hi, lo


def _dot(a, b):
    return jnp.dot(a, b, preferred_element_type=F32)


def _bf16_bits(x):
    return lax.bitcast_convert_type(x.astype(BF16).astype(F32), U32)


def _load_token_rows(ref, n, first=0, stride=ROW_TILES):
    w = jnp.concatenate([ref[pl.ds(first + c, n, stride=stride), :] for c in range(ROW_TILES)], axis=1)
    lo = lax.bitcast_convert_type(w << 16, F32)
    hi = lax.bitcast_convert_type(w & jnp.uint32(_HI_MASK), F32)
    return jnp.concatenate([lo, hi], axis=1)


def _store_token_rows(ref, val):
    n = val.shape[0]
    w = (_bf16_bits(val[:, :PACKED_WIDTH]) >> 16) | (_bf16_bits(val[:, PACKED_WIDTH:]) & jnp.uint32(_HI_MASK))
    for c in range(ROW_TILES):
        ref[pl.ds(c, n, stride=ROW_TILES), :] = w[:, c * LANES:(c + 1) * LANES]


def _params(sem, vmem=VMEM_LIMIT):
    return pltpu.CompilerParams(dimension_semantics=sem, vmem_limit_bytes=vmem)


def _ada_kernel(c_ref, w_ref, b_ref, o_ref):
    c = c_ref[...]
    a = c * jax.nn.sigmoid(c)
    a_hi, a_lo = _split_bf16(a)
    w_hi, w_lo = _split_bf16(w_ref[...])
    o_ref[...] = _dot(a_hi, w_hi) + _dot(a_hi, w_lo) + _dot(a_lo, w_hi) + b_ref[...]


def _ada(c, w_ada, b_ada):
    nb = c.shape[0]
    n_out = w_ada.shape[1]
    blk = D_MODEL
    return pl.pallas_call(
        _ada_kernel,
        grid=(n_out // blk,),
        in_specs=[
            pl.BlockSpec((nb, D_MODEL), lambda j: (0, 0)),
            pl.BlockSpec((D_MODEL, blk), lambda j: (0, j)),
            pl.BlockSpec((1, blk), lambda j: (0, j)),
        ],
        out_specs=pl.BlockSpec((nb, blk), lambda j: (0, j)),
        out_shape=jax.ShapeDtypeStruct((nb, n_out), F32),
        compiler_params=_params(("arbitrary",)),
        name="ada",
    )(c, w_ada, b_ada.reshape(1, n_out))


_C_CQ = 0
_C_CKV = _C_CQ + Q_LORA
_C_KR = _C_CKV + KV_LORA
_C_U = _C_KR + LANES
_C_V = _C_U + GMLP_WIDTH
_C_END = _C_V + GMLP_WIDTH
_ROPE_LO = QK_NOPE
_ROLL_PARTNER = LANES - QK_ROPE


def _premix_kernel(x_ref, mod_ref, cos_ref, sin_ref, gpre_ref, win_ref, gq_ref, wuq_ref, gkv_ref, wukv_ref,
                   gv_ref, wsp_ref, bsp_ref, ggo_ref, q_ref, kt_ref, v_ref, sn_ref):
    x = x_ref[0]
    mod = mod_ref[0]
    shift1, scale1 = mod[0:1], mod[1:2]
    h = _rms(x) * (gpre_ref[...] * (1.0 + scale1)) + shift1
    z = _dot(h.astype(BF16), win_ref[...])

    cosb = cos_ref[...]
    sinb = sin_ref[...]
    lane = lax.broadcasted_iota(jnp.int32, (1, LANES), 1)
    nope_mask = jnp.where(lane < QK_NOPE, 1.0, 0.0).astype(F32)

    qscale = (QK_NOPE + QK_ROPE) ** -0.5 * math.log2(math.e)
    cq_tab = (nope_mask + cosb) * qscale
    sq_tab = sinb * qscale
    cqn = (_rms(z[:, _C_CQ:_C_CKV]) * gq_ref[...]).astype(BF16)
    qb = _dot(cqn, wuq_ref[...])
    for hd in range(N_HEADS):
        blk = qb[:, hd * HEAD_PAD:(hd + 1) * HEAD_PAD]
        qh = blk * cq_tab + pltpu.roll(blk, _ROLL_PARTNER, 1) * sq_tab
        q_ref[0, hd] = qh.astype(BF16)

    ckvn = (_rms(z[:, _C_CKV:_C_KR]) * gkv_ref[...]).astype(BF16)
    kvb = _dot(ckvn, wukv_ref[...])
    krb = z[:, _C_KR:_C_U]
    krope = krb * cosb + pltpu.roll(krb, _ROLL_PARTNER, 1) * sinb
    v_off = N_HEADS * HEAD_PAD
    for hd in range(N_HEADS):
        kh = kvb[:, hd * HEAD_PAD:(hd + 1) * HEAD_PAD] + krope
        for c in range(kt_ref.shape[2]):
            kt_ref[0, hd, c] = kh[c * K_TILE:(c + 1) * K_TILE].T.astype(BF16)
        ones_lane = V_HEAD if hd % 2 == 0 else 0
        vh = kvb[:, v_off + hd * HEAD_PAD:v_off + (hd + 1) * HEAD_PAD] + jnp.where(lane == ones_lane, 1.0, 0.0)
        v_ref[0, hd] = vh.astype(BF16)

    ua = _gelu_tanh(z[:, _C_U:_C_V])
    vn = (_rms(_gelu_tanh(z[:, _C_V:_C_END])) * gv_ref[...]).astype(BF16)
    n_tok = x.shape[0]
    bsp = bsp_ref[...]
    rows = []
    for n in range(n_tok // CHUNK):
        cols = []
        for j in range(GMLP_GROUPS // 2):
            rhs = vn[n * CHUNK:(n + 1) * CHUNK, j * LANES:(j + 1) * LANES]
            ab = _dot(wsp_ref[j], rhs)
            cols.append(jnp.where(lane < GMLP_GROUP_DIM, ab[:CHUNK], ab[CHUNK:]))
        rows.append(jnp.concatenate(cols, axis=1) + bsp)
    s = ua * jnp.concatenate(rows, axis=0)
    sn_ref[0] = (_rms(s) * ggo_ref[...]).astype(BF16)


def _premix(x, mod, P):
    nb, seq, _ = x.shape
    tm = TOKEN_TILE
    nck = seq // K_TILE
    const = lambda i, b: (0, 0)
    return pl.pallas_call(
        _premix_kernel,
        grid=(seq // tm, nb),
        in_specs=[
            pl.BlockSpec((1, tm, D_MODEL), lambda i, b: (b, i, 0)),
            pl.BlockSpec((1, SUBLANES, D_MODEL), lambda i, b: (b, 0, 0)),
            pl.BlockSpec((tm, LANES), lambda i, b: (i, 0)),
            pl.BlockSpec((tm, LANES), lambda i, b: (i, 0)),
            pl.BlockSpec((1, D_MODEL), const),
            pl.BlockSpec((D_MODEL, _C_END), const),
            pl.BlockSpec((1, Q_LORA), const),
            pl.BlockSpec((Q_LORA, N_HEADS * HEAD_PAD), const),
            pl.BlockSpec((1, KV_LORA), const),
            pl.BlockSpec((KV_LORA, 2 * N_HEADS * HEAD_PAD), const),
            pl.BlockSpec((1, GMLP_WIDTH), const),
            pl.BlockSpec((GMLP_GROUPS // 2, 2 * CHUNK, CHUNK), lambda i, b: (0, 0, 0)),
            pl.BlockSpec((CHUNK, GMLP_WIDTH), const),
            pl.BlockSpec((1, GMLP_WIDTH), const),
        ],
        out_specs=[
            pl.BlockSpec((1, N_HEADS, tm, HEAD_PAD), lambda i, b: (b, 0, i, 0)),
            pl.BlockSpec((1, N_HEADS, tm // K_TILE, HEAD_PAD, K_TILE), lambda i, b: (b, 0, i, 0, 0)),
            pl.BlockSpec((1, N_HEADS, tm, HEAD_PAD), lambda i, b: (b, 0, i, 0)),
            pl.BlockSpec((1, tm, GMLP_WIDTH), lambda i, b: (b, i, 0)),
        ],
        out_shape=[
            jax.ShapeDtypeStruct((nb, N_HEADS, seq, HEAD_PAD), BF16),
            jax.ShapeDtypeStruct((nb, N_HEADS, nck, HEAD_PAD, K_TILE), BF16),
            jax.ShapeDtypeStruct((nb, N_HEADS, seq, HEAD_PAD), BF16),
            jax.ShapeDtypeStruct((nb, seq, GMLP_WIDTH), BF16),
        ],
        compiler_params=_params(("arbitrary", "arbitrary")),
        name="premix",
    )(x, mod, *P["rope"][seq], P["g_pre1"], P["w_in"], P["g_q"], P["w_uq"], P["g_kv"], P["w_ukv"],
      P["g_v"], P["w_sp"], P["b_sp"], P["g_gmlp_out"])


def _attn_kernel(q_ref, kt_ref, v_ref, o_ref):
    n_chunks = kt_ref.shape[2]
    tk = kt_ref.shape[4]
    lane = lax.broadcasted_iota(jnp.int32, (1, LANES), 1)

    def one_head(hd, ones_lane):
        q = q_ref[0, hd]
        m = None
        acc = None
        for c in range(n_chunks):
            s = _dot(q, kt_ref[0, hd, c])
            v = v_ref[0, hd, c * tk:(c + 1) * tk, :]
            smax = jnp.max(s, axis=1, keepdims=True)
            if c == 0:
                m = smax
                acc = _dot(jnp.exp2((s - m).astype(BF16)), v)
            else:
                m_new = jnp.maximum(m, smax)
                acc = acc * jnp.exp2(m - m_new) + _dot(jnp.exp2((s - m_new).astype(BF16)), v)
                m = m_new
        row_sum = acc[:, ones_lane:ones_lane + 1]
        return acc * (1.0 / row_sum)

    def pair(j, carry):
        even = one_head(2 * j, V_HEAD)
        odd = one_head(2 * j + 1, 0)
        o_ref[0, j] = jnp.where(lane < V_HEAD, even, odd).astype(BF16)
        return carry

    lax.fori_loop(0, N_HEADS // 2, pair, 0)


def _attention(q, kt, v):
    nb, _, seq, _ = q.shape
    nck = kt.shape[2]
    return pl.pallas_call(
        _attn_kernel,
        grid=(nb, seq // Q_TILE),
        in_specs=[
            pl.BlockSpec((1, N_HEADS, Q_TILE, HEAD_PAD), lambda b, i: (b, 0, i, 0)),
            pl.BlockSpec((1, N_HEADS, nck, HEAD_PAD, K_TILE), lambda b, i: (b, 0, 0, 0, 0)),
            pl.BlockSpec((1, N_HEADS, seq, HEAD_PAD), lambda b, i: (b, 0, 0, 0)),
        ],
        out_specs=pl.BlockSpec((1, N_HEADS // 2, Q_TILE, LANES), lambda b, i: (b, 0, i, 0)),
        out_shape=jax.ShapeDtypeStruct((nb, N_HEADS // 2, seq, LANES), BF16),
        compiler_params=_params(("arbitrary", "arbitrary")),
        name="attn",
    )(q, kt, v)


_R_GROUP_ROW = N_EXPERTS


def _postmix_kernel(a_ref, sn_ref, x_ref, mod_ref, gao_ref, wout_ref, gpost_ref, gpre_ref, wr_ref, br_ref, tri_ref,
                    x1_ref, h2_ref, eid_ref, rank_ref, wcol_ref, cnt_ref, run_ref):
    first = jnp.logical_and(pl.program_id(0) == 0, pl.program_id(1) == 0)

    @pl.when(first)
    def _():
        run_ref[...] = jnp.zeros_like(run_ref)

    mod = mod_ref[0]
    gate1, shift2, scale2 = mod[2:3], mod[3:4], mod[4:5]
    a = jnp.concatenate([a_ref[0, j] for j in range(N_HEADS // 2)], axis=1).astype(F32)
    an = (_rms(a) * gao_ref[...]).astype(BF16)
    merged = jnp.concatenate([an, sn_ref[0]], axis=1)
    o = _dot(merged, wout_ref[...])
    x1 = x_ref[0] + gate1 * (_rms(o) * gpost_ref[...])
    x1_ref[0] = x1
    h2 = _rms(x1) * (gpre_ref[...] * (1.0 + scale2)) + shift2
    n_tok = h2.shape[0]
    _store_token_rows(h2_ref, h2)

    h_hi, h_lo = _split_bf16(h2)
    wr = wr_ref[...]
    hh = _dot(h_hi, wr)
    lh = _dot(h_lo, wr[:, :LANES])
    logits = hh[:, :LANES] + hh[:, LANES:] + lh + br_ref[...]
    lt = logits.T

    neg = jnp.float32(-jnp.inf)
    row8 = lax.broadcasted_iota(jnp.int32, (SUBLANES, n_tok), 0).astype(F32)
    lg = jnp.where(row8 < N_EXPERT_GROUPS, lt[_R_GROUP_ROW:_R_GROUP_ROW + SUBLANES], neg)
    gmax = jnp.max(lg, axis=0, keepdims=True)
    gi = jnp.min(jnp.where(lg == gmax, row8, float(SUBLANES)), axis=0, keepdims=True)
    pg_sel = 1.0 / jnp.sum(jnp.exp(lg - gmax), axis=0, keepdims=True)

    le = jnp.zeros((EXPERTS_PER_GROUP, n_tok), F32)
    for g in range(N_EXPERT_GROUPS):
        le = jnp.where(gi == float(g), lt[g * EXPERTS_PER_GROUP:(g + 1) * EXPERTS_PER_GROUP], le)
    v1 = jnp.max(le, axis=0, keepdims=True)
    i1 = jnp.min(jnp.where(le == v1, row8, float(SUBLANES)), axis=0, keepdims=True)
    le2 = jnp.where(row8 == i1, neg, le)
    v2 = jnp.max(le2, axis=0, keepdims=True)
    i2 = jnp.min(jnp.where(le2 == v2, row8, float(SUBLANES)), axis=0, keepdims=True)
    r = jnp.exp(v2 - v1)
    w1 = pg_sel / (1.0 + r)
    w2 = w1 * r
    e1 = gi * float(EXPERTS_PER_GROUP) + i1
    e2 = gi * float(EXPERTS_PER_GROUP) + i2
    eid_ref[0] = jnp.concatenate([e1, e2], axis=0).astype(jnp.int32)

    row32 = lax.broadcasted_iota(jnp.int32, (N_EXPERTS, n_tok), 0).astype(F32)
    hit1 = row32 == e1
    hit2 = row32 == e2
    onehot = jnp.where(jnp.logical_or(hit1, hit2), 1.0, 0.0)
    run = run_ref[...][:, 0:1]
    ranks1, ranks2 = [], []
    for c in range(n_tok // RANK_CHUNK):
        sl = slice(c * RANK_CHUNK, (c + 1) * RANK_CHUNK)
        oh = onehot[:, sl]
        before = _dot(oh.astype(BF16), tri_ref[...]) + run
        ranks1.append(jnp.sum(jnp.where(hit1[:, sl], before, 0.0), axis=0, keepdims=True))
        ranks2.append(jnp.sum(jnp.where(hit2[:, sl], before, 0.0), axis=0, keepdims=True))
        run = run + jnp.sum(oh, axis=1, keepdims=True)
    rank_ref[0] = jnp.concatenate(
        [jnp.concatenate(ranks1, axis=1), jnp.concatenate(ranks2, axis=1)], axis=0).astype(jnp.int32)
    run_b = jnp.broadcast_to(run, run_ref.shape)
    run_ref[...] = run_b
    cnt_ref[...] = run_b

    wcol_ref[0] = jnp.where(row8 == 0.0, w1, jnp.where(row8 == 1.0, w2, 0.0))


def _postmix(a, sn, x, mod, P):
    nb, seq, _ = x.shape
    tm = POST_TILE
    const = lambda b, i: (0, 0)
    return pl.pallas_call(
        _postmix_kernel,
        grid=(nb, seq // tm),
        in_specs=[
            pl.BlockSpec((1, N_HEADS // 2, tm, LANES), lambda b, i: (b, 0, i, 0)),
            pl.BlockSpec((1, tm, GMLP_WIDTH), lambda b, i: (b, i, 0)),
            pl.BlockSpec((1, tm, D_MODEL), lambda b, i: (b, i, 0)),
            pl.BlockSpec((1, SUBLANES, D_MODEL), lambda b, i: (b, 0, 0)),
            pl.BlockSpec((1, N_HEADS * V_HEAD), const),
            pl.BlockSpec((D_MODEL, D_MODEL), const),
            pl.BlockSpec((1, D_MODEL), const),
            pl.BlockSpec((1, D_MODEL), const),
            pl.BlockSpec((D_MODEL, 2 * LANES), const),
            pl.BlockSpec((1, LANES), const),
            pl.BlockSpec((RANK_CHUNK, RANK_CHUNK), const),
        ],
        out_specs=[
            pl.BlockSpec((1, tm, D_MODEL), lambda b, i: (b, i, 0)),
            pl.BlockSpec((tm * ROW_TILES, LANES), lambda b, i: (b * (seq // tm) + i, 0)),
            pl.BlockSpec((1, 2, tm), lambda b, i: (b, 0, i)),
            pl.BlockSpec((1, 2, tm), lambda b, i: (b, 0, i)),
            pl.BlockSpec((1, SUBLANES, tm), lambda b, i: (b, 0, i)),
            pl.BlockSpec((N_EXPERTS, LANES), const),
        ],
        out_shape=[
            jax.ShapeDtypeStruct((nb, seq, D_MODEL), F32),
            jax.ShapeDtypeStruct((nb * seq * ROW_TILES, LANES), U32),
            jax.ShapeDtypeStruct((nb, 2, seq), jnp.int32),
            jax.ShapeDtypeStruct((nb, 2, seq), jnp.int32),
            jax.ShapeDtypeStruct((nb, SUBLANES, seq), F32),
            jax.ShapeDtypeStruct((N_EXPERTS, LANES), F32),
        ],
        scratch_shapes=[pltpu.VMEM((N_EXPERTS, LANES), F32)],
        compiler_params=_params(("arbitrary", "arbitrary")),
        name="postmix",
    )(a, sn, x, mod, P["g_attn_out"], P["w_out"], P["g_post1"], P["g_pre2"], P["w_router"], P["b_router"], P["tri"])


def _sc_mesh():
    return plsc.VectorSubcoreMesh(core_axis_name="c", subcore_axis_name="s")


def _sc_worker():
    return lax.axis_index("s") * SC_CORES + lax.axis_index("c")


def _sc_gather_rows(table, idx):
    n = idx.shape[0]
    assert n % (SC_WORKERS * SC_ROWS) == 0
    per_w = n // SC_WORKERS
    steps = per_w // SC_ROWS
    idx3 = idx.reshape(SC_WORKERS, steps, SC_ROWS)

    @functools.partial(
        pl.kernel, mesh=_sc_mesh(),
        out_type=jax.ShapeDtypeStruct((n, ROW_TILES, LANES), U32),
        scratch_types=[pltpu.VMEM((steps, SC_ROWS), jnp.int32), pltpu.VMEM((SC_ROWS, ROW_TILES, LANES), U32),
                       pltpu.SemaphoreType.DMA],
        name="sc_gather_rows",
    )
    def gather(table_hbm, idx_hbm, out_hbm, idx_v, rows_v, sem):
        wid = _sc_worker()
        pltpu.sync_copy(idx_hbm.at[wid], idx_v)

        @pl.loop(0, steps)
        def _(j):
            pltpu.async_copy(table_hbm.at[idx_v.at[j]], rows_v, sem).wait()
            pltpu.sync_copy(rows_v, out_hbm.at[pl.ds(wid * per_w + j * SC_ROWS, SC_ROWS)])

    return gather(table, idx3)


def _sc_scatter_rows(src, idx, n_dst):
    n_dup, n = idx.shape
    assert n % (SC_WORKERS * SC_ROWS) == 0
    per_w = n // SC_WORKERS
    steps = per_w // SC_ROWS
    idx4 = idx.reshape(n_dup, SC_WORKERS, steps, SC_ROWS).transpose(1, 0, 2, 3)

    @functools.partial(
        pl.kernel, mesh=_sc_mesh(),
        out_type=jax.ShapeDtypeStruct((n_dst, ROW_TILES, LANES), U32),
        scratch_types=[pltpu.VMEM((n_dup, steps, SC_ROWS), jnp.int32), pltpu.VMEM((SC_ROWS, ROW_TILES, LANES), U32),
                       pltpu.SemaphoreType.DMA],
        name="sc_scatter_rows",
    )
    def scatter(src_hbm, idx_hbm, dst_hbm, idx_v, rows_v, sem):
        wid = _sc_worker()
        pltpu.sync_copy(idx_hbm.at[wid], idx_v)

        @pl.loop(0, steps)
        def _(j):
            pltpu.sync_copy(src_hbm.at[pl.ds(wid * per_w + j * SC_ROWS, SC_ROWS)], rows_v)
            copies = [pltpu.async_copy(rows_v, dst_hbm.at[idx_v.at[k, j]], sem) for k in range(n_dup)]
            for copy in copies:
                copy.wait()

    return scatter(src, idx4)


def _expert_kernel(te_ref, tr_ref, nv_ref, xs_ref, wg_ref, wu_ref, wd_ref, ys_ref, wgu_bf, wd_bf):
    i = pl.program_id(0)
    valid = i < nv_ref[0]
    new_expert = jnp.logical_or(i == 0, te_ref[i] != te_ref[jnp.maximum(i - 1, 0)])

    @pl.when(jnp.logical_and(valid, new_expert))
    def _():
        wgu_bf[:, :EXPERT_FF] = wg_ref[0].astype(BF16)
        wgu_bf[:, EXPERT_FF:] = wu_ref[0].astype(BF16)
        wd_bf[...] = wd_ref[0].astype(BF16)

    @pl.when(valid)
    def _():
        n_rows = xs_ref.shape[0] // ROW_TILES
        x = _load_token_rows(xs_ref, n_rows)
        row = lax.broadcasted_iota(jnp.int32, (n_rows, 1), 0)
        x = jnp.where(row < tr_ref[i], x, 0.0).astype(BF16)
        gu = _dot(x, wgu_bf[...])
        g, u = gu[:, :EXPERT_FF], gu[:, EXPERT_FF:]
        act = (g * jax.nn.sigmoid(g) * u).astype(BF16)
        _store_token_rows(ys_ref, _dot(act, wd_bf[...]))

    @pl.when(jnp.logical_not(valid))
    def _():
        ys_ref[...] = jnp.zeros_like(ys_ref)


def _expert_tile(n_tok):
    return EXPERT_TILE_LARGE if 2 * n_tok // N_EXPERTS >= 2 * EXPERT_TILE_LARGE else EXPERT_TILE_SMALL


def _experts(xs, tile_expert, tile_rows, n_valid, tile, P):
    rows_blk = tile * ROW_TILES
    assert xs.shape[0] % rows_blk == 0
    n_tiles = xs.shape[0] // rows_blk

    def row_map(i, te, tr, nv):
        return (jnp.minimum(i, nv[0] - 1), 0)

    def out_map(i, te, tr, nv):
        return (i, 0)

    def w_map(i, te, tr, nv):
        return (te[jnp.minimum(i, nv[0] - 1)], 0, 0)

    return pl.pallas_call(
        _expert_kernel,
        grid_spec=pltpu.PrefetchScalarGridSpec(
            num_scalar_prefetch=3,
            grid=(n_tiles,),
            in_specs=[
                pl.BlockSpec((rows_blk, LANES), row_map),
                pl.BlockSpec((1, D_MODEL, EXPERT_FF), w_map),
                pl.BlockSpec((1, D_MODEL, EXPERT_FF), w_map),
                pl.BlockSpec((1, EXPERT_FF, D_MODEL), w_map),
            ],
            out_specs=pl.BlockSpec((rows_blk, LANES), out_map),
            scratch_shapes=[pltpu.VMEM((D_MODEL, 2 * EXPERT_FF), BF16), pltpu.VMEM((EXPERT_FF, D_MODEL), BF16)],
        ),
        out_shape=jax.ShapeDtypeStruct(xs.shape, U32),
        compiler_params=_params(("arbitrary",)),
        name="experts",
    )(tile_expert, tile_rows, n_valid, xs, P["w_gate"], P["w_up"], P["w_down"])


def _final_kernel(y_ref, wcol_ref, x1_ref, mod_ref, gpost_ref, o_ref):
    w_rows = wcol_ref[0]
    n_tok = w_rows.shape[1]
    w = jnp.concatenate([w_rows, jnp.zeros((LANES - SUBLANES, n_tok), F32)], axis=0).T
    w0, w1 = w[:, 0:1], w[:, 1:2]
    pair = 2 * ROW_TILES
    m = w0 * _load_token_rows(y_ref, n_tok, 0, pair) + w1 * _load_token_rows(y_ref, n_tok, ROW_TILES, pair)
    gate2 = mod_ref[0][5:6]
    o_ref[0] = x1_ref[0] + gate2 * (_rms(m) * gpost_ref[...])


def _final(yg, wcol, x1, mod, P):
    nb, seq, _ = x1.shape
    tm = POST_TILE
    nt = seq // tm
    deep = pl.Buffered(FINAL_INPUT_BUFFERS)

    def pipelined(yg_hbm, wcol_hbm, x1_hbm, mod_hbm, gpost_hbm, o_hbm):
        pltpu.emit_pipeline(
            _final_kernel,
            grid=(nb, nt),
            in_specs=[
                pl.BlockSpec((tm * 2 * ROW_TILES, LANES), lambda b, i: (b * nt + i, 0), pipeline_mode=deep),
                pl.BlockSpec((1, SUBLANES, tm), lambda b, i: (b, 0, i), pipeline_mode=deep),
                pl.BlockSpec((1, tm, D_MODEL), lambda b, i: (b, i, 0), pipeline_mode=deep),
                pl.BlockSpec((1, SUBLANES, D_MODEL), lambda b, i: (b, 0, 0)),
                pl.BlockSpec((1, D_MODEL), lambda b, i: (0, 0)),
            ],
            out_specs=[pl.BlockSpec((1, tm, D_MODEL), lambda b, i: (b, i, 0))],
        )(yg_hbm, wcol_hbm, x1_hbm, mod_hbm, gpost_hbm, o_hbm)

    any_spec = pl.BlockSpec(memory_space=pl.ANY)
    return pl.pallas_call(
        pipelined,
        in_specs=[any_spec] * 5,
        out_specs=any_spec,
        out_shape=jax.ShapeDtypeStruct((nb, seq, D_MODEL), F32),
        compiler_params=pltpu.CompilerParams(vmem_limit_bytes=VMEM_LIMIT),
        name="final",
    )(yg, wcol, x1, mod, P["g_post2"])


def _prepare(w):
    f = lambda a: a.astype(F32)
    P = {}
    for k in ("g_pre1", "g_post1", "g_pre2", "g_post2", "g_q", "g_kv", "g_attn_out", "g_gmlp_out"):
        P[k] = f(w[k]).reshape(1, -1)
    P["g_v"] = f(w["g_v_gmlp"]).reshape(1, -1)

    w_in = f(w["w_in"])
    o0, o1, o2, o3 = Q_LORA, Q_LORA + KV_LORA, Q_LORA + KV_LORA + QK_ROPE, Q_LORA + KV_LORA + QK_ROPE + GMLP_WIDTH
    w_kr = w_in[:, o1:o2]
    kr_partner = jnp.concatenate([-w_kr[:, ROPE_HALF:], w_kr[:, :ROPE_HALF]], axis=1)
    rope_blk = jnp.concatenate([jnp.zeros((D_MODEL, QK_NOPE), F32), w_kr, kr_partner], axis=1)
    P["w_in"] = jnp.concatenate([w_in[:, :o1], rope_blk, w_in[:, o2:o3], w_in[:, o3:]], axis=1).astype(BF16)

    w_uq = f(w["w_uq"]).reshape(Q_LORA, N_HEADS, QK_NOPE + QK_ROPE)
    q_rope = w_uq[:, :, QK_NOPE:]
    q_partner = jnp.concatenate([-q_rope[:, :, ROPE_HALF:], q_rope[:, :, :ROPE_HALF]], axis=2)
    P["w_uq"] = jnp.concatenate([w_uq, q_partner], axis=2).reshape(Q_LORA, N_HEADS * HEAD_PAD).astype(BF16)

    w_ukv = f(w["w_ukv"]).reshape(KV_LORA, N_HEADS, QK_NOPE + V_HEAD)
    zeros = jnp.zeros((KV_LORA, N_HEADS, HEAD_PAD - QK_NOPE), F32)
    w_k = jnp.concatenate([w_ukv[:, :, :QK_NOPE], zeros], axis=2)
    w_v = w_ukv[:, :, QK_NOPE:]
    even = (jnp.arange(N_HEADS) % 2 == 0)[None, :, None]
    zv = jnp.zeros_like(w_v)
    w_v = jnp.concatenate([jnp.where(even, w_v, zv), jnp.where(even, zv, w_v)], axis=2)
    P["w_ukv"] = jnp.concatenate([w_k.reshape(KV_LORA, -1), w_v.reshape(KV_LORA, -1)], axis=1).astype(BF16)

    P["w_sp"] = f(w["w_spatial"]).reshape(GMLP_GROUPS // 2, 2 * CHUNK, CHUNK).astype(BF16)
    P["b_sp"] = jnp.repeat(f(w["b_spatial"]).T, GMLP_GROUP_DIM, axis=1)

    P["w_out"] = f(w["w_out"]).astype(BF16)

    pad = jnp.zeros((D_MODEL, LANES - N_EXPERTS - N_EXPERT_GROUPS), F32)
    wr = jnp.concatenate([f(w["w_router_expert"]), f(w["w_router_group"]), pad], axis=1)
    wr_hi = wr.astype(BF16)
    wr_lo = (wr - wr_hi.astype(F32)).astype(BF16)
    P["w_router"] = jnp.concatenate([wr_hi, wr_lo], axis=1)
    P["b_router"] = jnp.concatenate(
        [f(w["b_router_expert"]), f(w["b_router_group"]), jnp.zeros((LANES - N_EXPERTS - N_EXPERT_GROUPS,), F32)]
    ).reshape(1, LANES)
    P["tri"] = jnp.triu(jnp.ones((RANK_CHUNK, RANK_CHUNK), F32), k=1).astype(BF16)

    P["w_gate"], P["w_up"], P["w_down"] = f(w["w_gate"]), f(w["w_up"]), f(w["w_down"])
    return P


def _rope_tables(seq):
    inv = ROPE_THETA ** (-jnp.arange(ROPE_HALF, dtype=F32) / ROPE_HALF)
    ang = jnp.arange(seq, dtype=F32)[:, None] * inv[None, :]
    z_lo = jnp.zeros((seq, _ROPE_LO), F32)
    z_hi = jnp.zeros((seq, LANES - _ROPE_LO - QK_ROPE), F32)
    cos = jnp.concatenate([z_lo, jnp.cos(ang), jnp.cos(ang), z_hi], axis=1)
    sin = jnp.concatenate([z_lo, jnp.sin(ang), jnp.sin(ang), z_hi], axis=1)
    return cos, sin


def _layer(x, mod, P):
    nb, seq, d_model = x.shape
    assert d_model == D_MODEL and TOKEN_TILE % K_TILE == 0
    assert seq % TOKEN_TILE == 0 and seq % Q_TILE == 0 and seq % POST_TILE == 0 and POST_TILE % RANK_CHUNK == 0
    n_tok = nb * seq
    q, kt, v, sn = _premix(x, mod, P)
    a = _attention(q, kt, v)
    x1, h2rows, eid, rank, wcol, counts = _postmix(a, sn, x, mod, P)

    tile = _expert_tile(n_tok)
    cnt = counts[:, 0].astype(jnp.int32)
    padded = ((cnt + tile - 1) // tile) * tile
    ends = jnp.cumsum(padded)
    starts = ends - padded
    eflat = jnp.transpose(eid, (1, 0, 2)).reshape(2, n_tok)
    rflat = jnp.transpose(rank, (1, 0, 2)).reshape(2, n_tok)
    onehot = eflat[:, :, None] == jnp.arange(N_EXPERTS, dtype=jnp.int32)[None, None, :]
    pos = rflat + jnp.sum(jnp.where(onehot, starts[None, None, :], 0), axis=2)
    n_rows = 2 * n_tok + N_EXPERTS * tile
    n_tiles = n_rows // tile
    tile_start = jnp.arange(n_tiles, dtype=jnp.int32) * tile
    tile_expert = jnp.minimum(
        jnp.sum((tile_start[:, None] >= ends[None, :]).astype(jnp.int32), axis=1), N_EXPERTS - 1).astype(jnp.int32)
    expert_ids = jnp.arange(N_EXPERTS, dtype=jnp.int32)
    live_end = jnp.sum(jnp.where(tile_expert[:, None] == expert_ids[None, :], (starts + cnt)[None, :], 0), axis=1)
    tile_rows = jnp.clip(live_end - tile_start, 0, tile).astype(jnp.int32)
    n_valid = (ends[-1:] // tile).astype(jnp.int32)

    as_tiles = lambda a: a.reshape(-1, ROW_TILES, LANES)
    as_rows = lambda a: a.reshape(-1, LANES)
    xs = _sc_scatter_rows(as_tiles(h2rows), pos, n_rows)
    ys = _experts(as_rows(xs), tile_expert, tile_rows, n_valid, tile, P)
    yg = as_rows(_sc_gather_rows(as_tiles(ys), pos.T.reshape(2 * n_tok)))
    return _final(yg, wcol, x1, mod, P)


def kernel(x_prompt, x_sample, c_prompt, c_sample, w_ada, b_ada, g_pre1, g_post1, g_pre2, g_post2, w_in, g_q, w_uq,
           g_kv, w_ukv, g_v_gmlp, w_spatial, b_spatial, g_attn_out, g_gmlp_out, w_out, w_router_group,
           b_router_group, w_router_expert, b_router_expert, w_gate, w_up, w_down):
    P = _prepare(dict(
        g_pre1=g_pre1, g_post1=g_post1, g_pre2=g_pre2, g_post2=g_post2, w_in=w_in, g_q=g_q, w_uq=w_uq, g_kv=g_kv,
        w_ukv=w_ukv, g_v_gmlp=g_v_gmlp, w_spatial=w_spatial, b_spatial=b_spatial, g_attn_out=g_attn_out,
        g_gmlp_out=g_gmlp_out, w_out=w_out, w_router_group=w_router_group, b_router_group=b_router_group,
        w_router_expert=w_router_expert, b_router_expert=b_router_expert, w_gate=w_gate, w_up=w_up, w_down=w_down))
    P["rope"] = {seq: _rope_tables(seq) for seq in {x_prompt.shape[1], x_sample.shape[1]}}

    nbp = c_prompt.shape[0]
    c_all = jnp.concatenate([c_prompt, c_sample], axis=0).astype(F32)
    mod = _ada(c_all, w_ada.astype(F32), b_ada.astype(F32))
    mod = mod.reshape(c_all.shape[0], 6, D_MODEL)
    mod = jnp.concatenate([mod, jnp.zeros((c_all.shape[0], SUBLANES - 6, D_MODEL), F32)], axis=1)

    y_prompt = _layer(x_prompt, mod[:nbp], P)
    y_sample = _layer(x_sample, mod[nbp:], P)
    return (y_prompt, y_sample)
```

```python
import functools
import math

import jax
import jax.numpy as jnp
from jax import lax
from jax.experimental import pallas as pl
from jax.experimental.pallas import tpu as pltpu
from jax.experimental.pallas import tpu_sc as plsc

F32 = jnp.float32
BF16 = jnp.bfloat16

D_MODEL = 1024
N_HEADS = 8
QK_NOPE = 64
QK_ROPE = 32
ROPE_HALF = QK_ROPE // 2
V_HEAD = 64
Q_LORA = 256
KV_LORA = 128
GMLP_WIDTH = 512
GMLP_GROUPS = 8
GMLP_GROUP_DIM = 64
CHUNK = 128
N_EXPERTS = 32
N_EXPERT_GROUPS = 4
EXPERTS_PER_GROUP = 8
EXPERT_FF = 256
ROPE_THETA = 10000.0
EPS = 1e-6

LANES = 128
SUBLANES = 8
HEAD_PAD = LANES

TOKEN_TILE = 1024
POST_TILE = 1024
Q_TILE = 1024
K_TILE = 512
EXPERT_TILE_SMALL = 512
EXPERT_TILE_LARGE = 1024
SC_CORES = 2
SC_WORKERS = 32
SC_ROWS = 128
RANK_CHUNK = 256
FINAL_INPUT_BUFFERS = 3
VMEM_LIMIT = 56 * 1024 * 1024

U32 = jnp.uint32
PACKED_WIDTH = D_MODEL // 2
ROW_TILES = PACKED_WIDTH // LANES
_HI_MASK = 0xFFFF0000

_SQRT_2_OVER_PI = math.sqrt(2.0 / math.pi)


def _rms(x):
    return x * lax.rsqrt(jnp.mean(x * x, axis=-1, keepdims=True) + EPS)


def _gelu_tanh(x):
    return 0.5 * x * (1.0 + jnp.tanh(_SQRT_2_OVER_PI * (x + 0.044715 * (x * x * x))))


def _split_bf16(x):
    hi = x.astype(BF16)
    lo = (x - hi.astype(F32)).astype(BF16)
    return hi, lo


def _dot(a, b):
    return jnp.dot(a, b, preferred_element_type=F32)


def _bf16_bits(x):
    return lax.bitcast_convert_type(x.astype(BF16).astype(F32), U32)


def _load_token_rows(ref, n):
    w = jnp.concatenate([ref[pl.ds(c, n, stride=ROW_TILES), :] for c in range(ROW_TILES)], axis=1)
    lo = lax.bitcast_convert_type(w << 16, F32)
    hi = lax.bitcast_convert_type(w & jnp.uint32(_HI_MASK), F32)
    return jnp.concatenate([lo, hi], axis=1)


def _store_token_rows(ref, val):
    n = val.shape[0]
    w = (_bf16_bits(val[:, :PACKED_WIDTH]) >> 16) | (_bf16_bits(val[:, PACKED_WIDTH:]) & jnp.uint32(_HI_MASK))
    for c in range(ROW_TILES):
        ref[pl.ds(c, n, stride=ROW_TILES), :] = w[:, c * LANES:(c + 1) * LANES]


def _params(sem, vmem=VMEM_LIMIT):
    return pltpu.CompilerParams(dimension_semantics=sem, vmem_limit_bytes=vmem)


def _ada_kernel(c_ref, w_ref, b_ref, o_ref):
    c = c_ref[...]
    a = c * jax.nn.sigmoid(c)
    a_hi, a_lo = _split_bf16(a)
    w_hi, w_lo = _split_bf16(w_ref[...])
    o_ref[...] = _dot(a_hi, w_hi) + _dot(a_hi, w_lo) + _dot(a_lo, w_hi) + b_ref[...]


def _ada(c, w_ada, b_ada):
    nb = c.shape[0]
    n_out = w_ada.shape[1]
    blk = D_MODEL
    return pl.pallas_call(
        _ada_kernel,
        grid=(n_out // blk,),
        in_specs=[
            pl.BlockSpec((nb, D_MODEL), lambda j: (0, 0)),
            pl.BlockSpec((D_MODEL, blk), lambda j: (0, j)),
            pl.BlockSpec((1, blk), lambda j: (0, j)),
        ],
        out_specs=pl.BlockSpec((nb, blk), lambda j: (0, j)),
        out_shape=jax.ShapeDtypeStruct((nb, n_out), F32),
        compiler_params=_params(("arbitrary",)),
        name="ada",
    )(c, w_ada, b_ada.reshape(1, n_out))


_C_CQ = 0
_C_CKV = _C_CQ + Q_LORA
_C_KR = _C_CKV + KV_LORA
_C_U = _C_KR + LANES
_C_V = _C_U + GMLP_WIDTH
_C_END = _C_V + GMLP_WIDTH
_ROPE_LO = QK_NOPE
_ROLL_PARTNER = LANES - QK_ROPE


def _premix_kernel(x_ref, mod_ref, cos_ref, sin_ref, gpre_ref, win_ref, gq_ref, wuq_ref, gkv_ref, wukv_ref,
                   gv_ref, wsp_ref, bsp_ref, ggo_ref, q_ref, kt_ref, v_ref, sn_ref):
    x = x_ref[0]
    mod = mod_ref[0]
    shift1, scale1 = mod[0:1], mod[1:2]
    h = _rms(x) * (gpre_ref[...] * (1.0 + scale1)) + shift1
    z = _dot(h.astype(BF16), win_ref[...])

    cosb = cos_ref[...]
    sinb = sin_ref[...]
    lane = lax.broadcasted_iota(jnp.int32, (1, LANES), 1)
    nope_mask = jnp.where(lane < QK_NOPE, 1.0, 0.0).astype(F32)

    qscale = (QK_NOPE + QK_ROPE) ** -0.5 * math.log2(math.e)
    cq_tab = (nope_mask + cosb) * qscale
    sq_tab = sinb * qscale
    cqn = (_rms(z[:, _C_CQ:_C_CKV]) * gq_ref[...]).astype(BF16)
    qb = _dot(cqn, wuq_ref[...])
    for hd in range(N_HEADS):
        blk = qb[:, hd * HEAD_PAD:(hd + 1) * HEAD_PAD]
        qh = blk * cq_tab + pltpu.roll(blk, _ROLL_PARTNER, 1) * sq_tab
        q_ref[0, hd] = qh.astype(BF16)

    ckvn = (_rms(z[:, _C_CKV:_C_KR]) * gkv_ref[...]).astype(BF16)
    kvb = _dot(ckvn, wukv_ref[...])
    krb = z[:, _C_KR:_C_U]
    krope = krb * cosb + pltpu.roll(krb, _ROLL_PARTNER, 1) * sinb
    v_off = N_HEADS * HEAD_PAD
    for hd in range(N_HEADS):
        kh = kvb[:, hd * HEAD_PAD:(hd + 1) * HEAD_PAD] + krope
        for c in range(kt_ref.shape[2]):
            kt_ref[0, hd, c] = kh[c * K_TILE:(c + 1) * K_TILE].T.astype(BF16)
        ones_lane = V_HEAD if hd % 2 == 0 else 0
        vh = kvb[:, v_off + hd * HEAD_PAD:v_off + (hd + 1) * HEAD_PAD] + jnp.where(lane == ones_lane, 1.0, 0.0)
        v_ref[0, hd] = vh.astype(BF16)

    ua = _gelu_tanh(z[:, _C_U:_C_V])
    vn = (_rms(_gelu_tanh(z[:, _C_V:_C_END])) * gv_ref[...]).astype(BF16)
    n_tok = x.shape[0]
    bsp = bsp_ref[...]
    rows = []
    for n in range(n_tok // CHUNK):
        cols = []
        for j in range(GMLP_GROUPS // 2):
            rhs = vn[n * CHUNK:(n + 1) * CHUNK, j * LANES:(j + 1) * LANES]
            ab = _dot(wsp_ref[j], rhs)
            cols.append(jnp.where(lane < GMLP_GROUP_DIM, ab[:CHUNK], ab[CHUNK:]))
        rows.append(jnp.concatenate(cols, axis=1) + bsp)
    s = ua * jnp.concatenate(rows, axis=0)
    sn_ref[0] = (_rms(s) * ggo_ref[...]).astype(BF16)


def _premix(x, mod, P):
    nb, seq, _ = x.shape
    tm = TOKEN_TILE
    nck = seq // K_TILE
    const = lambda i, b: (0, 0)
    return pl.pallas_call(
        _premix_kernel,
        grid=(seq // tm, nb),
        in_specs=[
            pl.BlockSpec((1, tm, D_MODEL), lambda i, b: (b, i, 0)),
            pl.BlockSpec((1, SUBLANES, D_MODEL), lambda i, b: (b, 0, 0)),
            pl.BlockSpec((tm, LANES), lambda i, b: (i, 0)),
            pl.BlockSpec((tm, LANES), lambda i, b: (i, 0)),
            pl.BlockSpec((1, D_MODEL), const),
            pl.BlockSpec((D_MODEL, _C_END), const),
            pl.BlockSpec((1, Q_LORA), const),
            pl.BlockSpec((Q_LORA, N_HEADS * HEAD_PAD), const),
            pl.BlockSpec((1, KV_LORA), const),
            pl.BlockSpec((KV_LORA, 2 * N_HEADS * HEAD_PAD), const),
            pl.BlockSpec((1, GMLP_WIDTH), const),
            pl.BlockSpec((GMLP_GROUPS // 2, 2 * CHUNK, CHUNK), lambda i, b: (0, 0, 0)),
            pl.BlockSpec((CHUNK, GMLP_WIDTH), const),
            pl.BlockSpec((1, GMLP_WIDTH), const),
        ],
        out_specs=[
            pl.BlockSpec((1, N_HEADS, tm, HEAD_PAD), lambda i, b: (b, 0, i, 0)),
            pl.BlockSpec((1, N_HEADS, tm // K_TILE, HEAD_PAD, K_TILE), lambda i, b: (b, 0, i, 0, 0)),
            pl.BlockSpec((1, N_HEADS, tm, HEAD_PAD), lambda i, b: (b, 0, i, 0)),
            pl.BlockSpec((1, tm, GMLP_WIDTH), lambda i, b: (b, i, 0)),
        ],
        out_shape=[
            jax.ShapeDtypeStruct((nb, N_HEADS, seq, HEAD_PAD), BF16),
            jax.ShapeDtypeStruct((nb, N_HEADS, nck, HEAD_PAD, K_TILE), BF16),
            jax.ShapeDtypeStruct((nb, N_HEADS, seq, HEAD_PAD), BF16),
            jax.ShapeDtypeStruct((nb, seq, GMLP_WIDTH), BF16),
        ],
        compiler_params=_params(("arbitrary", "arbitrary")),
        name="premix",
    )(x, mod, *P["rope"][seq], P["g_pre1"], P["w_in"], P["g_q"], P["w_uq"], P["g_kv"], P["w_ukv"],
      P["g_v"], P["w_sp"], P["b_sp"], P["g_gmlp_out"])


def _attn_kernel(q_ref, kt_ref, v_ref, o_ref):
    n_chunks = kt_ref.shape[2]
    tk = kt_ref.shape[4]
    lane = lax.broadcasted_iota(jnp.int32, (1, LANES), 1)

    def one_head(hd, ones_lane):
        q = q_ref[0, hd]
        m = None
        acc = None
        for c in range(n_chunks):
            s = _dot(q, kt_ref[0, hd, c])
            v = v_ref[0, hd, c * tk:(c + 1) * tk, :]
            smax = jnp.max(s, axis=1, keepdims=True)
            if c == 0:
                m = smax
                acc = _dot(jnp.exp2((s - m).astype(BF16)), v)
            else:
                m_new = jnp.maximum(m, smax)
                acc = acc * jnp.exp2(m - m_new) + _dot(jnp.exp2((s - m_new).astype(BF16)), v)
                m = m_new
        row_sum = acc[:, ones_lane:ones_lane + 1]
        return acc * (1.0 / row_sum)

    def pair(j, carry):
        even = one_head(2 * j, V_HEAD)
        odd = one_head(2 * j + 1, 0)
        o_ref[0, j] = jnp.where(lane < V_HEAD, even, odd).astype(BF16)
        return carry

    lax.fori_loop(0, N_HEADS // 2, pair, 0)


def _attention(q, kt, v):
    nb, _, seq, _ = q.shape
    nck = kt.shape[2]
    return pl.pallas_call(
        _attn_kernel,
        grid=(nb, seq // Q_TILE),
        in_specs=[
            pl.BlockSpec((1, N_HEADS, Q_TILE, HEAD_PAD), lambda b, i: (b, 0, i, 0)),
            pl.BlockSpec((1, N_HEADS, nck, HEAD_PAD, K_TILE), lambda b, i: (b, 0, 0, 0, 0)),
            pl.BlockSpec((1, N_HEADS, seq, HEAD_PAD), lambda b, i: (b, 0, 0, 0)),
        ],
        out_specs=pl.BlockSpec((1, N_HEADS // 2, Q_TILE, LANES), lambda b, i: (b, 0, i, 0)),
        out_shape=jax.ShapeDtypeStruct((nb, N_HEADS // 2, seq, LANES), BF16),
        compiler_params=_params(("arbitrary", "arbitrary")),
        name="attn",
    )(q, kt, v)


_R_GROUP_ROW = N_EXPERTS


def _postmix_kernel(a_ref, sn_ref, x_ref, mod_ref, gao_ref, wout_ref, gpost_ref, gpre_ref, wr_ref, br_ref, tri_ref,
                    x1_ref, h2_ref, eid_ref, rank_ref, wcol_ref, cnt_ref, run_ref):
    first = jnp.logical_and(pl.program_id(0) == 0, pl.program_id(1) == 0)

    @pl.when(first)
    def _():
        run_ref[...] = jnp.zeros_like(run_ref)

    mod = mod_ref[0]
    gate1, shift2, scale2 = mod[2:3], mod[3:4], mod[4:5]
    a = jnp.concatenate([a_ref[0, j] for j in range(N_HEADS // 2)], axis=1).astype(F32)
    an = (_rms(a) * gao_ref[...]).astype(BF16)
    merged = jnp.concatenate([an, sn_ref[0]], axis=1)
    o = _dot(merged, wout_ref[...])
    x1 = x_ref[0] + gate1 * (_rms(o) * gpost_ref[...])
    x1_ref[0] = x1
    h2 = _rms(x1) * (gpre_ref[...] * (1.0 + scale2)) + shift2
    n_tok = h2.shape[0]
    _store_token_rows(h2_ref, h2)

    h_hi, h_lo = _split_bf16(h2)
    wr = wr_ref[...]
    hh = _dot(h_hi, wr)
    lh = _dot(h_lo, wr[:, :LANES])
    logits = hh[:, :LANES] + hh[:, LANES:] + lh + br_ref[...]
    lt = logits.T

    neg = jnp.float32(-jnp.inf)
    row8 = lax.broadcasted_iota(jnp.int32, (SUBLANES, n_tok), 0).astype(F32)
    lg = jnp.where(row8 < N_EXPERT_GROUPS, lt[_R_GROUP_ROW:_R_GROUP_ROW + SUBLANES], neg)
    gmax = jnp.max(lg, axis=0, keepdims=True)
    gi = jnp.min(jnp.where(lg == gmax, row8, float(SUBLANES)), axis=0, keepdims=True)
    pg_sel = 1.0 / jnp.sum(jnp.exp(lg - gmax), axis=0, keepdims=True)

    le = jnp.zeros((EXPERTS_PER_GROUP, n_tok), F32)
    for g in range(N_EXPERT_GROUPS):
        le = jnp.where(gi == float(g), lt[g * EXPERTS_PER_GROUP:(g + 1) * EXPERTS_PER_GROUP], le)
    v1 = jnp.max(le, axis=0, keepdims=True)
    i1 = jnp.min(jnp.where(le == v1, row8, float(SUBLANES)), axis=0, keepdims=True)
    le2 = jnp.where(row8 == i1, neg, le)
    v2 = jnp.max(le2, axis=0, keepdims=True)
    i2 = jnp.min(jnp.where(le2 == v2, row8, float(SUBLANES)), axis=0, keepdims=True)
    r = jnp.exp(v2 - v1)
    w1 = pg_sel / (1.0 + r)
    w2 = w1 * r
    e1 = gi * float(EXPERTS_PER_GROUP) + i1
    e2 = gi * float(EXPERTS_PER_GROUP) + i2
    eid_ref[...] = jnp.concatenate([e1, e2], axis=0).astype(jnp.int32)

    row32 = lax.broadcasted_iota(jnp.int32, (N_EXPERTS, n_tok), 0).astype(F32)
    hit1 = row32 == e1
    hit2 = row32 == e2
    onehot = jnp.where(jnp.logical_or(hit1, hit2), 1.0, 0.0)
    run = run_ref[...][:, 0:1]
    ranks1, ranks2 = [], []
    for c in range(n_tok // RANK_CHUNK):
        sl = slice(c * RANK_CHUNK, (c + 1) * RANK_CHUNK)
        oh = onehot[:, sl]
        before = _dot(oh.astype(BF16), tri_ref[...]) + run
        ranks1.append(jnp.sum(jnp.where(hit1[:, sl], before, 0.0), axis=0, keepdims=True))
        ranks2.append(jnp.sum(jnp.where(hit2[:, sl], before, 0.0), axis=0, keepdims=True))
        run = run + jnp.sum(oh, axis=1, keepdims=True)
    rank_ref[...] = jnp.concatenate(
        [jnp.concatenate(ranks1, axis=1), jnp.concatenate(ranks2, axis=1)], axis=0).astype(jnp.int32)
    run_b = jnp.broadcast_to(run, run_ref.shape)
    run_ref[...] = run_b
    cnt_ref[...] = run_b

    wcol_ref[0] = jnp.where(row8 == 0.0, w1, jnp.where(row8 == 1.0, w2, 0.0))


def _postmix(a, sn, x, mod, P):
    nb, seq, _ = x.shape
    tm = POST_TILE
    const = lambda b, i: (0, 0)
    return pl.pallas_call(
        _postmix_kernel,
        grid=(nb, seq // tm),
        in_specs=[
            pl.BlockSpec((1, N_HEADS // 2, tm, LANES), lambda b, i: (b, 0, i, 0)),
            pl.BlockSpec((1, tm, GMLP_WIDTH), lambda b, i: (b, i, 0)),
            pl.BlockSpec((1, tm, D_MODEL), lambda b, i: (b, i, 0)),
            pl.BlockSpec((1, SUBLANES, D_MODEL), lambda b, i: (b, 0, 0)),
            pl.BlockSpec((1, N_HEADS * V_HEAD), const),
            pl.BlockSpec((D_MODEL, D_MODEL), const),
            pl.BlockSpec((1, D_MODEL), const),
            pl.BlockSpec((1, D_MODEL), const),
            pl.BlockSpec((D_MODEL, 2 * LANES), const),
            pl.BlockSpec((1, LANES), const),
            pl.BlockSpec((RANK_CHUNK, RANK_CHUNK), const),
        ],
        out_specs=[
            pl.BlockSpec((1, tm, D_MODEL), lambda b, i: (b, i, 0)),
            pl.BlockSpec((tm * ROW_TILES, LANES), lambda b, i: (b * (seq // tm) + i, 0)),
            pl.BlockSpec((2, tm), lambda b, i: (0, b * (seq // tm) + i)),
            pl.BlockSpec((2, tm), lambda b, i: (0, b * (seq // tm) + i)),
            pl.BlockSpec((1, SUBLANES, tm), lambda b, i: (b, 0, i)),
            pl.BlockSpec((N_EXPERTS, LANES), const),
        ],
        out_shape=[
            jax.ShapeDtypeStruct((nb, seq, D_MODEL), F32),
            jax.ShapeDtypeStruct((nb * seq * ROW_TILES, LANES), U32),
            jax.ShapeDtypeStruct((2, nb * seq), jnp.int32),
            jax.ShapeDtypeStruct((2, nb * seq), jnp.int32),
            jax.ShapeDtypeStruct((nb, SUBLANES, seq), F32),
            jax.ShapeDtypeStruct((N_EXPERTS, LANES), F32),
        ],
        scratch_shapes=[pltpu.VMEM((N_EXPERTS, LANES), F32)],
        compiler_params=_params(("arbitrary", "arbitrary")),
        name="postmix",
    )(a, sn, x, mod, P["g_attn_out"], P["w_out"], P["g_post1"], P["g_pre2"], P["w_router"], P["b_router"], P["tri"])


def _sc_mesh():
    return plsc.VectorSubcoreMesh(core_axis_name="c", subcore_axis_name="s")


def _sc_worker():
    return lax.axis_index("s") * SC_CORES + lax.axis_index("c")


def _sc_gather_rows(table, idx):
    n = idx.shape[0]
    assert n % (SC_WORKERS * SC_ROWS) == 0
    per_w = n // SC_WORKERS
    steps = per_w // SC_ROWS
    idx3 = idx.reshape(SC_WORKERS, steps, SC_ROWS)

    @functools.partial(
        pl.kernel, mesh=_sc_mesh(),
        out_type=jax.ShapeDtypeStruct((n, ROW_TILES, LANES), U32),
        scratch_types=[pltpu.VMEM((steps, SC_ROWS), jnp.int32), pltpu.VMEM((SC_ROWS, ROW_TILES, LANES), U32),
                       pltpu.SemaphoreType.DMA],
        name="sc_gather_rows",
    )
    def gather(table_hbm, idx_hbm, out_hbm, idx_v, rows_v, sem):
        wid = _sc_worker()
        pltpu.sync_copy(idx_hbm.at[wid], idx_v)

        @pl.loop(0, steps)
        def _(j):
            pltpu.async_copy(table_hbm.at[idx_v.at[j]], rows_v, sem).wait()
            pltpu.sync_copy(rows_v, out_hbm.at[pl.ds(wid * per_w + j * SC_ROWS, SC_ROWS)])

    return gather(table, idx3)


def _sc_scatter_rows(src, idx, n_dst):
    n_dup, n = idx.shape
    assert n % (SC_WORKERS * SC_ROWS) == 0
    per_w = n // SC_WORKERS
    steps = per_w // SC_ROWS
    idx4 = idx.reshape(n_dup, SC_WORKERS, steps, SC_ROWS).transpose(1, 0, 2, 3)

    @functools.partial(
        pl.kernel, mesh=_sc_mesh(),
        out_type=jax.ShapeDtypeStruct((n_dst, ROW_TILES, LANES), U32),
        scratch_types=[pltpu.VMEM((n_dup, steps, SC_ROWS), jnp.int32), pltpu.VMEM((SC_ROWS, ROW_TILES, LANES), U32),
                       pltpu.SemaphoreType.DMA],
        name="sc_scatter_rows",
    )
    def scatter(src_hbm, idx_hbm, dst_hbm, idx_v, rows_v, sem):
        wid = _sc_worker()
        pltpu.sync_copy(idx_hbm.at[wid], idx_v)

        @pl.loop(0, steps)
        def _(j):
            pltpu.sync_copy(src_hbm.at[pl.ds(wid * per_w + j * SC_ROWS, SC_ROWS)], rows_v)
            copies = [pltpu.async_copy(rows_v, dst_hbm.at[idx_v.at[k, j]], sem) for k in range(n_dup)]
            for copy in copies:
                copy.wait()

    return scatter(src, idx4)


def _expert_kernel(te_ref, tr_ref, nv_ref, xs_ref, wg_ref, wu_ref, wd_ref, ys_ref, wgu_bf, wd_bf):
    i = pl.program_id(0)
    valid = i < nv_ref[0]
    new_expert = jnp.logical_or(i == 0, te_ref[i] != te_ref[jnp.maximum(i - 1, 0)])

    @pl.when(jnp.logical_and(valid, new_expert))
    def _():
        wgu_bf[:, :EXPERT_FF] = wg_ref[0].astype(BF16)
        wgu_bf[:, EXPERT_FF:] = wu_ref[0].astype(BF16)
        wd_bf[...] = wd_ref[0].astype(BF16)

    @pl.when(valid)
    def _():
        n_rows = xs_ref.shape[0] // ROW_TILES
        x = _load_token_rows(xs_ref, n_rows)
        row = lax.broadcasted_iota(jnp.int32, (n_rows, 1), 0)
        x = jnp.where(row < tr_ref[i], x, 0.0).astype(BF16)
        gu = _dot(x, wgu_bf[...])
        g, u = gu[:, :EXPERT_FF], gu[:, EXPERT_FF:]
        act = (g * jax.nn.sigmoid(g) * u).astype(BF16)
        _store_token_rows(ys_ref, _dot(act, wd_bf[...]))

    @pl.when(jnp.logical_not(valid))
    def _():
        ys_ref[...] = jnp.zeros_like(ys_ref)


def _expert_tile(n_tok):
    return EXPERT_TILE_LARGE if 2 * n_tok // N_EXPERTS >= 2 * EXPERT_TILE_LARGE else EXPERT_TILE_SMALL


def _experts(xs, tile_expert, tile_rows, n_valid, tile, P):
    rows_blk = tile * ROW_TILES
    assert xs.shape[0] % rows_blk == 0
    n_tiles = xs.shape[0] // rows_blk

    def row_map(i, te, tr, nv):
        return (jnp.minimum(i, nv[0] - 1), 0)

    def out_map(i, te, tr, nv):
        return (i, 0)

    def w_map(i, te, tr, nv):
        return (te[jnp.minimum(i, nv[0] - 1)], 0, 0)

    return pl.pallas_call(
        _expert_kernel,
        grid_spec=pltpu.PrefetchScalarGridSpec(
            num_scalar_prefetch=3,
            grid=(n_tiles,),
            in_specs=[
                pl.BlockSpec((rows_blk, LANES), row_map),
                pl.BlockSpec((1, D_MODEL, EXPERT_FF), w_map),
                pl.BlockSpec((1, D_MODEL, EXPERT_FF), w_map),
                pl.BlockSpec((1, EXPERT_FF, D_MODEL), w_map),
            ],
            out_specs=pl.BlockSpec((rows_blk, LANES), out_map),
            scratch_shapes=[pltpu.VMEM((D_MODEL, 2 * EXPERT_FF), BF16), pltpu.VMEM((EXPERT_FF, D_MODEL), BF16)],
        ),
        out_shape=jax.ShapeDtypeStruct(xs.shape, U32),
        compiler_params=_params(("arbitrary",)),
        name="experts",
    )(tile_expert, tile_rows, n_valid, xs, P["w_gate"], P["w_up"], P["w_down"])


def _final_kernel(y0_ref, y1_ref, wcol_ref, x1_ref, mod_ref, gpost_ref, o_ref):
    w_rows = wcol_ref[0]
    n_tok = w_rows.shape[1]
    w = jnp.concatenate([w_rows, jnp.zeros((LANES - SUBLANES, n_tok), F32)], axis=0).T
    w0, w1 = w[:, 0:1], w[:, 1:2]
    m = w0 * _load_token_rows(y0_ref, n_tok) + w1 * _load_token_rows(y1_ref, n_tok)
    gate2 = mod_ref[0][5:6]
    o_ref[0] = x1_ref[0] + gate2 * (_rms(m) * gpost_ref[...])


def _final(yg, wcol, x1, mod, P):
    nb, seq, _ = x1.shape
    tm = POST_TILE
    nt = seq // tm
    n_tok_tiles = nb * nt
    deep = pl.Buffered(FINAL_INPUT_BUFFERS)

    def pipelined(yg_hbm, wcol_hbm, x1_hbm, mod_hbm, gpost_hbm, o_hbm):
        pltpu.emit_pipeline(
            _final_kernel,
            grid=(nb, nt),
            in_specs=[
                pl.BlockSpec((tm * ROW_TILES, LANES), lambda b, i: (b * nt + i, 0), pipeline_mode=deep),
                pl.BlockSpec((tm * ROW_TILES, LANES), lambda b, i: (n_tok_tiles + b * nt + i, 0), pipeline_mode=deep),
                pl.BlockSpec((1, SUBLANES, tm), lambda b, i: (b, 0, i), pipeline_mode=deep),
                pl.BlockSpec((1, tm, D_MODEL), lambda b, i: (b, i, 0), pipeline_mode=deep),
                pl.BlockSpec((1, SUBLANES, D_MODEL), lambda b, i: (b, 0, 0)),
                pl.BlockSpec((1, D_MODEL), lambda b, i: (0, 0)),
            ],
            out_specs=[pl.BlockSpec((1, tm, D_MODEL), lambda b, i: (b, i, 0))],
        )(yg_hbm, yg_hbm, wcol_hbm, x1_hbm, mod_hbm, gpost_hbm, o_hbm)

    any_spec = pl.BlockSpec(memory_space=pl.ANY)
    return pl.pallas_call(
        pipelined,
        in_specs=[any_spec] * 5,
        out_specs=any_spec,
        out_shape=jax.ShapeDtypeStruct((nb, seq, D_MODEL), F32),
        compiler_params=pltpu.CompilerParams(vmem_limit_bytes=VMEM_LIMIT),
        name="final",
    )(yg, wcol, x1, mod, P["g_post2"])


def _prepare(w):
    f = lambda a: a.astype(F32)
    P = {}
    for k in ("g_pre1", "g_post1", "g_pre2", "g_post2", "g_q", "g_kv", "g_attn_out", "g_gmlp_out"):
        P[k] = f(w[k]).reshape(1, -1)
    P["g_v"] = f(w["g_v_gmlp"]).reshape(1, -1)

    w_in = f(w["w_in"])
    o0, o1, o2, o3 = Q_LORA, Q_LORA + KV_LORA, Q_LORA + KV_LORA + QK_ROPE, Q_LORA + KV_LORA + QK_ROPE + GMLP_WIDTH
    w_kr = w_in[:, o1:o2]
    kr_partner = jnp.concatenate([-w_kr[:, ROPE_HALF:], w_kr[:, :ROPE_HALF]], axis=1)
    rope_blk = jnp.concatenate([jnp.zeros((D_MODEL, QK_NOPE), F32), w_kr, kr_partner], axis=1)
    P["w_in"] = jnp.concatenate([w_in[:, :o1], rope_blk, w_in[:, o2:o3], w_in[:, o3:]], axis=1).astype(BF16)

    w_uq = f(w["w_uq"]).reshape(Q_LORA, N_HEADS, QK_NOPE + QK_ROPE)
    q_rope = w_uq[:, :, QK_NOPE:]
    q_partner = jnp.concatenate([-q_rope[:, :, ROPE_HALF:], q_rope[:, :, :ROPE_HALF]], axis=2)
    P["w_uq"] = jnp.concatenate([w_uq, q_partner], axis=2).reshape(Q_LORA, N_HEADS * HEAD_PAD).astype(BF16)

    w_ukv = f(w["w_ukv"]).reshape(KV_LORA, N_HEADS, QK_NOPE + V_HEAD)
    zeros = jnp.zeros((KV_LORA, N_HEADS, HEAD_PAD - QK_NOPE), F32)
    w_k = jnp.concatenate([w_ukv[:, :, :QK_NOPE], zeros], axis=2)
    w_v = w_ukv[:, :, QK_NOPE:]
    even = (jnp.arange(N_HEADS) % 2 == 0)[None, :, None]
    zv = jnp.zeros_like(w_v)
    w_v = jnp.concatenate([jnp.where(even, w_v, zv), jnp.where(even, zv, w_v)], axis=2)
    P["w_ukv"] = jnp.concatenate([w_k.reshape(KV_LORA, -1), w_v.reshape(KV_LORA, -1)], axis=1).astype(BF16)

    P["w_sp"] = f(w["w_spatial"]).reshape(GMLP_GROUPS // 2, 2 * CHUNK, CHUNK).astype(BF16)
    P["b_sp"] = jnp.repeat(f(w["b_spatial"]).T, GMLP_GROUP_DIM, axis=1)

    P["w_out"] = f(w["w_out"]).astype(BF16)

    pad = jnp.zeros((D_MODEL, LANES - N_EXPERTS - N_EXPERT_GROUPS), F32)
    wr = jnp.concatenate([f(w["w_router_expert"]), f(w["w_router_group"]), pad], axis=1)
    wr_hi = wr.astype(BF16)
    wr_lo = (wr - wr_hi.astype(F32)).astype(BF16)
    P["w_router"] = jnp.concatenate([wr_hi, wr_lo], axis=1)
    P["b_router"] = jnp.concatenate(
        [f(w["b_router_expert"]), f(w["b_router_group"]), jnp.zeros((LANES - N_EXPERTS - N_EXPERT_GROUPS,), F32)]
    ).reshape(1, LANES)
    P["tri"] = jnp.triu(jnp.ones((RANK_CHUNK, RANK_CHUNK), F32), k=1).astype(BF16)

    P["w_gate"], P["w_up"], P["w_down"] = f(w["w_gate"]), f(w["w_up"]), f(w["w_down"])
    return P


def _rope_tables(seq):
    inv = ROPE_THETA ** (-jnp.arange(ROPE_HALF, dtype=F32) / ROPE_HALF)
    ang = jnp.arange(seq, dtype=F32)[:, None] * inv[None, :]
    z_lo = jnp.zeros((seq, _ROPE_LO), F32)
    z_hi = jnp.zeros((seq, LANES - _ROPE_LO - QK_ROPE), F32)
    cos = jnp.concatenate([z_lo, jnp.cos(ang), jnp.cos(ang), z_hi], axis=1)
    sin = jnp.concatenate([z_lo, jnp.sin(ang), jnp.sin(ang), z_hi], axis=1)
    return cos, sin


def _layer(x, mod, P):
    nb, seq, d_model = x.shape
    assert d_model == D_MODEL and TOKEN_TILE % K_TILE == 0
    assert seq % TOKEN_TILE == 0 and seq % Q_TILE == 0 and seq % POST_TILE == 0 and POST_TILE % RANK_CHUNK == 0
    n_tok = nb * seq
    q, kt, v, sn = _premix(x, mod, P)
    a = _attention(q, kt, v)
    x1, h2rows, eid, rank, wcol, counts = _postmix(a, sn, x, mod, P)

    tile = _expert_tile(n_tok)
    cnt = counts[:, 0].astype(jnp.int32)
    padded = ((cnt + tile - 1) // tile) * tile
    ends = jnp.cumsum(padded)
    starts = ends - padded
    onehot = eid[:, :, None] == jnp.arange(N_EXPERTS, dtype=jnp.int32)[None, None, :]
    pos = rank + jnp.sum(jnp.where(onehot, starts[None, None, :], 0), axis=2)
    n_rows = 2 * n_tok + N_EXPERTS * tile
    n_tiles = n_rows // tile
    tile_start = jnp.arange(n_tiles, dtype=jnp.int32) * tile
    tile_expert = jnp.minimum(
        jnp.sum((tile_start[:, None] >= ends[None, :]).astype(jnp.int32), axis=1), N_EXPERTS - 1).astype(jnp.int32)
    expert_ids = jnp.arange(N_EXPERTS, dtype=jnp.int32)
    live_end = jnp.sum(jnp.where(tile_expert[:, None] == expert_ids[None, :], (starts + cnt)[None, :], 0), axis=1)
    tile_rows = jnp.clip(live_end - tile_start, 0, tile).astype(jnp.int32)
    n_valid = (ends[-1:] // tile).astype(jnp.int32)

    as_tiles = lambda a: a.reshape(-1, ROW_TILES, LANES)
    as_rows = lambda a: a.reshape(-1, LANES)
    xs = _sc_scatter_rows(as_tiles(h2rows), pos, n_rows)
    ys = _experts(as_rows(xs), tile_expert, tile_rows, n_valid, tile, P)
    yg = as_rows(_sc_gather_rows(as_tiles(ys), pos.reshape(2 * n_tok)))
    return _final(yg, wcol, x1, mod, P)


def kernel(x_prompt, x_sample, c_prompt, c_sample, w_ada, b_ada, g_pre1, g_post1, g_pre2, g_post2, w_in, g_q, w_uq,
           g_kv, w_ukv, g_v_gmlp, w_spatial, b_spatial, g_attn_out, g_gmlp_out, w_out, w_router_group,
           b_router_group, w_router_expert, b_router_expert, w_gate, w_up, w_down):
    P = _prepare(dict(
        g_pre1=g_pre1, g_post1=g_post1, g_pre2=g_pre2, g_post2=g_post2, w_in=w_in, g_q=g_q, w_uq=w_uq, g_kv=g_kv,
        w_ukv=w_ukv, g_v_gmlp=g_v_gmlp, w_spatial=w_spatial, b_spatial=b_spatial, g_attn_out=g_attn_out,
        g_gmlp_out=g_gmlp_out, w_out=w_out, w_router_group=w_router_group, b_router_group=b_router_group,
        w_router_expert=w_router_expert, b_router_expert=b_router_expert, w_gate=w_gate, w_up=w_up, w_down=w_down))
    P["rope"] = {seq: _rope_tables(seq) for seq in {x_prompt.shape[1], x_sample.shape[1]}}

    nbp = c_prompt.shape[0]
    c_all = jnp.concatenate([c_prompt, c_sample], axis=0).astype(F32)
    mod = _ada(c_all, w_ada.astype(F32), b_ada.astype(F32))
    mod = mod.reshape(c_all.shape[0], 6, D_MODEL)
    mod = jnp.concatenate([mod, jnp.zeros((c_all.shape[0], SUBLANES - 6, D_MODEL), F32)], axis=1)

    y_prompt = _layer(x_prompt, mod[:nbp], P)
    y_sample = _layer(x_sample, mod[nbp:], P)
    return (y_prompt, y_sample)
```

```python
import functools
import math

import jax
import jax.numpy as jnp
from jax import lax
from jax.experimental import pallas as pl
from jax.experimental.pallas import tpu as pltpu
from jax.experimental.pallas import tpu_sc as plsc

F32 = jnp.float32
BF16 = jnp.bfloat16

D_MODEL = 1024
N_HEADS = 8
QK_NOPE = 64
QK_ROPE = 32
ROPE_HALF = QK_ROPE // 2
V_HEAD = 64
Q_LORA = 256
KV_LORA = 128
GMLP_WIDTH = 512
GMLP_GROUPS = 8
GMLP_GROUP_DIM = 64
CHUNK = 128
N_EXPERTS = 32
N_EXPERT_GROUPS = 4
EXPERTS_PER_GROUP = 8
EXPERT_FF = 256
ROPE_THETA = 10000.0
EPS = 1e-6

LANES = 128
SUBLANES = 8
HEAD_PAD = LANES

TOKEN_TILE = 1024
POST_TILE = 1024
Q_TILE = 1024
K_TILE = 512
EXPERT_TILE_SMALL = 512
EXPERT_TILE_LARGE = 1024
SC_CORES = 2
SC_WORKERS = 32
SC_ROWS = 128
RANK_CHUNK = 256
FINAL_INPUT_BUFFERS = 3
VMEM_LIMIT = 56 * 1024 * 1024

U32 = jnp.uint32
PACKED_WIDTH = D_MODEL // 2
ROW_TILES = PACKED_WIDTH // LANES
_HI_MASK = 0xFFFF0000

_SQRT_2_OVER_PI = math.sqrt(2.0 / math.pi)


def _rms(x):
    return x * lax.rsqrt(jnp.mean(x * x, axis=-1, keepdims=True) + EPS)


def _gelu_tanh(x):
    return 0.5 * x * (1.0 + jnp.tanh(_SQRT_2_OVER_PI * (x + 0.044715 * (x * x * x))))


def _split_bf16(x):
    hi = x.astype(BF16)
    lo = (x - hi.astype(F32)).astype(BF16)
    return hi, lo


def _dot(a, b):
    return jnp.dot(a, b, preferred_element_type=F32)


def _bf16_bits(x):
    return lax.bitcast_convert_type(x.astype(BF16).astype(F32), U32)


def _load_token_rows(ref, n):
    w = jnp.concatenate([ref[pl.ds(c, n, stride=ROW_TILES), :] for c in range(ROW_TILES)], axis=1)
    lo = lax.bitcast_convert_type(w << 16, F32)
    hi = lax.bitcast_convert_type(w & jnp.uint32(_HI_MASK), F32)
    return jnp.concatenate([lo, hi], axis=1)


def _store_token_rows(ref, val):
    n = val.shape[0]
    w = (_bf16_bits(val[:, :PACKED_WIDTH]) >> 16) | (_bf16_bits(val[:, PACKED_WIDTH:]) & jnp.uint32(_HI_MASK))
    for c in range(ROW_TILES):
        ref[pl.ds(c, n, stride=ROW_TILES), :] = w[:, c * LANES:(c + 1) * LANES]


def _params(sem, vmem=VMEM_LIMIT):
    return pltpu.CompilerParams(dimension_semantics=sem, vmem_limit_bytes=vmem)


def _ada_kernel(c_ref, w_ref, b_ref, o_ref):
    c = c_ref[...]
    a = c * jax.nn.sigmoid(c)
    a_hi, a_lo = _split_bf16(a)
    w_hi, w_lo = _split_bf16(w_ref[...])
    o_ref[...] = _dot(a_hi, w_hi) + _dot(a_hi, w_lo) + _dot(a_lo, w_hi) + b_ref[...]


def _ada(c, w_ada, b_ada):
    nb = c.shape[0]
    n_out = w_ada.shape[1]
    blk = D_MODEL
    return pl.pallas_call(
        _ada_kernel,
        grid=(n_out // blk,),
        in_specs=[
            pl.BlockSpec((nb, D_MODEL), lambda j: (0, 0)),
            pl.BlockSpec((D_MODEL, blk), lambda j: (0, j)),
            pl.BlockSpec((1, blk), lambda j: (0, j)),
        ],
        out_specs=pl.BlockSpec((nb, blk), lambda j: (0, j)),
        out_shape=jax.ShapeDtypeStruct((nb, n_out), F32),
        compiler_params=_params(("arbitrary",)),
        name="ada",
    )(c, w_ada, b_ada.reshape(1, n_out))


_C_CQ = 0
_C_CKV = _C_CQ + Q_LORA
_C_KR = _C_CKV + KV_LORA
_C_U = _C_KR + LANES
_C_V = _C_U + GMLP_WIDTH
_C_END = _C_V + GMLP_WIDTH
_ROPE_LO = QK_NOPE
_ROLL_PARTNER = LANES - QK_ROPE


def _premix_kernel(x_ref, mod_ref, cos_ref, sin_ref, gpre_ref, win_ref, gq_ref, wuq_ref, gkv_ref, wukv_ref,
                   gv_ref, wsp_ref, bsp_ref, ggo_ref, q_ref, kt_ref, v_ref, sn_ref):
    x = x_ref[0]
    mod = mod_ref[0]
    shift1, scale1 = mod[0:1], mod[1:2]
    h = _rms(x) * (gpre_ref[...] * (1.0 + scale1)) + shift1
    z = _dot(h.astype(BF16), win_ref[...])

    cosb = cos_ref[...]
    sinb = sin_ref[...]
    lane = lax.broadcasted_iota(jnp.int32, (1, LANES), 1)
    nope_mask = jnp.where(lane < QK_NOPE, 1.0, 0.0).astype(F32)

    qscale = (QK_NOPE + QK_ROPE) ** -0.5 * math.log2(math.e)
    cq_tab = (nope_mask + cosb) * qscale
    sq_tab = sinb * qscale
    cqn = (_rms(z[:, _C_CQ:_C_CKV]) * gq_ref[...]).astype(BF16)
    qb = _dot(cqn, wuq_ref[...])
    for hd in range(N_HEADS):
        blk = qb[:, hd * HEAD_PAD:(hd + 1) * HEAD_PAD]
        qh = blk * cq_tab + pltpu.roll(blk, _ROLL_PARTNER, 1) * sq_tab
        q_ref[0, hd] = qh.astype(BF16)

    ckvn = (_rms(z[:, _C_CKV:_C_KR]) * gkv_ref[...]).astype(BF16)
    kvb = _dot(ckvn, wukv_ref[...])
    krb = z[:, _C_KR:_C_U]
    krope = krb * cosb + pltpu.roll(krb, _ROLL_PARTNER, 1) * sinb
    v_off = N_HEADS * HEAD_PAD
    for hd in range(N_HEADS):
        kh = kvb[:, hd * HEAD_PAD:(hd + 1) * HEAD_PAD] + krope
        for c in range(kt_ref.shape[2]):
            kt_ref[0, hd, c] = kh[c * K_TILE:(c + 1) * K_TILE].T.astype(BF16)
        ones_lane = V_HEAD if hd % 2 == 0 else 0
        vh = kvb[:, v_off + hd * HEAD_PAD:v_off + (hd + 1) * HEAD_PAD] + jnp.where(lane == ones_lane, 1.0, 0.0)
        v_ref[0, hd] = vh.astype(BF16)

    ua = _gelu_tanh(z[:, _C_U:_C_V])
    vn = (_rms(_gelu_tanh(z[:, _C_V:_C_END])) * gv_ref[...]).astype(BF16)
    n_tok = x.shape[0]
    bsp = bsp_ref[...]
    rows = []
    for n in range(n_tok // CHUNK):
        cols = []
        for j in range(GMLP_GROUPS // 2):
            rhs = vn[n * CHUNK:(n + 1) * CHUNK, j * LANES:(j + 1) * LANES]
            ab = _dot(wsp_ref[j], rhs)
            cols.append(jnp.where(lane < GMLP_GROUP_DIM, ab[:CHUNK], ab[CHUNK:]))
        rows.append(jnp.concatenate(cols, axis=1) + bsp)
    s = ua * jnp.concatenate(rows, axis=0)
    sn_ref[0] = (_rms(s) * ggo_ref[...]).astype(BF16)


def _premix(x, mod, P):
    nb, seq, _ = x.shape
    tm = TOKEN_TILE
    nck = seq // K_TILE
    const = lambda i, b: (0, 0)
    return pl.pallas_call(
        _premix_kernel,
        grid=(seq // tm, nb),
        in_specs=[
            pl.BlockSpec((1, tm, D_MODEL), lambda i, b: (b, i, 0)),
            pl.BlockSpec((1, SUBLANES, D_MODEL), lambda i, b: (b, 0, 0)),
            pl.BlockSpec((tm, LANES), lambda i, b: (i, 0)),
            pl.BlockSpec((tm, LANES), lambda i, b: (i, 0)),
            pl.BlockSpec((1, D_MODEL), const),
            pl.BlockSpec((D_MODEL, _C_END), const),
            pl.BlockSpec((1, Q_LORA), const),
            pl.BlockSpec((Q_LORA, N_HEADS * HEAD_PAD), const),
            pl.BlockSpec((1, KV_LORA), const),
            pl.BlockSpec((KV_LORA, 2 * N_HEADS * HEAD_PAD), const),
            pl.BlockSpec((1, GMLP_WIDTH), const),
            pl.BlockSpec((GMLP_GROUPS // 2, 2 * CHUNK, CHUNK), lambda i, b: (0, 0, 0)),
            pl.BlockSpec((CHUNK, GMLP_WIDTH), const),
            pl.BlockSpec((1, GMLP_WIDTH), const),
        ],
        out_specs=[
            pl.BlockSpec((1, N_HEADS, tm, HEAD_PAD), lambda i, b: (b, 0, i, 0)),
            pl.BlockSpec((1, N_HEADS, tm // K_TILE, HEAD_PAD, K_TILE), lambda i, b: (b, 0, i, 0, 0)),
            pl.BlockSpec((1, N_HEADS, tm, HEAD_PAD), lambda i, b: (b, 0, i, 0)),
            pl.BlockSpec((1, tm, GMLP_WIDTH), lambda i, b: (b, i, 0)),
        ],
        out_shape=[
            jax.ShapeDtypeStruct((nb, N_HEADS, seq, HEAD_PAD), BF16),
            jax.ShapeDtypeStruct((nb, N_HEADS, nck, HEAD_PAD, K_TILE), BF16),
            jax.ShapeDtypeStruct((nb, N_HEADS, seq, HEAD_PAD), BF16),
            jax.ShapeDtypeStruct((nb, seq, GMLP_WIDTH), BF16),
        ],
        compiler_params=_params(("arbitrary", "arbitrary")),
        name="premix",
    )(x, mod, *P["rope"][seq], P["g_pre1"], P["w_in"], P["g_q"], P["w_uq"], P["g_kv"], P["w_ukv"],
      P["g_v"], P["w_sp"], P["b_sp"], P["g_gmlp_out"])


def _attn_kernel(q_ref, kt_ref, v_ref, o_ref):
    n_chunks = kt_ref.shape[2]
    tk = kt_ref.shape[4]
    lane = lax.broadcasted_iota(jnp.int32, (1, LANES), 1)

    def one_head(hd, ones_lane):
        q = q_ref[0, hd]
        m = None
        acc = None
        for c in range(n_chunks):
            s = _dot(q, kt_ref[0, hd, c])
            v = v_ref[0, hd, c * tk:(c + 1) * tk, :]
            smax = jnp.max(s, axis=1, keepdims=True)
            if c == 0:
                m = smax
                acc = _dot(jnp.exp2((s - m).astype(BF16)), v)
            else:
                m_new = jnp.maximum(m, smax)
                acc = acc * jnp.exp2(m - m_new) + _dot(jnp.exp2((s - m_new).astype(BF16)), v)
                m = m_new
        row_sum = acc[:, ones_lane:ones_lane + 1]
        return acc * (1.0 / row_sum)

    def pair(j, carry):
        even = one_head(2 * j, V_HEAD)
        odd = one_head(2 * j + 1, 0)
        o_ref[0, j] = jnp.where(lane < V_HEAD, even, odd).astype(BF16)
        return carry

    lax.fori_loop(0, N_HEADS // 2, pair, 0)


def _attention(q, kt, v):
    nb, _, seq, _ = q.shape
    nck = kt.shape[2]
    return pl.pallas_call(
        _attn_kernel,
        grid=(nb, seq // Q_TILE),
        in_specs=[
            pl.BlockSpec((1, N_HEADS, Q_TILE, HEAD_PAD), lambda b, i: (b, 0, i, 0)),
            pl.BlockSpec((1, N_HEADS, nck, HEAD_PAD, K_TILE), lambda b, i: (b, 0, 0, 0, 0)),
            pl.BlockSpec((1, N_HEADS, seq, HEAD_PAD), lambda b, i: (b, 0, 0, 0)),
        ],
        out_specs=pl.BlockSpec((1, N_HEADS // 2, Q_TILE, LANES), lambda b, i: (b, 0, i, 0)),
        out_shape=jax.ShapeDtypeStruct((nb, N_HEADS // 2, seq, LANES), BF16),
        compiler_params=_params(("arbitrary", "arbitrary")),
        name="attn",
    )(q, kt, v)


_R_GROUP_ROW = N_EXPERTS


def _postmix_kernel(a_ref, sn_ref, x_ref, mod_ref, gao_ref, wout_ref, gpost_ref, gpre_ref, wr_ref, br_ref, tri_ref,
                    x1_ref, h2_ref, eid_ref, rank_ref, wcol_ref, cnt_ref, run_ref):
    first = jnp.logical_and(pl.program_id(0) == 0, pl.program_id(1) == 0)

    @pl.when(first)
    def _():
        run_ref[...] = jnp.zeros_like(run_ref)

    mod = mod_ref[0]
    gate1, shift2, scale2 = mod[2:3], mod[3:4], mod[4:5]
    a = jnp.concatenate([a_ref[0, j] for j in range(N_HEADS // 2)], axis=1).astype(F32)
    an = (_rms(a) * gao_ref[...]).astype(BF16)
    merged = jnp.concatenate([an, sn_ref[0]], axis=1)
    o = _dot(merged, wout_ref[...])
    x1 = x_ref[0] + gate1 * (_rms(o) * gpost_ref[...])
    x1_ref[0] = x1
    h2 = _rms(x1) * (gpre_ref[...] * (1.0 + scale2)) + shift2
    n_tok = h2.shape[0]
    _store_token_rows(h2_ref, h2)

    h_hi, h_lo = _split_bf16(h2)
    wr = wr_ref[...]
    hh = _dot(h_hi, wr)
    lh = _dot(h_lo, wr[:, :LANES])
    logits = hh[:, :LANES] + hh[:, LANES:] + lh + br_ref[...]
    lt = logits.T

    neg = jnp.float32(-jnp.inf)
    row8 = lax.broadcasted_iota(jnp.int32, (SUBLANES, n_tok), 0).astype(F32)
    lg = jnp.where(row8 < N_EXPERT_GROUPS, lt[_R_GROUP_ROW:_R_GROUP_ROW + SUBLANES], neg)
    gmax = jnp.max(lg, axis=0, keepdims=True)
    gi = jnp.min(jnp.where(lg == gmax, row8, float(SUBLANES)), axis=0, keepdims=True)
    pg_sel = 1.0 / jnp.sum(jnp.exp(lg - gmax), axis=0, keepdims=True)

    le = jnp.zeros((EXPERTS_PER_GROUP, n_tok), F32)
    for g in range(N_EXPERT_GROUPS):
        le = jnp.where(gi == float(g), lt[g * EXPERTS_PER_GROUP:(g + 1) * EXPERTS_PER_GROUP], le)
    v1 = jnp.max(le, axis=0, keepdims=True)
    i1 = jnp.min(jnp.where(le == v1, row8, float(SUBLANES)), axis=0, keepdims=True)
    le2 = jnp.where(row8 == i1, neg, le)
    v2 = jnp.max(le2, axis=0, keepdims=True)
    i2 = jnp.min(jnp.where(le2 == v2, row8, float(SUBLANES)), axis=0, keepdims=True)
    r = jnp.exp(v2 - v1)
    w1 = pg_sel / (1.0 + r)
    w2 = w1 * r
    e1 = gi * float(EXPERTS_PER_GROUP) + i1
    e2 = gi * float(EXPERTS_PER_GROUP) + i2
    eid_ref[...] = jnp.concatenate([e1, e2], axis=0).astype(jnp.int32)

    row32 = lax.broadcasted_iota(jnp.int32, (N_EXPERTS, n_tok), 0).astype(F32)
    hit1 = row32 == e1
    hit2 = row32 == e2
    onehot = jnp.where(jnp.logical_or(hit1, hit2), 1.0, 0.0)
    run = run_ref[...][:, 0:1]
    ranks1, ranks2 = [], []
    for c in range(n_tok // RANK_CHUNK):
        sl = slice(c * RANK_CHUNK, (c + 1) * RANK_CHUNK)
        oh = onehot[:, sl]
        before = _dot(oh.astype(BF16), tri_ref[...]) + run
        ranks1.append(jnp.sum(jnp.where(hit1[:, sl], before, 0.0), axis=0, keepdims=True))
        ranks2.append(jnp.sum(jnp.where(hit2[:, sl], before, 0.0), axis=0, keepdims=True))
        run = run + jnp.sum(oh, axis=1, keepdims=True)
    rank_ref[...] = jnp.concatenate(
        [jnp.concatenate(ranks1, axis=1), jnp.concatenate(ranks2, axis=1)], axis=0).astype(jnp.int32)
    run_b = jnp.broadcast_to(run, run_ref.shape)
    run_ref[...] = run_b
    cnt_ref[...] = run_b

    wcol_ref[0] = jnp.where(row8 == 0.0, w1, jnp.where(row8 == 1.0, w2, 0.0))


def _postmix(a, sn, x, mod, P):
    nb, seq, _ = x.shape
    tm = POST_TILE
    const = lambda b, i: (0, 0)
    return pl.pallas_call(
        _postmix_kernel,
        grid=(nb, seq // tm),
        in_specs=[
            pl.BlockSpec((1, N_HEADS // 2, tm, LANES), lambda b, i: (b, 0, i, 0)),
            pl.BlockSpec((1, tm, GMLP_WIDTH), lambda b, i: (b, i, 0)),
            pl.BlockSpec((1, tm, D_MODEL), lambda b, i: (b, i, 0)),
            pl.BlockSpec((1, SUBLANES, D_MODEL), lambda b, i: (b, 0, 0)),
            pl.BlockSpec((1, N_HEADS * V_HEAD), const),
            pl.BlockSpec((D_MODEL, D_MODEL), const),
            pl.BlockSpec((1, D_MODEL), const),
            pl.BlockSpec((1, D_MODEL), const),
            pl.BlockSpec((D_MODEL, 2 * LANES), const),
            pl.BlockSpec((1, LANES), const),
            pl.BlockSpec((RANK_CHUNK, RANK_CHUNK), const),
        ],
        out_specs=[
            pl.BlockSpec((1, tm, D_MODEL), lambda b, i: (b, i, 0)),
            pl.BlockSpec((tm * ROW_TILES, LANES), lambda b, i: (b * (seq // tm) + i, 0)),
            pl.BlockSpec((2, tm), lambda b, i: (0, b * (seq // tm) + i)),
            pl.BlockSpec((2, tm), lambda b, i: (0, b * (seq // tm) + i)),
            pl.BlockSpec((1, SUBLANES, tm), lambda b, i: (b, 0, i)),
            pl.BlockSpec((N_EXPERTS, LANES), const),
        ],
        out_shape=[
            jax.ShapeDtypeStruct((nb, seq, D_MODEL), F32),
            jax.ShapeDtypeStruct((nb * seq * ROW_TILES, LANES), U32),
            jax.ShapeDtypeStruct((2, nb * seq), jnp.int32),
            jax.ShapeDtypeStruct((2, nb * seq), jnp.int32),
            jax.ShapeDtypeStruct((nb, SUBLANES, seq), F32),
            jax.ShapeDtypeStruct((N_EXPERTS, LANES), F32),
        ],
        scratch_shapes=[pltpu.VMEM((N_EXPERTS, LANES), F32)],
        compiler_params=_params(("arbitrary", "arbitrary")),
        name="postmix",
    )(a, sn, x, mod, P["g_attn_out"], P["w_out"], P["g_post1"], P["g_pre2"], P["w_router"], P["b_router"], P["tri"])


def _sc_mesh():
    return plsc.VectorSubcoreMesh(core_axis_name="c", subcore_axis_name="s")


def _sc_worker():
    return lax.axis_index("s") * SC_CORES + lax.axis_index("c")


def _sc_gather_rows(table, idx):
    n = idx.shape[0]
    assert n % (SC_WORKERS * SC_ROWS) == 0
    per_w = n // SC_WORKERS
    steps = per_w // SC_ROWS
    idx3 = idx.reshape(SC_WORKERS, steps, SC_ROWS)

    @functools.partial(
        pl.kernel, mesh=_sc_mesh(),
        out_type=jax.ShapeDtypeStruct((n, ROW_TILES, LANES), U32),
        scratch_types=[pltpu.VMEM((steps, SC_ROWS), jnp.int32), pltpu.VMEM((SC_ROWS, ROW_TILES, LANES), U32),
                       pltpu.SemaphoreType.DMA],
        name="sc_gather_rows",
    )
    def gather(table_hbm, idx_hbm, out_hbm, idx_v, rows_v, sem):
        wid = _sc_worker()
        pltpu.sync_copy(idx_hbm.at[wid], idx_v)

        @pl.loop(0, steps)
        def _(j):
            pltpu.async_copy(table_hbm.at[idx_v.at[j]], rows_v, sem).wait()
            pltpu.sync_copy(rows_v, out_hbm.at[pl.ds(wid * per_w + j * SC_ROWS, SC_ROWS)])

    return gather(table, idx3)


def _sc_scatter_rows(src, idx, n_dst):
    n_dup, n = idx.shape
    assert n % (SC_WORKERS * SC_ROWS) == 0
    per_w = n // SC_WORKERS
    steps = per_w // SC_ROWS
    idx4 = idx.reshape(n_dup, SC_WORKERS, steps, SC_ROWS)

    @functools.partial(
        pl.kernel, mesh=_sc_mesh(),
        out_type=jax.ShapeDtypeStruct((n_dst, ROW_TILES, LANES), U32),
        scratch_types=[pltpu.VMEM((n_dup, steps, SC_ROWS), jnp.int32), pltpu.VMEM((SC_ROWS, ROW_TILES, LANES), U32),
                       pltpu.SemaphoreType.DMA],
        name="sc_scatter_rows",
    )
    def scatter(src_hbm, idx_hbm, dst_hbm, idx_v, rows_v, sem):
        wid = _sc_worker()
        for k in range(n_dup):
            pltpu.sync_copy(idx_hbm.at[k, wid], idx_v.at[k])

        @pl.loop(0, steps)
        def _(j):
            pltpu.sync_copy(src_hbm.at[pl.ds(wid * per_w + j * SC_ROWS, SC_ROWS)], rows_v)
            copies = [pltpu.async_copy(rows_v, dst_hbm.at[idx_v.at[k, j]], sem) for k in range(n_dup)]
            for copy in copies:
                copy.wait()

    return scatter(src, idx4)


def _expert_kernel(te_ref, tr_ref, nv_ref, xs_ref, wg_ref, wu_ref, wd_ref, ys_ref, wgu_bf, wd_bf):
    i = pl.program_id(0)
    valid = i < nv_ref[0]
    new_expert = jnp.logical_or(i == 0, te_ref[i] != te_ref[jnp.maximum(i - 1, 0)])

    @pl.when(jnp.logical_and(valid, new_expert))
    def _():
        wgu_bf[:, :EXPERT_FF] = wg_ref[0].astype(BF16)
        wgu_bf[:, EXPERT_FF:] = wu_ref[0].astype(BF16)
        wd_bf[...] = wd_ref[0].astype(BF16)

    @pl.when(valid)
    def _():
        n_rows = xs_ref.shape[0] // ROW_TILES
        x = _load_token_rows(xs_ref, n_rows)
        row = lax.broadcasted_iota(jnp.int32, (n_rows, 1), 0)
        x = jnp.where(row < tr_ref[i], x, 0.0).astype(BF16)
        gu = _dot(x, wgu_bf[...])
        g, u = gu[:, :EXPERT_FF], gu[:, EXPERT_FF:]
        act = (g * jax.nn.sigmoid(g) * u).astype(BF16)
        _store_token_rows(ys_ref, _dot(act, wd_bf[...]))

    @pl.when(jnp.logical_not(valid))
    def _():
        ys_ref[...] = jnp.zeros_like(ys_ref)


def _expert_tile(n_tok):
    return EXPERT_TILE_LARGE if 2 * n_tok // N_EXPERTS >= 2 * EXPERT_TILE_LARGE else EXPERT_TILE_SMALL


def _experts(xs, tile_expert, tile_rows, n_valid, tile, P):
    rows_blk = tile * ROW_TILES
    assert xs.shape[0] % rows_blk == 0
    n_tiles = xs.shape[0] // rows_blk

    def row_map(i, te, tr, nv):
        return (jnp.minimum(i, nv[0] - 1), 0)

    def out_map(i, te, tr, nv):
        return (i, 0)

    def w_map(i, te, tr, nv):
        return (te[jnp.minimum(i, nv[0] - 1)], 0, 0)

    return pl.pallas_call(
        _expert_kernel,
        grid_spec=pltpu.PrefetchScalarGridSpec(
            num_scalar_prefetch=3,
            grid=(n_tiles,),
            in_specs=[
                pl.BlockSpec((rows_blk, LANES), row_map),
                pl.BlockSpec((1, D_MODEL, EXPERT_FF), w_map),
                pl.BlockSpec((1, D_MODEL, EXPERT_FF), w_map),
                pl.BlockSpec((1, EXPERT_FF, D_MODEL), w_map),
            ],
            out_specs=pl.BlockSpec((rows_blk, LANES), out_map),
            scratch_shapes=[pltpu.VMEM((D_MODEL, 2 * EXPERT_FF), BF16), pltpu.VMEM((EXPERT_FF, D_MODEL), BF16)],
        ),
        out_shape=jax.ShapeDtypeStruct(xs.shape, U32),
        compiler_params=_params(("arbitrary",)),
        name="experts",
    )(tile_expert, tile_rows, n_valid, xs, P["w_gate"], P["w_up"], P["w_down"])


def _final_kernel(y0_ref, y1_ref, wcol_ref, x1_ref, mod_ref, gpost_ref, o_ref):
    w_rows = wcol_ref[0]
    n_tok = w_rows.shape[1]
    w = jnp.concatenate([w_rows, jnp.zeros((LANES - SUBLANES, n_tok), F32)], axis=0).T
    w0, w1 = w[:, 0:1], w[:, 1:2]
    m = w0 * _load_token_rows(y0_ref, n_tok) + w1 * _load_token_rows(y1_ref, n_tok)
    gate2 = mod_ref[0][5:6]
    o_ref[0] = x1_ref[0] + gate2 * (_rms(m) * gpost_ref[...])


def _final(yg, wcol, x1, mod, P):
    nb, seq, _ = x1.shape
    tm = POST_TILE
    nt = seq // tm
    n_tok_tiles = nb * nt
    deep = pl.Buffered(FINAL_INPUT_BUFFERS)

    def pipelined(yg_hbm, wcol_hbm, x1_hbm, mod_hbm, gpost_hbm, o_hbm):
        pltpu.emit_pipeline(
            _final_kernel,
            grid=(nb, nt),
            in_specs=[
                pl.BlockSpec((tm * ROW_TILES, LANES), lambda b, i: (b * nt + i, 0), pipeline_mode=deep),
                pl.BlockSpec((tm * ROW_TILES, LANES), lambda b, i: (n_tok_tiles + b * nt + i, 0), pipeline_mode=deep),
                pl.BlockSpec((1, SUBLANES, tm), lambda b, i: (b, 0, i), pipeline_mode=deep),
                pl.BlockSpec((1, tm, D_MODEL), lambda b, i: (b, i, 0), pipeline_mode=deep),
                pl.BlockSpec((1, SUBLANES, D_MODEL), lambda b, i: (b, 0, 0)),
                pl.BlockSpec((1, D_MODEL), lambda b, i: (0, 0)),
            ],
            out_specs=[pl.BlockSpec((1, tm, D_MODEL), lambda b, i: (b, i, 0))],
        )(yg_hbm, yg_hbm, wcol_hbm, x1_hbm, mod_hbm, gpost_hbm, o_hbm)

    any_spec = pl.BlockSpec(memory_space=pl.ANY)
    return pl.pallas_call(
        pipelined,
        in_specs=[any_spec] * 5,
        out_specs=any_spec,
        out_shape=jax.ShapeDtypeStruct((nb, seq, D_MODEL), F32),
        compiler_params=pltpu.CompilerParams(vmem_limit_bytes=VMEM_LIMIT),
        name="final",
    )(yg, wcol, x1, mod, P["g_post2"])


def _prepare(w):
    f = lambda a: a.astype(F32)
    P = {}
    for k in ("g_pre1", "g_post1", "g_pre2", "g_post2", "g_q", "g_kv", "g_attn_out", "g_gmlp_out"):
        P[k] = f(w[k]).reshape(1, -1)
    P["g_v"] = f(w["g_v_gmlp"]).reshape(1, -1)

    w_in = f(w["w_in"])
    o0, o1, o2, o3 = Q_LORA, Q_LORA + KV_LORA, Q_LORA + KV_LORA + QK_ROPE, Q_LORA + KV_LORA + QK_ROPE + GMLP_WIDTH
    w_kr = w_in[:, o1:o2]
    kr_partner = jnp.concatenate([-w_kr[:, ROPE_HALF:], w_kr[:, :ROPE_HALF]], axis=1)
    rope_blk = jnp.concatenate([jnp.zeros((D_MODEL, QK_NOPE), F32), w_kr, kr_partner], axis=1)
    P["w_in"] = jnp.concatenate([w_in[:, :o1], rope_blk, w_in[:, o2:o3], w_in[:, o3:]], axis=1).astype(BF16)

    w_uq = f(w["w_uq"]).reshape(Q_LORA, N_HEADS, QK_NOPE + QK_ROPE)
    q_rope = w_uq[:, :, QK_NOPE:]
    q_partner = jnp.concatenate([-q_rope[:, :, ROPE_HALF:], q_rope[:, :, :ROPE_HALF]], axis=2)
    P["w_uq"] = jnp.concatenate([w_uq, q_partner], axis=2).reshape(Q_LORA, N_HEADS * HEAD_PAD).astype(BF16)

    w_ukv = f(w["w_ukv"]).reshape(KV_LORA, N_HEADS, QK_NOPE + V_HEAD)
    zeros = jnp.zeros((KV_LORA, N_HEADS, HEAD_PAD - QK_NOPE), F32)
    w_k = jnp.concatenate([w_ukv[:, :, :QK_NOPE], zeros], axis=2)
    w_v = w_ukv[:, :, QK_NOPE:]
    even = (jnp.arange(N_HEADS) % 2 == 0)[None, :, None]
    zv = jnp.zeros_like(w_v)
    w_v = jnp.concatenate([jnp.where(even, w_v, zv), jnp.where(even, zv, w_v)], axis=2)
    P["w_ukv"] = jnp.concatenate([w_k.reshape(KV_LORA, -1), w_v.reshape(KV_LORA, -1)], axis=1).astype(BF16)

    P["w_sp"] = f(w["w_spatial"]).reshape(GMLP_GROUPS // 2, 2 * CHUNK, CHUNK).astype(BF16)
    P["b_sp"] = jnp.repeat(f(w["b_spatial"]).T, GMLP_GROUP_DIM, axis=1)

    P["w_out"] = f(w["w_out"]).astype(BF16)

    pad = jnp.zeros((D_MODEL, LANES - N_EXPERTS - N_EXPERT_GROUPS), F32)
    wr = jnp.concatenate([f(w["w_router_expert"]), f(w["w_router_group"]), pad], axis=1)
    wr_hi = wr.astype(BF16)
    wr_lo = (wr - wr_hi.astype(F32)).astype(BF16)
    P["w_router"] = jnp.concatenate([wr_hi, wr_lo], axis=1)
    P["b_router"] = jnp.concatenate(
        [f(w["b_router_expert"]), f(w["b_router_group"]), jnp.zeros((LANES - N_EXPERTS - N_EXPERT_GROUPS,), F32)]
    ).reshape(1, LANES)
    P["tri"] = jnp.triu(jnp.ones((RANK_CHUNK, RANK_CHUNK), F32), k=1).astype(BF16)

    P["w_gate"], P["w_up"], P["w_down"] = f(w["w_gate"]), f(w["w_up"]), f(w["w_down"])
    return P


def _rope_tables(seq):
    inv = ROPE_THETA ** (-jnp.arange(ROPE_HALF, dtype=F32) / ROPE_HALF)
    ang = jnp.arange(seq, dtype=F32)[:, None] * inv[None, :]
    z_lo = jnp.zeros((seq, _ROPE_LO), F32)
    z_hi = jnp.zeros((seq, LANES - _ROPE_LO - QK_ROPE), F32)
    cos = jnp.concatenate([z_lo, jnp.cos(ang), jnp.cos(ang), z_hi], axis=1)
    sin = jnp.concatenate([z_lo, jnp.sin(ang), jnp.sin(ang), z_hi], axis=1)
    return cos, sin


def _layer(x, mod, P):
    nb, seq, d_model = x.shape
    assert d_model == D_MODEL and TOKEN_TILE % K_TILE == 0
    assert seq % TOKEN_TILE == 0 and seq % Q_TILE == 0 and seq % POST_TILE == 0 and POST_TILE % RANK_CHUNK == 0
    n_tok = nb * seq
    q, kt, v, sn = _premix(x, mod, P)
    a = _attention(q, kt, v)
    x1, h2rows, eid, rank, wcol, counts = _postmix(a, sn, x, mod, P)

    tile = _expert_tile(n_tok)
    cnt = counts[:, 0].astype(jnp.int32)
    padded = ((cnt + tile - 1) // tile) * tile
    ends = jnp.cumsum(padded)
    starts = ends - padded
    onehot = eid[:, :, None] == jnp.arange(N_EXPERTS, dtype=jnp.int32)[None, None, :]
    pos = rank + jnp.sum(jnp.where(onehot, starts[None, None, :], 0), axis=2)
    n_rows = 2 * n_tok + N_EXPERTS * tile
    n_tiles = n_rows // tile
    tile_start = jnp.arange(n_tiles, dtype=jnp.int32) * tile
    tile_expert = jnp.minimum(
        jnp.sum((tile_start[:, None] >= ends[None, :]).astype(jnp.int32), axis=1), N_EXPERTS - 1).astype(jnp.int32)
    expert_ids = jnp.arange(N_EXPERTS, dtype=jnp.int32)
    live_end = jnp.sum(jnp.where(tile_expert[:, None] == expert_ids[None, :], (starts + cnt)[None, :], 0), axis=1)
    tile_rows = jnp.clip(live_end - tile_start, 0, tile).astype(jnp.int32)
    n_valid = (ends[-1:] // tile).astype(jnp.int32)

    as_tiles = lambda a: a.reshape(-1, ROW_TILES, LANES)
    as_rows = lambda a: a.reshape(-1, LANES)
    xs = _sc_scatter_rows(as_tiles(h2rows), pos, n_rows)
    ys = _experts(as_rows(xs), tile_expert, tile_rows, n_valid, tile, P)
    yg = as_rows(_sc_gather_rows(as_tiles(ys), pos.reshape(2 * n_tok)))
    return _final(yg, wcol, x1, mod, P)


def kernel(x_prompt, x_sample, c_prompt, c_sample, w_ada, b_ada, g_pre1, g_post1, g_pre2, g_post2, w_in, g_q, w_uq,
           g_kv, w_ukv, g_v_gmlp, w_spatial, b_spatial, g_attn_out, g_gmlp_out, w_out, w_router_group,
           b_router_group, w_router_expert, b_router_expert, w_gate, w_up, w_down):
    P = _prepare(dict(
        g_pre1=g_pre1, g_post1=g_post1, g_pre2=g_pre2, g_post2=g_post2, w_in=w_in, g_q=g_q, w_uq=w_uq, g_kv=g_kv,
        w_ukv=w_ukv, g_v_gmlp=g_v_gmlp, w_spatial=w_spatial, b_spatial=b_spatial, g_attn_out=g_attn_out,
        g_gmlp_out=g_gmlp_out, w_out=w_out, w_router_group=w_router_group, b_router_group=b_router_group,
        w_router_expert=w_router_expert, b_router_expert=b_router_expert, w_gate=w_gate, w_up=w_up, w_down=w_down))
    P["rope"] = {seq: _rope_tables(seq) for seq in {x_prompt.shape[1], x_sample.shape[1]}}

    nbp = c_prompt.shape[0]
    c_all = jnp.concatenate([c_prompt, c_sample], axis=0).astype(F32)
    mod = _ada(c_all, w_ada.astype(F32), b_ada.astype(F32))
    mod = mod.reshape(c_all.shape[0], 6, D_MODEL)
    mod = jnp.concatenate([mod, jnp.zeros((c_all.shape[0], SUBLANES - 6, D_MODEL), F32)], axis=1)

    y_prompt = _layer(x_prompt, mod[:nbp], P)
    y_sample = _layer(x_sample, mod[nbp:], P)
    return (y_prompt, y_sample)
```

```python
import functools
import math

import jax
import jax.numpy as jnp
from jax import lax
from jax.experimental import pallas as pl
from jax.experimental.pallas import tpu as pltpu
from jax.experimental.pallas import tpu_sc as plsc

F32 = jnp.float32
BF16 = jnp.bfloat16

D_MODEL = 1024
N_HEADS = 8
QK_NOPE = 64
QK_ROPE = 32
ROPE_HALF = QK_ROPE // 2
V_HEAD = 64
Q_LORA = 256
KV_LORA = 128
GMLP_WIDTH = 512
GMLP_GROUPS = 8
GMLP_GROUP_DIM = 64
CHUNK = 128
N_EXPERTS = 32
N_EXPERT_GROUPS = 4
EXPERTS_PER_GROUP = 8
EXPERT_FF = 256
ROPE_THETA = 10000.0
EPS = 1e-6

LANES = 128
SUBLANES = 8
HEAD_PAD = LANES

TOKEN_TILE = 1024
POST_TILE = 1024
Q_TILE = 1024
K_TILE = 512
EXPERT_TILE_SMALL = 512
EXPERT_TILE_LARGE = 1024
SC_CORES = 2
SC_WORKERS = 32
SC_ROWS = 128
RANK_CHUNK = 256
FINAL_INPUT_BUFFERS = 3
VMEM_LIMIT = 56 * 1024 * 1024

U32 = jnp.uint32
PACKED_WIDTH = D_MODEL // 2
ROW_TILES = PACKED_WIDTH // LANES
_HI_MASK = 0xFFFF0000

_SQRT_2_OVER_PI = math.sqrt(2.0 / math.pi)


def _rms(x):
    return x * lax.rsqrt(jnp.mean(x * x, axis=-1, keepdims=True) + EPS)


def _gelu_tanh(x):
    return 0.5 * x * (1.0 + jnp.tanh(_SQRT_2_OVER_PI * (x + 0.044715 * (x * x * x))))


def _split_bf16(x):
    hi = x.astype(BF16)
    lo = (x - hi.astype(F32)).astype(BF16)
    return hi, lo


def _dot(a, b):
    return jnp.dot(a, b, preferred_element_type=F32)


def _bf16_bits(x):
    return lax.bitcast_convert_type(x.astype(BF16).astype(F32), U32)


def _load_token_rows(ref, n):
    w = jnp.concatenate([ref[pl.ds(c, n, stride=ROW_TILES), :] for c in range(ROW_TILES)], axis=1)
    lo = lax.bitcast_convert_type(w << 16, F32)
    hi = lax.bitcast_convert_type(w & jnp.uint32(_HI_MASK), F32)
    return jnp.concatenate([lo, hi], axis=1)


def _store_token_rows(ref, val):
    n = val.shape[0]
    w = (_bf16_bits(val[:, :PACKED_WIDTH]) >> 16) | (_bf16_bits(val[:, PACKED_WIDTH:]) & jnp.uint32(_HI_MASK))
    for c in range(ROW_TILES):
        ref[pl.ds(c, n, stride=ROW_TILES), :] = w[:, c * LANES:(c + 1) * LANES]


def _params(sem, vmem=VMEM_LIMIT):
    return pltpu.CompilerParams(dimension_semantics=sem, vmem_limit_bytes=vmem)


def _ada_kernel(c_ref, w_ref, b_ref, o_ref):
    c = c_ref[...]
    a = c * jax.nn.sigmoid(c)
    a_hi, a_lo = _split_bf16(a)
    w_hi, w_lo = _split_bf16(w_ref[...])
    o_ref[...] = _dot(a_hi, w_hi) + _dot(a_hi, w_lo) + _dot(a_lo, w_hi) + b_ref[...]


def _ada(c, w_ada, b_ada):
    nb = c.shape[0]
    n_out = w_ada.shape[1]
    blk = D_MODEL
    return pl.pallas_call(
        _ada_kernel,
        grid=(n_out // blk,),
        in_specs=[
            pl.BlockSpec((nb, D_MODEL), lambda j: (0, 0)),
            pl.BlockSpec((D_MODEL, blk), lambda j: (0, j)),
            pl.BlockSpec((1, blk), lambda j: (0, j)),
        ],
        out_specs=pl.BlockSpec((nb, blk), lambda j: (0, j)),
        out_shape=jax.ShapeDtypeStruct((nb, n_out), F32),
        compiler_params=_params(("arbitrary",)),
        name="ada",
    )(c, w_ada, b_ada.reshape(1, n_out))


_C_CQ = 0
_C_CKV = _C_CQ + Q_LORA
_C_KR = _C_CKV + KV_LORA
_C_U = _C_KR + LANES
_C_V = _C_U + GMLP_WIDTH
_C_END = _C_V + GMLP_WIDTH
_ROPE_LO = QK_NOPE
_ROLL_PARTNER = LANES - QK_ROPE


def _premix_kernel(x_ref, mod_ref, cos_ref, sin_ref, gpre_ref, win_ref, gq_ref, wuq_ref, gkv_ref, wukv_ref,
                   gv_ref, wsp_ref, bsp_ref, ggo_ref, q_ref, kt_ref, v_ref, sn_ref):
    x = x_ref[0]
    mod = mod_ref[0]
    shift1, scale1 = mod[0:1], mod[1:2]
    h = _rms(x) * (gpre_ref[...] * (1.0 + scale1)) + shift1
    z = _dot(h.astype(BF16), win_ref[...])

    cosb = cos_ref[...]
    sinb = sin_ref[...]
    lane = lax.broadcasted_iota(jnp.int32, (1, LANES), 1)
    nope_mask = jnp.where(lane < QK_NOPE, 1.0, 0.0).astype(F32)

    qscale = (QK_NOPE + QK_ROPE) ** -0.5 * math.log2(math.e)
    cq_tab = (nope_mask + cosb) * qscale
    sq_tab = sinb * qscale
    cqn = (_rms(z[:, _C_CQ:_C_CKV]) * gq_ref[...]).astype(BF16)
    qb = _dot(cqn, wuq_ref[...])
    for hd in range(N_HEADS):
        blk = qb[:, hd * HEAD_PAD:(hd + 1) * HEAD_PAD]
        qh = blk * cq_tab + pltpu.roll(blk, _ROLL_PARTNER, 1) * sq_tab
        q_ref[0, hd] = qh.astype(BF16)

    ckvn = (_rms(z[:, _C_CKV:_C_KR]) * gkv_ref[...]).astype(BF16)
    kvb = _dot(ckvn, wukv_ref[...])
    krb = z[:, _C_KR:_C_U]
    krope = krb * cosb + pltpu.roll(krb, _ROLL_PARTNER, 1) * sinb
    v_off = N_HEADS * HEAD_PAD
    for hd in range(N_HEADS):
        kh = kvb[:, hd * HEAD_PAD:(hd + 1) * HEAD_PAD] + krope
        for c in range(kt_ref.shape[2]):
            kt_ref[0, hd, c] = kh[c * K_TILE:(c + 1) * K_TILE].T.astype(BF16)
        ones_lane = V_HEAD if hd % 2 == 0 else 0
        vh = kvb[:, v_off + hd * HEAD_PAD:v_off + (hd + 1) * HEAD_PAD] + jnp.where(lane == ones_lane, 1.0, 0.0)
        v_ref[0, hd] = vh.astype(BF16)

    ua = _gelu_tanh(z[:, _C_U:_C_V])
    vn = (_rms(_gelu_tanh(z[:, _C_V:_C_END])) * gv_ref[...]).astype(BF16)
    n_tok = x.shape[0]
    bsp = bsp_ref[...]
    rows = []
    for n in range(n_tok // CHUNK):
        cols = []
        for j in range(GMLP_GROUPS // 2):
            rhs = vn[n * CHUNK:(n + 1) * CHUNK, j * LANES:(j + 1) * LANES]
            ab = _dot(wsp_ref[j], rhs)
            cols.append(jnp.where(lane < GMLP_GROUP_DIM, ab[:CHUNK], ab[CHUNK:]))
        rows.append(jnp.concatenate(cols, axis=1) + bsp)
    s = ua * jnp.concatenate(rows, axis=0)
    sn_ref[0] = (_rms(s) * ggo_ref[...]).astype(BF16)


def _premix(x, mod, P):
    nb, seq, _ = x.shape
    tm = TOKEN_TILE
    nck = seq // K_TILE
    const = lambda i, b: (0, 0)
    return pl.pallas_call(
        _premix_kernel,
        grid=(seq // tm, nb),
        in_specs=[
            pl.BlockSpec((1, tm, D_MODEL), lambda i, b: (b, i, 0)),
            pl.BlockSpec((1, SUBLANES, D_MODEL), lambda i, b: (b, 0, 0)),
            pl.BlockSpec((tm, LANES), lambda i, b: (i, 0)),
            pl.BlockSpec((tm, LANES), lambda i, b: (i, 0)),
            pl.BlockSpec((1, D_MODEL), const),
            pl.BlockSpec((D_MODEL, _C_END), const),
            pl.BlockSpec((1, Q_LORA), const),
            pl.BlockSpec((Q_LORA, N_HEADS * HEAD_PAD), const),
            pl.BlockSpec((1, KV_LORA), const),
            pl.BlockSpec((KV_LORA, 2 * N_HEADS * HEAD_PAD), const),
            pl.BlockSpec((1, GMLP_WIDTH), const),
            pl.BlockSpec((GMLP_GROUPS // 2, 2 * CHUNK, CHUNK), lambda i, b: (0, 0, 0)),
            pl.BlockSpec((CHUNK, GMLP_WIDTH), const),
            pl.BlockSpec((1, GMLP_WIDTH), const),
        ],
        out_specs=[
            pl.BlockSpec((1, N_HEADS, tm, HEAD_PAD), lambda i, b: (b, 0, i, 0)),
            pl.BlockSpec((1, N_HEADS, tm // K_TILE, HEAD_PAD, K_TILE), lambda i, b: (b, 0, i, 0, 0)),
            pl.BlockSpec((1, N_HEADS, tm, HEAD_PAD), lambda i, b: (b, 0, i, 0)),
            pl.BlockSpec((1, tm, GMLP_WIDTH), lambda i, b: (b, i, 0)),
        ],
        out_shape=[
            jax.ShapeDtypeStruct((nb, N_HEADS, seq, HEAD_PAD), BF16),
            jax.ShapeDtypeStruct((nb, N_HEADS, nck, HEAD_PAD, K_TILE), BF16),
            jax.ShapeDtypeStruct((nb, N_HEADS, seq, HEAD_PAD), BF16),
            jax.ShapeDtypeStruct((nb, seq, GMLP_WIDTH), BF16),
        ],
        compiler_params=_params(("arbitrary", "arbitrary")),
        name="premix",
    )(x, mod, *P["rope"][seq], P["g_pre1"], P["w_in"], P["g_q"], P["w_uq"], P["g_kv"], P["w_ukv"],
      P["g_v"], P["w_sp"], P["b_sp"], P["g_gmlp_out"])


def _attn_kernel(q_ref, kt_ref, v_ref, o_ref):
    n_chunks = kt_ref.shape[2]
    tk = kt_ref.shape[4]
    lane = lax.broadcasted_iota(jnp.int32, (1, LANES), 1)

    def one_head(hd, ones_lane):
        q = q_ref[0, hd]
        m = None
        acc = None
        for c in range(n_chunks):
            s = _dot(q, kt_ref[0, hd, c])
            v = v_ref[0, hd, c * tk:(c + 1) * tk, :]
            smax = jnp.max(s, axis=1, keepdims=True)
            if c == 0:
                m = smax
                acc = _dot(jnp.exp2((s - m).astype(BF16)), v)
            else:
                m_new = jnp.maximum(m, smax)
                acc = acc * jnp.exp2(m - m_new) + _dot(jnp.exp2((s - m_new).astype(BF16)), v)
                m = m_new
        row_sum = acc[:, ones_lane:ones_lane + 1]
        return acc * (1.0 / row_sum)

    def pair(j, carry):
        even = one_head(2 * j, V_HEAD)
        odd = one_head(2 * j + 1, 0)
        o_ref[0, j] = jnp.where(lane < V_HEAD, even, odd).astype(BF16)
        return carry

    lax.fori_loop(0, N_HEADS // 2, pair, 0)


def _attention(q, kt, v):
    nb, _, seq, _ = q.shape
    nck = kt.shape[2]
    return pl.pallas_call(
        _attn_kernel,
        grid=(nb, seq // Q_TILE),
        in_specs=[
            pl.BlockSpec((1, N_HEADS, Q_TILE, HEAD_PAD), lambda b, i: (b, 0, i, 0)),
            pl.BlockSpec((1, N_HEADS, nck, HEAD_PAD, K_TILE), lambda b, i: (b, 0, 0, 0, 0)),
            pl.BlockSpec((1, N_HEADS, seq, HEAD_PAD), lambda b, i: (b, 0, 0, 0)),
        ],
        out_specs=pl.BlockSpec((1, N_HEADS // 2, Q_TILE, LANES), lambda b, i: (b, 0, i, 0)),
        out_shape=jax.ShapeDtypeStruct((nb, N_HEADS // 2, seq, LANES), BF16),
        compiler_params=_params(("arbitrary", "arbitrary")),
        name="attn",
    )(q, kt, v)


_R_GROUP_ROW = N_EXPERTS


def _postmix_kernel(a_ref, sn_ref, x_ref, mod_ref, gao_ref, wout_ref, gpost_ref, gpre_ref, wr_ref, br_ref, tri_ref,
                    x1_ref, h2_ref, eid_ref, rank_ref, wcol_ref, cnt_ref, run_ref):
    first = jnp.logical_and(pl.program_id(0) == 0, pl.program_id(1) == 0)

    @pl.when(first)
    def _():
        run_ref[...] = jnp.zeros_like(run_ref)

    mod = mod_ref[0]
    gate1, shift2, scale2 = mod[2:3], mod[3:4], mod[4:5]
    a = jnp.concatenate([a_ref[0, j] for j in range(N_HEADS // 2)], axis=1).astype(F32)
    an = (_rms(a) * gao_ref[...]).astype(BF16)
    merged = jnp.concatenate([an, sn_ref[0]], axis=1)
    o = _dot(merged, wout_ref[...])
    x1 = x_ref[0] + gate1 * (_rms(o) * gpost_ref[...])
    x1_ref[0] = x1
    h2 = _rms(x1) * (gpre_ref[...] * (1.0 + scale2)) + shift2
    n_tok = h2.shape[0]
    _store_token_rows(h2_ref, h2)

    h_hi, h_lo = _split_bf16(h2)
    wr = wr_ref[...]
    hh = _dot(h_hi, wr)
    lh = _dot(h_lo, wr[:, :LANES])
    logits = hh[:, :LANES] + hh[:, LANES:] + lh + br_ref[...]
    lt = logits.T

    neg = jnp.float32(-jnp.inf)
    row8 = lax.broadcasted_iota(jnp.int32, (SUBLANES, n_tok), 0).astype(F32)
    lg = jnp.where(row8 < N_EXPERT_GROUPS, lt[_R_GROUP_ROW:_R_GROUP_ROW + SUBLANES], neg)
    gmax = jnp.max(lg, axis=0, keepdims=True)
    gi = jnp.min(jnp.where(lg == gmax, row8, float(SUBLANES)), axis=0, keepdims=True)
    pg_sel = 1.0 / jnp.sum(jnp.exp(lg - gmax), axis=0, keepdims=True)

    le = jnp.zeros((EXPERTS_PER_GROUP, n_tok), F32)
    for g in range(N_EXPERT_GROUPS):
        le = jnp.where(gi == float(g), lt[g * EXPERTS_PER_GROUP:(g + 1) * EXPERTS_PER_GROUP], le)
    v1 = jnp.max(le, axis=0, keepdims=True)
    i1 = jnp.min(jnp.where(le == v1, row8, float(SUBLANES)), axis=0, keepdims=True)
    le2 = jnp.where(row8 == i1, neg, le)
    v2 = jnp.max(le2, axis=0, keepdims=True)
    i2 = jnp.min(jnp.where(le2 == v2, row8, float(SUBLANES)), axis=0, keepdims=True)
    r = jnp.exp(v2 - v1)
    w1 = pg_sel / (1.0 + r)
    w2 = w1 * r
    e1 = gi * float(EXPERTS_PER_GROUP) + i1
    e2 = gi * float(EXPERTS_PER_GROUP) + i2
    eid_ref[...] = jnp.concatenate([e1, e2], axis=0).astype(jnp.int32)

    row32 = lax.broadcasted_iota(jnp.int32, (N_EXPERTS, n_tok), 0).astype(F32)
    hit1 = row32 == e1
    hit2 = row32 == e2
    onehot = jnp.where(jnp.logical_or(hit1, hit2), 1.0, 0.0)
    run = run_ref[...][:, 0:1]
    ranks1, ranks2 = [], []
    for c in range(n_tok // RANK_CHUNK):
        sl = slice(c * RANK_CHUNK, (c + 1) * RANK_CHUNK)
        oh = onehot[:, sl]
        before = _dot(oh.astype(BF16), tri_ref[...]) + run
        ranks1.append(jnp.sum(jnp.where(hit1[:, sl], before, 0.0), axis=0, keepdims=True))
        ranks2.append(jnp.sum(jnp.where(hit2[:, sl], before, 0.0), axis=0, keepdims=True))
        run = run + jnp.sum(oh, axis=1, keepdims=True)
    rank_ref[...] = jnp.concatenate(
        [jnp.concatenate(ranks1, axis=1), jnp.concatenate(ranks2, axis=1)], axis=0).astype(jnp.int32)
    run_b = jnp.broadcast_to(run, run_ref.shape)
    run_ref[...] = run_b
    cnt_ref[...] = run_b

    wcol_ref[0] = jnp.where(row8 == 0.0, w1, jnp.where(row8 == 1.0, w2, 0.0))


def _postmix(a, sn, x, mod, P):
    nb, seq, _ = x.shape
    tm = POST_TILE
    const = lambda b, i: (0, 0)
    return pl.pallas_call(
        _postmix_kernel,
        grid=(nb, seq // tm),
        in_specs=[
            pl.BlockSpec((1, N_HEADS // 2, tm, LANES), lambda b, i: (b, 0, i, 0)),
            pl.BlockSpec((1, tm, GMLP_WIDTH), lambda b, i: (b, i, 0)),
            pl.BlockSpec((1, tm, D_MODEL), lambda b, i: (b, i, 0)),
            pl.BlockSpec((1, SUBLANES, D_MODEL), lambda b, i: (b, 0, 0)),
            pl.BlockSpec((1, N_HEADS * V_HEAD), const),
            pl.BlockSpec((D_MODEL, D_MODEL), const),
            pl.BlockSpec((1, D_MODEL), const),
            pl.BlockSpec((1, D_MODEL), const),
            pl.BlockSpec((D_MODEL, 2 * LANES), const),
            pl.BlockSpec((1, LANES), const),
            pl.BlockSpec((RANK_CHUNK, RANK_CHUNK), const),
        ],
        out_specs=[
            pl.BlockSpec((1, tm, D_MODEL), lambda b, i: (b, i, 0)),
            pl.BlockSpec((tm * ROW_TILES, LANES), lambda b, i: (b * (seq // tm) + i, 0)),
            pl.BlockSpec((2, tm), lambda b, i: (0, b * (seq // tm) + i)),
            pl.BlockSpec((2, tm), lambda b, i: (0, b * (seq // tm) + i)),
            pl.BlockSpec((1, SUBLANES, tm), lambda b, i: (b, 0, i)),
            pl.BlockSpec((N_EXPERTS, LANES), const),
        ],
        out_shape=[
            jax.ShapeDtypeStruct((nb, seq, D_MODEL), F32),
            jax.ShapeDtypeStruct((nb * seq * ROW_TILES, LANES), U32),
            jax.ShapeDtypeStruct((2, nb * seq), jnp.int32),
            jax.ShapeDtypeStruct((2, nb * seq), jnp.int32),
            jax.ShapeDtypeStruct((nb, SUBLANES, seq), F32),
            jax.ShapeDtypeStruct((N_EXPERTS, LANES), F32),
        ],
        scratch_shapes=[pltpu.VMEM((N_EXPERTS, LANES), F32)],
        compiler_params=_params(("arbitrary", "arbitrary")),
        name="postmix",
    )(a, sn, x, mod, P["g_attn_out"], P["w_out"], P["g_post1"], P["g_pre2"], P["w_router"], P["b_router"], P["tri"])


def _sc_mesh():
    return plsc.VectorSubcoreMesh(core_axis_name="c", subcore_axis_name="s")


def _sc_worker():
    return lax.axis_index("s") * SC_CORES + lax.axis_index("c")


def _sc_gather_rows(table, idx):
    n = idx.shape[0]
    assert n % (SC_WORKERS * SC_ROWS) == 0
    per_w = n // SC_WORKERS
    steps = per_w // SC_ROWS
    idx3 = idx.reshape(SC_WORKERS, steps, SC_ROWS)

    @functools.partial(
        pl.kernel, mesh=_sc_mesh(),
        out_type=jax.ShapeDtypeStruct((n, ROW_TILES, LANES), U32),
        scratch_types=[pltpu.VMEM((steps, SC_ROWS), jnp.int32), pltpu.VMEM((SC_ROWS, ROW_TILES, LANES), U32),
                       pltpu.SemaphoreType.DMA],
        name="sc_gather_rows",
    )
    def gather(table_hbm, idx_hbm, out_hbm, idx_v, rows_v, sem):
        wid = _sc_worker()
        pltpu.sync_copy(idx_hbm.at[wid], idx_v)

        @pl.loop(0, steps)
        def _(j):
            pltpu.async_copy(table_hbm.at[idx_v.at[j]], rows_v, sem).wait()
            pltpu.sync_copy(rows_v, out_hbm.at[pl.ds(wid * per_w + j * SC_ROWS, SC_ROWS)])

    return gather(table, idx3)


def _sc_scatter_rows(src, idx, n_dst):
    n_dup, n = idx.shape
    assert n % (SC_WORKERS * SC_ROWS) == 0
    per_w = n // SC_WORKERS
    steps = per_w // SC_ROWS
    idx4 = idx.reshape(n_dup, SC_WORKERS, steps, SC_ROWS).transpose(1, 0, 2, 3)

    @functools.partial(
        pl.kernel, mesh=_sc_mesh(),
        out_type=jax.ShapeDtypeStruct((n_dst, ROW_TILES, LANES), U32),
        scratch_types=[pltpu.VMEM((n_dup, steps, SC_ROWS), jnp.int32), pltpu.VMEM((SC_ROWS, ROW_TILES, LANES), U32),
                       pltpu.SemaphoreType.DMA],
        name="sc_scatter_rows",
    )
    def scatter(src_hbm, idx_hbm, dst_hbm, idx_v, rows_v, sem):
        wid = _sc_worker()
        pltpu.sync_copy(idx_hbm.at[wid], idx_v)

        @pl.loop(0, steps)
        def _(j):
            pltpu.sync_copy(src_hbm.at[pl.ds(wid * per_w + j * SC_ROWS, SC_ROWS)], rows_v)
            copies = [pltpu.async_copy(rows_v, dst_hbm.at[idx_v.at[k, j]], sem) for k in range(n_dup)]
            for copy in copies:
                copy.wait()

    return scatter(src, idx4)


def _expert_kernel(te_ref, tr_ref, nv_ref, xs_ref, wg_ref, wu_ref, wd_ref, ys_ref, wgu_bf, wd_bf):
    i = pl.program_id(0)
    valid = i < nv_ref[0]
    new_expert = jnp.logical_or(i == 0, te_ref[i] != te_ref[jnp.maximum(i - 1, 0)])

    @pl.when(jnp.logical_and(valid, new_expert))
    def _():
        wgu_bf[:, :EXPERT_FF] = wg_ref[0].astype(BF16)
        wgu_bf[:, EXPERT_FF:] = wu_ref[0].astype(BF16)
        wd_bf[...] = wd_ref[0].astype(BF16)

    @pl.when(valid)
    def _():
        n_rows = xs_ref.shape[0] // ROW_TILES
        x = _load_token_rows(xs_ref, n_rows)
        row = lax.broadcasted_iota(jnp.int32, (n_rows, 1), 0)
        x = jnp.where(row < tr_ref[i], x.astype(BF16), jnp.zeros((), BF16))
        gu = _dot(x, wgu_bf[...])
        g, u = gu[:, :EXPERT_FF], gu[:, EXPERT_FF:]
        act = (g * jax.nn.sigmoid(g) * u).astype(BF16)
        _store_token_rows(ys_ref, _dot(act, wd_bf[...]))

    @pl.when(jnp.logical_not(valid))
    def _():
        ys_ref[...] = jnp.zeros_like(ys_ref)


def _expert_tile(n_tok):
    return EXPERT_TILE_LARGE if 2 * n_tok // N_EXPERTS >= 2 * EXPERT_TILE_LARGE else EXPERT_TILE_SMALL


def _experts(xs, tile_expert, tile_rows, n_valid, tile, P):
    rows_blk = tile * ROW_TILES
    assert xs.shape[0] % rows_blk == 0
    n_tiles = xs.shape[0] // rows_blk

    def row_map(i, te, tr, nv):
        return (jnp.minimum(i, nv[0] - 1), 0)

    def out_map(i, te, tr, nv):
        return (i, 0)

    def w_map(i, te, tr, nv):
        return (te[jnp.minimum(i, nv[0] - 1)], 0, 0)

    return pl.pallas_call(
        _expert_kernel,
        grid_spec=pltpu.PrefetchScalarGridSpec(
            num_scalar_prefetch=3,
            grid=(n_tiles,),
            in_specs=[
                pl.BlockSpec((rows_blk, LANES), row_map),
                pl.BlockSpec((1, D_MODEL, EXPERT_FF), w_map),
                pl.BlockSpec((1, D_MODEL, EXPERT_FF), w_map),
                pl.BlockSpec((1, EXPERT_FF, D_MODEL), w_map),
            ],
            out_specs=pl.BlockSpec((rows_blk, LANES), out_map),
            scratch_shapes=[pltpu.VMEM((D_MODEL, 2 * EXPERT_FF), BF16), pltpu.VMEM((EXPERT_FF, D_MODEL), BF16)],
        ),
        out_shape=jax.ShapeDtypeStruct(xs.shape, U32),
        compiler_params=_params(("arbitrary",)),
        name="experts",
    )(tile_expert, tile_rows, n_valid, xs, P["w_gate"], P["w_up"], P["w_down"])


def _final_kernel(y0_ref, y1_ref, wcol_ref, x1_ref, mod_ref, gpost_ref, o_ref):
    w_rows = wcol_ref[0]
    n_tok = w_rows.shape[1]
    w = jnp.concatenate([w_rows, jnp.zeros((LANES - SUBLANES, n_tok), F32)], axis=0).T
    w0, w1 = w[:, 0:1], w[:, 1:2]
    m = w0 * _load_token_rows(y0_ref, n_tok) + w1 * _load_token_rows(y1_ref, n_tok)
    gate2 = mod_ref[0][5:6]
    o_ref[0] = x1_ref[0] + gate2 * (_rms(m) * gpost_ref[...])


def _final(yg, wcol, x1, mod, P):
    nb, seq, _ = x1.shape
    tm = POST_TILE
    nt = seq // tm
    n_tok_tiles = nb * nt
    deep = pl.Buffered(FINAL_INPUT_BUFFERS)

    def pipelined(yg_hbm, wcol_hbm, x1_hbm, mod_hbm, gpost_hbm, o_hbm):
        pltpu.emit_pipeline(
            _final_kernel,
            grid=(nb, nt),
            in_specs=[
                pl.BlockSpec((tm * ROW_TILES, LANES), lambda b, i: (b * nt + i, 0), pipeline_mode=deep),
                pl.BlockSpec((tm * ROW_TILES, LANES), lambda b, i: (n_tok_tiles + b * nt + i, 0), pipeline_mode=deep),
                pl.BlockSpec((1, SUBLANES, tm), lambda b, i: (b, 0, i), pipeline_mode=deep),
                pl.BlockSpec((1, tm, D_MODEL), lambda b, i: (b, i, 0), pipeline_mode=deep),
                pl.BlockSpec((1, SUBLANES, D_MODEL), lambda b, i: (b, 0, 0)),
                pl.BlockSpec((1, D_MODEL), lambda b, i: (0, 0)),
            ],
            out_specs=[pl.BlockSpec((1, tm, D_MODEL), lambda b, i: (b, i, 0))],
        )(yg_hbm, yg_hbm, wcol_hbm, x1_hbm, mod_hbm, gpost_hbm, o_hbm)

    any_spec = pl.BlockSpec(memory_space=pl.ANY)
    return pl.pallas_call(
        pipelined,
        in_specs=[any_spec] * 5,
        out_specs=any_spec,
        out_shape=jax.ShapeDtypeStruct((nb, seq, D_MODEL), F32),
        compiler_params=pltpu.CompilerParams(vmem_limit_bytes=VMEM_LIMIT),
        name="final",
    )(yg, wcol, x1, mod, P["g_post2"])


def _prepare(w):
    f = lambda a: a.astype(F32)
    P = {}
    for k in ("g_pre1", "g_post1", "g_pre2", "g_post2", "g_q", "g_kv", "g_attn_out", "g_gmlp_out"):
        P[k] = f(w[k]).reshape(1, -1)
    P["g_v"] = f(w["g_v_gmlp"]).reshape(1, -1)

    w_in = f(w["w_in"])
    o0, o1, o2, o3 = Q_LORA, Q_LORA + KV_LORA, Q_LORA + KV_LORA + QK_ROPE, Q_LORA + KV_LORA + QK_ROPE + GMLP_WIDTH
    w_kr = w_in[:, o1:o2]
    kr_partner = jnp.concatenate([-w_kr[:, ROPE_HALF:], w_kr[:, :ROPE_HALF]], axis=1)
    rope_blk = jnp.concatenate([jnp.zeros((D_MODEL, QK_NOPE), F32), w_kr, kr_partner], axis=1)
    P["w_in"] = jnp.concatenate([w_in[:, :o1], rope_blk, w_in[:, o2:o3], w_in[:, o3:]], axis=1).astype(BF16)

    w_uq = f(w["w_uq"]).reshape(Q_LORA, N_HEADS, QK_NOPE + QK_ROPE)
    q_rope = w_uq[:, :, QK_NOPE:]
    q_partner = jnp.concatenate([-q_rope[:, :, ROPE_HALF:], q_rope[:, :, :ROPE_HALF]], axis=2)
    P["w_uq"] = jnp.concatenate([w_uq, q_partner], axis=2).reshape(Q_LORA, N_HEADS * HEAD_PAD).astype(BF16)

    w_ukv = f(w["w_ukv"]).reshape(KV_LORA, N_HEADS, QK_NOPE + V_HEAD)
    zeros = jnp.zeros((KV_LORA, N_HEADS, HEAD_PAD - QK_NOPE), F32)
    w_k = jnp.concatenate([w_ukv[:, :, :QK_NOPE], zeros], axis=2)
    w_v = w_ukv[:, :, QK_NOPE:]
    even = (jnp.arange(N_HEADS) % 2 == 0)[None, :, None]
    zv = jnp.zeros_like(w_v)
    w_v = jnp.concatenate([jnp.where(even, w_v, zv), jnp.where(even, zv, w_v)], axis=2)
    P["w_ukv"] = jnp.concatenate([w_k.reshape(KV_LORA, -1), w_v.reshape(KV_LORA, -1)], axis=1).astype(BF16)

    P["w_sp"] = f(w["w_spatial"]).reshape(GMLP_GROUPS // 2, 2 * CHUNK, CHUNK).astype(BF16)
    P["b_sp"] = jnp.repeat(f(w["b_spatial"]).T, GMLP_GROUP_DIM, axis=1)

    P["w_out"] = f(w["w_out"]).astype(BF16)

    pad = jnp.zeros((D_MODEL, LANES - N_EXPERTS - N_EXPERT_GROUPS), F32)
    wr = jnp.concatenate([f(w["w_router_expert"]), f(w["w_router_group"]), pad], axis=1)
    wr_hi = wr.astype(BF16)
    wr_lo = (wr - wr_hi.astype(F32)).astype(BF16)
    P["w_router"] = jnp.concatenate([wr_hi, wr_lo], axis=1)
    P["b_router"] = jnp.concatenate(
        [f(w["b_router_expert"]), f(w["b_router_group"]), jnp.zeros((LANES - N_EXPERTS - N_EXPERT_GROUPS,), F32)]
    ).reshape(1, LANES)
    P["tri"] = jnp.triu(jnp.ones((RANK_CHUNK, RANK_CHUNK), F32), k=1).astype(BF16)

    P["w_gate"], P["w_up"], P["w_down"] = f(w["w_gate"]), f(w["w_up"]), f(w["w_down"])
    return P


def _rope_tables(seq):
    inv = ROPE_THETA ** (-jnp.arange(ROPE_HALF, dtype=F32) / ROPE_HALF)
    ang = jnp.arange(seq, dtype=F32)[:, None] * inv[None, :]
    z_lo = jnp.zeros((seq, _ROPE_LO), F32)
    z_hi = jnp.zeros((seq, LANES - _ROPE_LO - QK_ROPE), F32)
    cos = jnp.concatenate([z_lo, jnp.cos(ang), jnp.cos(ang), z_hi], axis=1)
    sin = jnp.concatenate([z_lo, jnp.sin(ang), jnp.sin(ang), z_hi], axis=1)
    return cos, sin


def _layer(x, mod, P):
    nb, seq, d_model = x.shape
    assert d_model == D_MODEL and TOKEN_TILE % K_TILE == 0
    assert seq % TOKEN_TILE == 0 and seq % Q_TILE == 0 and seq % POST_TILE == 0 and POST_TILE % RANK_CHUNK == 0
    n_tok = nb * seq
    q, kt, v, sn = _premix(x, mod, P)
    a = _attention(q, kt, v)
    x1, h2rows, eid, rank, wcol, counts = _postmix(a, sn, x, mod, P)

    tile = _expert_tile(n_tok)
    cnt = counts[:, 0].astype(jnp.int32)
    padded = ((cnt + tile - 1) // tile) * tile
    ends = jnp.cumsum(padded)
    starts = ends - padded
    onehot = eid[:, :, None] == jnp.arange(N_EXPERTS, dtype=jnp.int32)[None, None, :]
    pos = rank + jnp.sum(jnp.where(onehot, starts[None, None, :], 0), axis=2)
    n_rows = 2 * n_tok + N_EXPERTS * tile
    n_tiles = n_rows // tile
    tile_start = jnp.arange(n_tiles, dtype=jnp.int32) * tile
    tile_expert = jnp.minimum(
        jnp.sum((tile_start[:, None] >= ends[None, :]).astype(jnp.int32), axis=1), N_EXPERTS - 1).astype(jnp.int32)
    expert_ids = jnp.arange(N_EXPERTS, dtype=jnp.int32)
    live_end = jnp.sum(jnp.where(tile_expert[:, None] == expert_ids[None, :], (starts + cnt)[None, :], 0), axis=1)
    tile_rows = jnp.clip(live_end - tile_start, 0, tile).astype(jnp.int32)
    n_valid = (ends[-1:] // tile).astype(jnp.int32)

    as_tiles = lambda a: a.reshape(-1, ROW_TILES, LANES)
    as_rows = lambda a: a.reshape(-1, LANES)
    xs = _sc_scatter_rows(as_tiles(h2rows), pos, n_rows)
    ys = _experts(as_rows(xs), tile_expert, tile_rows, n_valid, tile, P)
    yg = as_rows(_sc_gather_rows(as_tiles(ys), pos.reshape(2 * n_tok)))
    return _final(yg, wcol, x1, mod, P)


def kernel(x_prompt, x_sample, c_prompt, c_sample, w_ada, b_ada, g_pre1, g_post1, g_pre2, g_post2, w_in, g_q, w_uq,
           g_kv, w_ukv, g_v_gmlp, w_spatial, b_spatial, g_attn_out, g_gmlp_out, w_out, w_router_group,
           b_router_group, w_router_expert, b_router_expert, w_gate, w_up, w_down):
    P = _prepare(dict(
        g_pre1=g_pre1, g_post1=g_post1, g_pre2=g_pre2, g_post2=g_post2, w_in=w_in, g_q=g_q, w_uq=w_uq, g_kv=g_kv,
        w_ukv=w_ukv, g_v_gmlp=g_v_gmlp, w_spatial=w_spatial, b_spatial=b_spatial, g_attn_out=g_attn_out,
        g_gmlp_out=g_gmlp_out, w_out=w_out, w_router_group=w_router_group, b_router_group=b_router_group,
        w_router_expert=w_router_expert, b_router_expert=b_router_expert, w_gate=w_gate, w_up=w_up, w_down=w_down))
    P["rope"] = {seq: _rope_tables(seq) for seq in {x_prompt.shape[1], x_sample.shape[1]}}

    nbp = c_prompt.shape[0]
    c_all = jnp.concatenate([c_prompt, c_sample], axis=0).astype(F32)
    mod = _ada(c_all, w_ada.astype(F32), b_ada.astype(F32))
    mod = mod.reshape(c_all.shape[0], 6, D_MODEL)
    mod = jnp.concatenate([mod, jnp.zeros((c_all.shape[0], SUBLANES - 6, D_MODEL), F32)], axis=1)

    y_prompt = _layer(x_prompt, mod[:nbp], P)
    y_sample = _layer(x_sample, mod[nbp:], P)
    return (y_prompt, y_sample)
```
